```python
import math, functools
import jax, jax.numpy as jnp
from jax import lax
import numpy as np

D_MODEL = 1024
BATCH = 16
SEQ = 256
DEPTH = 4
DEC_BATCH = 8
DEC_SEQ = 4096
PAST_LEN = 512

GRID_W = 64
MIX_WIDTH = D_MODEL
CONV_CH = D_MODEL // 4
HYENA_CH = D_MODEL // 4
FOURIER_CH = D_MODEL // 4
HEAD_DIM = 64
N_Q_HEADS = (D_MODEL // 4) // HEAD_DIM
N_KV_HEADS = N_Q_HEADS // 2
Q_PER_KV = N_Q_HEADS // N_KV_HEADS
ATT_CH = N_Q_HEADS * HEAD_DIM
IN_COLS = 2 * CONV_CH + 3 * HYENA_CH + FOURIER_CH + (N_Q_HEADS + 2 * N_KV_HEADS) * HEAD_DIM
CONV_WIDTH = 31
SHORT_WIDTH = 3
HYENA_ORDER = 2
HYENA_EMB = 33
HYENA_FFN = 64
HYENA_DECAY_TARGET = 1e-2
HYENA_FAST_PCT = 0.3
HYENA_SLOW_PCT = 1.5
HYENA_SHIFT = 0.05
WINDOW = 128
ATT_BLOCK = 128
ROPE_BASE = 10000.0
N_GROUPS = 4
EXPERTS_PER_GROUP = 8
N_EXPERTS = N_GROUPS * EXPERTS_PER_GROUP
TOP_K = 2
D_EXPERT = 256
MOE_BLOCK = 256
N_MOD = 6
EPS = 1e-6
NEG_INF = -1e30

kernel_name = "hybrid_prefix_diffusion_trunk_step"


def rmsnorm(x, g):
    xf = x.astype(jnp.float32)
    y = xf * lax.rsqrt(jnp.mean(xf * xf, axis=-1, keepdims=True) + EPS)
    return (y * g.astype(jnp.float32)).astype(x.dtype)


def adaln(cvec, w, b):
    m = jax.nn.silu(cvec) @ w + b
    return [part[:, None, :] for part in jnp.split(m, N_MOD, axis=-1)]


def depthwise_conv(x, w, b):
    K, C = w.shape
    p = (K - 1) // 2
    y = lax.conv_general_dilated(x, w[:, None, :].astype(x.dtype), (1,), [(p, p)],
                                 dimension_numbers=('NWC', 'WIO', 'NWC'), feature_group_count=C)
    return y + b.astype(x.dtype)


def axial_rope_tables(L):
    rows = L // GRID_W
    row_pos = jnp.repeat(jnp.arange(rows, dtype=jnp.float32), GRID_W)
    col_pos = jnp.tile(jnp.arange(GRID_W, dtype=jnp.float32), rows)
    n_freq = HEAD_DIM // 4
    inv = ROPE_BASE ** (-jnp.arange(n_freq, dtype=jnp.float32) / n_freq)
    ang = jnp.concatenate([row_pos[:, None] * inv, col_pos[:, None] * inv], axis=-1)
    return jnp.cos(ang), jnp.sin(ang)


def apply_rope(x, cos, sin):
    x1, x2 = jnp.split(x, 2, axis=-1)
    c = cos[None, :, None, :].astype(x.dtype)
    s = sin[None, :, None, :].astype(x.dtype)
    return jnp.concatenate([x1 * c - x2 * s, x1 * s + x2 * c], axis=-1)


def conformer_conv(u, dw_w, dw_b, ln_g, ln_b):
    a, g = jnp.split(u, 2, axis=-1)
    z = depthwise_conv(a * jax.nn.sigmoid(g), dw_w, dw_b).astype(jnp.float32)
    mu = jnp.mean(z, axis=-1, keepdims=True)
    var = jnp.mean(jnp.square(z - mu), axis=-1, keepdims=True)
    zn = (z - mu) * lax.rsqrt(var + EPS) * ln_g.astype(jnp.float32) + ln_b.astype(jnp.float32)
    return jax.nn.silu(zn).astype(u.dtype)


def hyena_kernel_fft(L, fw1, fb1, fr1, fw2, fb2, fr2, fw3, fb3):
    f = lambda a: a.astype(jnp.float32)
    t = jnp.linspace(0.0, 1.0, L, dtype=jnp.float32)[:, None]
    n_bands = (HYENA_EMB - 1) // 2
    w = 2.0 * math.pi * jnp.arange(L, dtype=jnp.float32)[:, None] / L
    fr = jnp.linspace(1e-4, n_bands - 1, n_bands, dtype=jnp.float32)[None, :]
    z = jnp.concatenate([t, jnp.cos(fr * w), -jnp.sin(fr * w)], axis=-1)
    h = jnp.sin(f(fr1) * (z @ f(fw1) + f(fb1)))
    h = jnp.sin(f(fr2) * (h @ f(fw2) + f(fb2)))
    h = (h @ f(fw3) + f(fb3)).reshape(L, 2, HYENA_ORDER, HYENA_CH)
    max_decay = math.log(HYENA_DECAY_TARGET) / HYENA_FAST_PCT
    min_decay = math.log(HYENA_DECAY_TARGET) / HYENA_SLOW_PCT
    deltas = jnp.linspace(min_decay, max_decay, HYENA_CH, dtype=jnp.float32)
    window = jnp.exp(-t * jnp.abs(deltas)[None, :]) + HYENA_SHIFT
    h = h * window[:, None, None, :]
    hf, hb = h[:, 0], h[:, 1]
    k = jnp.concatenate([hf[:1] + hb[:1], hf[1:], jnp.zeros_like(hf[:1]), hb[:0:-1]], axis=0)
    k = k / (jnp.sum(jnp.abs(k), axis=0, keepdims=True) + EPS)
    return jnp.fft.rfft(k, axis=0)


def hyena(u, short_w, short_b, kf, hy_d):
    L = u.shape[1]
    u = depthwise_conv(u, short_w, short_b)
    v, x1, x2 = jnp.split(u, 3, axis=-1)
    z = v.astype(jnp.float32)
    for o, gate in enumerate((x1, x2)):
        zf = jnp.fft.rfft(z, n=2 * L, axis=1)
        y = jnp.fft.irfft(zf * kf[None, :, o, :], n=2 * L, axis=1)[:, :L]
        z = gate.astype(jnp.float32) * (y + hy_d[o].astype(jnp.float32) * z)
    return z.astype(u.dtype)


def fourier_mix(u):
    return jnp.fft.fft2(u.astype(jnp.float32), axes=(1, 2), norm='ortho').real.astype(u.dtype)


def context_attention(q, k, v, sink):
    B, S = q.shape[:2]
    scale = HEAD_DIM ** -0.5
    sink_b = jnp.broadcast_to(sink.astype(jnp.float32).reshape(1, N_KV_HEADS, Q_PER_KV, 1, 1),
                              (B, N_KV_HEADS, Q_PER_KV, ATT_BLOCK, 1))

    def block(i):
        qb = lax.dynamic_slice_in_dim(q, i * ATT_BLOCK, ATT_BLOCK, 1).reshape(
            B, ATT_BLOCK, N_KV_HEADS, Q_PER_KV, HEAD_DIM)
        s = jnp.einsum('bqgrd,bkgd->bgrqk', qb, k).astype(jnp.float32) * scale
        p = jax.nn.softmax(jnp.concatenate([s, sink_b], axis=-1), axis=-1)[..., :S]
        o = jnp.einsum('bgrqk,bkgd->bqgrd', p.astype(v.dtype), v)
        return o.reshape(B, ATT_BLOCK, ATT_CH)

    out = lax.map(block, jnp.arange(S // ATT_BLOCK))
    return out.transpose(1, 0, 2, 3).reshape(B, S, ATT_CH)


def latent_attention(q, k, v, ck, cv, sink, cos, sin):
    B, L = q.shape[:2]
    P = ck.shape[1]
    span = ATT_BLOCK + 2 * WINDOW
    scale = HEAD_DIM ** -0.5
    qr = apply_rope(q, cos, sin)
    kr = apply_rope(k, cos, sin)
    pad = ((0, 0), (WINDOW, WINDOW), (0, 0), (0, 0))
    kp = jnp.pad(kr, pad)
    vp = jnp.pad(v, pad)
    sink_b = jnp.broadcast_to(sink.astype(jnp.float32).reshape(1, N_KV_HEADS, Q_PER_KV, 1, 1),
                              (B, N_KV_HEADS, Q_PER_KV, ATT_BLOCK, 1))

    def block(i):
        q0 = i * ATT_BLOCK
        qb = lax.dynamic_slice_in_dim(qr, q0, ATT_BLOCK, 1).reshape(B, ATT_BLOCK, N_KV_HEADS, Q_PER_KV, HEAD_DIM)
        qc = lax.dynamic_slice_in_dim(q, q0, ATT_BLOCK, 1).reshape(B, ATT_BLOCK, N_KV_HEADS, Q_PER_KV, HEAD_DIM)
        kb = lax.dynamic_slice_in_dim(kp, q0, span, 1)
        vb = lax.dynamic_slice_in_dim(vp, q0, span, 1)
        s_loc = jnp.einsum('bqgrd,bkgd->bgrqk', qb, kb).astype(jnp.float32) * scale
        qpos = q0 + jnp.arange(ATT_BLOCK)
        kpos = q0 - WINDOW + jnp.arange(span)
        ok = (jnp.abs(qpos[:, None] - kpos[None, :]) <= WINDOW) & (kpos >= 0)[None, :] & (kpos < L)[None, :]
        s_loc = jnp.where(ok, s_loc, NEG_INF)
        s_ctx = jnp.einsum('bqgrd,bkgd->bgrqk', qc, ck).astype(jnp.float32) * scale
        p = jax.nn.softmax(jnp.concatenate([s_loc, s_ctx, sink_b], axis=-1), axis=-1)
        o = (jnp.einsum('bgrqk,bkgd->bqgrd', p[..., :span].astype(v.dtype), vb)
             + jnp.einsum('bgrqk,bkgd->bqgrd', p[..., span:span + P].astype(cv.dtype), cv))
        return o.reshape(B, ATT_BLOCK, ATT_CH)

    out = lax.map(block, jnp.arange(L // ATT_BLOCK))
    return out.transpose(1, 0, 2, 3).reshape(B, L, ATT_CH)


def token_mixers(h, w_in, dw_w, dw_b, ln_g, ln_b, short_w, short_b, hy_d, w_out, kf, attend):
    B, L, _ = h.shape
    sizes = [2 * CONV_CH, 3 * HYENA_CH, FOURIER_CH, ATT_CH, N_KV_HEADS * HEAD_DIM, N_KV_HEADS * HEAD_DIM]
    u = h @ w_in
    ua, ub, uc, uq, uk, uv = jnp.split(u, np.cumsum(sizes)[:-1].tolist(), axis=-1)
    ya = conformer_conv(ua, dw_w, dw_b, ln_g, ln_b)
    yb = hyena(ub, short_w, short_b, kf, hy_d)
    yc = fourier_mix(uc)
    q = uq.reshape(B, L, N_Q_HEADS, HEAD_DIM)
    k = uk.reshape(B, L, N_KV_HEADS, HEAD_DIM)
    v = uv.reshape(B, L, N_KV_HEADS, HEAD_DIM)
    yd = attend(q, k, v)
    y = jnp.concatenate([ya, yb, yc, yd], axis=-1) @ w_out
    return y, k, v


def hier_moe(h, rc_w, rc_b, rf_w, rf_b, e_gate, e_up, e_down):
    B, L, D = h.shape
    t = h.reshape(-1, D)
    N = t.shape[0]
    rows = jnp.arange(N)
    lc = (t @ rc_w + rc_b).astype(jnp.float32)
    pc = jax.nn.softmax(lc, axis=-1)
    grp = lax.top_k(lc, 1)[1][:, 0]
    pg = pc[rows, grp]
    lf = (t @ rf_w + rf_b).astype(jnp.float32).reshape(N, N_GROUPS, EXPERTS_PER_GROUP)
    top_l, top_i = lax.top_k(lf[rows, grp], TOP_K)
    gw = jax.nn.softmax(top_l, axis=-1) * pg[:, None]
    eid = grp[:, None] * EXPERTS_PER_GROUP + top_i
    A = N * TOP_K
    flat_e = eid.reshape(-1)
    flat_t = jnp.repeat(jnp.arange(N, dtype=jnp.int32), TOP_K)
    flat_w = gw.reshape(-1)
    order = jnp.argsort(flat_e)
    se = flat_e[order]
    counts = jnp.bincount(flat_e, length=N_EXPERTS)
    padded = (counts + MOE_BLOCK - 1) // MOE_BLOCK * MOE_BLOCK
    pend = jnp.cumsum(padded)
    pstart = pend - padded
    start = jnp.cumsum(counts) - counts
    dest = pstart[se] + jnp.arange(A) - start[se]
    nb = -(-A // MOE_BLOCK) + N_EXPERTS
    slot_tok = jnp.zeros((nb * MOE_BLOCK,), jnp.int32).at[dest].set(flat_t[order])
    slot_w = jnp.zeros((nb * MOE_BLOCK,), jnp.float32).at[dest].set(flat_w[order])
    blk_e = jnp.minimum(jnp.searchsorted(pend, jnp.arange(nb) * MOE_BLOCK, side='right'), N_EXPERTS - 1)

    def run(args):
        tok, e = args
        xb = t[tok]
        return (jax.nn.silu(xb @ e_gate[e]) * (xb @ e_up[e])) @ e_down[e]

    yb = lax.map(run, (slot_tok.reshape(nb, MOE_BLOCK), blk_e))
    y = yb.reshape(-1, D) * slot_w[:, None].astype(yb.dtype)
    out = jnp.zeros_like(t).at[slot_tok].add(y)
    return out.reshape(B, L, D)


def trunk_layer(x, mods, n1, n2, mix_p, kf, attend, moe_p):
    sh1, sc1, g1, sh2, sc2, g2 = mods
    h = rmsnorm(x, n1) * (1 + sc1) + sh1
    y, k, v = token_mixers(h, *mix_p, kf, attend)
    x = x + g1 * y
    h = rmsnorm(x, n2) * (1 + sc2) + sh2
    x = x + g2 * hier_moe(h, *moe_p)
    return x, k, v


def setup_inputs(seed: int = 0) -> dict:
    key = jax.random.key(seed)
    ks = iter(jax.random.split(key, 48))
    nrm = lambda shape, scale: jax.random.normal(next(ks), shape, jnp.float32) * scale
    D = D_MODEL
    return {
        "x_prompt": nrm((BATCH, SEQ, D), 1.0),
        "x_sample": nrm((DEC_BATCH, DEC_SEQ, D), 1.0),
        "cache_k": nrm((DEC_BATCH, DEPTH, PAST_LEN, N_KV_HEADS, HEAD_DIM), 1.0),
        "cache_v": nrm((DEC_BATCH, DEPTH, PAST_LEN, N_KV_HEADS, HEAD_DIM), 1.0),
        "c": nrm((DEC_BATCH, D), 1.0),
        "c_ctx": nrm((D,), 1.0),
        "ada_w": nrm((DEPTH, D, N_MOD * D), 0.5 * D ** -0.5),
        "ada_b": nrm((DEPTH, N_MOD * D), 0.02),
        "norm1_g": 1.0 + nrm((DEPTH, D), 0.1),
        "norm2_g": 1.0 + nrm((DEPTH, D), 0.1),
        "w_in": nrm((DEPTH, D, IN_COLS), D ** -0.5),
        "conv_dw_w": nrm((DEPTH, CONV_WIDTH, CONV_CH), CONV_WIDTH ** -0.5),
        "conv_dw_b": nrm((DEPTH, CONV_CH), 0.02),
        "conv_ln_g": 1.0 + nrm((DEPTH, CONV_CH), 0.1),
        "conv_ln_b": nrm((DEPTH, CONV_CH), 0.02),
        "hy_short_w": nrm((DEPTH, SHORT_WIDTH, 3 * HYENA_CH), SHORT_WIDTH ** -0.5),
        "hy_short_b": nrm((DEPTH, 3 * HYENA_CH), 0.02),
        "hy_fw1": nrm((DEPTH, HYENA_EMB, HYENA_FFN), HYENA_EMB ** -0.5),
        "hy_fb1": nrm((DEPTH, HYENA_FFN), 0.02),
        "hy_freq1": 1.0 + nrm((DEPTH, HYENA_FFN), 0.1),
        "hy_fw2": nrm((DEPTH, HYENA_FFN, HYENA_FFN), HYENA_FFN ** -0.5),
        "hy_fb2": nrm((DEPTH, HYENA_FFN), 0.02),
        "hy_freq2": 1.0 + nrm((DEPTH, HYENA_FFN), 0.1),
        "hy_fw3": nrm((DEPTH, HYENA_FFN, 2 * HYENA_ORDER * HYENA_CH), HYENA_FFN ** -0.5),
        "hy_fb3": nrm((DEPTH, 2 * HYENA_ORDER * HYENA_CH), 0.02),
        "hy_d": nrm((DEPTH, HYENA_ORDER, HYENA_CH), 1.0),
        "attn_sink": nrm((DEPTH, N_Q_HEADS), 0.5),
        "w_out": nrm((DEPTH, MIX_WIDTH, D), MIX_WIDTH ** -0.5),
        "router_coarse_w": nrm((DEPTH, D, N_GROUPS), D ** -0.5),
        "router_coarse_b": nrm((DEPTH, N_GROUPS), 0.01),
        "router_fine_w": nrm((DEPTH, D, N_EXPERTS), D ** -0.5),
        "router_fine_b": nrm((DEPTH, N_EXPERTS), 0.01),
        "exp_gate": nrm((DEPTH, N_EXPERTS, D, D_EXPERT), D ** -0.5),
        "exp_up": nrm((DEPTH, N_EXPERTS, D, D_EXPERT), D ** -0.5),
        "exp_down": nrm((DEPTH, N_EXPERTS, D_EXPERT, D), D_EXPERT ** -0.5),
        "norm_f_g": 1.0 + nrm((D,), 0.1),
    }


def reference(x_prompt, x_sample, cache_k, cache_v, c, c_ctx, ada_w, ada_b, norm1_g, norm2_g, w_in,
              conv_dw_w, conv_dw_b, conv_ln_g, conv_ln_b, hy_short_w, hy_short_b, hy_fw1, hy_fb1, hy_freq1,
              hy_fw2, hy_fb2, hy_freq2, hy_fw3, hy_fb3, hy_d, attn_sink, w_out, router_coarse_w,
              router_coarse_b, router_fine_w, router_fine_b, exp_gate, exp_up, exp_down, norm_f_g):
    L_ctx = x_prompt.shape[1]
    L_lat = x_sample.shape[1]
    cos, sin = axial_rope_tables(L_lat)
    xp, xs = x_prompt, x_sample
    ks_out, vs_out = [], []
    for l in range(DEPTH):
        mods_p = adaln(c_ctx[None, :], ada_w[l], ada_b[l])
        mods_s = adaln(c, ada_w[l], ada_b[l])
        filt = (hy_fw1[l], hy_fb1[l], hy_freq1[l], hy_fw2[l], hy_fb2[l], hy_freq2[l], hy_fw3[l], hy_fb3[l])
        kf_p = hyena_kernel_fft(L_ctx, *filt)
        kf_s = hyena_kernel_fft(L_lat, *filt)
        mix_p = (w_in[l], conv_dw_w[l], conv_dw_b[l], conv_ln_g[l], conv_ln_b[l],
                 hy_short_w[l], hy_short_b[l], hy_d[l], w_out[l])
        moe_p = (router_coarse_w[l], router_coarse_b[l], router_fine_w[l], router_fine_b[l],
                 exp_gate[l], exp_up[l], exp_down[l])
        attend_p = functools.partial(context_attention, sink=attn_sink[l])
        attend_s = functools.partial(latent_attention, ck=cache_k[:, l], cv=cache_v[:, l],
                                     sink=attn_sink[l], cos=cos, sin=sin)
        xp, k_ctx, v_ctx = trunk_layer(xp, mods_p, norm1_g[l], norm2_g[l], mix_p, kf_p, attend_p, moe_p)
        ks_out.append(k_ctx)
        vs_out.append(v_ctx)
        xs, _, _ = trunk_layer(xs, mods_s, norm1_g[l], norm2_g[l], mix_p, kf_s, attend_s, moe_p)
    y_prompt = rmsnorm(xp, norm_f_g)
    y_sample = rmsnorm(xs, norm_f_g)
    state_k = jnp.stack(ks_out, axis=1)
    state_v = jnp.stack(vs_out, axis=1)
    return (y_prompt, y_sample, state_k, state_v)
```

```python
import functools
import math

import numpy as np
import jax
import jax.numpy as jnp
from jax import lax
from jax.experimental import pallas as pl
from jax.experimental.pallas import tpu as pltpu

F32 = jnp.float32
BF16 = jnp.bfloat16

HEAD_DIM = 64
N_Q_HEADS = 4
N_KV_HEADS = 2
Q_PER_KV = N_Q_HEADS // N_KV_HEADS
WINDOW = 128
ATT_BLOCK = 128
GRID_W = 64
ROPE_BASE = 10000.0
N_GROUPS = 4
EXPERTS_PER_GROUP = 8
N_EXPERTS = N_GROUPS * EXPERTS_PER_GROUP
MOE_BLOCK = 256
N_MOD = 6
EPS = 1e-6
NEG_INF = -1e30
HYENA_DECAY_TARGET = 1e-2
HYENA_FAST_PCT = 0.3
HYENA_SLOW_PCT = 1.5
HYENA_SHIFT = 0.05

LANES = 128
SUBLANES = 8
VMEM_LIMIT = 56 * 1024 * 1024

MIX_CH = 256


def _params(*sem):
    return pltpu.CompilerParams(dimension_semantics=sem, vmem_limit_bytes=VMEM_LIMIT)


def _silu(x):
    return x * jax.nn.sigmoid(x)


def _ada_body(c_ref, w_ref, b_ref, o_ref):
    s = _silu(c_ref[...]).astype(BF16)
    o_ref[0] = jnp.dot(s, w_ref[0].astype(BF16), preferred_element_type=F32) + b_ref[0]


def adaln_all(cvec, ada_w, ada_b):
    depth, d, n6 = ada_w.shape
    r = cvec.shape[0]
    tn = n6 // 4
    out = pl.pallas_call(
        _ada_body,
        grid=(depth, n6 // tn),
        in_specs=[
            pl.BlockSpec((r, d), lambda l, j: (0, 0)),
            pl.BlockSpec((1, d, tn), lambda l, j: (l, 0, j)),
            pl.BlockSpec((1, 1, tn), lambda l, j: (l, 0, j)),
        ],
        out_specs=pl.BlockSpec((1, r, tn), lambda l, j: (l, 0, j)),
        out_shape=jax.ShapeDtypeStruct((depth, r, n6), F32),
        compiler_params=_params("arbitrary", "arbitrary"),
        name="adaln",
    )(cvec, ada_w, ada_b.reshape(depth, 1, n6))
    return out.reshape(depth, r, N_MOD, d)


def _swap_halves(x):
    pieces = []
    for j in range(x.shape[1] // LANES):
        xj = x[:, j * LANES:(j + 1) * LANES]
        fwd = pltpu.roll(xj, LANES - HEAD_DIM // 2, axis=1)
        bwd = pltpu.roll(xj, HEAD_DIM // 2, axis=1)
        lane = lax.broadcasted_iota(jnp.int32, xj.shape, 1)
        pieces.append(jnp.where((lane % HEAD_DIM) < HEAD_DIM // 2, fwd, bwd))
    return pieces[0] if len(pieces) == 1 else jnp.concatenate(pieces, axis=1)


def _rmsnorm_mod(x, g, scale, shift):
    ms = jnp.mean(x * x, axis=-1, keepdims=True)
    return (x * lax.rsqrt(ms + EPS) * g) * (1.0 + scale) + shift


def _in_body(*refs, fuse_res, rope):
    it = iter(refs)
    x_ref = next(it)
    if fuse_res:
        moe_ref = next(it)
        pmod_ref = next(it)
    mod_ref = next(it)
    g_ref = next(it)
    w_ref = next(it)
    if rope:
        cos_ref = next(it)
        sin_ref = next(it)
    outs = list(it)
    x = x_ref[0]
    if fuse_res:
        x = x + pmod_ref[0, 5:6, :] * moe_ref[0]
        outs.pop(0)[0] = x
    h = _rmsnorm_mod(x, g_ref[...], mod_ref[0, 1:2, :], mod_ref[0, 0:1, :])
    u = jnp.dot(h.astype(BF16), w_ref[...], preferred_element_type=F32)
    c = MIX_CH
    ua_ref, ub_ref, uc_ref, uq_ref = outs[:4]
    ua_ref[0] = u[:, 0:2 * c]
    ub_ref[0] = u[:, 2 * c:5 * c]
    uc_ref[0] = u[:, 5 * c:6 * c].astype(BF16)
    q = u[:, 6 * c:7 * c]
    k = u[:, 7 * c:7 * c + c // 2]
    v = u[:, 7 * c + c // 2:8 * c]
    uq_ref[0] = q
    if rope:
        uqr_ref, ukv_ref = outs[4:]
        cs = cos_ref[...]
        sn = sin_ref[...]
        uqr_ref[0] = q * cs + _swap_halves(q) * sn
        kr = k * cs[:, :c // 2] + _swap_halves(k) * sn[:, :c // 2]
        ukv_ref[0] = jnp.concatenate([kr, v], axis=1)
    else:
        outs[4][0] = u[:, 7 * c:8 * c]


def in_projection(x, mods, mod_row0, norm_g, w_in_bf, *, res=None, rope=None, tm):
    b, l, d = x.shape
    c = MIX_CH
    grid = (b, l // tm)
    row = (lambda bb: 0) if mod_row0 is None else (lambda bb: mod_row0 + bb)
    xspec = pl.BlockSpec((1, tm, d), lambda bb, i: (bb, i, 0))
    mspec = pl.BlockSpec((1, N_MOD, d), lambda bb, i: (row(bb), 0, 0))
    args, specs = [x], [xspec]
    if res is not None:
        args += [res[0], res[1]]
        specs += [xspec, mspec]
    args += [mods, norm_g.reshape(1, d), w_in_bf]
    specs += [mspec, pl.BlockSpec((1, d), lambda bb, i: (0, 0)),
              pl.BlockSpec(w_in_bf.shape, lambda bb, i: (0, 0))]
    if rope is not None:
        args += [rope[0], rope[1]]
        specs += [pl.BlockSpec((tm, c), lambda bb, i: (i, 0))] * 2

    def ospec(w):
        return pl.BlockSpec((1, tm, w), lambda bb, i: (bb, i, 0))

    out_shape, out_specs = [], []
    if res is not None:
        out_shape.append(jax.ShapeDtypeStruct((b, l, d), F32))
        out_specs.append(xspec)
    widths = [(2 * c, F32), (3 * c, F32), (c, BF16), (c, F32)] + ([(c, F32)] if rope is not None else []) + [(c, F32)]
    for w, dt in widths:
        out_shape.append(jax.ShapeDtypeStruct((b, l, w), dt))
        out_specs.append(ospec(w))
    return pl.pallas_call(
        functools.partial(_in_body, fuse_res=res is not None, rope=rope is not None),
        grid=grid, in_specs=specs, out_specs=out_specs, out_shape=out_shape,
        compiler_params=_params("arbitrary", "arbitrary"),
        name="in_proj",
    )(*args)


def _dw_tile(win, w_ref, n_taps, first, rows):
    acc = w_ref[0:1, :] * win[first:first + rows]
    for k in range(1, n_taps):
        acc = acc + w_ref[k:k + 1, :] * win[first + k:first + k + rows]
    return acc


CONV_PAD = 16
CONV_ROWS = 64


def _conf_body(u_ref, w_ref, b_ref, g_ref, beta_ref, o_ref, gp_ref, *, seq, n_taps):
    c = MIX_CH
    r = CONV_ROWS
    half = (n_taps - 1) // 2
    zero = jnp.zeros((CONV_PAD, c), F32)
    gp_ref[0:CONV_PAD, :] = zero
    gp_ref[CONV_PAD + seq:CONV_PAD + seq + CONV_PAD, :] = zero

    def fill(i, carry):
        r0 = pl.multiple_of(i * r, r)
        a = u_ref[0, pl.ds(r0, r), 0:c]
        g = u_ref[0, pl.ds(r0, r), c:2 * c]
        gp_ref[pl.ds(CONV_PAD + r0, r), :] = a * jax.nn.sigmoid(g)
        return carry

    lax.fori_loop(0, seq // r, fill, 0)

    def tile(i, carry):
        r0 = pl.multiple_of(i * r, r)
        win = gp_ref[pl.ds(r0, r + 2 * CONV_PAD), :]
        z = _dw_tile(win, w_ref, n_taps, CONV_PAD - half, r) + b_ref[...]
        mu = jnp.mean(z, axis=-1, keepdims=True)
        zc = z - mu
        var = jnp.mean(zc * zc, axis=-1, keepdims=True)
        zn = zc * lax.rsqrt(var + EPS) * g_ref[...] + beta_ref[...]
        o_ref[0, pl.ds(r0, r), :] = _silu(zn)
        return carry

    lax.fori_loop(0, seq // r, tile, 0)


def conformer_conv(ua, dw_w, dw_b, ln_g, ln_b):
    b, l, c2 = ua.shape
    c = MIX_CH
    k = dw_w.shape[0]
    vec = pl.BlockSpec((1, c), lambda bb: (0, 0))
    return pl.pallas_call(
        functools.partial(_conf_body, seq=l, n_taps=k),
        grid=(b,),
        in_specs=[pl.BlockSpec((1, l, c2), lambda bb: (bb, 0, 0)),
                  pl.BlockSpec((k, c), lambda bb: (0, 0)), vec, vec, vec],
        out_specs=pl.BlockSpec((1, l, c), lambda bb: (bb, 0, 0)),
        out_shape=jax.ShapeDtypeStruct((b, l, c), F32),
        scratch_shapes=[pltpu.VMEM((l + 2 * CONV_PAD, c), F32)],
        compiler_params=_params("arbitrary"),
        name="conformer",
    )(ua, dw_w, dw_b.reshape(1, c), ln_g.reshape(1, c), ln_b.reshape(1, c))


SHORT_PAD = 8


def _short_body(u_ref, w_ref, b_ref, o_ref, xp_ref, *, seq, n_taps):
    c = MIX_CH
    r = CONV_ROWS
    half = (n_taps - 1) // 2
    zero = jnp.zeros((SHORT_PAD, c), F32)
    xp_ref[0:SHORT_PAD, :] = zero
    xp_ref[SHORT_PAD + seq:SHORT_PAD + seq + SHORT_PAD, :] = zero

    def fill(i, carry):
        r0 = pl.multiple_of(i * r, r)
        xp_ref[pl.ds(SHORT_PAD + r0, r), :] = u_ref[0, pl.ds(r0, r), :]
        return carry

    lax.fori_loop(0, seq // r, fill, 0)

    def tile(i, carry):
        r0 = pl.multiple_of(i * r, r)
        win = xp_ref[pl.ds(r0, r + 2 * SHORT_PAD), :]
        o_ref[0, pl.ds(r0, r), :] = _dw_tile(win, w_ref, n_taps, SHORT_PAD - half, r) + b_ref[...]
        return carry

    lax.fori_loop(0, seq // r, tile, 0)


def hyena_short_conv(ub, short_w, short_b):
    b, l, c3 = ub.shape
    c = MIX_CH
    k = short_w.shape[0]
    return pl.pallas_call(
        functools.partial(_short_body, seq=l, n_taps=k),
        grid=(b, c3 // c),
        in_specs=[pl.BlockSpec((1, l, c), lambda bb, j: (bb, 0, j)),
                  pl.BlockSpec((k, c), lambda bb, j: (0, j)),
                  pl.BlockSpec((1, c), lambda bb, j: (0, j))],
        out_specs=pl.BlockSpec((1, l, c), lambda bb, j: (bb, 0, j)),
        out_shape=jax.ShapeDtypeStruct((b, l, c3), F32),
        scratch_shapes=[pltpu.VMEM((l + 2 * SHORT_PAD, c), F32)],
        compiler_params=_params("arbitrary", "arbitrary"),
        name="hyena_short",
    )(ub, short_w, short_b.reshape(1, c3))


def _fft_sizes(seq):
    n = 2 * seq
    n2 = 128 if n >= 4096 else 32
    return n, n // n2, n2


def _stage1_tables(n, n1, n2, h):
    k1 = np.arange(n1)[None, :, None]
    i1 = np.arange(h)[None, None, :]
    i2 = np.arange(n2)[:, None, None]
    th = 2.0 * np.pi * ((k1 * (i1 * n2 + i2)) % n) / n
    cs, sn = np.cos(th), np.sin(th)
    return np.concatenate([np.concatenate([cs, sn], 2), np.concatenate([-sn, cs], 2)], 1).astype(np.float32)


def _stage3_tables(n, n1, n2, h):
    o1 = np.arange(h)[None, :, None]
    k1 = np.arange(n1)[None, None, :]
    i2 = np.arange(n2)[:, None, None]
    th = 2.0 * np.pi * ((k1 * (o1 * n2 + i2)) % n) / n
    cs, sn = np.cos(th) / n, np.sin(th) / n
    return np.concatenate([np.concatenate([cs, -sn], 2), np.concatenate([sn, cs], 2)], 1).astype(np.float32)


def _mid_tables(n2):
    a = np.arange(n2)
    th = 2.0 * np.pi * ((a[:, None] * a[None, :]) % n2) / n2
    cs, sn = np.cos(th), np.sin(th)
    fwd = np.concatenate([np.concatenate([cs, sn], 1), np.concatenate([-sn, cs], 1)], 0)
    inv = np.concatenate([np.concatenate([cs, -sn], 1), np.concatenate([sn, cs], 1)], 0)
    return fwd.astype(np.float32), inv.astype(np.float32)


def _dot(a, b, precise):
    if precise:
        return jnp.dot(a, b, preferred_element_type=F32, precision=lax.Precision.HIGHEST)
    return jnp.dot(a.astype(BF16), b.astype(BF16), preferred_element_type=F32)


def _stage1_body(zr_ref, zi_ref, t_ref, o_ref, *, n1, n2, h, precise):
    def step(i2, carry):
        xr = zr_ref[0, pl.ds(i2, h, stride=n2), :]
        xi = zi_ref[0, pl.ds(i2, h, stride=n2), :]
        a = _dot(t_ref[i2], jnp.concatenate([xr, xi], axis=0), precise)
        o_ref[0, 0, pl.ds(i2, n1, stride=n2), :] = a[:n1]
        o_ref[0, 1, pl.ds(i2, n1, stride=n2), :] = a[n1:]
        return carry

    lax.fori_loop(0, n2, step, 0)


def fft_stage1(zsrc, col_block, n_pairs, imag_offset, table, *, n1, n2, h, precise=False):
    rows = h * n2
    c = MIX_CH
    nc = c // LANES
    return pl.pallas_call(
        functools.partial(_stage1_body, n1=n1, n2=n2, h=h, precise=precise),
        grid=(n_pairs, nc),
        in_specs=[pl.BlockSpec((1, rows, LANES), lambda p, j: (p, 0, col_block * nc + j)),
                  pl.BlockSpec((1, rows, LANES), lambda p, j: (p + imag_offset, 0, col_block * nc + j)),
                  pl.BlockSpec(table.shape, lambda p, j: (0, 0, 0))],
        out_specs=pl.BlockSpec((1, 2, n1 * n2, LANES), lambda p, j: (p, 0, 0, j)),
        out_shape=jax.ShapeDtypeStruct((n_pairs, 2, n1 * n2, c), F32),
        compiler_params=_params("arbitrary", "arbitrary"),
        name="fft_stage1",
    )(zsrc, zsrc, table)


def _mid_body(a_ref, k_ref, f_ref, g_ref, o_ref, *, n2, kc):
    for j in range(kc):
        rows = slice(j * n2, (j + 1) * n2)
        blk = jnp.concatenate([a_ref[0, 0, rows, :], a_ref[0, 1, rows, :]], axis=0)
        s = _dot(f_ref[...], blk, False)
        sr, si = s[:n2], s[n2:]
        kr, ki = k_ref[0, rows, :], k_ref[1, rows, :]
        y = jnp.concatenate([sr * kr - si * ki, sr * ki + si * kr], axis=0)
        bb = _dot(g_ref[...], y, False)
        o_ref[0, 0, rows, :] = bb[:n2]
        o_ref[0, 1, rows, :] = bb[n2:]


def fft_mid(a, kf, order, f_fwd, f_inv, *, n1, n2):
    p = a.shape[0]
    c = MIX_CH
    kc = min(n1, 8)
    rows = kc * n2
    blk = pl.BlockSpec((1, 2, rows, c), lambda pp, j: (pp, 0, j, 0))
    mat = pl.BlockSpec(f_fwd.shape, lambda pp, j: (0, 0))
    return pl.pallas_call(
        functools.partial(_mid_body, n2=n2, kc=kc),
        grid=(p, n1 // kc),
        in_specs=[blk, pl.BlockSpec((2, rows, c), lambda pp, j: (0, j, order)), mat, mat],
        out_specs=blk,
        out_shape=jax.ShapeDtypeStruct(a.shape, F32),
        compiler_params=_params("arbitrary", "arbitrary"),
        name="fft_mid",
    )(a, kf, f_fwd, f_inv)


def _mid_fwd_body(a_ref, f_ref, o_ref, *, n2, kc):
    for j in range(kc):
        rows = slice(j * n2, (j + 1) * n2)
        blk = jnp.concatenate([a_ref[0, 0, rows, :], a_ref[0, 1, rows, :]], axis=0)
        s = _dot(f_ref[...], blk, True)
        o_ref[0, 0, rows, :] = s[:n2]
        o_ref[0, 1, rows, :] = s[n2:]


def fft_mid_forward(a, f_fwd, *, n1, n2):
    p = a.shape[0]
    c = MIX_CH
    kc = min(n1, 8)
    rows = kc * n2
    blk = pl.BlockSpec((1, 2, rows, c), lambda pp, j: (pp, 0, j, 0))
    return pl.pallas_call(
        functools.partial(_mid_fwd_body, n2=n2, kc=kc),
        grid=(p, n1 // kc),
        in_specs=[blk, pl.BlockSpec(f_fwd.shape, lambda pp, j: (0, 0))],
        out_specs=blk,
        out_shape=jax.ShapeDtypeStruct(a.shape, F32),
        compiler_params=_params("arbitrary", "arbitrary"),
        name="fft_mid_fwd",
    )(a, f_fwd)


def _stage3_body(b_ref, t_ref, zr_ref, zi_ref, gr_ref, gi_ref, d_ref, or_ref, oi_ref, *, n1, n2, h):
    def step(i2, carry):
        br = b_ref[0, 0, pl.ds(i2, n1, stride=n2), :]
        bi = b_ref[0, 1, pl.ds(i2, n1, stride=n2), :]
        y = _dot(t_ref[i2], jnp.concatenate([br, bi], axis=0), False)
        rows = pl.ds(i2, h, stride=n2)
        d = d_ref[...]
        or_ref[0, rows, :] = gr_ref[0, rows, :] * (y[:h] + d * zr_ref[0, rows, :])
        oi_ref[0, rows, :] = gi_ref[0, rows, :] * (y[h:] + d * zi_ref[0, rows, :])
        return carry

    lax.fori_loop(0, n2, step, 0)


def fft_stage3(bw, table, zsrc, z_col, gsrc, g_col, d_vec, imag_offset, *, n1, n2, h):
    p = bw.shape[0]
    c = MIX_CH
    nc = c // LANES
    rows = h * n2

    def src(col, off):
        return pl.BlockSpec((1, rows, LANES), lambda pp, j: (pp + off, 0, col * nc + j))

    ospec = pl.BlockSpec((1, rows, LANES), lambda pp, j: (pp, 0, j))
    return pl.pallas_call(
        functools.partial(_stage3_body, n1=n1, n2=n2, h=h),
        grid=(p, nc),
        in_specs=[pl.BlockSpec((1, 2, n1 * n2, LANES), lambda pp, j: (pp, 0, 0, j)),
                  pl.BlockSpec(table.shape, lambda pp, j: (0, 0, 0)),
                  src(z_col, 0), src(z_col, imag_offset), src(g_col, 0), src(g_col, imag_offset),
                  pl.BlockSpec((1, LANES), lambda pp, j: (0, j))],
        out_specs=[ospec, ospec],
        out_shape=[jax.ShapeDtypeStruct((p, rows, c), F32)] * 2,
        compiler_params=_params("arbitrary", "arbitrary"),
        name="fft_stage3",
    )(bw, table, zsrc, zsrc, gsrc, gsrc, d_vec.reshape(1, c))


def hyena_filter_time(seq, fw1, fb1, fr1, fw2, fb2, fr2, fw3, fb3):
    hp = lax.Precision.HIGHEST
    t = jnp.linspace(0.0, 1.0, seq, dtype=F32)[:, None]
    n_bands = (fw1.shape[0] - 1) // 2
    w = 2.0 * math.pi * jnp.arange(seq, dtype=F32)[:, None] / seq
    fr = jnp.linspace(1e-4, n_bands - 1, n_bands, dtype=F32)[None, :]
    z = jnp.concatenate([t, jnp.cos(fr * w), -jnp.sin(fr * w)], axis=-1)
    h = jnp.sin(fr1 * (jnp.dot(z, fw1, precision=hp) + fb1))
    h = jnp.sin(fr2 * (jnp.dot(h, fw2, precision=hp) + fb2))
    c = MIX_CH
    h = (jnp.dot(h, fw3, precision=hp) + fb3).reshape(seq, 2, 2, c)
    max_decay = math.log(HYENA_DECAY_TARGET) / HYENA_FAST_PCT
    min_decay = math.log(HYENA_DECAY_TARGET) / HYENA_SLOW_PCT
    deltas = jnp.linspace(min_decay, max_decay, c, dtype=F32)
    window = jnp.exp(-t * jnp.abs(deltas)[None, :]) + HYENA_SHIFT
    h = h * window[:, None, None, :]
    hf, hb = h[:, 0], h[:, 1]
    k = jnp.concatenate([hf[:1] + hb[:1], hf[1:], jnp.zeros_like(hf[:1]), hb[:0:-1]], axis=0)
    k = k / (jnp.sum(jnp.abs(k), axis=0, keepdims=True) + EPS)
    return k.reshape(2 * seq, 2 * c)


def hyena_filter_spectrum(seq, filt):
    n, n1, n2 = _fft_sizes(seq)
    k = hyena_filter_time(seq, *filt)
    src = jnp.stack([k, jnp.zeros_like(k)], axis=0)
    t1 = jnp.asarray(_stage1_tables(n, n1, n2, n1))
    f_fwd, _ = _mid_tables(n2)
    planes = []
    for o in range(2):
        a = fft_stage1(src, o, 1, 1, t1, n1=n1, n2=n2, h=n1, precise=True)
        planes.append(fft_mid_forward(a, jnp.asarray(f_fwd), n1=n1, n2=n2)[0])
    return jnp.concatenate(planes, axis=-1)


def hyena_mixer(ub, short_w, short_b, hy_d, kf):
    b, l, _ = ub.shape
    n, n1, n2 = _fft_sizes(l)
    h = n1 // 2
    p = b // 2
    u = hyena_short_conv(ub, short_w, short_b)
    t1 = jnp.asarray(_stage1_tables(n, n1, n2, h)).astype(BF16)
    t3 = jnp.asarray(_stage3_tables(n, n1, n2, h)).astype(BF16)
    f_fwd, f_inv = _mid_tables(n2)
    f_fwd = jnp.asarray(f_fwd).astype(BF16)
    f_inv = jnp.asarray(f_inv).astype(BF16)
    a = fft_stage1(u, 0, p, p, t1, n1=n1, n2=n2, h=h)
    bw = fft_mid(a, kf, 0, f_fwd, f_inv, n1=n1, n2=n2)
    zr, zi = fft_stage3(bw, t3, u, 0, u, 1, hy_d[0], p, n1=n1, n2=n2, h=h)
    z = jnp.concatenate([zr, zi], axis=0)
    a = fft_stage1(z, 0, p, p, t1, n1=n1, n2=n2, h=h)
    bw = fft_mid(a, kf, 1, f_fwd, f_inv, n1=n1, n2=n2)
    zr, zi = fft_stage3(bw, t3, z, 0, u, 2, hy_d[1], p, n1=n1, n2=n2, h=h)
    return jnp.concatenate([zr, zi], axis=0)


def _fnet_body(cl_ref, sl_ref, x_ref, cc_ref, sc_ref, o_ref, *, scale):
    x = x_ref[0]
    pr = jnp.dot(cl_ref[...], x, preferred_element_type=F32).astype(BF16)
    qr = jnp.dot(sl_ref[...], x, preferred_element_type=F32).astype(BF16)
    o_ref[0] = (jnp.dot(pr, cc_ref[...], preferred_element_type=F32)
                - jnp.dot(qr, sc_ref[...], preferred_element_type=F32)) * scale


def _dft_tables(n):
    a = jnp.arange(n, dtype=jnp.int32)
    th = ((a[:, None] * a[None, :]) % n).astype(F32) * (2.0 * math.pi / n)
    return jnp.cos(th).astype(BF16), jnp.sin(th).astype(BF16)


def fnet_mixer(uc, tables):
    b, l, c = uc.shape
    cl, sl, cc, sc = tables
    tm = min(l, 512)
    row = pl.BlockSpec((tm, l), lambda i, bb: (i, 0))
    sq = pl.BlockSpec((c, c), lambda i, bb: (0, 0))
    return pl.pallas_call(
        functools.partial(_fnet_body, scale=1.0 / math.sqrt(l * c)),
        grid=(l // tm, b),
        in_specs=[row, row, pl.BlockSpec((1, l, c), lambda i, bb: (bb, 0, 0)), sq, sq],
        out_specs=pl.BlockSpec((1, tm, c), lambda i, bb: (bb, i, 0)),
        out_shape=jax.ShapeDtypeStruct((b, l, c), F32),
        compiler_params=_params("arbitrary", "arbitrary"),
        name="fnet",
    )(cl, sl, uc, cc, sc)


def _heads_rows(x, g):
    h0 = Q_PER_KV * g
    return jnp.concatenate([x[:, (h0 + r) * HEAD_DIM:(h0 + r + 1) * HEAD_DIM] for r in range(Q_PER_KV)], axis=0)


def _qk(q, k):
    return lax.dot_general(q.astype(BF16), k.astype(BF16), (((1,), (1,)), ((), ())),
                           preferred_element_type=F32)


def _sink_col(sink_ref, g, rows):
    ridx = lax.broadcasted_iota(jnp.int32, (Q_PER_KV * rows, 1), 0)
    col = jnp.full((Q_PER_KV * rows, 1), sink_ref[Q_PER_KV * g], F32)
    for r in range(1, Q_PER_KV):
        col = jnp.where(ridx >= r * rows, sink_ref[Q_PER_KV * g + r], col)
    return col


def _lat_attn_body(sink_ref, q_ref, qr_ref, kp_ref, kc_ref, kn_ref, ck_ref, cv_ref, o_ref):
    i = pl.program_id(1)
    nblk = pl.num_programs(1)
    blk = ATT_BLOCK
    span = blk + 2 * WINDOW
    scale = HEAD_DIM ** -0.5
    q = q_ref[0]
    qr = qr_ref[0]
    kv = jnp.concatenate([kp_ref[0], kc_ref[0], kn_ref[0]], axis=0)
    ck = ck_ref[0, 0]
    cv = cv_ref[0, 0]
    kvw = N_KV_HEADS * HEAD_DIM
    r = lax.broadcasted_iota(jnp.int32, (Q_PER_KV * blk, span), 0) % blk
    j = lax.broadcasted_iota(jnp.int32, (Q_PER_KV * blk, span), 1)
    ok = (j >= r) & (j <= r + 2 * WINDOW)
    ok = ok & ((i > 0) | (j >= WINDOW)) & ((i < nblk - 1) | (j < WINDOW + blk))
    outs = []
    for g in range(N_KV_HEADS):
        kl = kv[:, g * HEAD_DIM:(g + 1) * HEAD_DIM]
        vl = kv[:, kvw + g * HEAD_DIM:kvw + (g + 1) * HEAD_DIM]
        s_loc = jnp.where(ok, _qk(_heads_rows(qr, g), kl) * scale, NEG_INF)
        s_ctx = _qk(_heads_rows(q, g), ck[:, g * HEAD_DIM:(g + 1) * HEAD_DIM]) * scale
        sink = _sink_col(sink_ref, g, blk)
        m = jnp.maximum(jnp.maximum(jnp.max(s_loc, axis=-1, keepdims=True),
                                    jnp.max(s_ctx, axis=-1, keepdims=True)), sink)
        e_loc = jnp.exp(s_loc - m)
        e_ctx = jnp.exp(s_ctx - m)
        den = jnp.sum(e_loc, axis=-1, keepdims=True) + jnp.sum(e_ctx, axis=-1, keepdims=True) + jnp.exp(sink - m)
        inv = 1.0 / den
        o = (jnp.dot((e_loc * inv).astype(BF16), vl.astype(BF16), preferred_element_type=F32)
             + jnp.dot((e_ctx * inv).astype(BF16), cv[:, g * HEAD_DIM:(g + 1) * HEAD_DIM].astype(BF16),
                       preferred_element_type=F32))
        outs += [o[rr * blk:(rr + 1) * blk] for rr in range(Q_PER_KV)]
    o_ref[0] = jnp.concatenate(outs, axis=1)


def latent_attention(uq, uqr, ukv, cache_k, cache_v, layer, sink):
    b, l, c = uq.shape
    p = cache_k.shape[2]
    blk = ATT_BLOCK
    nblk = l // blk
    qspec = pl.BlockSpec((1, blk, c), lambda bb, i: (bb, i, 0))
    cspec = pl.BlockSpec((1, 1, p, cache_k.shape[3]), lambda bb, i: (bb, layer, 0, 0))
    return pl.pallas_call(
        _lat_attn_body,
        grid=(b, nblk),
        in_specs=[pl.BlockSpec(memory_space=pltpu.SMEM), qspec, qspec,
                  pl.BlockSpec((1, blk, c), lambda bb, i: (bb, jnp.maximum(i - 1, 0), 0)),
                  qspec,
                  pl.BlockSpec((1, blk, c), lambda bb, i: (bb, jnp.minimum(i + 1, nblk - 1), 0)),
                  cspec, cspec],
        out_specs=qspec,
        out_shape=jax.ShapeDtypeStruct((b, l, c), F32),
        compiler_params=_params("arbitrary", "arbitrary"),
        name="latent_attention",
    )(sink, uq, uqr, ukv, ukv, ukv, cache_k, cache_v)


def _ctx_attn_body(sink_ref, q_ref, kv_ref, o_ref, *, seq):
    scale = HEAD_DIM ** -0.5
    q = q_ref[0]
    kv = kv_ref[0]
    kvw = N_KV_HEADS * HEAD_DIM
    outs = []
    for g in range(N_KV_HEADS):
        kl = kv[:, g * HEAD_DIM:(g + 1) * HEAD_DIM]
        vl = kv[:, kvw + g * HEAD_DIM:kvw + (g + 1) * HEAD_DIM]
        s = _qk(_heads_rows(q, g), kl) * scale
        sink = _sink_col(sink_ref, g, seq)
        m = jnp.maximum(jnp.max(s, axis=-1, keepdims=True), sink)
        e = jnp.exp(s - m)
        den = jnp.sum(e, axis=-1, keepdims=True) + jnp.exp(sink - m)
        o = jnp.dot((e * (1.0 / den)).astype(BF16), vl.astype(BF16), preferred_element_type=F32)
        outs += [o[rr * seq:(rr + 1) * seq] for rr in range(Q_PER_KV)]
    o_ref[0] = jnp.concatenate(outs, axis=1)


def context_attention(uq, ukv, sink):
    b, s, c = uq.shape
    spec = pl.BlockSpec((1, s, c), lambda bb: (bb, 0, 0))
    return pl.pallas_call(
        functools.partial(_ctx_attn_body, seq=s),
        grid=(b,),
        in_specs=[pl.BlockSpec(memory_space=pltpu.SMEM), spec, spec],
        out_specs=spec,
        out_shape=jax.ShapeDtypeStruct((b, s, c), F32),
        compiler_params=_params("arbitrary"),
        name="context_attention",
    )(sink, uq, ukv)


ROUTE_LANES = 128


def _out_body(ya_ref, yb_ref, yc_ref, yd_ref, x_ref, mod_ref, g_ref, w_ref, rw_ref, rb_ref,
              x1_ref, h_ref, route_ref):
    c = MIX_CH
    y = jnp.dot(ya_ref[0].astype(BF16), w_ref[0:c, :], preferred_element_type=F32)
    for j, ref in enumerate((yb_ref, yc_ref, yd_ref), start=1):
        y = y + jnp.dot(ref[0].astype(BF16), w_ref[j * c:(j + 1) * c, :], preferred_element_type=F32)
    x1 = x_ref[0] + mod_ref[0, 2:3, :] * y
    x1_ref[0] = x1
    h = _rmsnorm_mod(x1, g_ref[...], mod_ref[0, 4:5, :], mod_ref[0, 3:4, :])
    h_ref[0] = h.astype(BF16)
    logits = jnp.dot(h, rw_ref[...], preferred_element_type=F32, precision=lax.Precision.HIGHEST) + rb_ref[...]
    lane = lax.broadcasted_iota(jnp.int32, logits.shape, 1)
    is_c = lane < N_GROUPS
    lc = jnp.where(is_c, logits, NEG_INF)
    mc = jnp.max(lc, axis=-1, keepdims=True)
    grp = jnp.min(jnp.where(lc == mc, lane, ROUTE_LANES), axis=-1, keepdims=True)
    pg = 1.0 / jnp.sum(jnp.where(is_c, jnp.exp(lc - mc), 0.0), axis=-1, keepdims=True)
    lo = N_GROUPS + grp * EXPERTS_PER_GROUP
    in_g = (lane >= lo) & (lane < lo + EXPERTS_PER_GROUP)
    lf = jnp.where(in_g, logits, NEG_INF)
    t1 = jnp.max(lf, axis=-1, keepdims=True)
    i1 = jnp.min(jnp.where(lf == t1, lane, ROUTE_LANES), axis=-1, keepdims=True)
    lf2 = jnp.where(lane == i1, NEG_INF, lf)
    t2 = jnp.max(lf2, axis=-1, keepdims=True)
    i2 = jnp.min(jnp.where(lf2 == t2, lane, ROUTE_LANES), axis=-1, keepdims=True)
    e2 = jnp.exp(t2 - t1)
    w1 = pg / (1.0 + e2)
    w2 = pg * e2 / (1.0 + e2)
    rec = jnp.where(lane == 0, (i1 - N_GROUPS).astype(F32),
                    jnp.where(lane == 1, (i2 - N_GROUPS).astype(F32),
                              jnp.where(lane == 2, w1, jnp.where(lane == 3, w2, 0.0))))
    route_ref[0] = rec


def out_projection(ys, x, mods, mod_row0, norm_g, w_out_bf, rw, rb, *, tm):
    b, l, d = x.shape
    c = MIX_CH
    row = (lambda bb: 0) if mod_row0 is None else (lambda bb: mod_row0 + bb)
    yspec = pl.BlockSpec((1, tm, c), lambda bb, i: (bb, i, 0))
    xspec = pl.BlockSpec((1, tm, d), lambda bb, i: (bb, i, 0))
    return pl.pallas_call(
        _out_body,
        grid=(b, l // tm),
        in_specs=[yspec] * 4 + [xspec,
                                pl.BlockSpec((1, N_MOD, d), lambda bb, i: (row(bb), 0, 0)),
                                pl.BlockSpec((1, d), lambda bb, i: (0, 0)),
                                pl.BlockSpec(w_out_bf.shape, lambda bb, i: (0, 0)),
                                pl.BlockSpec(rw.shape, lambda bb, i: (0, 0)),
                                pl.BlockSpec(rb.shape, lambda bb, i: (0, 0))],
        out_specs=[xspec, xspec, pl.BlockSpec((1, tm, ROUTE_LANES), lambda bb, i: (bb, i, 0))],
        out_shape=[jax.ShapeDtypeStruct((b, l, d), F32), jax.ShapeDtypeStruct((b, l, d), BF16),
                   jax.ShapeDtypeStruct((b, l, ROUTE_LANES), F32)],
        compiler_params=_params("arbitrary", "arbitrary"),
        name="out_proj",
    )(*ys, x, mods, norm_g.reshape(1, d), w_out_bf, rw, rb)


def _expert_body(be_ref, nu_ref, xs_ref, sw_ref, wg_ref, wu_ref, wd_ref, o_ref, wg_s, wu_s, wd_s):
    i = pl.program_id(0)
    prev = be_ref[jnp.maximum(i - 1, 0)]

    @pl.when((i == 0) | (be_ref[i] != prev))
    def _():
        wg_s[...] = wg_ref[0].astype(BF16)
        wu_s[...] = wu_ref[0].astype(BF16)
        wd_s[...] = wd_ref[0].astype(BF16)

    @pl.when(i < nu_ref[0])
    def _():
        x = xs_ref[...]
        g = jnp.dot(x, wg_s[...], preferred_element_type=F32)
        u = jnp.dot(x, wu_s[...], preferred_element_type=F32)
        a = (_silu(g) * u).astype(BF16)
        o_ref[...] = jnp.dot(a, wd_s[...], preferred_element_type=F32) * sw_ref[...]

    @pl.when(i >= nu_ref[0])
    def _():
        o_ref[...] = jnp.zeros(o_ref.shape, F32)


def expert_ffn(xs, slot_w, blk_e, n_used, e_gate, e_up, e_down):
    rows, d = xs.shape
    nb = rows // MOE_BLOCK
    de = e_gate.shape[-1]
    grid_spec = pltpu.PrefetchScalarGridSpec(
        num_scalar_prefetch=2,
        grid=(nb,),
        in_specs=[pl.BlockSpec((MOE_BLOCK, d), lambda i, be, nu: (i, 0)),
                  pl.BlockSpec((MOE_BLOCK, 1), lambda i, be, nu: (i, 0)),
                  pl.BlockSpec((1, d, de), lambda i, be, nu: (be[i], 0, 0)),
                  pl.BlockSpec((1, d, de), lambda i, be, nu: (be[i], 0, 0)),
                  pl.BlockSpec((1, de, d), lambda i, be, nu: (be[i], 0, 0))],
        out_specs=pl.BlockSpec((MOE_BLOCK, d), lambda i, be, nu: (i, 0)),
        scratch_shapes=[pltpu.VMEM((d, de), BF16), pltpu.VMEM((d, de), BF16), pltpu.VMEM((de, d), BF16)],
    )
    return pl.pallas_call(
        _expert_body, grid_spec=grid_spec,
        out_shape=jax.ShapeDtypeStruct((rows, d), F32),
        compiler_params=_params("arbitrary"),
        name="expert_ffn",
    )(blk_e, n_used, xs, slot_w.reshape(rows, 1), e_gate, e_up, e_down)


def moe_dispatch_plan(route):
    n = route.shape[0]
    eid = route[:, 0:2].astype(jnp.int32)
    gw = route[:, 2:4]
    a = n * 2
    flat_e = eid.reshape(-1)
    flat_t = jnp.repeat(jnp.arange(n, dtype=jnp.int32), 2)
    flat_w = gw.reshape(-1)
    order = jnp.argsort(flat_e)
    se = flat_e[order]
    counts = jnp.bincount(flat_e, length=N_EXPERTS)
    padded = (counts + MOE_BLOCK - 1) // MOE_BLOCK * MOE_BLOCK
    pend = jnp.cumsum(padded)
    pstart = pend - padded
    start = jnp.cumsum(counts) - counts
    dest = pstart[se] + jnp.arange(a) - start[se]
    nb = -(-a // MOE_BLOCK) + N_EXPERTS
    slot_tok = jnp.zeros((nb * MOE_BLOCK,), jnp.int32).at[dest].set(flat_t[order])
    slot_w = jnp.zeros((nb * MOE_BLOCK,), F32).at[dest].set(flat_w[order])
    blk_e = jnp.minimum(jnp.searchsorted(pend, jnp.arange(nb) * MOE_BLOCK, side='right'), N_EXPERTS - 1)
    n_used = (pend[-1] // MOE_BLOCK).astype(jnp.int32).reshape(1)
    return slot_tok, slot_w, blk_e.astype(jnp.int32), n_used


def hier_moe(h_bf, route, e_gate, e_up, e_down):
    b, l, d = h_bf.shape
    t = h_bf.reshape(b * l, d)
    slot_tok, slot_w, blk_e, n_used = moe_dispatch_plan(route.reshape(b * l, ROUTE_LANES))
    xs = t[slot_tok]
    y = expert_ffn(xs, slot_w, blk_e, n_used, e_gate, e_up, e_down)
    out = jnp.zeros((b * l, d), F32).at[slot_tok].add(y)
    return out.reshape(b, l, d)


def _final_body(x_ref, moe_ref, mod_ref, g_ref, o_ref):
    x = x_ref[0] + mod_ref[0, 5:6, :] * moe_ref[0]
    ms = jnp.mean(x * x, axis=-1, keepdims=True)
    o_ref[0] = x * lax.rsqrt(ms + EPS) * g_ref[...]


def final_norm(x1, moe, mods, mod_row0, norm_g, *, tm):
    b, l, d = x1.shape
    row = (lambda bb: 0) if mod_row0 is None else (lambda bb: mod_row0 + bb)
    xspec = pl.BlockSpec((1, tm, d), lambda bb, i: (bb, i, 0))
    return pl.pallas_call(
        _final_body,
        grid=(b, l // tm),
        in_specs=[xspec, xspec, pl.BlockSpec((1, N_MOD, d), lambda bb, i: (row(bb), 0, 0)),
                  pl.BlockSpec((1, d), lambda bb, i: (0, 0))],
        out_specs=xspec,
        out_shape=jax.ShapeDtypeStruct((b, l, d), F32),
        compiler_params=_params("arbitrary", "arbitrary"),
        name="final_norm",
    )(x1, moe, mods, norm_g.reshape(1, d))


def _rope_tables(seq):
    rows = seq // GRID_W
    row_pos = jnp.repeat(jnp.arange(rows, dtype=F32), GRID_W)
    col_pos = jnp.tile(jnp.arange(GRID_W, dtype=F32), rows)
    n_freq = HEAD_DIM // 4
    inv = ROPE_BASE ** (-jnp.arange(n_freq, dtype=F32) / n_freq)
    ang = jnp.concatenate([row_pos[:, None] * inv, col_pos[:, None] * inv], axis=-1)
    cs, sn = jnp.cos(ang), jnp.sin(ang)
    cos_f = jnp.tile(jnp.concatenate([cs, cs], axis=-1), (1, N_Q_HEADS))
    sin_s = jnp.tile(jnp.concatenate([-sn, sn], axis=-1), (1, N_Q_HEADS))
    return cos_f, sin_s


def kernel(x_prompt, x_sample, cache_k, cache_v, c, c_ctx, ada_w, ada_b, norm1_g, norm2_g, w_in, conv_dw_w, conv_dw_b, conv_ln_g, conv_ln_b, hy_short_w, hy_short_b, hy_fw1, hy_fb1, hy_freq1, hy_fw2, hy_fb2, hy_freq2, hy_fw3, hy_fb3, hy_d, attn_sink, w_out, router_coarse_w, router_coarse_b, router_fine_w, router_fine_b, exp_gate, exp_up, exp_down, norm_f_g):
    depth = ada_w.shape[0]
    bp, lp, d = x_prompt.shape
    bs, ls, _ = x_sample.shape
    assert bp % 2 == 0 and bs % 2 == 0 and ls % (GRID_W * ATT_BLOCK // GRID_W) == 0

    n_rows = -(-(1 + bs) // SUBLANES) * SUBLANES
    cvec = jnp.concatenate([c_ctx[None, :], c, jnp.zeros((n_rows - 1 - bs, d), F32)], axis=0)
    mods = adaln_all(cvec, ada_w, ada_b)

    rope = _rope_tables(ls)
    fnet_tabs = {}
    for seq in {lp, ls}:
        fnet_tabs[seq] = _dft_tables(seq) + _dft_tables(MIX_CH)
    ck = cache_k.reshape(cache_k.shape[0], depth, cache_k.shape[2], -1)
    cv = cache_v.reshape(cache_v.shape[0], depth, cache_v.shape[2], -1)
    pad = ROUTE_LANES - N_GROUPS - N_EXPERTS

    tm_p = min(lp, 512)
    tm_s = min(ls, 512)
    xp, xs = x_prompt, x_sample
    res_p = res_s = None
    ks_out, vs_out = [], []
    for l in range(depth):
        w_in_bf = w_in[l].astype(BF16)
        w_out_bf = w_out[l].astype(BF16)
        rw = jnp.concatenate([router_coarse_w[l], router_fine_w[l], jnp.zeros((d, pad), F32)], axis=1)
        rb = jnp.concatenate([router_coarse_b[l], router_fine_b[l], jnp.zeros((pad,), F32)])[None, :]
        filt = (hy_fw1[l], hy_fb1[l], hy_freq1[l], hy_fw2[l], hy_fb2[l], hy_freq2[l], hy_fw3[l], hy_fb3[l])
        sink = attn_sink[l]

        def mixers(ua, ub, uc, yd, seq):
            ya = conformer_conv(ua, conv_dw_w[l], conv_dw_b[l], conv_ln_g[l], conv_ln_b[l])
            yb = hyena_mixer(ub, hy_short_w[l], hy_short_b[l], hy_d[l], hyena_filter_spectrum(seq, filt))
            yc = fnet_mixer(uc, fnet_tabs[seq])
            return (ya, yb, yc, yd)

        outs = in_projection(xp, mods[l], None, norm1_g[l], w_in_bf, res=res_p, tm=tm_p)
        if res_p is not None:
            xp, outs = outs[0], outs[1:]
        ua, ub, uc, uq, ukv = outs
        kvw = N_KV_HEADS * HEAD_DIM
        ks_out.append(ukv[..., :kvw].reshape(bp, lp, N_KV_HEADS, HEAD_DIM))
        vs_out.append(ukv[..., kvw:].reshape(bp, lp, N_KV_HEADS, HEAD_DIM))
        ys = mixers(ua, ub, uc, context_attention(uq, ukv, sink), lp)
        x1p, hp, route = out_projection(ys, xp, mods[l], None, norm2_g[l], w_out_bf, rw, rb, tm=tm_p)
        res_p = (hier_moe(hp, route, exp_gate[l], exp_up[l], exp_down[l]), mods[l])
        xp = x1p

        outs = in_projection(xs, mods[l], 1, norm1_g[l], w_in_bf, res=res_s, rope=rope, tm=tm_s)
        if res_s is not None:
            xs, outs = outs[0], outs[1:]
        ua, ub, uc, uq, uqr, ukv = outs
        ys = mixers(ua, ub, uc, latent_attention(uq, uqr, ukv, ck, cv, l, sink), ls)
        x1s, hs, route = out_projection(ys, xs, mods[l], 1, norm2_g[l], w_out_bf, rw, rb, tm=tm_s)
        res_s = (hier_moe(hs, route, exp_gate[l], exp_up[l], exp_down[l]), mods[l])
        xs = x1s

    y_prompt = final_norm(xp, res_p[0], res_p[1], None, norm_f_g, tm=tm_p)
    y_sample = final_norm(xs, res_s[0], res_s[1], 1, norm_f_g, tm=tm_s)
    return (y_prompt, y_sample, jnp.stack(ks_out, axis=1), jnp.stack(vs_out, axis=1))
```

```python
import functools
import math

import numpy as np
import jax
import jax.numpy as jnp
from jax import lax
from jax.experimental import pallas as pl
from jax.experimental.pallas import tpu as pltpu
from jax.experimental.pallas import tpu_sc as plsc

F32 = jnp.float32
BF16 = jnp.bfloat16

HEAD_DIM = 64
N_Q_HEADS = 4
N_KV_HEADS = 2
Q_PER_KV = N_Q_HEADS // N_KV_HEADS
WINDOW = 128
ATT_BLOCK = 128
GRID_W = 64
ROPE_BASE = 10000.0
N_GROUPS = 4
EXPERTS_PER_GROUP = 8
N_EXPERTS = N_GROUPS * EXPERTS_PER_GROUP
MOE_BLOCK = 256
N_MOD = 6
EPS = 1e-6
NEG_INF = -1e30
HYENA_DECAY_TARGET = 1e-2
HYENA_FAST_PCT = 0.3
HYENA_SLOW_PCT = 1.5
HYENA_SHIFT = 0.05

LANES = 128
SUBLANES = 8
VMEM_LIMIT = 56 * 1024 * 1024

MIX_CH = 256
ROUTE_LANES = 128
SC_CORES = 2
SC_WORKERS = SC_CORES * 16


def _params(*sem):
    return pltpu.CompilerParams(dimension_semantics=sem, vmem_limit_bytes=VMEM_LIMIT)


def _silu(x):
    return x * jax.nn.sigmoid(x)


def _ada_body(c_ref, w_ref, b_ref, o_ref):
    s = _silu(c_ref[...]).astype(BF16)
    o_ref[0] = jnp.dot(s, w_ref[0].astype(BF16), preferred_element_type=F32) + b_ref[0]


def adaln_all(cvec, ada_w, ada_b):
    depth, d, n6 = ada_w.shape
    r = cvec.shape[0]
    tn = n6 // 4
    out = pl.pallas_call(
        _ada_body,
        grid=(depth, n6 // tn),
        in_specs=[
            pl.BlockSpec((r, d), lambda l, j: (0, 0)),
            pl.BlockSpec((1, d, tn), lambda l, j: (l, 0, j)),
            pl.BlockSpec((1, 1, tn), lambda l, j: (l, 0, j)),
        ],
        out_specs=pl.BlockSpec((1, r, tn), lambda l, j: (l, 0, j)),
        out_shape=jax.ShapeDtypeStruct((depth, r, n6), F32),
        compiler_params=_params("arbitrary", "arbitrary"),
        name="adaln",
    )(cvec, ada_w, ada_b.reshape(depth, 1, n6))
    return out.reshape(depth, r, N_MOD, d)


def _swap_halves(x):
    pieces = []
    for j in range(x.shape[1] // LANES):
        xj = x[:, j * LANES:(j + 1) * LANES]
        fwd = pltpu.roll(xj, LANES - HEAD_DIM // 2, axis=1)
        bwd = pltpu.roll(xj, HEAD_DIM // 2, axis=1)
        lane = lax.broadcasted_iota(jnp.int32, xj.shape, 1)
        pieces.append(jnp.where((lane % HEAD_DIM) < HEAD_DIM // 2, fwd, bwd))
    return pieces[0] if len(pieces) == 1 else jnp.concatenate(pieces, axis=1)


def _rmsnorm_mod(x, g, scale, shift):
    ms = jnp.mean(x * x, axis=-1, keepdims=True)
    return (x * lax.rsqrt(ms + EPS) * g) * (1.0 + scale) + shift


def _moe_residual(x1, y0_ref, y1_ref, route_ref, pmod_ref):
    moe = route_ref[0, :, 2:3] * y0_ref[0, 0] + route_ref[0, :, 3:4] * y1_ref[0, 0]
    return x1 + pmod_ref[0, 5:6, :] * moe


def _residual_specs(res, tm, d, row):
    pair, route, pmods = res
    args = [pair, pair, route, pmods]
    specs = [pl.BlockSpec((1, 1, tm, d), lambda bb, i: (0, bb, i, 0)),
             pl.BlockSpec((1, 1, tm, d), lambda bb, i: (1, bb, i, 0)),
             pl.BlockSpec((1, tm, ROUTE_LANES), lambda bb, i: (bb, i, 0)),
             pl.BlockSpec((1, N_MOD, d), lambda bb, i: (row(bb), 0, 0))]
    return args, specs


def _in_body(*refs, fuse_res, rope):
    it = iter(refs)
    x_ref = next(it)
    if fuse_res:
        res_refs = [next(it) for _ in range(4)]
    mod_ref = next(it)
    g_ref = next(it)
    w_ref = next(it)
    if rope:
        cos_ref = next(it)
        sin_ref = next(it)
    outs = list(it)
    x = x_ref[0]
    if fuse_res:
        x = _moe_residual(x, *res_refs)
        outs.pop(0)[0] = x
    h = _rmsnorm_mod(x, g_ref[...], mod_ref[0, 1:2, :], mod_ref[0, 0:1, :])
    u = jnp.dot(h.astype(BF16), w_ref[...], preferred_element_type=F32)
    c = MIX_CH
    ua_ref, ub_ref, uc_ref, uq_ref = outs[:4]
    ua_ref[0] = u[:, 0:2 * c]
    ub_ref[0] = u[:, 2 * c:5 * c]
    uc_ref[0] = u[:, 5 * c:6 * c].astype(BF16)
    q = u[:, 6 * c:7 * c]
    k = u[:, 7 * c:7 * c + c // 2]
    v = u[:, 7 * c + c // 2:8 * c]
    uq_ref[0] = q
    if rope:
        uqr_ref, ukv_ref = outs[4:]
        cs = cos_ref[...]
        sn = sin_ref[...]
        uqr_ref[0] = q * cs + _swap_halves(q) * sn
        kr = k * cs[:, :c // 2] + _swap_halves(k) * sn[:, :c // 2]
        ukv_ref[0] = jnp.concatenate([kr, v], axis=1)
    else:
        outs[4][0] = u[:, 7 * c:8 * c]


def in_projection(x, mods, mod_row0, norm_g, w_in_bf, *, res=None, rope=None, tm):
    b, l, d = x.shape
    c = MIX_CH
    grid = (b, l // tm)
    row = (lambda bb: 0) if mod_row0 is None else (lambda bb: mod_row0 + bb)
    xspec = pl.BlockSpec((1, tm, d), lambda bb, i: (bb, i, 0))
    mspec = pl.BlockSpec((1, N_MOD, d), lambda bb, i: (row(bb), 0, 0))
    args, specs = [x], [xspec]
    if res is not None:
        rargs, rspecs = _residual_specs(res, tm, d, row)
        args += rargs
        specs += rspecs
    args += [mods, norm_g.reshape(1, d), w_in_bf]
    specs += [mspec, pl.BlockSpec((1, d), lambda bb, i: (0, 0)),
              pl.BlockSpec(w_in_bf.shape, lambda bb, i: (0, 0))]
    if rope is not None:
        args += [rope[0], rope[1]]
        specs += [pl.BlockSpec((tm, c), lambda bb, i: (i, 0))] * 2

    def ospec(w):
        return pl.BlockSpec((1, tm, w), lambda bb, i: (bb, i, 0))

    out_shape, out_specs = [], []
    if res is not None:
        out_shape.append(jax.ShapeDtypeStruct((b, l, d), F32))
        out_specs.append(xspec)
    widths = [(2 * c, F32), (3 * c, F32), (c, BF16), (c, F32)] + ([(c, F32)] if rope is not None else []) + [(c, F32)]
    for w, dt in widths:
        out_shape.append(jax.ShapeDtypeStruct((b, l, w), dt))
        out_specs.append(ospec(w))
    return pl.pallas_call(
        functools.partial(_in_body, fuse_res=res is not None, rope=rope is not None),
        grid=grid, in_specs=specs, out_specs=out_specs, out_shape=out_shape,
        compiler_params=_params("arbitrary", "arbitrary"),
        name="in_proj",
    )(*args)


def _dw_tile(win, w_ref, n_taps, first, rows):
    acc = w_ref[0:1, :] * win[first:first + rows]
    for k in range(1, n_taps):
        acc = acc + w_ref[k:k + 1, :] * win[first + k:first + k + rows]
    return acc


CONV_PAD = 16
CONV_ROWS = 64


def _conf_body(u_ref, w_ref, b_ref, g_ref, beta_ref, o_ref, gp_ref, *, seq, n_taps):
    c = MIX_CH
    r = CONV_ROWS
    half = (n_taps - 1) // 2
    zero = jnp.zeros((CONV_PAD, c), F32)
    gp_ref[0:CONV_PAD, :] = zero
    gp_ref[CONV_PAD + seq:CONV_PAD + seq + CONV_PAD, :] = zero

    def fill(i, carry):
        r0 = pl.multiple_of(i * r, r)
        a = u_ref[0, pl.ds(r0, r), 0:c]
        g = u_ref[0, pl.ds(r0, r), c:2 * c]
        gp_ref[pl.ds(CONV_PAD + r0, r), :] = a * jax.nn.sigmoid(g)
        return carry

    lax.fori_loop(0, seq // r, fill, 0)

    def tile(i, carry):
        r0 = pl.multiple_of(i * r, r)
        win = gp_ref[pl.ds(r0, r + 2 * CONV_PAD), :]
        z = _dw_tile(win, w_ref, n_taps, CONV_PAD - half, r) + b_ref[...]
        mu = jnp.mean(z, axis=-1, keepdims=True)
        zc = z - mu
        var = jnp.mean(zc * zc, axis=-1, keepdims=True)
        zn = zc * lax.rsqrt(var + EPS) * g_ref[...] + beta_ref[...]
        o_ref[0, pl.ds(r0, r), :] = _silu(zn)
        return carry

    lax.fori_loop(0, seq // r, tile, 0)


def conformer_conv(ua, dw_w, dw_b, ln_g, ln_b):
    b, l, c2 = ua.shape
    c = MIX_CH
    k = dw_w.shape[0]
    vec = pl.BlockSpec((1, c), lambda bb: (0, 0))
    return pl.pallas_call(
        functools.partial(_conf_body, seq=l, n_taps=k),
        grid=(b,),
        in_specs=[pl.BlockSpec((1, l, c2), lambda bb: (bb, 0, 0)),
                  pl.BlockSpec((k, c), lambda bb: (0, 0)), vec, vec, vec],
        out_specs=pl.BlockSpec((1, l, c), lambda bb: (bb, 0, 0)),
        out_shape=jax.ShapeDtypeStruct((b, l, c), F32),
        scratch_shapes=[pltpu.VMEM((l + 2 * CONV_PAD, c), F32)],
        compiler_params=_params("arbitrary"),
        name="conformer",
    )(ua, dw_w, dw_b.reshape(1, c), ln_g.reshape(1, c), ln_b.reshape(1, c))


SHORT_PAD = 8


def _short_body(u_ref, w_ref, b_ref, o_ref, xp_ref, *, seq, n_taps):
    c = MIX_CH
    r = CONV_ROWS
    half = (n_taps - 1) // 2
    zero = jnp.zeros((SHORT_PAD, c), F32)
    xp_ref[0:SHORT_PAD, :] = zero
    xp_ref[SHORT_PAD + seq:SHORT_PAD + seq + SHORT_PAD, :] = zero

    def fill(i, carry):
        r0 = pl.multiple_of(i * r, r)
        xp_ref[pl.ds(SHORT_PAD + r0, r), :] = u_ref[0, pl.ds(r0, r), :]
        return carry

    lax.fori_loop(0, seq // r, fill, 0)

    def tile(i, carry):
        r0 = pl.multiple_of(i * r, r)
        win = xp_ref[pl.ds(r0, r + 2 * SHORT_PAD), :]
        o_ref[0, pl.ds(r0, r), :] = _dw_tile(win, w_ref, n_taps, SHORT_PAD - half, r) + b_ref[...]
        return carry

    lax.fori_loop(0, seq // r, tile, 0)


def hyena_short_conv(ub, short_w, short_b):
    b, l, c3 = ub.shape
    c = MIX_CH
    k = short_w.shape[0]
    return pl.pallas_call(
        functools.partial(_short_body, seq=l, n_taps=k),
        grid=(b, c3 // c),
        in_specs=[pl.BlockSpec((1, l, c), lambda bb, j: (bb, 0, j)),
                  pl.BlockSpec((k, c), lambda bb, j: (0, j)),
                  pl.BlockSpec((1, c), lambda bb, j: (0, j))],
        out_specs=pl.BlockSpec((1, l, c), lambda bb, j: (bb, 0, j)),
        out_shape=jax.ShapeDtypeStruct((b, l, c3), F32),
        scratch_shapes=[pltpu.VMEM((l + 2 * SHORT_PAD, c), F32)],
        compiler_params=_params("arbitrary", "arbitrary"),
        name="hyena_short",
    )(ub, short_w, short_b.reshape(1, c3))


FFT_UNROLL = 8


def _fft_sizes(seq):
    n = 2 * seq
    n2 = 128 if n >= 4096 else 32
    return n, n // n2, n2


def _stage1_tables(n, n1, n2, h):
    k1 = np.arange(n1)[None, :, None]
    i1 = np.arange(h)[None, None, :]
    i2 = np.arange(n2)[:, None, None]
    th = 2.0 * np.pi * ((k1 * (i1 * n2 + i2)) % n) / n
    cs, sn = np.cos(th), np.sin(th)
    return np.concatenate([np.concatenate([cs, sn], 2), np.concatenate([-sn, cs], 2)], 1).astype(np.float32)


def _stage3_tables(n, n1, n2, h):
    o1 = np.arange(h)[None, :, None]
    k1 = np.arange(n1)[None, None, :]
    i2 = np.arange(n2)[:, None, None]
    th = 2.0 * np.pi * ((k1 * (o1 * n2 + i2)) % n) / n
    cs, sn = np.cos(th) / n, np.sin(th) / n
    return np.concatenate([np.concatenate([cs, -sn], 2), np.concatenate([sn, cs], 2)], 1).astype(np.float32)


def _mid_tables(n2):
    a = np.arange(n2)
    th = 2.0 * np.pi * ((a[:, None] * a[None, :]) % n2) / n2
    cs, sn = np.cos(th), np.sin(th)
    fwd = np.concatenate([np.concatenate([cs, sn], 1), np.concatenate([-sn, cs], 1)], 0)
    inv = np.concatenate([np.concatenate([cs, -sn], 1), np.concatenate([sn, cs], 1)], 0)
    return fwd.astype(np.float32), inv.astype(np.float32)


def _dot(a, b, precise):
    if precise:
        return jnp.dot(a, b, preferred_element_type=F32, precision=lax.Precision.HIGHEST)
    return jnp.dot(a.astype(BF16), b.astype(BF16), preferred_element_type=F32)


def _stage1_body(zr_ref, zi_ref, t_ref, o_ref, *, n1, n2, h, precise):
    def step(i2, carry):
        xr = zr_ref[0, pl.ds(i2, h, stride=n2), :]
        xi = zi_ref[0, pl.ds(i2, h, stride=n2), :]
        a = _dot(t_ref[i2], jnp.concatenate([xr, xi], axis=0), precise)
        o_ref[0, 0, pl.ds(i2, n1, stride=n2), :] = a[:n1]
        o_ref[0, 1, pl.ds(i2, n1, stride=n2), :] = a[n1:]
        return carry

    lax.fori_loop(0, n2, step, 0, unroll=FFT_UNROLL)


def fft_stage1(zsrc, col_block, n_pairs, imag_offset, table, *, n1, n2, h, precise=False):
    rows = h * n2
    c = MIX_CH
    nc = c // LANES
    return pl.pallas_call(
        functools.partial(_stage1_body, n1=n1, n2=n2, h=h, precise=precise),
        grid=(n_pairs, nc),
        in_specs=[pl.BlockSpec((1, rows, LANES), lambda p, j: (p, 0, col_block * nc + j)),
                  pl.BlockSpec((1, rows, LANES), lambda p, j: (p + imag_offset, 0, col_block * nc + j)),
                  pl.BlockSpec(table.shape, lambda p, j: (0, 0, 0))],
        out_specs=pl.BlockSpec((1, 2, n1 * n2, LANES), lambda p, j: (p, 0, 0, j)),
        out_shape=jax.ShapeDtypeStruct((n_pairs, 2, n1 * n2, c), F32),
        compiler_params=_params("arbitrary", "arbitrary"),
        name="fft_stage1",
    )(zsrc, zsrc, table)


def _mid_body(a_ref, k_ref, f_ref, g_ref, o_ref, *, n2, kc):
    for j in range(kc):
        rows = slice(j * n2, (j + 1) * n2)
        blk = jnp.concatenate([a_ref[0, 0, rows, :], a_ref[0, 1, rows, :]], axis=0)
        s = _dot(f_ref[...], blk, False)
        sr, si = s[:n2], s[n2:]
        kr, ki = k_ref[0, rows, :], k_ref[1, rows, :]
        y = jnp.concatenate([sr * kr - si * ki, sr * ki + si * kr], axis=0)
        bb = _dot(g_ref[...], y, False)
        o_ref[0, 0, rows, :] = bb[:n2]
        o_ref[0, 1, rows, :] = bb[n2:]


def fft_mid(a, kf, order, f_fwd, f_inv, *, n1, n2):
    p = a.shape[0]
    c = MIX_CH
    kc = min(n1, 8)
    rows = kc * n2
    blk = pl.BlockSpec((1, 2, rows, c), lambda pp, j: (pp, 0, j, 0))
    mat = pl.BlockSpec(f_fwd.shape, lambda pp, j: (0, 0))
    return pl.pallas_call(
        functools.partial(_mid_body, n2=n2, kc=kc),
        grid=(p, n1 // kc),
        in_specs=[blk, pl.BlockSpec((2, rows, c), lambda pp, j: (0, j, order)), mat, mat],
        out_specs=blk,
        out_shape=jax.ShapeDtypeStruct(a.shape, F32),
        compiler_params=_params("arbitrary", "arbitrary"),
        name="fft_mid",
    )(a, kf, f_fwd, f_inv)


def _mid_fwd_body(a_ref, f_ref, o_ref, *, n2, kc):
    for j in range(kc):
        rows = slice(j * n2, (j + 1) * n2)
        blk = jnp.concatenate([a_ref[0, 0, rows, :], a_ref[0, 1, rows, :]], axis=0)
        s = _dot(f_ref[...], blk, True)
        o_ref[0, 0, rows, :] = s[:n2]
        o_ref[0, 1, rows, :] = s[n2:]


def fft_mid_forward(a, f_fwd, *, n1, n2):
    p = a.shape[0]
    c = MIX_CH
    kc = min(n1, 8)
    rows = kc * n2
    blk = pl.BlockSpec((1, 2, rows, c), lambda pp, j: (pp, 0, j, 0))
    return pl.pallas_call(
        functools.partial(_mid_fwd_body, n2=n2, kc=kc),
        grid=(p, n1 // kc),
        in_specs=[blk, pl.BlockSpec(f_fwd.shape, lambda pp, j: (0, 0))],
        out_specs=blk,
        out_shape=jax.ShapeDtypeStruct(a.shape, F32),
        compiler_params=_params("arbitrary", "arbitrary"),
        name="fft_mid_fwd",
    )(a, f_fwd)


def _stage3_body(b_ref, t_ref, zr_ref, zi_ref, gr_ref, gi_ref, d_ref, or_ref, oi_ref, *, n1, n2, h):
    def step(i2, carry):
        br = b_ref[0, 0, pl.ds(i2, n1, stride=n2), :]
        bi = b_ref[0, 1, pl.ds(i2, n1, stride=n2), :]
        y = _dot(t_ref[i2], jnp.concatenate([br, bi], axis=0), False)
        rows = pl.ds(i2, h, stride=n2)
        d = d_ref[...]
        or_ref[0, rows, :] = gr_ref[0, rows, :] * (y[:h] + d * zr_ref[0, rows, :])
        oi_ref[0, rows, :] = gi_ref[0, rows, :] * (y[h:] + d * zi_ref[0, rows, :])
        return carry

    lax.fori_loop(0, n2, step, 0, unroll=FFT_UNROLL)


def fft_stage3(bw, table, zsrc, z_col, gsrc, g_col, d_vec, imag_offset, *, n1, n2, h):
    p = bw.shape[0]
    c = MIX_CH
    nc = c // LANES
    rows = h * n2

    def src(col, off):
        return pl.BlockSpec((1, rows, LANES), lambda pp, j: (pp + off, 0, col * nc + j))

    ospec = pl.BlockSpec((1, rows, LANES), lambda pp, j: (pp, 0, j))
    return pl.pallas_call(
        functools.partial(_stage3_body, n1=n1, n2=n2, h=h),
        grid=(p, nc),
        in_specs=[pl.BlockSpec((1, 2, n1 * n2, LANES), lambda pp, j: (pp, 0, 0, j)),
                  pl.BlockSpec(table.shape, lambda pp, j: (0, 0, 0)),
                  src(z_col, 0), src(z_col, imag_offset), src(g_col, 0), src(g_col, imag_offset),
                  pl.BlockSpec((1, LANES), lambda pp, j: (0, j))],
        out_specs=[ospec, ospec],
        out_shape=[jax.ShapeDtypeStruct((p, rows, c), F32)] * 2,
        compiler_params=_params("arbitrary", "arbitrary"),
        name="fft_stage3",
    )(bw, table, zsrc, zsrc, gsrc, gsrc, d_vec.reshape(1, c))


def hyena_filter_time(seq, fw1, fb1, fr1, fw2, fb2, fr2, fw3, fb3):
    hp = lax.Precision.HIGHEST
    t = jnp.linspace(0.0, 1.0, seq, dtype=F32)[:, None]
    n_bands = (fw1.shape[0] - 1) // 2
    w = 2.0 * math.pi * jnp.arange(seq, dtype=F32)[:, None] / seq
    fr = jnp.linspace(1e-4, n_bands - 1, n_bands, dtype=F32)[None, :]
    z = jnp.concatenate([t, jnp.cos(fr * w), -jnp.sin(fr * w)], axis=-1)
    h = jnp.sin(fr1 * (jnp.dot(z, fw1, precision=hp) + fb1))
    h = jnp.sin(fr2 * (jnp.dot(h, fw2, precision=hp) + fb2))
    c = MIX_CH
    h = (jnp.dot(h, fw3, precision=hp) + fb3).reshape(seq, 2, 2, c)
    max_decay = math.log(HYENA_DECAY_TARGET) / HYENA_FAST_PCT
    min_decay = math.log(HYENA_DECAY_TARGET) / HYENA_SLOW_PCT
    deltas = jnp.linspace(min_decay, max_decay, c, dtype=F32)
    window = jnp.exp(-t * jnp.abs(deltas)[None, :]) + HYENA_SHIFT
    h = h * window[:, None, None, :]
    hf, hb = h[:, 0], h[:, 1]
    k = jnp.concatenate([hf[:1] + hb[:1], hf[1:], jnp.zeros_like(hf[:1]), hb[:0:-1]], axis=0)
    k = k / (jnp.sum(jnp.abs(k), axis=0, keepdims=True) + EPS)
    return k.reshape(2 * seq, 2 * c)


def hyena_filter_spectrum(seq, filt):
    n, n1, n2 = _fft_sizes(seq)
    k = hyena_filter_time(seq, *filt)
    src = jnp.stack([k, jnp.zeros_like(k)], axis=0)
    t1 = jnp.asarray(_stage1_tables(n, n1, n2, n1))
    f_fwd, _ = _mid_tables(n2)
    planes = []
    for o in range(2):
        a = fft_stage1(src, o, 1, 1, t1, n1=n1, n2=n2, h=n1, precise=True)
        planes.append(fft_mid_forward(a, jnp.asarray(f_fwd), n1=n1, n2=n2)[0])
    return jnp.concatenate(planes, axis=-1)


def hyena_mixer(ub, short_w, short_b, hy_d, kf):
    b, l, _ = ub.shape
    n, n1, n2 = _fft_sizes(l)
    h = n1 // 2
    p = b // 2
    u = hyena_short_conv(ub, short_w, short_b)
    t1 = jnp.asarray(_stage1_tables(n, n1, n2, h)).astype(BF16)
    t3 = jnp.asarray(_stage3_tables(n, n1, n2, h)).astype(BF16)
    f_fwd, f_inv = _mid_tables(n2)
    f_fwd = jnp.asarray(f_fwd).astype(BF16)
    f_inv = jnp.asarray(f_inv).astype(BF16)
    a = fft_stage1(u, 0, p, p, t1, n1=n1, n2=n2, h=h)
    bw = fft_mid(a, kf, 0, f_fwd, f_inv, n1=n1, n2=n2)
    zr, zi = fft_stage3(bw, t3, u, 0, u, 1, hy_d[0], p, n1=n1, n2=n2, h=h)
    z = jnp.concatenate([zr, zi], axis=0)
    a = fft_stage1(z, 0, p, p, t1, n1=n1, n2=n2, h=h)
    bw = fft_mid(a, kf, 1, f_fwd, f_inv, n1=n1, n2=n2)
    zr, zi = fft_stage3(bw, t3, z, 0, u, 2, hy_d[1], p, n1=n1, n2=n2, h=h)
    return jnp.concatenate([zr, zi], axis=0)


def _fnet_body(cl_ref, sl_ref, x_ref, cc_ref, sc_ref, o_ref, *, scale):
    x = x_ref[0]
    pr = jnp.dot(cl_ref[...], x, preferred_element_type=F32).astype(BF16)
    qr = jnp.dot(sl_ref[...], x, preferred_element_type=F32).astype(BF16)
    o_ref[0] = (jnp.dot(pr, cc_ref[...], preferred_element_type=F32)
                - jnp.dot(qr, sc_ref[...], preferred_element_type=F32)) * scale


def _dft_tables(n):
    a = jnp.arange(n, dtype=jnp.int32)
    th = ((a[:, None] * a[None, :]) % n).astype(F32) * (2.0 * math.pi / n)
    return jnp.cos(th).astype(BF16), jnp.sin(th).astype(BF16)


def fnet_mixer(uc, tables):
    b, l, c = uc.shape
    cl, sl, cc, sc = tables
    tm = min(l, 512)
    row = pl.BlockSpec((tm, l), lambda i, bb: (i, 0))
    sq = pl.BlockSpec((c, c), lambda i, bb: (0, 0))
    return pl.pallas_call(
        functools.partial(_fnet_body, scale=1.0 / math.sqrt(l * c)),
        grid=(l // tm, b),
        in_specs=[row, row, pl.BlockSpec((1, l, c), lambda i, bb: (bb, 0, 0)), sq, sq],
        out_specs=pl.BlockSpec((1, tm, c), lambda i, bb: (bb, i, 0)),
        out_shape=jax.ShapeDtypeStruct((b, l, c), F32),
        compiler_params=_params("arbitrary", "arbitrary"),
        name="fnet",
    )(cl, sl, uc, cc, sc)


def _heads_rows(x, g):
    h0 = Q_PER_KV * g
    return jnp.concatenate([x[:, (h0 + r) * HEAD_DIM:(h0 + r + 1) * HEAD_DIM] for r in range(Q_PER_KV)], axis=0)


def _qk(q, k):
    return lax.dot_general(q.astype(BF16), k.astype(BF16), (((1,), (1,)), ((), ())),
                           preferred_element_type=F32)


def _sink_col(sink_ref, g, rows):
    ridx = lax.broadcasted_iota(jnp.int32, (Q_PER_KV * rows, 1), 0)
    col = jnp.full((Q_PER_KV * rows, 1), sink_ref[Q_PER_KV * g], F32)
    for r in range(1, Q_PER_KV):
        col = jnp.where(ridx >= r * rows, sink_ref[Q_PER_KV * g + r], col)
    return col


def _lat_attn_body(sink_ref, q_ref, qr_ref, kp_ref, kc_ref, kn_ref, ck_ref, cv_ref, o_ref):
    i = pl.program_id(1)
    nblk = pl.num_programs(1)
    blk = ATT_BLOCK
    span = blk + 2 * WINDOW
    scale = HEAD_DIM ** -0.5
    q = q_ref[0]
    qr = qr_ref[0]
    kv = jnp.concatenate([kp_ref[0], kc_ref[0], kn_ref[0]], axis=0)
    ck = ck_ref[0, 0]
    cv = cv_ref[0, 0]
    kvw = N_KV_HEADS * HEAD_DIM
    r = lax.broadcasted_iota(jnp.int32, (Q_PER_KV * blk, span), 0) % blk
    j = lax.broadcasted_iota(jnp.int32, (Q_PER_KV * blk, span), 1)
    ok = (j >= r) & (j <= r + 2 * WINDOW)
    ok = ok & ((i > 0) | (j >= WINDOW)) & ((i < nblk - 1) | (j < WINDOW + blk))
    outs = []
    for g in range(N_KV_HEADS):
        kl = kv[:, g * HEAD_DIM:(g + 1) * HEAD_DIM]
        vl = kv[:, kvw + g * HEAD_DIM:kvw + (g + 1) * HEAD_DIM]
        s_loc = jnp.where(ok, _qk(_heads_rows(qr, g), kl) * scale, NEG_INF)
        s_ctx = _qk(_heads_rows(q, g), ck[:, g * HEAD_DIM:(g + 1) * HEAD_DIM]) * scale
        sink = _sink_col(sink_ref, g, blk)
        m = jnp.maximum(jnp.maximum(jnp.max(s_loc, axis=-1, keepdims=True),
                                    jnp.max(s_ctx, axis=-1, keepdims=True)), sink)
        e_loc = jnp.exp(s_loc - m)
        e_ctx = jnp.exp(s_ctx - m)
        den = jnp.sum(e_loc, axis=-1, keepdims=True) + jnp.sum(e_ctx, axis=-1, keepdims=True) + jnp.exp(sink - m)
        inv = 1.0 / den
        o = (jnp.dot((e_loc * inv).astype(BF16), vl.astype(BF16), preferred_element_type=F32)
             + jnp.dot((e_ctx * inv).astype(BF16), cv[:, g * HEAD_DIM:(g + 1) * HEAD_DIM].astype(BF16),
                       preferred_element_type=F32))
        outs += [o[rr * blk:(rr + 1) * blk] for rr in range(Q_PER_KV)]
    o_ref[0] = jnp.concatenate(outs, axis=1)


def latent_attention(uq, uqr, ukv, cache_k, cache_v, layer, sink):
    b, l, c = uq.shape
    p = cache_k.shape[2]
    blk = ATT_BLOCK
    nblk = l // blk
    qspec = pl.BlockSpec((1, blk, c), lambda bb, i: (bb, i, 0))
    cspec = pl.BlockSpec((1, 1, p, cache_k.shape[3]), lambda bb, i: (bb, layer, 0, 0))
    return pl.pallas_call(
        _lat_attn_body,
        grid=(b, nblk),
        in_specs=[pl.BlockSpec(memory_space=pltpu.SMEM), qspec, qspec,
                  pl.BlockSpec((1, blk, c), lambda bb, i: (bb, jnp.maximum(i - 1, 0), 0)),
                  qspec,
                  pl.BlockSpec((1, blk, c), lambda bb, i: (bb, jnp.minimum(i + 1, nblk - 1), 0)),
                  cspec, cspec],
        out_specs=qspec,
        out_shape=jax.ShapeDtypeStruct((b, l, c), F32),
        compiler_params=_params("arbitrary", "arbitrary"),
        name="latent_attention",
    )(sink, uq, uqr, ukv, ukv, ukv, cache_k, cache_v)


def _ctx_attn_body(sink_ref, q_ref, kv_ref, o_ref, *, seq):
    scale = HEAD_DIM ** -0.5
    q = q_ref[0]
    kv = kv_ref[0]
    kvw = N_KV_HEADS * HEAD_DIM
    outs = []
    for g in range(N_KV_HEADS):
        kl = kv[:, g * HEAD_DIM:(g + 1) * HEAD_DIM]
        vl = kv[:, kvw + g * HEAD_DIM:kvw + (g + 1) * HEAD_DIM]
        s = _qk(_heads_rows(q, g), kl) * scale
        sink = _sink_col(sink_ref, g, seq)
        m = jnp.maximum(jnp.max(s, axis=-1, keepdims=True), sink)
        e = jnp.exp(s - m)
        den = jnp.sum(e, axis=-1, keepdims=True) + jnp.exp(sink - m)
        o = jnp.dot((e * (1.0 / den)).astype(BF16), vl.astype(BF16), preferred_element_type=F32)
        outs += [o[rr * seq:(rr + 1) * seq] for rr in range(Q_PER_KV)]
    o_ref[0] = jnp.concatenate(outs, axis=1)


def context_attention(uq, ukv, sink):
    b, s, c = uq.shape
    spec = pl.BlockSpec((1, s, c), lambda bb: (bb, 0, 0))
    return pl.pallas_call(
        functools.partial(_ctx_attn_body, seq=s),
        grid=(b,),
        in_specs=[pl.BlockSpec(memory_space=pltpu.SMEM), spec, spec],
        out_specs=spec,
        out_shape=jax.ShapeDtypeStruct((b, s, c), F32),
        compiler_params=_params("arbitrary"),
        name="context_attention",
    )(sink, uq, ukv)


def _pack_bf16_pairs(hi_rounded):
    k = hi_rounded.shape[1] // 2
    bits = lax.bitcast_convert_type(hi_rounded, jnp.uint32)
    return bits[:, :k] | (bits[:, k:] >> 16)


def _unpack_bf16_pairs(packed):
    a = lax.bitcast_convert_type(packed & jnp.uint32(0xFFFF0000), F32)
    b = lax.bitcast_convert_type(packed << 16, F32)
    return jnp.concatenate([a, b], axis=1).astype(BF16)


def _out_body(ya_ref, yb_ref, yc_ref, yd_ref, x_ref, mod_ref, g_ref, w_ref, rw_ref, rb_ref,
              x1_ref, h_ref, route_ref):
    c = MIX_CH
    y = jnp.dot(ya_ref[0].astype(BF16), w_ref[0:c, :], preferred_element_type=F32)
    for j, ref in enumerate((yb_ref, yc_ref, yd_ref), start=1):
        y = y + jnp.dot(ref[0].astype(BF16), w_ref[j * c:(j + 1) * c, :], preferred_element_type=F32)
    x1 = x_ref[0] + mod_ref[0, 2:3, :] * y
    x1_ref[0] = x1
    h = _rmsnorm_mod(x1, g_ref[...], mod_ref[0, 4:5, :], mod_ref[0, 3:4, :])
    h_hi = h.astype(BF16)
    h_hi32 = h_hi.astype(F32)
    h_ref[0] = _pack_bf16_pairs(h_hi32)
    h_lo = (h - h_hi32).astype(BF16)
    tm = h.shape[0]
    prod = jnp.dot(jnp.concatenate([h_hi, h_lo], axis=0), rw_ref[...], preferred_element_type=F32)
    logits = (prod[:tm, :ROUTE_LANES] + prod[:tm, ROUTE_LANES:]
              + prod[tm:, :ROUTE_LANES] + prod[tm:, ROUTE_LANES:]) + rb_ref[...]
    lane = lax.broadcasted_iota(jnp.int32, logits.shape, 1)
    is_c = lane < N_GROUPS
    lc = jnp.where(is_c, logits, NEG_INF)
    mc = jnp.max(lc, axis=-1, keepdims=True)
    grp = jnp.min(jnp.where(lc == mc, lane, ROUTE_LANES), axis=-1, keepdims=True)
    pg = 1.0 / jnp.sum(jnp.where(is_c, jnp.exp(lc - mc), 0.0), axis=-1, keepdims=True)
    lo = N_GROUPS + grp * EXPERTS_PER_GROUP
    in_g = (lane >= lo) & (lane < lo + EXPERTS_PER_GROUP)
    lf = jnp.where(in_g, logits, NEG_INF)
    t1 = jnp.max(lf, axis=-1, keepdims=True)
    i1 = jnp.min(jnp.where(lf == t1, lane, ROUTE_LANES), axis=-1, keepdims=True)
    lf2 = jnp.where(lane == i1, NEG_INF, lf)
    t2 = jnp.max(lf2, axis=-1, keepdims=True)
    i2 = jnp.min(jnp.where(lf2 == t2, lane, ROUTE_LANES), axis=-1, keepdims=True)
    e2 = jnp.exp(t2 - t1)
    w1 = pg / (1.0 + e2)
    w2 = pg * e2 / (1.0 + e2)
    rec = jnp.where(lane == 0, (i1 - N_GROUPS).astype(F32),
                    jnp.where(lane == 1, (i2 - N_GROUPS).astype(F32),
                              jnp.where(lane == 2, w1, jnp.where(lane == 3, w2, 0.0))))
    route_ref[0] = rec


def out_projection(ys, x, mods, mod_row0, norm_g, w_out_bf, rw, rb, *, tm):
    b, l, d = x.shape
    c = MIX_CH
    row = (lambda bb: 0) if mod_row0 is None else (lambda bb: mod_row0 + bb)
    yspec = pl.BlockSpec((1, tm, c), lambda bb, i: (bb, i, 0))
    xspec = pl.BlockSpec((1, tm, d), lambda bb, i: (bb, i, 0))
    return pl.pallas_call(
        _out_body,
        grid=(b, l // tm),
        in_specs=[yspec] * 4 + [xspec,
                                pl.BlockSpec((1, N_MOD, d), lambda bb, i: (row(bb), 0, 0)),
                                pl.BlockSpec((1, d), lambda bb, i: (0, 0)),
                                pl.BlockSpec(w_out_bf.shape, lambda bb, i: (0, 0)),
                                pl.BlockSpec(rw.shape, lambda bb, i: (0, 0)),
                                pl.BlockSpec(rb.shape, lambda bb, i: (0, 0))],
        out_specs=[xspec, pl.BlockSpec((1, tm, d // 2), lambda bb, i: (bb, i, 0)),
                   pl.BlockSpec((1, tm, ROUTE_LANES), lambda bb, i: (bb, i, 0))],
        out_shape=[jax.ShapeDtypeStruct((b, l, d), F32), jax.ShapeDtypeStruct((b, l, d // 2), jnp.uint32),
                   jax.ShapeDtypeStruct((b, l, ROUTE_LANES), F32)],
        compiler_params=_params("arbitrary", "arbitrary"),
        name="out_proj",
    )(*ys, x, mods, norm_g.reshape(1, d), w_out_bf, rw, rb)


def _expert_body(be_ref, nv_ref, xs_ref, wg_ref, wu_ref, wd_ref, o_ref, wg_s, wu_s, wd_s):
    i = pl.program_id(0)
    prev = be_ref[jnp.maximum(i - 1, 0)]

    @pl.when((i == 0) | (be_ref[i] != prev))
    def _():
        wg_s[...] = wg_ref[0].astype(BF16)
        wu_s[...] = wu_ref[0].astype(BF16)
        wd_s[...] = wd_ref[0].astype(BF16)

    @pl.when(nv_ref[i] > 0)
    def _():
        row = lax.broadcasted_iota(jnp.int32, xs_ref.shape, 0)
        x = _unpack_bf16_pairs(jnp.where(row < nv_ref[i], xs_ref[...], jnp.uint32(0)))
        g = jnp.dot(x, wg_s[...], preferred_element_type=F32)
        u = jnp.dot(x, wu_s[...], preferred_element_type=F32)
        a = (_silu(g) * u).astype(BF16)
        o_ref[...] = jnp.dot(a, wd_s[...], preferred_element_type=F32)

    @pl.when(nv_ref[i] <= 0)
    def _():
        o_ref[...] = jnp.zeros(o_ref.shape, F32)


def expert_ffn(xs, blk_e, n_valid, e_gate, e_up, e_down):
    rows, dh = xs.shape
    d = 2 * dh
    nb = rows // MOE_BLOCK
    de = e_gate.shape[-1]
    grid_spec = pltpu.PrefetchScalarGridSpec(
        num_scalar_prefetch=2,
        grid=(nb,),
        in_specs=[pl.BlockSpec((MOE_BLOCK, dh), lambda i, be, nv: (i, 0)),
                  pl.BlockSpec((1, d, de), lambda i, be, nv: (be[i], 0, 0)),
                  pl.BlockSpec((1, d, de), lambda i, be, nv: (be[i], 0, 0)),
                  pl.BlockSpec((1, de, d), lambda i, be, nv: (be[i], 0, 0))],
        out_specs=pl.BlockSpec((MOE_BLOCK, d), lambda i, be, nv: (i, 0)),
        scratch_shapes=[pltpu.VMEM((d, de), BF16), pltpu.VMEM((d, de), BF16), pltpu.VMEM((de, d), BF16)],
    )
    return pl.pallas_call(
        _expert_body, grid_spec=grid_spec,
        out_shape=jax.ShapeDtypeStruct((rows, d), F32),
        compiler_params=_params("arbitrary"),
        name="expert_ffn",
    )(blk_e, n_valid, xs, e_gate, e_up, e_down)


RANK_TILE = 256


def _rank_body(route_ref, rank_ref, cnt_ref, carry_ref):
    @pl.when(pl.program_id(0) == 0)
    def _():
        carry_ref[...] = jnp.zeros(carry_ref.shape, F32)

    t = RANK_TILE
    rec = route_ref[...]
    lane = lax.broadcasted_iota(jnp.int32, rec.shape, 1)
    lanef = lane.astype(F32)
    e0 = (lanef == rec[:, 0:1]).astype(F32)
    e1 = (lanef == rec[:, 1:2]).astype(F32)
    both = e0 + e1
    earlier = (lax.broadcasted_iota(jnp.int32, (t, t), 0) > lax.broadcasted_iota(jnp.int32, (t, t), 1))
    before = jnp.dot(earlier.astype(BF16), both.astype(BF16), preferred_element_type=F32) + carry_ref[0:1, :]
    r0 = jnp.sum(e0 * before, axis=-1, keepdims=True)
    r1 = jnp.sum(e1 * before, axis=-1, keepdims=True)
    rank_ref[...] = jnp.where(lane == 0, r0, jnp.where(lane == 1, r1, 0.0))
    carry_ref[...] = carry_ref[...] + jnp.sum(both, axis=0, keepdims=True)
    cnt_ref[...] = carry_ref[...]


def moe_rank(route):
    n = route.shape[0]
    t = RANK_TILE
    return pl.pallas_call(
        _rank_body,
        grid=(n // t,),
        in_specs=[pl.BlockSpec((t, ROUTE_LANES), lambda i: (i, 0))],
        out_specs=[pl.BlockSpec((t, ROUTE_LANES), lambda i: (i, 0)),
                   pl.BlockSpec((SUBLANES, ROUTE_LANES), lambda i: (0, 0))],
        out_shape=[jax.ShapeDtypeStruct((n, ROUTE_LANES), F32),
                   jax.ShapeDtypeStruct((SUBLANES, ROUTE_LANES), F32)],
        scratch_shapes=[pltpu.VMEM((SUBLANES, ROUTE_LANES), F32)],
        compiler_params=_params("arbitrary"),
        name="moe_rank",
    )(route)


def _sc_mesh():
    return plsc.VectorSubcoreMesh(core_axis_name="c", subcore_axis_name="s")


def _sc_worker():
    return lax.axis_index("s") * SC_CORES + lax.axis_index("c")


DISPATCH_ROWS = 64
COMBINE_ROWS = 32


def sc_dispatch(rows, dest, n_slots):
    n, w = rows.shape
    ch = DISPATCH_ROWS
    per_w = n // SC_WORKERS
    n_ch = per_w // ch

    @functools.partial(
        pl.kernel, mesh=_sc_mesh(),
        out_type=jax.ShapeDtypeStruct((n_slots, w), rows.dtype),
        scratch_types=[pltpu.VMEM((ch,), jnp.int32), pltpu.VMEM((ch, w), rows.dtype)],
    )
    def scatter_kernel(rows_hbm, dest_hbm, out_hbm, idx_v, rows_v):
        wid = _sc_worker()

        @pl.loop(0, n_ch)
        def _(j):
            chunk = wid * n_ch + j
            pltpu.sync_copy(rows_hbm.at[pl.ds(pl.multiple_of(chunk * ch, ch), ch)], rows_v)
            for k in range(2):
                pltpu.sync_copy(dest_hbm.at[k, chunk], idx_v)
                pltpu.sync_copy(rows_v, out_hbm.at[idx_v])

    return scatter_kernel(rows, dest)


def sc_gather_rows(table, idx):
    s, w = table.shape
    m = idx.shape[0]
    ch = COMBINE_ROWS
    per_w = m // SC_WORKERS
    n_ch = per_w // ch

    @functools.partial(
        pl.kernel, mesh=_sc_mesh(),
        out_type=jax.ShapeDtypeStruct((m, w), table.dtype),
        scratch_types=[pltpu.VMEM((ch,), jnp.int32), pltpu.VMEM((ch, w), table.dtype), pltpu.SemaphoreType.DMA],
    )
    def gather_kernel(table_hbm, idx_hbm, out_hbm, idx_v, rows_v, sem):
        wid = _sc_worker()

        @pl.loop(0, n_ch)
        def _(j):
            off = pl.multiple_of((wid * n_ch + j) * ch, ch)
            pltpu.sync_copy(idx_hbm.at[pl.ds(off, ch)], idx_v)
            pltpu.async_copy(table_hbm.at[idx_v], rows_v, sem).wait()
            pltpu.sync_copy(rows_v, out_hbm.at[pl.ds(off, ch)])

    return gather_kernel(table, idx)


def hier_moe(h_packed, route, e_gate, e_up, e_down):
    b, l, dh = h_packed.shape
    n = b * l
    assert n % (SC_WORKERS * DISPATCH_ROWS) == 0 and (2 * n) % (SC_WORKERS * COMBINE_ROWS) == 0
    route2 = route.reshape(n, ROUTE_LANES)
    rank, cnt = moe_rank(route2)
    counts = cnt[0, :N_EXPERTS].astype(jnp.int32)
    padded = (counts + MOE_BLOCK - 1) // MOE_BLOCK * MOE_BLOCK
    pend = jnp.cumsum(padded)
    pstart = pend - padded
    nb = -(-2 * n // MOE_BLOCK) + N_EXPERTS
    blk0 = jnp.arange(nb, dtype=jnp.int32) * MOE_BLOCK
    blk_e = jnp.minimum(jnp.searchsorted(pend, blk0, side='right'), N_EXPERTS - 1).astype(jnp.int32)
    n_valid = jnp.clip(pstart[blk_e] + counts[blk_e] - blk0, 0, MOE_BLOCK).astype(jnp.int32)
    dest = (pstart[route2[:, 0:2].astype(jnp.int32)] + rank[:, 0:2].astype(jnp.int32)).T
    xs = sc_dispatch(h_packed.reshape(n, dh), dest.reshape(2, n // DISPATCH_ROWS, DISPATCH_ROWS), nb * MOE_BLOCK)
    y = expert_ffn(xs, blk_e, n_valid, e_gate, e_up, e_down)
    return sc_gather_rows(y, dest.reshape(2 * n)).reshape(2, b, l, 2 * dh)


def _final_body(x_ref, y0_ref, y1_ref, route_ref, pmod_ref, g_ref, o_ref):
    x = _moe_residual(x_ref[0], y0_ref, y1_ref, route_ref, pmod_ref)
    ms = jnp.mean(x * x, axis=-1, keepdims=True)
    o_ref[0] = x * lax.rsqrt(ms + EPS) * g_ref[...]


def final_norm(x1, res, mod_row0, norm_g, *, tm):
    b, l, d = x1.shape
    row = (lambda bb: 0) if mod_row0 is None else (lambda bb: mod_row0 + bb)
    xspec = pl.BlockSpec((1, tm, d), lambda bb, i: (bb, i, 0))
    rargs, rspecs = _residual_specs(res, tm, d, row)
    return pl.pallas_call(
        _final_body,
        grid=(b, l // tm),
        in_specs=[xspec] + rspecs + [pl.BlockSpec((1, d), lambda bb, i: (0, 0))],
        out_specs=xspec,
        out_shape=jax.ShapeDtypeStruct((b, l, d), F32),
        compiler_params=_params("arbitrary", "arbitrary"),
        name="final_norm",
    )(x1, *rargs, norm_g.reshape(1, d))


def _rope_tables(seq):
    rows = seq // GRID_W
    row_pos = jnp.repeat(jnp.arange(rows, dtype=F32), GRID_W)
    col_pos = jnp.tile(jnp.arange(GRID_W, dtype=F32), rows)
    n_freq = HEAD_DIM // 4
    inv = ROPE_BASE ** (-jnp.arange(n_freq, dtype=F32) / n_freq)
    ang = jnp.concatenate([row_pos[:, None] * inv, col_pos[:, None] * inv], axis=-1)
    cs, sn = jnp.cos(ang), jnp.sin(ang)
    cos_f = jnp.tile(jnp.concatenate([cs, cs], axis=-1), (1, N_Q_HEADS))
    sin_s = jnp.tile(jnp.concatenate([-sn, sn], axis=-1), (1, N_Q_HEADS))
    return cos_f, sin_s


def kernel(x_prompt, x_sample, cache_k, cache_v, c, c_ctx, ada_w, ada_b, norm1_g, norm2_g, w_in, conv_dw_w, conv_dw_b, conv_ln_g, conv_ln_b, hy_short_w, hy_short_b, hy_fw1, hy_fb1, hy_freq1, hy_fw2, hy_fb2, hy_freq2, hy_fw3, hy_fb3, hy_d, attn_sink, w_out, router_coarse_w, router_coarse_b, router_fine_w, router_fine_b, exp_gate, exp_up, exp_down, norm_f_g):
    depth = ada_w.shape[0]
    bp, lp, d = x_prompt.shape
    bs, ls, _ = x_sample.shape
    assert bp % 2 == 0 and bs % 2 == 0 and ls % ATT_BLOCK == 0 and ls % GRID_W == 0

    n_rows = -(-(1 + bs) // SUBLANES) * SUBLANES
    cvec = jnp.concatenate([c_ctx[None, :], c, jnp.zeros((n_rows - 1 - bs, d), F32)], axis=0)
    mods = adaln_all(cvec, ada_w, ada_b)

    rope = _rope_tables(ls)
    fnet_tabs = {}
    for seq in {lp, ls}:
        fnet_tabs[seq] = _dft_tables(seq) + _dft_tables(MIX_CH)
    ck = cache_k.reshape(cache_k.shape[0], depth, cache_k.shape[2], -1)
    cv = cache_v.reshape(cache_v.shape[0], depth, cache_v.shape[2], -1)
    pad = ROUTE_LANES - N_GROUPS - N_EXPERTS

    tm_p = min(lp, 512)
    tm_s = min(ls, 512)
    xp, xs = x_prompt, x_sample
    res_p = res_s = None
    ks_out, vs_out = [], []
    for l in range(depth):
        w_in_bf = w_in[l].astype(BF16)
        w_out_bf = w_out[l].astype(BF16)
        rw = jnp.concatenate([router_coarse_w[l], router_fine_w[l], jnp.zeros((d, pad), F32)], axis=1)
        rw_hi = rw.astype(BF16)
        rw = jnp.concatenate([rw_hi, (rw - rw_hi.astype(F32)).astype(BF16)], axis=1)
        rb = jnp.concatenate([router_coarse_b[l], router_fine_b[l], jnp.zeros((pad,), F32)])[None, :]
        filt = (hy_fw1[l], hy_fb1[l], hy_freq1[l], hy_fw2[l], hy_fb2[l], hy_freq2[l], hy_fw3[l], hy_fb3[l])
        sink = attn_sink[l]

        def mixers(ua, ub, uc, yd, seq):
            ya = conformer_conv(ua, conv_dw_w[l], conv_dw_b[l], conv_ln_g[l], conv_ln_b[l])
            yb = hyena_mixer(ub, hy_short_w[l], hy_short_b[l], hy_d[l], hyena_filter_spectrum(seq, filt))
            yc = fnet_mixer(uc, fnet_tabs[seq])
            return (ya, yb, yc, yd)

        outs = in_projection(xp, mods[l], None, norm1_g[l], w_in_bf, res=res_p, tm=tm_p)
        if res_p is not None:
            xp, outs = outs[0], outs[1:]
        ua, ub, uc, uq, ukv = outs
        kvw = N_KV_HEADS * HEAD_DIM
        ks_out.append(ukv[..., :kvw].reshape(bp, lp, N_KV_HEADS, HEAD_DIM))
        vs_out.append(ukv[..., kvw:].reshape(bp, lp, N_KV_HEADS, HEAD_DIM))
        ys = mixers(ua, ub, uc, context_attention(uq, ukv, sink), lp)
        x1p, hp, route = out_projection(ys, xp, mods[l], None, norm2_g[l], w_out_bf, rw, rb, tm=tm_p)
        res_p = (hier_moe(hp, route, exp_gate[l], exp_up[l], exp_down[l]), route, mods[l])
        xp = x1p

        outs = in_projection(xs, mods[l], 1, norm1_g[l], w_in_bf, res=res_s, rope=rope, tm=tm_s)
        if res_s is not None:
            xs, outs = outs[0], outs[1:]
        ua, ub, uc, uq, uqr, ukv = outs
        ys = mixers(ua, ub, uc, latent_attention(uq, uqr, ukv, ck, cv, l, sink), ls)
        x1s, hs, route = out_projection(ys, xs, mods[l], 1, norm2_g[l], w_out_bf, rw, rb, tm=tm_s)
        res_s = (hier_moe(hs, route, exp_gate[l], exp_up[l], exp_down[l]), route, mods[l])
        xs = x1s

    y_prompt = final_norm(xp, res_p, None, norm_f_g, tm=tm_p)
    y_sample = final_norm(xs, res_s, 1, norm_f_g, tm=tm_s)
    return (y_prompt, y_sample, jnp.stack(ks_out, axis=1), jnp.stack(vs_out, axis=1))
```

```python
import functools
import math

import numpy as np
import jax
import jax.numpy as jnp
from jax import lax
from jax.experimental import pallas as pl
from jax.experimental.pallas import tpu as pltpu
from jax.experimental.pallas import tpu_sc as plsc

F32 = jnp.float32
BF16 = jnp.bfloat16

HEAD_DIM = 64
ATT_SCALE = HEAD_DIM ** -0.5
N_Q_HEADS = 4
N_KV_HEADS = 2
Q_PER_KV = N_Q_HEADS // N_KV_HEADS
WINDOW = 128
ATT_BLOCK = 128
GRID_W = 64
ROPE_BASE = 10000.0
N_GROUPS = 4
EXPERTS_PER_GROUP = 8
N_EXPERTS = N_GROUPS * EXPERTS_PER_GROUP
MOE_BLOCK = 256
N_MOD = 6
EPS = 1e-6
NEG_INF = -1e30
HYENA_DECAY_TARGET = 1e-2
HYENA_FAST_PCT = 0.3
HYENA_SLOW_PCT = 1.5
HYENA_SHIFT = 0.05

LANES = 128
SUBLANES = 8
VMEM_LIMIT = 56 * 1024 * 1024

MIX_CH = 256
ROUTE_LANES = 128
SC_CORES = 2
SC_WORKERS = SC_CORES * 16


def _params(*sem):
    return pltpu.CompilerParams(dimension_semantics=sem, vmem_limit_bytes=VMEM_LIMIT)


def _silu(x):
    return x * jax.nn.sigmoid(x)


def _ada_body(c_ref, w_ref, b_ref, o_ref):
    s = _silu(c_ref[...]).astype(BF16)
    o_ref[0] = jnp.dot(s, w_ref[0].astype(BF16), preferred_element_type=F32) + b_ref[0]


def adaln_all(cvec, ada_w, ada_b):
    depth, d, n6 = ada_w.shape
    r = cvec.shape[0]
    tn = n6 // 4
    out = pl.pallas_call(
        _ada_body,
        grid=(depth, n6 // tn),
        in_specs=[
            pl.BlockSpec((r, d), lambda l, j: (0, 0)),
            pl.BlockSpec((1, d, tn), lambda l, j: (l, 0, j)),
            pl.BlockSpec((1, 1, tn), lambda l, j: (l, 0, j)),
        ],
        out_specs=pl.BlockSpec((1, r, tn), lambda l, j: (l, 0, j)),
        out_shape=jax.ShapeDtypeStruct((depth, r, n6), F32),
        compiler_params=_params("arbitrary", "arbitrary"),
        name="adaln",
    )(cvec, ada_w, ada_b.reshape(depth, 1, n6))
    return out.reshape(depth, r, N_MOD, d)


def _swap_halves(x):
    pieces = []
    for j in range(x.shape[1] // LANES):
        xj = x[:, j * LANES:(j + 1) * LANES]
        fwd = pltpu.roll(xj, LANES - HEAD_DIM // 2, axis=1)
        bwd = pltpu.roll(xj, HEAD_DIM // 2, axis=1)
        lane = lax.broadcasted_iota(jnp.int32, xj.shape, 1)
        pieces.append(jnp.where((lane % HEAD_DIM) < HEAD_DIM // 2, fwd, bwd))
    return pieces[0] if len(pieces) == 1 else jnp.concatenate(pieces, axis=1)


def _rmsnorm_mod(x, g, scale, shift):
    ms = jnp.mean(x * x, axis=-1, keepdims=True)
    return (x * lax.rsqrt(ms + EPS) * g) * (1.0 + scale) + shift


def _moe_residual(x1, y0_ref, y1_ref, route_ref, pmod_ref):
    moe = route_ref[0, :, 2:3] * y0_ref[0, 0] + route_ref[0, :, 3:4] * y1_ref[0, 0]
    return x1 + pmod_ref[0, 5:6, :] * moe


def _residual_specs(res, tm, d, row):
    pair, route, pmods = res
    args = [pair, pair, route, pmods]
    specs = [pl.BlockSpec((1, 1, tm, d), lambda bb, i: (0, bb, i, 0)),
             pl.BlockSpec((1, 1, tm, d), lambda bb, i: (1, bb, i, 0)),
             pl.BlockSpec((1, tm, ROUTE_LANES), lambda bb, i: (bb, i, 0)),
             pl.BlockSpec((1, N_MOD, d), lambda bb, i: (row(bb), 0, 0))]
    return args, specs


def _in_body(*refs, fuse_res, rope):
    it = iter(refs)
    x_ref = next(it)
    if fuse_res:
        res_refs = [next(it) for _ in range(4)]
    mod_ref = next(it)
    g_ref = next(it)
    w_ref = next(it)
    if rope:
        cos_ref = next(it)
        sin_ref = next(it)
    outs = list(it)
    x = x_ref[0]
    if fuse_res:
        x = _moe_residual(x, *res_refs)
        outs.pop(0)[0] = x
    h = _rmsnorm_mod(x, g_ref[...], mod_ref[0, 1:2, :], mod_ref[0, 0:1, :])
    u = jnp.dot(h.astype(BF16), w_ref[...], preferred_element_type=F32)
    c = MIX_CH
    ua_ref, ub_ref, uc_ref, uq_ref = outs[:4]
    ua_ref[0] = u[:, 0:2 * c]
    ub_ref[0] = u[:, 2 * c:5 * c]
    uc_ref[0] = u[:, 5 * c:6 * c]
    q = u[:, 6 * c:7 * c] * ATT_SCALE
    k = u[:, 7 * c:7 * c + c // 2]
    v = u[:, 7 * c + c // 2:8 * c]
    uq_ref[0] = q
    if rope:
        uqr_ref, ukv_ref = outs[4:]
        cs = cos_ref[...]
        sn = sin_ref[...]
        uqr_ref[0] = q * cs + _swap_halves(q) * sn
        kr = k * cs[:, :c // 2] + _swap_halves(k) * sn[:, :c // 2]
        ukv_ref[0] = jnp.concatenate([kr, v], axis=1)
    else:
        outs[4][0] = u[:, 7 * c:8 * c]


def in_projection(x, mods, mod_row0, norm_g, w_in_bf, *, res=None, rope=None, tm):
    b, l, d = x.shape
    c = MIX_CH
    grid = (b, l // tm)
    row = (lambda bb: 0) if mod_row0 is None else (lambda bb: mod_row0 + bb)
    xspec = pl.BlockSpec((1, tm, d), lambda bb, i: (bb, i, 0))
    mspec = pl.BlockSpec((1, N_MOD, d), lambda bb, i: (row(bb), 0, 0))
    args, specs = [x], [xspec]
    if res is not None:
        rargs, rspecs = _residual_specs(res, tm, d, row)
        args += rargs
        specs += rspecs
    args += [mods, norm_g.reshape(1, d), w_in_bf]
    specs += [mspec, pl.BlockSpec((1, d), lambda bb, i: (0, 0)),
              pl.BlockSpec(w_in_bf.shape, lambda bb, i: (0, 0))]
    if rope is not None:
        args += [rope[0], rope[1]]
        specs += [pl.BlockSpec((tm, c), lambda bb, i: (i, 0))] * 2

    def ospec(w):
        return pl.BlockSpec((1, tm, w), lambda bb, i: (bb, i, 0))

    out_shape, out_specs = [], []
    if res is not None:
        out_shape.append(jax.ShapeDtypeStruct((b, l, d), F32))
        out_specs.append(xspec)
    widths = [(2 * c, F32), (3 * c, F32), (c, F32), (c, F32)] + ([(c, F32)] if rope is not None else []) + [(c, F32)]
    for w, dt in widths:
        out_shape.append(jax.ShapeDtypeStruct((b, l, w), dt))
        out_specs.append(ospec(w))
    return pl.pallas_call(
        functools.partial(_in_body, fuse_res=res is not None, rope=rope is not None),
        grid=grid, in_specs=specs, out_specs=out_specs, out_shape=out_shape,
        compiler_params=_params("arbitrary", "arbitrary"),
        name="in_proj",
    )(*args)


def _dw_tile(win, w_ref, n_taps, first, rows):
    acc = w_ref[0:1, :] * win[first:first + rows]
    for k in range(1, n_taps):
        acc = acc + w_ref[k:k + 1, :] * win[first + k:first + k + rows]
    return acc


CONV_PAD = 16
CONV_ROWS = 64


def _conf_body(u_ref, w_ref, b_ref, g_ref, beta_ref, o_ref, gp_ref, *, seq, n_taps):
    c = MIX_CH
    r = CONV_ROWS
    half = (n_taps - 1) // 2
    zero = jnp.zeros((CONV_PAD, c), F32)
    gp_ref[0:CONV_PAD, :] = zero
    gp_ref[CONV_PAD + seq:CONV_PAD + seq + CONV_PAD, :] = zero

    def fill(i, carry):
        r0 = pl.multiple_of(i * r, r)
        a = u_ref[0, pl.ds(r0, r), 0:c]
        g = u_ref[0, pl.ds(r0, r), c:2 * c]
        gp_ref[pl.ds(CONV_PAD + r0, r), :] = a * jax.nn.sigmoid(g)
        return carry

    lax.fori_loop(0, seq // r, fill, 0)

    def tile(i, carry):
        r0 = pl.multiple_of(i * r, r)
        win = gp_ref[pl.ds(r0, r + 2 * CONV_PAD), :]
        z = _dw_tile(win, w_ref, n_taps, CONV_PAD - half, r) + b_ref[...]
        mu = jnp.mean(z, axis=-1, keepdims=True)
        zc = z - mu
        var = jnp.mean(zc * zc, axis=-1, keepdims=True)
        zn = zc * lax.rsqrt(var + EPS) * g_ref[...] + beta_ref[...]
        o_ref[0, pl.ds(r0, r), :] = _silu(zn)
        return carry

    lax.fori_loop(0, seq // r, tile, 0)


def conformer_conv(ua, dw_w, dw_b, ln_g, ln_b):
    b, l, c2 = ua.shape
    c = MIX_CH
    k = dw_w.shape[0]
    vec = pl.BlockSpec((1, c), lambda bb: (0, 0))
    return pl.pallas_call(
        functools.partial(_conf_body, seq=l, n_taps=k),
        grid=(b,),
        in_specs=[pl.BlockSpec((1, l, c2), lambda bb: (bb, 0, 0)),
                  pl.BlockSpec((k, c), lambda bb: (0, 0)), vec, vec, vec],
        out_specs=pl.BlockSpec((1, l, c), lambda bb: (bb, 0, 0)),
        out_shape=jax.ShapeDtypeStruct((b, l, c), F32),
        scratch_shapes=[pltpu.VMEM((l + 2 * CONV_PAD, c), F32)],
        compiler_params=_params("arbitrary"),
        name="conformer",
    )(ua, dw_w, dw_b.reshape(1, c), ln_g.reshape(1, c), ln_b.reshape(1, c))


SHORT_PAD = 8


def _short_body(u_ref, w_ref, b_ref, o_ref, xp_ref, *, seq, n_taps):
    c = MIX_CH
    r = CONV_ROWS
    half = (n_taps - 1) // 2
    zero = jnp.zeros((SHORT_PAD, c), F32)
    xp_ref[0:SHORT_PAD, :] = zero
    xp_ref[SHORT_PAD + seq:SHORT_PAD + seq + SHORT_PAD, :] = zero

    def fill(i, carry):
        r0 = pl.multiple_of(i * r, r)
        xp_ref[pl.ds(SHORT_PAD + r0, r), :] = u_ref[0, pl.ds(r0, r), :]
        return carry

    lax.fori_loop(0, seq // r, fill, 0)

    def tile(i, carry):
        r0 = pl.multiple_of(i * r, r)
        win = xp_ref[pl.ds(r0, r + 2 * SHORT_PAD), :]
        o_ref[0, pl.ds(r0, r), :] = _dw_tile(win, w_ref, n_taps, SHORT_PAD - half, r) + b_ref[...]
        return carry

    lax.fori_loop(0, seq // r, tile, 0)


def hyena_short_conv(ub, short_w, short_b):
    b, l, c3 = ub.shape
    c = MIX_CH
    k = short_w.shape[0]
    return pl.pallas_call(
        functools.partial(_short_body, seq=l, n_taps=k),
        grid=(b, c3 // c),
        in_specs=[pl.BlockSpec((1, l, c), lambda bb, j: (bb, 0, j)),
                  pl.BlockSpec((k, c), lambda bb, j: (0, j)),
                  pl.BlockSpec((1, c), lambda bb, j: (0, j))],
        out_specs=pl.BlockSpec((1, l, c), lambda bb, j: (bb, 0, j)),
        out_shape=jax.ShapeDtypeStruct((b, l, c3), F32),
        scratch_shapes=[pltpu.VMEM((l + 2 * SHORT_PAD, c), F32)],
        compiler_params=_params("arbitrary", "arbitrary"),
        name="hyena_short",
    )(ub, short_w, short_b.reshape(1, c3))


FFT_UNROLL = 8


def _fft_sizes(seq):
    n = 2 * seq
    n2 = 128 if n >= 4096 else 32
    return n, n // n2, n2


def _stage1_tables(n, n1, n2, h):
    k1 = np.arange(n1)[None, :, None]
    i1 = np.arange(h)[None, None, :]
    i2 = np.arange(n2)[:, None, None]
    th = 2.0 * np.pi * ((k1 * (i1 * n2 + i2)) % n) / n
    cs, sn = np.cos(th), np.sin(th)
    return np.concatenate([np.concatenate([cs, sn], 2), np.concatenate([-sn, cs], 2)], 1).astype(np.float32)


FFT_GROUP = SUBLANES


def _kron_matrices(n, n1, h):
    k1 = np.arange(n1)[:, None]
    i1 = np.arange(h)[None, :]
    th = 2.0 * np.pi * ((k1 * i1) % n1) / n1
    eye = np.eye(FFT_GROUP)
    cs, sn = np.kron(np.cos(th), eye), np.kron(np.sin(th), eye)
    fwd = np.block([[cs, sn], [-sn, cs]])
    inv = np.block([[cs.T, -sn.T], [sn.T, cs.T]]) / n
    return fwd.astype(np.float32), inv.astype(np.float32)


def _twiddle_tables(n, n1, n2):
    k1 = jnp.arange(n1, dtype=jnp.int32)[:, None]
    i2 = jnp.arange(n2, dtype=jnp.int32)[None, :]
    th = ((k1 * i2) % n).astype(F32) * (2.0 * math.pi / n)
    shape = (n1, n2 // FFT_GROUP, FFT_GROUP, LANES)
    full = lambda a: jnp.broadcast_to(a.reshape(shape[:3] + (1,)), shape)
    return full(jnp.cos(th)), full(jnp.sin(th))


def _mid_tables(n2):
    a = np.arange(n2)
    th = 2.0 * np.pi * ((a[:, None] * a[None, :]) % n2) / n2
    cs, sn = np.cos(th), np.sin(th)
    fwd = np.concatenate([np.concatenate([cs, sn], 1), np.concatenate([-sn, cs], 1)], 0)
    inv = np.concatenate([np.concatenate([cs, -sn], 1), np.concatenate([sn, cs], 1)], 0)
    return fwd.astype(np.float32), inv.astype(np.float32)


def _dot(a, b, precise):
    if precise:
        return jnp.dot(a, b, preferred_element_type=F32, precision=lax.Precision.HIGHEST)
    return jnp.dot(a.astype(BF16), b.astype(BF16), preferred_element_type=F32)


def _stage1_body(zr_ref, zi_ref, t_ref, o_ref, *, n1, n2, h, precise):
    def step(i2, carry):
        xr = zr_ref[0, pl.ds(i2, h, stride=n2), :]
        xi = zi_ref[0, pl.ds(i2, h, stride=n2), :]
        a = _dot(t_ref[i2], jnp.concatenate([xr, xi], axis=0), precise)
        o_ref[0, 0, pl.ds(i2, n1, stride=n2), :] = a[:n1]
        o_ref[0, 1, pl.ds(i2, n1, stride=n2), :] = a[n1:]
        return carry

    lax.fori_loop(0, n2, step, 0, unroll=FFT_UNROLL)


def fft_stage1(zsrc, col_block, n_pairs, imag_offset, table, *, n1, n2, h, precise=False):
    rows = h * n2
    c = MIX_CH
    nc = c // LANES
    return pl.pallas_call(
        functools.partial(_stage1_body, n1=n1, n2=n2, h=h, precise=precise),
        grid=(n_pairs, nc),
        in_specs=[pl.BlockSpec((1, rows, LANES), lambda p, j: (p, 0, col_block * nc + j)),
                  pl.BlockSpec((1, rows, LANES), lambda p, j: (p + imag_offset, 0, col_block * nc + j)),
                  pl.BlockSpec(table.shape, lambda p, j: (0, 0, 0))],
        out_specs=pl.BlockSpec((1, 2, n1 * n2, LANES), lambda p, j: (p, 0, 0, j)),
        out_shape=jax.ShapeDtypeStruct((n_pairs, 2, n1 * n2, c), F32),
        compiler_params=_params("arbitrary", "arbitrary"),
        name="fft_stage1",
    )(zsrc, zsrc, table)


def _mid_body(a_ref, k_ref, f_ref, g_ref, o_ref, *, n2, kc):
    for j in range(kc):
        rows = slice(j * n2, (j + 1) * n2)
        blk = jnp.concatenate([a_ref[0, 0, rows, :], a_ref[0, 1, rows, :]], axis=0)
        s = _dot(f_ref[...], blk, False)
        sr, si = s[:n2], s[n2:]
        kr, ki = k_ref[0, rows, :], k_ref[1, rows, :]
        y = jnp.concatenate([sr * kr - si * ki, sr * ki + si * kr], axis=0)
        bb = _dot(g_ref[...], y, False)
        o_ref[0, 0, rows, :] = bb[:n2]
        o_ref[0, 1, rows, :] = bb[n2:]


def fft_mid(a, kf, order, f_fwd, f_inv, *, n1, n2):
    p = a.shape[0]
    c = MIX_CH
    kc = min(n1, 8)
    rows = kc * n2
    blk = pl.BlockSpec((1, 2, rows, c), lambda j, pp: (pp, 0, j, 0))
    mat = pl.BlockSpec(f_fwd.shape, lambda j, pp: (0, 0))
    return pl.pallas_call(
        functools.partial(_mid_body, n2=n2, kc=kc),
        grid=(n1 // kc, p),
        in_specs=[blk, pl.BlockSpec((2, rows, c), lambda j, pp: (0, j, order)), mat, mat],
        out_specs=blk,
        out_shape=jax.ShapeDtypeStruct(a.shape, F32),
        compiler_params=_params("arbitrary", "arbitrary"),
        name="fft_mid",
    )(a, kf, f_fwd, f_inv)


def _mid_fwd_body(a_ref, f_ref, o_ref, *, n2, kc):
    for j in range(kc):
        rows = slice(j * n2, (j + 1) * n2)
        blk = jnp.concatenate([a_ref[0, 0, rows, :], a_ref[0, 1, rows, :]], axis=0)
        s = _dot(f_ref[...], blk, True)
        o_ref[0, 0, rows, :] = s[:n2]
        o_ref[0, 1, rows, :] = s[n2:]


def fft_mid_forward(a, f_fwd, *, n1, n2):
    p = a.shape[0]
    c = MIX_CH
    kc = min(n1, 8)
    rows = kc * n2
    blk = pl.BlockSpec((1, 2, rows, c), lambda pp, j: (pp, 0, j, 0))
    return pl.pallas_call(
        functools.partial(_mid_fwd_body, n2=n2, kc=kc),
        grid=(p, n1 // kc),
        in_specs=[blk, pl.BlockSpec(f_fwd.shape, lambda pp, j: (0, 0))],
        out_specs=blk,
        out_shape=jax.ShapeDtypeStruct(a.shape, F32),
        compiler_params=_params("arbitrary", "arbitrary"),
        name="fft_mid_fwd",
    )(a, f_fwd)


def _lanes(t, width):
    return t if width == LANES else jnp.concatenate([t] * (width // LANES), axis=-1)


def _kron1_body(zr_ref, zi_ref, m_ref, c_ref, s_ref, o_ref, *, n1, h, gs):
    j, c = FFT_GROUP, MIX_CH
    for s in range(gs):
        xr = zr_ref[0, :, s].reshape(h * j, c)
        xi = zi_ref[0, :, s].reshape(h * j, c)
        a = _dot(m_ref[...], jnp.concatenate([xr, xi], axis=0), False)
        ar = a[:n1 * j].reshape(n1, j, c)
        ai = a[n1 * j:].reshape(n1, j, c)
        cs, sn = _lanes(c_ref[:, s], c), _lanes(s_ref[:, s], c)
        o_ref[0, 0, :, s] = ar * cs + ai * sn
        o_ref[0, 1, :, s] = ai * cs - ar * sn


def _group_step(n_groups):
    return min(n_groups, 4)


def kron_stage1(z5, col, n_pairs, imag_offset, mat, twc, tws, *, n1, h):
    g, j, c = z5.shape[2], FFT_GROUP, MIX_CH
    gs = _group_step(g)
    tw = pl.BlockSpec((n1, gs, j, LANES), lambda gi, p: (0, gi, 0, 0))
    return pl.pallas_call(
        functools.partial(_kron1_body, n1=n1, h=h, gs=gs),
        grid=(g // gs, n_pairs),
        in_specs=[pl.BlockSpec((1, h, gs, j, c), lambda gi, p: (p, 0, gi, 0, col)),
                  pl.BlockSpec((1, h, gs, j, c), lambda gi, p: (p + imag_offset, 0, gi, 0, col)),
                  pl.BlockSpec(mat.shape, lambda gi, p: (0, 0)), tw, tw],
        out_specs=pl.BlockSpec((1, 2, n1, gs, j, c), lambda gi, p: (p, 0, 0, gi, 0, 0)),
        out_shape=jax.ShapeDtypeStruct((n_pairs, 2, n1, g, j, c), F32),
        compiler_params=_params("arbitrary", "arbitrary"),
        name="fft_kron1",
    )(z5, z5, mat, twc, tws)


def _kron3_body(b_ref, m_ref, c_ref, s_ref, zr_ref, zi_ref, gr_ref, gi_ref, d_ref, o_ref, *, n1, h, gs):
    j, c = FFT_GROUP, MIX_CH
    d = d_ref[...].reshape(1, 1, c)
    for s in range(gs):
        br, bi = b_ref[0, 0, :, s], b_ref[0, 1, :, s]
        cs, sn = _lanes(c_ref[:, s], c), _lanes(s_ref[:, s], c)
        xr = (br * cs - bi * sn).reshape(n1 * j, c)
        xi = (br * sn + bi * cs).reshape(n1 * j, c)
        y = _dot(m_ref[...], jnp.concatenate([xr, xi], axis=0), False)
        yr = y[:h * j].reshape(h, j, c)
        yi = y[h * j:].reshape(h, j, c)
        o_ref[0, 0, :, s] = gr_ref[0, :, s] * (yr + d * zr_ref[0, :, s])
        o_ref[1, 0, :, s] = gi_ref[0, :, s] * (yi + d * zi_ref[0, :, s])


def kron_stage3(bw, mat, twc, tws, z5, z_col, g5, g_col, d_vec, imag_offset, *, n1, h):
    p = bw.shape[0]
    g, j, c = bw.shape[3], FFT_GROUP, MIX_CH
    gs = _group_step(g)
    tw = pl.BlockSpec((n1, gs, j, LANES), lambda gi, pp: (0, gi, 0, 0))

    def src(col, off):
        return pl.BlockSpec((1, h, gs, j, c), lambda gi, pp: (pp + off, 0, gi, 0, col))

    return pl.pallas_call(
        functools.partial(_kron3_body, n1=n1, h=h, gs=gs),
        grid=(g // gs, p),
        in_specs=[pl.BlockSpec((1, 2, n1, gs, j, c), lambda gi, pp: (pp, 0, 0, gi, 0, 0)),
                  pl.BlockSpec(mat.shape, lambda gi, pp: (0, 0)), tw, tw,
                  src(z_col, 0), src(z_col, imag_offset), src(g_col, 0), src(g_col, imag_offset),
                  pl.BlockSpec((1, c), lambda gi, pp: (0, 0))],
        out_specs=pl.BlockSpec((2, 1, h, gs, j, c), lambda gi, pp: (0, pp, 0, gi, 0, 0)),
        out_shape=jax.ShapeDtypeStruct((2, p, h, g, j, c), F32),
        compiler_params=_params("arbitrary", "arbitrary"),
        name="fft_kron3",
    )(bw, mat, twc, tws, z5, z5, g5, g5, d_vec.reshape(1, c))


def hyena_filter_time(seq, fw1, fb1, fr1, fw2, fb2, fr2, fw3, fb3):
    hp = lax.Precision.HIGHEST
    t = jnp.linspace(0.0, 1.0, seq, dtype=F32)[:, None]
    n_bands = (fw1.shape[0] - 1) // 2
    w = 2.0 * math.pi * jnp.arange(seq, dtype=F32)[:, None] / seq
    fr = jnp.linspace(1e-4, n_bands - 1, n_bands, dtype=F32)[None, :]
    z = jnp.concatenate([t, jnp.cos(fr * w), -jnp.sin(fr * w)], axis=-1)
    h = jnp.sin(fr1 * (jnp.dot(z, fw1, precision=hp) + fb1))
    h = jnp.sin(fr2 * (jnp.dot(h, fw2, precision=hp) + fb2))
    c = MIX_CH
    h = (jnp.dot(h, fw3, precision=hp) + fb3).reshape(seq, 2, 2, c)
    max_decay = math.log(HYENA_DECAY_TARGET) / HYENA_FAST_PCT
    min_decay = math.log(HYENA_DECAY_TARGET) / HYENA_SLOW_PCT
    deltas = jnp.linspace(min_decay, max_decay, c, dtype=F32)
    window = jnp.exp(-t * jnp.abs(deltas)[None, :]) + HYENA_SHIFT
    h = h * window[:, None, None, :]
    hf, hb = h[:, 0], h[:, 1]
    k = jnp.concatenate([hf[:1] + hb[:1], hf[1:], jnp.zeros_like(hf[:1]), hb[:0:-1]], axis=0)
    k = k / (jnp.sum(jnp.abs(k), axis=0, keepdims=True) + EPS)
    return k.reshape(2 * seq, 2 * c)


def hyena_filter_spectrum(seq, filt):
    n, n1, n2 = _fft_sizes(seq)
    k = hyena_filter_time(seq, *filt)
    src = jnp.stack([k, jnp.zeros_like(k)], axis=0)
    t1 = jnp.asarray(_stage1_tables(n, n1, n2, n1))
    f_fwd, _ = _mid_tables(n2)
    planes = []
    for o in range(2):
        a = fft_stage1(src, o, 1, 1, t1, n1=n1, n2=n2, h=n1, precise=True)
        planes.append(fft_mid_forward(a, jnp.asarray(f_fwd), n1=n1, n2=n2)[0])
    return jnp.concatenate(planes, axis=-1)


def hyena_mixer(ub, short_w, short_b, hy_d, kf, tabs):
    b, l, _ = ub.shape
    n, n1, n2 = _fft_sizes(l)
    h = n1 // 2
    p = b // 2
    c = MIX_CH
    g = n2 // FFT_GROUP
    u = hyena_short_conv(ub, short_w, short_b)
    u5 = u.reshape(b, h, g, FFT_GROUP, u.shape[-1])
    m_fwd, m_inv, twc, tws, f_fwd, f_inv = tabs
    z5 = u5
    for o in range(2):
        a = kron_stage1(z5, 0, p, p, m_fwd, twc, tws, n1=n1, h=h)
        bw = fft_mid(a.reshape(p, 2, n, c), kf, o, f_fwd, f_inv, n1=n1, n2=n2)
        z = kron_stage3(bw.reshape(a.shape), m_inv, twc, tws, z5, 0, u5, 1 + o, hy_d[o], p, n1=n1, h=h)
        z5 = z.reshape(b, h, g, FFT_GROUP, c)
    return z5.reshape(b, l, c)


def hyena_tables(seq):
    n, n1, n2 = _fft_sizes(seq)
    m_fwd, m_inv = _kron_matrices(n, n1, n1 // 2)
    twc, tws = _twiddle_tables(n, n1, n2)
    f_fwd, f_inv = _mid_tables(n2)
    bf = lambda a: jnp.asarray(a).astype(BF16)
    return bf(m_fwd), bf(m_inv), twc, tws, bf(f_fwd), bf(f_inv)


def _fnet_body(cl_ref, sl_ref, x_ref, cc_ref, sc_ref, o_ref, *, scale):
    x = x_ref[0].astype(BF16)
    pr = jnp.dot(cl_ref[...], x, preferred_element_type=F32).astype(BF16)
    qr = jnp.dot(sl_ref[...], x, preferred_element_type=F32).astype(BF16)
    o_ref[0] = (jnp.dot(pr, cc_ref[...], preferred_element_type=F32)
                - jnp.dot(qr, sc_ref[...], preferred_element_type=F32)) * scale


def _dft_tables(n):
    a = jnp.arange(n, dtype=jnp.int32)
    th = ((a[:, None] * a[None, :]) % n).astype(F32) * (2.0 * math.pi / n)
    return jnp.cos(th).astype(BF16), jnp.sin(th).astype(BF16)


FNET_DIRECT_MAX = 1024


def _fnet1_body(x_ref, ccs_ref, m_ref, c_ref, s_ref, o_ref, *, n1, gs):
    j, c = FFT_GROUP, MIX_CH
    for s in range(gs):
        x = x_ref[0, :, s].reshape(n1 * j, c).astype(BF16)
        z = jnp.dot(x, ccs_ref[...], preferred_element_type=F32)
        a = _dot(m_ref[...], jnp.concatenate([z[:, :c], z[:, c:]], axis=0), False)
        ar = a[:n1 * j].reshape(n1, j, c)
        ai = a[n1 * j:].reshape(n1, j, c)
        cs, sn = _lanes(c_ref[:, s], c), _lanes(s_ref[:, s], c)
        o_ref[0, 0, :, s] = ar * cs + ai * sn
        o_ref[0, 1, :, s] = ai * cs - ar * sn


def _fnet2_body(a_ref, m_ref, o_ref, *, n2, scale):
    j, c = FFT_GROUP, MIX_CH
    x = a_ref[0].reshape(2 * j * n2, c)
    y = _dot(m_ref[...], x, False) * scale
    o_ref[0, :, 0] = y.reshape(n2, j, c)


def fnet_tables(seq):
    c = MIX_CH
    cc, sc = _dft_tables(c)
    if seq <= FNET_DIRECT_MAX:
        return _dft_tables(seq) + (cc, sc)
    n2 = 128
    n1 = seq // n2
    m_fwd, _ = _kron_matrices(seq, n1, n1)
    twc, tws = _twiddle_tables(seq, n1, n2)
    a = np.arange(n2)
    th = 2.0 * np.pi * ((a[:, None] * a[None, :]) % n2) / n2
    eye = np.eye(FFT_GROUP)
    m2 = np.concatenate([np.einsum('kn,ij->kijn', f, eye).reshape(n2 * FFT_GROUP, FFT_GROUP * n2)
                         for f in (np.cos(th), np.sin(th))], axis=1).astype(np.float32)
    ccs = jnp.concatenate([cc, -sc], axis=1)
    return ccs, jnp.asarray(m_fwd).astype(BF16), twc, tws, jnp.asarray(m2).astype(BF16)


def fnet_two_stage(uc, tables):
    b, l, c = uc.shape
    ccs, m_fwd, twc, tws, m2 = tables
    j = FFT_GROUP
    n2 = 128
    n1 = l // n2
    g = n2 // j
    gs = _group_step(g)
    tw = pl.BlockSpec((n1, gs, j, LANES), lambda gi, bb: (0, gi, 0, 0))
    a = pl.pallas_call(
        functools.partial(_fnet1_body, n1=n1, gs=gs),
        grid=(g // gs, b),
        in_specs=[pl.BlockSpec((1, n1, gs, j, c), lambda gi, bb: (bb, 0, gi, 0, 0)),
                  pl.BlockSpec(ccs.shape, lambda gi, bb: (0, 0)),
                  pl.BlockSpec(m_fwd.shape, lambda gi, bb: (0, 0)), tw, tw],
        out_specs=pl.BlockSpec((1, 2, n1, gs, j, c), lambda gi, bb: (bb, 0, 0, gi, 0, 0)),
        out_shape=jax.ShapeDtypeStruct((b, 2, n1, g, j, c), F32),
        compiler_params=_params("arbitrary", "arbitrary"),
        name="fnet_stage1",
    )(uc.reshape(b, n1, g, j, c), ccs, m_fwd, twc, tws)
    out = pl.pallas_call(
        functools.partial(_fnet2_body, n2=n2, scale=1.0 / math.sqrt(l * c)),
        grid=(b, n1 // j),
        in_specs=[pl.BlockSpec((1, 2, j, n2, c), lambda bb, q: (bb, 0, q, 0, 0)),
                  pl.BlockSpec(m2.shape, lambda bb, q: (0, 0))],
        out_specs=pl.BlockSpec((1, n2, 1, j, c), lambda bb, q: (bb, 0, q, 0, 0)),
        out_shape=jax.ShapeDtypeStruct((b, n2, n1 // j, j, c), F32),
        compiler_params=_params("arbitrary", "arbitrary"),
        name="fnet_stage2",
    )(a.reshape(b, 2, n1, n2, c), m2)
    return out.reshape(b, l, c)


def fnet_mixer(uc, tables):
    b, l, c = uc.shape
    if l > FNET_DIRECT_MAX:
        return fnet_two_stage(uc, tables)
    cl, sl, cc, sc = tables
    tm = min(l, 512)
    row = pl.BlockSpec((tm, l), lambda i, bb: (i, 0))
    sq = pl.BlockSpec((c, c), lambda i, bb: (0, 0))
    return pl.pallas_call(
        functools.partial(_fnet_body, scale=1.0 / math.sqrt(l * c)),
        grid=(l // tm, b),
        in_specs=[row, row, pl.BlockSpec((1, l, c), lambda i, bb: (bb, 0, 0)), sq, sq],
        out_specs=pl.BlockSpec((1, tm, c), lambda i, bb: (bb, i, 0)),
        out_shape=jax.ShapeDtypeStruct((b, l, c), F32),
        compiler_params=_params("arbitrary", "arbitrary"),
        name="fnet",
    )(cl, sl, uc, cc, sc)


def _heads_rows(x, g):
    h0 = Q_PER_KV * g
    return jnp.concatenate([x[:, (h0 + r) * HEAD_DIM:(h0 + r + 1) * HEAD_DIM] for r in range(Q_PER_KV)], axis=0)


def _qk(q, k):
    return lax.dot_general(q.astype(BF16), k.astype(BF16), (((1,), (1,)), ((), ())),
                           preferred_element_type=F32)


def _sink_col(sink_ref, g, rows):
    ridx = lax.broadcasted_iota(jnp.int32, (Q_PER_KV * rows, 1), 0)
    col = jnp.full((Q_PER_KV * rows, 1), sink_ref[Q_PER_KV * g], F32)
    for r in range(1, Q_PER_KV):
        col = jnp.where(ridx >= r * rows, sink_ref[Q_PER_KV * g + r], col)
    return col


def _lat_attn_body(sink_ref, q_ref, qr_ref, kp_ref, kc_ref, kn_ref, ck_ref, cv_ref, o_ref):
    i = pl.program_id(1)
    nblk = pl.num_programs(1)
    blk = ATT_BLOCK
    span = blk + 2 * WINDOW
    q = q_ref[0]
    qr = qr_ref[0]
    kv = jnp.concatenate([kp_ref[0], kc_ref[0], kn_ref[0]], axis=0)
    ck = ck_ref[0, 0]
    cv = cv_ref[0, 0]
    kvw = N_KV_HEADS * HEAD_DIM
    r = lax.broadcasted_iota(jnp.int32, (Q_PER_KV * blk, span), 0) % blk
    j = lax.broadcasted_iota(jnp.int32, (Q_PER_KV * blk, span), 1)
    ok = (j >= r) & (j <= r + 2 * WINDOW)
    ok = ok & ((i > 0) | (j >= WINDOW)) & ((i < nblk - 1) | (j < WINDOW + blk))
    outs = []
    for g in range(N_KV_HEADS):
        kl = kv[:, g * HEAD_DIM:(g + 1) * HEAD_DIM]
        vl = kv[:, kvw + g * HEAD_DIM:kvw + (g + 1) * HEAD_DIM]
        s_loc = jnp.where(ok, _qk(_heads_rows(qr, g), kl), NEG_INF)
        s_ctx = _qk(_heads_rows(q, g), ck[:, g * HEAD_DIM:(g + 1) * HEAD_DIM])
        sink = _sink_col(sink_ref, g, blk)
        m = jnp.maximum(jnp.maximum(jnp.max(s_loc, axis=-1, keepdims=True),
                                    jnp.max(s_ctx, axis=-1, keepdims=True)), sink)
        e_loc = jnp.exp(s_loc - m)
        e_ctx = jnp.exp(s_ctx - m)
        den = jnp.sum(e_loc, axis=-1, keepdims=True) + jnp.sum(e_ctx, axis=-1, keepdims=True) + jnp.exp(sink - m)
        o = (jnp.dot(e_loc.astype(BF16), vl.astype(BF16), preferred_element_type=F32)
             + jnp.dot(e_ctx.astype(BF16), cv[:, g * HEAD_DIM:(g + 1) * HEAD_DIM].astype(BF16),
                       preferred_element_type=F32)) * (1.0 / den)
        outs += [o[rr * blk:(rr + 1) * blk] for rr in range(Q_PER_KV)]
    o_ref[0] = jnp.concatenate(outs, axis=1)


def latent_attention(uq, uqr, ukv, cache_k, cache_v, layer, sink):
    b, l, c = uq.shape
    p = cache_k.shape[2]
    blk = ATT_BLOCK
    nblk = l // blk
    qspec = pl.BlockSpec((1, blk, c), lambda bb, i: (bb, i, 0))
    cspec = pl.BlockSpec((1, 1, p, cache_k.shape[3]), lambda bb, i: (bb, layer, 0, 0))
    return pl.pallas_call(
        _lat_attn_body,
        grid=(b, nblk),
        in_specs=[pl.BlockSpec(memory_space=pltpu.SMEM), qspec, qspec,
                  pl.BlockSpec((1, blk, c), lambda bb, i: (bb, jnp.maximum(i - 1, 0), 0)),
                  qspec,
                  pl.BlockSpec((1, blk, c), lambda bb, i: (bb, jnp.minimum(i + 1, nblk - 1), 0)),
                  cspec, cspec],
        out_specs=qspec,
        out_shape=jax.ShapeDtypeStruct((b, l, c), F32),
        compiler_params=_params("arbitrary", "arbitrary"),
        name="latent_attention",
    )(sink, uq, uqr, ukv, ukv, ukv, cache_k, cache_v)


def _ctx_attn_body(sink_ref, q_ref, kv_ref, o_ref, *, seq):
    q = q_ref[0]
    kv = kv_ref[0]
    kvw = N_KV_HEADS * HEAD_DIM
    outs = []
    for g in range(N_KV_HEADS):
        kl = kv[:, g * HEAD_DIM:(g + 1) * HEAD_DIM]
        vl = kv[:, kvw + g * HEAD_DIM:kvw + (g + 1) * HEAD_DIM]
        s = _qk(_heads_rows(q, g), kl)
        sink = _sink_col(sink_ref, g, seq)
        m = jnp.maximum(jnp.max(s, axis=-1, keepdims=True), sink)
        e = jnp.exp(s - m)
        den = jnp.sum(e, axis=-1, keepdims=True) + jnp.exp(sink - m)
        o = jnp.dot(e.astype(BF16), vl.astype(BF16), preferred_element_type=F32) * (1.0 / den)
        outs += [o[rr * seq:(rr + 1) * seq] for rr in range(Q_PER_KV)]
    o_ref[0] = jnp.concatenate(outs, axis=1)


def context_attention(uq, ukv, sink):
    b, s, c = uq.shape
    spec = pl.BlockSpec((1, s, c), lambda bb: (bb, 0, 0))
    return pl.pallas_call(
        functools.partial(_ctx_attn_body, seq=s),
        grid=(b,),
        in_specs=[pl.BlockSpec(memory_space=pltpu.SMEM), spec, spec],
        out_specs=spec,
        out_shape=jax.ShapeDtypeStruct((b, s, c), F32),
        compiler_params=_params("arbitrary"),
        name="context_attention",
    )(sink, uq, ukv)


def _pack_bf16_pairs(hi_rounded):
    k = hi_rounded.shape[1] // 2
    bits = lax.bitcast_convert_type(hi_rounded, jnp.uint32)
    return bits[:, :k] | (bits[:, k:] >> 16)


def _unpack_bf16_pairs(packed):
    a = lax.bitcast_convert_type(packed & jnp.uint32(0xFFFF0000), F32)
    b = lax.bitcast_convert_type(packed << 16, F32)
    return jnp.concatenate([a, b], axis=1).astype(BF16)


def _out_body(ya_ref, yb_ref, yc_ref, yd_ref, x_ref, mod_ref, g_ref, w_ref, rw_ref, rb_ref,
              x1_ref, h_ref, route_ref):
    c = MIX_CH
    y = jnp.dot(ya_ref[0].astype(BF16), w_ref[0:c, :], preferred_element_type=F32)
    for j, ref in enumerate((yb_ref, yc_ref, yd_ref), start=1):
        y = y + jnp.dot(ref[0].astype(BF16), w_ref[j * c:(j + 1) * c, :], preferred_element_type=F32)
    x1 = x_ref[0] + mod_ref[0, 2:3, :] * y
    x1_ref[0] = x1
    h = _rmsnorm_mod(x1, g_ref[...], mod_ref[0, 4:5, :], mod_ref[0, 3:4, :])
    h_hi = h.astype(BF16)
    h_hi32 = h_hi.astype(F32)
    h_ref[0] = _pack_bf16_pairs(h_hi32)
    h_lo = (h - h_hi32).astype(BF16)
    tm = h.shape[0]
    prod = jnp.dot(jnp.concatenate([h_hi, h_lo], axis=0), rw_ref[...], preferred_element_type=F32)
    logits = (prod[:tm, :ROUTE_LANES] + prod[:tm, ROUTE_LANES:]
              + prod[tm:, :ROUTE_LANES] + prod[tm:, ROUTE_LANES:]) + rb_ref[...]
    lane = lax.broadcasted_iota(jnp.int32, logits.shape, 1)
    is_c = lane < N_GROUPS
    lc = jnp.where(is_c, logits, NEG_INF)
    mc = jnp.max(lc, axis=-1, keepdims=True)
    grp = jnp.min(jnp.where(lc == mc, lane, ROUTE_LANES), axis=-1, keepdims=True)
    pg = 1.0 / jnp.sum(jnp.where(is_c, jnp.exp(lc - mc), 0.0), axis=-1, keepdims=True)
    lo = N_GROUPS + grp * EXPERTS_PER_GROUP
    in_g = (lane >= lo) & (lane < lo + EXPERTS_PER_GROUP)
    lf = jnp.where(in_g, logits, NEG_INF)
    t1 = jnp.max(lf, axis=-1, keepdims=True)
    i1 = jnp.min(jnp.where(lf == t1, lane, ROUTE_LANES), axis=-1, keepdims=True)
    lf2 = jnp.where(lane == i1, NEG_INF, lf)
    t2 = jnp.max(lf2, axis=-1, keepdims=True)
    i2 = jnp.min(jnp.where(lf2 == t2, lane, ROUTE_LANES), axis=-1, keepdims=True)
    e2 = jnp.exp(t2 - t1)
    w1 = pg / (1.0 + e2)
    w2 = pg * e2 / (1.0 + e2)
    rec = jnp.where(lane == 0, (i1 - N_GROUPS).astype(F32),
                    jnp.where(lane == 1, (i2 - N_GROUPS).astype(F32),
                              jnp.where(lane == 2, w1, jnp.where(lane == 3, w2, 0.0))))
    route_ref[0] = rec


def out_projection(ys, x, mods, mod_row0, norm_g, w_out_bf, rw, rb, *, tm):
    b, l, d = x.shape
    c = MIX_CH
    row = (lambda bb: 0) if mod_row0 is None else (lambda bb: mod_row0 + bb)
    yspec = pl.BlockSpec((1, tm, c), lambda bb, i: (bb, i, 0))
    xspec = pl.BlockSpec((1, tm, d), lambda bb, i: (bb, i, 0))
    return pl.pallas_call(
        _out_body,
        grid=(b, l // tm),
        in_specs=[yspec] * 4 + [xspec,
                                pl.BlockSpec((1, N_MOD, d), lambda bb, i: (row(bb), 0, 0)),
                                pl.BlockSpec((1, d), lambda bb, i: (0, 0)),
                                pl.BlockSpec(w_out_bf.shape, lambda bb, i: (0, 0)),
                                pl.BlockSpec(rw.shape, lambda bb, i: (0, 0)),
                                pl.BlockSpec(rb.shape, lambda bb, i: (0, 0))],
        out_specs=[xspec, pl.BlockSpec((1, tm, d // 2), lambda bb, i: (bb, i, 0)),
                   pl.BlockSpec((1, tm, ROUTE_LANES), lambda bb, i: (bb, i, 0))],
        out_shape=[jax.ShapeDtypeStruct((b, l, d), F32), jax.ShapeDtypeStruct((b, l, d // 2), jnp.uint32),
                   jax.ShapeDtypeStruct((b, l, ROUTE_LANES), F32)],
        compiler_params=_params("arbitrary", "arbitrary"),
        name="out_proj",
    )(*ys, x, mods, norm_g.reshape(1, d), w_out_bf, rw, rb)


def _expert_body(be_ref, nv_ref, xs_ref, wg_ref, wu_ref, wd_ref, o_ref, wg_s, wu_s, wd_s):
    i = pl.program_id(0)
    prev = be_ref[jnp.maximum(i - 1, 0)]

    @pl.when((i == 0) | (be_ref[i] != prev))
    def _():
        wg_s[...] = wg_ref[0].astype(BF16)
        wu_s[...] = wu_ref[0].astype(BF16)
        wd_s[...] = wd_ref[0].astype(BF16)

    @pl.when(nv_ref[i] > 0)
    def _():
        row = lax.broadcasted_iota(jnp.int32, xs_ref.shape, 0)
        x = _unpack_bf16_pairs(jnp.where(row < nv_ref[i], xs_ref[...], jnp.uint32(0)))
        g = jnp.dot(x, wg_s[...], preferred_element_type=F32)
        u = jnp.dot(x, wu_s[...], preferred_element_type=F32)
        a = (_silu(g) * u).astype(BF16)
        o_ref[...] = jnp.dot(a, wd_s[...], preferred_element_type=F32)

    @pl.when(nv_ref[i] <= 0)
    def _():
        o_ref[...] = jnp.zeros(o_ref.shape, F32)


def expert_ffn(xs, blk_e, n_valid, e_gate, e_up, e_down):
    rows, dh = xs.shape
    d = 2 * dh
    nb = rows // MOE_BLOCK
    de = e_gate.shape[-1]
    grid_spec = pltpu.PrefetchScalarGridSpec(
        num_scalar_prefetch=2,
        grid=(nb,),
        in_specs=[pl.BlockSpec((MOE_BLOCK, dh), lambda i, be, nv: (i, 0)),
                  pl.BlockSpec((1, d, de), lambda i, be, nv: (be[i], 0, 0)),
                  pl.BlockSpec((1, d, de), lambda i, be, nv: (be[i], 0, 0)),
                  pl.BlockSpec((1, de, d), lambda i, be, nv: (be[i], 0, 0))],
        out_specs=pl.BlockSpec((MOE_BLOCK, d), lambda i, be, nv: (i, 0)),
        scratch_shapes=[pltpu.VMEM((d, de), BF16), pltpu.VMEM((d, de), BF16), pltpu.VMEM((de, d), BF16)],
    )
    return pl.pallas_call(
        _expert_body, grid_spec=grid_spec,
        out_shape=jax.ShapeDtypeStruct((rows, d), F32),
        compiler_params=_params("arbitrary"),
        name="expert_ffn",
    )(blk_e, n_valid, xs, e_gate, e_up, e_down)


RANK_TILE = 512


def _rank_body(route_ref, rank_ref, cnt_ref, carry_ref):
    @pl.when(pl.program_id(0) == 0)
    def _():
        carry_ref[...] = jnp.zeros(carry_ref.shape, F32)

    t = RANK_TILE
    rec = route_ref[...]
    lane = lax.broadcasted_iota(jnp.int32, rec.shape, 1)
    lanef = lane.astype(F32)
    e0 = (lanef == rec[:, 0:1]).astype(F32)
    e1 = (lanef == rec[:, 1:2]).astype(F32)
    both = e0 + e1
    earlier = (lax.broadcasted_iota(jnp.int32, (t, t), 0) > lax.broadcasted_iota(jnp.int32, (t, t), 1))
    before = jnp.dot(earlier.astype(BF16), both.astype(BF16), preferred_element_type=F32) + carry_ref[0:1, :]
    r0 = jnp.sum(e0 * before, axis=-1, keepdims=True)
    r1 = jnp.sum(e1 * before, axis=-1, keepdims=True)
    rank_ref[...] = jnp.where(lane == 0, r0, jnp.where(lane == 1, r1, 0.0))
    carry_ref[...] = carry_ref[...] + jnp.sum(both, axis=0, keepdims=True)
    cnt_ref[...] = carry_ref[...]


def moe_rank(route):
    n = route.shape[0]
    t = RANK_TILE
    return pl.pallas_call(
        _rank_body,
        grid=(n // t,),
        in_specs=[pl.BlockSpec((t, ROUTE_LANES), lambda i: (i, 0))],
        out_specs=[pl.BlockSpec((t, ROUTE_LANES), lambda i: (i, 0)),
                   pl.BlockSpec((SUBLANES, ROUTE_LANES), lambda i: (0, 0))],
        out_shape=[jax.ShapeDtypeStruct((n, ROUTE_LANES), F32),
                   jax.ShapeDtypeStruct((SUBLANES, ROUTE_LANES), F32)],
        scratch_shapes=[pltpu.VMEM((SUBLANES, ROUTE_LANES), F32)],
        compiler_params=_params("arbitrary"),
        name="moe_rank",
    )(route)


def _sc_mesh():
    return plsc.VectorSubcoreMesh(core_axis_name="c", subcore_axis_name="s")


def _sc_worker():
    return lax.axis_index("s") * SC_CORES + lax.axis_index("c")


DISPATCH_ROWS = 64
COMBINE_ROWS = 32


def sc_dispatch(rows, dest, n_slots):
    n, w = rows.shape
    ch = DISPATCH_ROWS
    per_w = n // SC_WORKERS
    n_ch = per_w // ch

    @functools.partial(
        pl.kernel, mesh=_sc_mesh(),
        out_type=jax.ShapeDtypeStruct((n_slots, w), rows.dtype),
        scratch_types=[pltpu.VMEM((ch,), jnp.int32), pltpu.VMEM((ch, w), rows.dtype)],
    )
    def scatter_kernel(rows_hbm, dest_hbm, out_hbm, idx_v, rows_v):
        wid = _sc_worker()

        @pl.loop(0, n_ch)
        def _(j):
            chunk = wid * n_ch + j
            pltpu.sync_copy(rows_hbm.at[pl.ds(pl.multiple_of(chunk * ch, ch), ch)], rows_v)
            for k in range(2):
                pltpu.sync_copy(dest_hbm.at[k, chunk], idx_v)
                pltpu.sync_copy(rows_v, out_hbm.at[idx_v])

    return scatter_kernel(rows, dest)


def sc_gather_rows(table, idx):
    s, w = table.shape
    m = idx.shape[0]
    ch = COMBINE_ROWS
    per_w = m // SC_WORKERS
    n_ch = per_w // ch

    @functools.partial(
        pl.kernel, mesh=_sc_mesh(),
        out_type=jax.ShapeDtypeStruct((m, w), table.dtype),
        scratch_types=[pltpu.VMEM((ch,), jnp.int32), pltpu.VMEM((ch, w), table.dtype), pltpu.SemaphoreType.DMA],
    )
    def gather_kernel(table_hbm, idx_hbm, out_hbm, idx_v, rows_v, sem):
        wid = _sc_worker()

        @pl.loop(0, n_ch)
        def _(j):
            off = pl.multiple_of((wid * n_ch + j) * ch, ch)
            pltpu.sync_copy(idx_hbm.at[pl.ds(off, ch)], idx_v)
            pltpu.async_copy(table_hbm.at[idx_v], rows_v, sem).wait()
            pltpu.sync_copy(rows_v, out_hbm.at[pl.ds(off, ch)])

    return gather_kernel(table, idx)


def hier_moe(h_packed, route, e_gate, e_up, e_down):
    b, l, dh = h_packed.shape
    n = b * l
    assert n % (SC_WORKERS * DISPATCH_ROWS) == 0 and (2 * n) % (SC_WORKERS * COMBINE_ROWS) == 0
    route2 = route.reshape(n, ROUTE_LANES)
    rank, cnt = moe_rank(route2)
    counts = cnt[0, :N_EXPERTS].astype(jnp.int32)
    padded = (counts + MOE_BLOCK - 1) // MOE_BLOCK * MOE_BLOCK
    pend = jnp.cumsum(padded)
    pstart = pend - padded
    nb = -(-2 * n // MOE_BLOCK) + N_EXPERTS
    blk0 = jnp.arange(nb, dtype=jnp.int32) * MOE_BLOCK
    blk_e = jnp.minimum(jnp.sum(pend[None, :] <= blk0[:, None], axis=1), N_EXPERTS - 1).astype(jnp.int32)
    n_valid = jnp.clip(pstart[blk_e] + counts[blk_e] - blk0, 0, MOE_BLOCK).astype(jnp.int32)
    dest = (pstart[route2[:, 0:2].astype(jnp.int32)] + rank[:, 0:2].astype(jnp.int32)).T
    xs = sc_dispatch(h_packed.reshape(n, dh), dest.reshape(2, n // DISPATCH_ROWS, DISPATCH_ROWS), nb * MOE_BLOCK)
    y = expert_ffn(xs, blk_e, n_valid, e_gate, e_up, e_down)
    return sc_gather_rows(y, dest.reshape(2 * n)).reshape(2, b, l, 2 * dh)


def _final_body(x_ref, y0_ref, y1_ref, route_ref, pmod_ref, g_ref, o_ref):
    x = _moe_residual(x_ref[0], y0_ref, y1_ref, route_ref, pmod_ref)
    ms = jnp.mean(x * x, axis=-1, keepdims=True)
    o_ref[0] = x * lax.rsqrt(ms + EPS) * g_ref[...]


def final_norm(x1, res, mod_row0, norm_g, *, tm):
    b, l, d = x1.shape
    row = (lambda bb: 0) if mod_row0 is None else (lambda bb: mod_row0 + bb)
    xspec = pl.BlockSpec((1, tm, d), lambda bb, i: (bb, i, 0))
    rargs, rspecs = _residual_specs(res, tm, d, row)
    return pl.pallas_call(
        _final_body,
        grid=(b, l // tm),
        in_specs=[xspec] + rspecs + [pl.BlockSpec((1, d), lambda bb, i: (0, 0))],
        out_specs=xspec,
        out_shape=jax.ShapeDtypeStruct((b, l, d), F32),
        compiler_params=_params("arbitrary", "arbitrary"),
        name="final_norm",
    )(x1, *rargs, norm_g.reshape(1, d))


def _rope_tables(seq):
    rows = seq // GRID_W
    row_pos = jnp.repeat(jnp.arange(rows, dtype=F32), GRID_W)
    col_pos = jnp.tile(jnp.arange(GRID_W, dtype=F32), rows)
    n_freq = HEAD_DIM // 4
    inv = ROPE_BASE ** (-jnp.arange(n_freq, dtype=F32) / n_freq)
    ang = jnp.concatenate([row_pos[:, None] * inv, col_pos[:, None] * inv], axis=-1)
    cs, sn = jnp.cos(ang), jnp.sin(ang)
    cos_f = jnp.tile(jnp.concatenate([cs, cs], axis=-1), (1, N_Q_HEADS))
    sin_s = jnp.tile(jnp.concatenate([-sn, sn], axis=-1), (1, N_Q_HEADS))
    return cos_f, sin_s


def kernel(x_prompt, x_sample, cache_k, cache_v, c, c_ctx, ada_w, ada_b, norm1_g, norm2_g, w_in, conv_dw_w, conv_dw_b, conv_ln_g, conv_ln_b, hy_short_w, hy_short_b, hy_fw1, hy_fb1, hy_freq1, hy_fw2, hy_fb2, hy_freq2, hy_fw3, hy_fb3, hy_d, attn_sink, w_out, router_coarse_w, router_coarse_b, router_fine_w, router_fine_b, exp_gate, exp_up, exp_down, norm_f_g):
    depth = ada_w.shape[0]
    bp, lp, d = x_prompt.shape
    bs, ls, _ = x_sample.shape
    assert bp % 2 == 0 and bs % 2 == 0 and ls % ATT_BLOCK == 0 and ls % GRID_W == 0

    n_rows = -(-(1 + bs) // SUBLANES) * SUBLANES
    cvec = jnp.concatenate([c_ctx[None, :], c, jnp.zeros((n_rows - 1 - bs, d), F32)], axis=0)
    mods = adaln_all(cvec, ada_w, ada_b)

    rope = _rope_tables(ls)
    fnet_tabs, hy_tabs = {}, {}
    for seq in {lp, ls}:
        fnet_tabs[seq] = fnet_tables(seq)
        hy_tabs[seq] = hyena_tables(seq)
    ck = cache_k.reshape(cache_k.shape[0], depth, cache_k.shape[2], -1)
    cv = cache_v.reshape(cache_v.shape[0], depth, cache_v.shape[2], -1)
    pad = ROUTE_LANES - N_GROUPS - N_EXPERTS

    tm_p = min(lp, 512)
    tm_s = min(ls, 512)
    xp, xs = x_prompt, x_sample
    res_p = res_s = None
    ks_out, vs_out = [], []
    for l in range(depth):
        w_in_bf = w_in[l].astype(BF16)
        w_out_bf = w_out[l].astype(BF16)
        rw = jnp.concatenate([router_coarse_w[l], router_fine_w[l], jnp.zeros((d, pad), F32)], axis=1)
        rw_hi = rw.astype(BF16)
        rw = jnp.concatenate([rw_hi, (rw - rw_hi.astype(F32)).astype(BF16)], axis=1)
        rb = jnp.concatenate([router_coarse_b[l], router_fine_b[l], jnp.zeros((pad,), F32)])[None, :]
        filt = (hy_fw1[l], hy_fb1[l], hy_freq1[l], hy_fw2[l], hy_fb2[l], hy_freq2[l], hy_fw3[l], hy_fb3[l])
        sink = attn_sink[l]

        def mixers(ua, ub, uc, yd, seq):
            ya = conformer_conv(ua, conv_dw_w[l], conv_dw_b[l], conv_ln_g[l], conv_ln_b[l])
            yb = hyena_mixer(ub, hy_short_w[l], hy_short_b[l], hy_d[l], hyena_filter_spectrum(seq, filt),
                             hy_tabs[seq])
            yc = fnet_mixer(uc, fnet_tabs[seq])
            return (ya, yb, yc, yd)

        outs = in_projection(xp, mods[l], None, norm1_g[l], w_in_bf, res=res_p, tm=tm_p)
        if res_p is not None:
            xp, outs = outs[0], outs[1:]
        ua, ub, uc, uq, ukv = outs
        kvw = N_KV_HEADS * HEAD_DIM
        ks_out.append(ukv[..., :kvw].reshape(bp, lp, N_KV_HEADS, HEAD_DIM))
        vs_out.append(ukv[..., kvw:].reshape(bp, lp, N_KV_HEADS, HEAD_DIM))
        ys = mixers(ua, ub, uc, context_attention(uq, ukv, sink), lp)
        x1p, hp, route = out_projection(ys, xp, mods[l], None, norm2_g[l], w_out_bf, rw, rb, tm=tm_p)
        res_p = (hier_moe(hp, route, exp_gate[l], exp_up[l], exp_down[l]), route, mods[l])
        xp = x1p

        outs = in_projection(xs, mods[l], 1, norm1_g[l], w_in_bf, res=res_s, rope=rope, tm=tm_s)
        if res_s is not None:
            xs, outs = outs[0], outs[1:]
        ua, ub, uc, uq, uqr, ukv = outs
        ys = mixers(ua, ub, uc, latent_attention(uq, uqr, ukv, ck, cv, l, sink), ls)
        x1s, hs, route = out_projection(ys, xs, mods[l], 1, norm2_g[l], w_out_bf, rw, rb, tm=tm_s)
        res_s = (hier_moe(hs, route, exp_gate[l], exp_up[l], exp_down[l]), route, mods[l])
        xs = x1s

    y_prompt = final_norm(xp, res_p, None, norm_f_g, tm=tm_p)
    y_sample = final_norm(xs, res_s, 1, norm_f_g, tm=tm_s)
    return (y_prompt, y_sample, jnp.stack(ks_out, axis=1), jnp.stack(vs_out, axis=1))
```

```python
import functools
import math

import numpy as np
import jax
import jax.numpy as jnp
from jax import lax
from jax.experimental import pallas as pl
from jax.experimental.pallas import tpu as pltpu
from jax.experimental.pallas import tpu_sc as plsc

F32 = jnp.float32
BF16 = jnp.bfloat16

HEAD_DIM = 64
LOG2E = math.log2(math.e)
ATT_SCALE = HEAD_DIM ** -0.5 * LOG2E
N_Q_HEADS = 4
N_KV_HEADS = 2
Q_PER_KV = N_Q_HEADS // N_KV_HEADS
WINDOW = 128
ATT_BLOCK = 128
GRID_W = 64
ROPE_BASE = 10000.0
N_GROUPS = 4
EXPERTS_PER_GROUP = 8
N_EXPERTS = N_GROUPS * EXPERTS_PER_GROUP
MOE_BLOCK = 256
N_MOD = 6
EPS = 1e-6
NEG_INF = -1e30
HYENA_DECAY_TARGET = 1e-2
HYENA_FAST_PCT = 0.3
HYENA_SLOW_PCT = 1.5
HYENA_SHIFT = 0.05

LANES = 128
SUBLANES = 8
VMEM_LIMIT = 56 * 1024 * 1024

MIX_CH = 256
ROUTE_LANES = 128
SC_CORES = 2
SC_WORKERS = SC_CORES * 16


def _params(*sem):
    return pltpu.CompilerParams(dimension_semantics=sem, vmem_limit_bytes=VMEM_LIMIT)


def _silu(x):
    return x * jax.nn.sigmoid(x)


def _ada_body(c_ref, w_ref, b_ref, o_ref):
    s = _silu(c_ref[...]).astype(BF16)
    o_ref[0] = jnp.dot(s, w_ref[0].astype(BF16), preferred_element_type=F32) + b_ref[0]


def adaln_all(cvec, ada_w, ada_b):
    depth, d, n6 = ada_w.shape
    r = cvec.shape[0]
    tn = n6 // 4
    out = pl.pallas_call(
        _ada_body,
        grid=(depth, n6 // tn),
        in_specs=[
            pl.BlockSpec((r, d), lambda l, j: (0, 0)),
            pl.BlockSpec((1, d, tn), lambda l, j: (l, 0, j)),
            pl.BlockSpec((1, 1, tn), lambda l, j: (l, 0, j)),
        ],
        out_specs=pl.BlockSpec((1, r, tn), lambda l, j: (l, 0, j)),
        out_shape=jax.ShapeDtypeStruct((depth, r, n6), F32),
        compiler_params=_params("arbitrary", "arbitrary"),
        name="adaln",
    )(cvec, ada_w, ada_b.reshape(depth, 1, n6))
    return out.reshape(depth, r, N_MOD, d)


def _swap_halves(x):
    pieces = []
    for j in range(x.shape[1] // LANES):
        xj = x[:, j * LANES:(j + 1) * LANES]
        fwd = pltpu.roll(xj, LANES - HEAD_DIM // 2, axis=1)
        bwd = pltpu.roll(xj, HEAD_DIM // 2, axis=1)
        lane = lax.broadcasted_iota(jnp.int32, xj.shape, 1)
        pieces.append(jnp.where((lane % HEAD_DIM) < HEAD_DIM // 2, fwd, bwd))
    return pieces[0] if len(pieces) == 1 else jnp.concatenate(pieces, axis=1)


def _rmsnorm_mod(x, g, scale, shift):
    ms = jnp.mean(x * x, axis=-1, keepdims=True)
    return (x * lax.rsqrt(ms + EPS) * g) * (1.0 + scale) + shift


def _moe_residual(x1, y0_ref, y1_ref, route_ref, pmod_ref):
    moe = route_ref[0, :, 2:3] * y0_ref[0, 0] + route_ref[0, :, 3:4] * y1_ref[0, 0]
    return x1 + pmod_ref[0, 5:6, :] * moe


def _residual_specs(res, tm, d, row):
    pair, route, pmods = res
    args = [pair, pair, route, pmods]
    specs = [pl.BlockSpec((1, 1, tm, d), lambda bb, i: (0, bb, i, 0)),
             pl.BlockSpec((1, 1, tm, d), lambda bb, i: (1, bb, i, 0)),
             pl.BlockSpec((1, tm, ROUTE_LANES), lambda bb, i: (bb, i, 0)),
             pl.BlockSpec((1, N_MOD, d), lambda bb, i: (row(bb), 0, 0))]
    return args, specs


def _in_body(*refs, fuse_res, rope):
    it = iter(refs)
    x_ref = next(it)
    if fuse_res:
        res_refs = [next(it) for _ in range(4)]
    mod_ref = next(it)
    g_ref = next(it)
    w_ref = next(it)
    if rope:
        cos_ref = next(it)
        sin_ref = next(it)
    outs = list(it)
    x = x_ref[0]
    if fuse_res:
        x = _moe_residual(x, *res_refs)
        outs.pop(0)[0] = x
    h = _rmsnorm_mod(x, g_ref[...], mod_ref[0, 1:2, :], mod_ref[0, 0:1, :])
    u = jnp.dot(h.astype(BF16), w_ref[...], preferred_element_type=F32)
    c = MIX_CH
    ua_ref, ub_ref, uc_ref, uq_ref = outs[:4]
    ua_ref[0] = u[:, 0:2 * c]
    ub_ref[0] = u[:, 2 * c:5 * c]
    uc_ref[0] = u[:, 5 * c:6 * c]
    q = u[:, 6 * c:7 * c] * ATT_SCALE
    k = u[:, 7 * c:7 * c + c // 2]
    v = u[:, 7 * c + c // 2:8 * c]
    uq_ref[0] = q
    if rope:
        uqr_ref, ukv_ref = outs[4:]
        cs = cos_ref[...]
        sn = sin_ref[...]
        uqr_ref[0] = q * cs + _swap_halves(q) * sn
        kr = k * cs[:, :c // 2] + _swap_halves(k) * sn[:, :c // 2]
        ukv_ref[0] = jnp.concatenate([kr, v], axis=1)
    else:
        outs[4][0] = u[:, 7 * c:8 * c]


def in_projection(x, mods, mod_row0, norm_g, w_in_bf, *, res=None, rope=None, tm):
    b, l, d = x.shape
    c = MIX_CH
    grid = (b, l // tm)
    row = (lambda bb: 0) if mod_row0 is None else (lambda bb: mod_row0 + bb)
    xspec = pl.BlockSpec((1, tm, d), lambda bb, i: (bb, i, 0))
    mspec = pl.BlockSpec((1, N_MOD, d), lambda bb, i: (row(bb), 0, 0))
    args, specs = [x], [xspec]
    if res is not None:
        rargs, rspecs = _residual_specs(res, tm, d, row)
        args += rargs
        specs += rspecs
    args += [mods, norm_g.reshape(1, d), w_in_bf]
    specs += [mspec, pl.BlockSpec((1, d), lambda bb, i: (0, 0)),
              pl.BlockSpec(w_in_bf.shape, lambda bb, i: (0, 0))]
    if rope is not None:
        args += [rope[0], rope[1]]
        specs += [pl.BlockSpec((tm, c), lambda bb, i: (i, 0))] * 2

    def ospec(w):
        return pl.BlockSpec((1, tm, w), lambda bb, i: (bb, i, 0))

    out_shape, out_specs = [], []
    if res is not None:
        out_shape.append(jax.ShapeDtypeStruct((b, l, d), F32))
        out_specs.append(xspec)
    widths = [(2 * c, F32), (3 * c, F32), (c, F32), (c, F32)] + ([(c, F32)] if rope is not None else []) + [(c, F32)]
    for w, dt in widths:
        out_shape.append(jax.ShapeDtypeStruct((b, l, w), dt))
        out_specs.append(ospec(w))
    return pl.pallas_call(
        functools.partial(_in_body, fuse_res=res is not None, rope=rope is not None),
        grid=grid, in_specs=specs, out_specs=out_specs, out_shape=out_shape,
        compiler_params=_params("arbitrary", "arbitrary"),
        name="in_proj",
    )(*args)


def _dw_tile(win, w_ref, n_taps, first, rows):
    acc = w_ref[0:1, :] * win[first:first + rows]
    for k in range(1, n_taps):
        acc = acc + w_ref[k:k + 1, :] * win[first + k:first + k + rows]
    return acc


CONV_PAD = 16
CONV_ROWS = 64
CONV_CHUNK = 512
CONV_TAIL = CONV_PAD + CONV_ROWS + SUBLANES + SUBLANES


def _conf_body(u_ref, w_ref, b_ref, g_ref, beta_ref, o_ref, gp_ref, sh_ref, *, seq, n_taps, chunk):
    c = MIX_CH
    r = CONV_ROWS
    first = CONV_PAD - (n_taps - 1) // 2
    gp_ref[0:CONV_PAD, :] = jnp.zeros((CONV_PAD, c), F32)
    gp_ref[CONV_PAD + seq:CONV_PAD + seq + CONV_TAIL, :] = jnp.zeros((CONV_TAIL, c), F32)

    def fill(i, carry):
        r0 = pl.multiple_of(i * r, r)
        a = u_ref[0, pl.ds(r0, r), 0:c]
        g = u_ref[0, pl.ds(r0, r), c:2 * c]
        gp_ref[pl.ds(CONV_PAD + r0, r), :] = a * jax.nn.sigmoid(g)
        return carry

    lax.fori_loop(0, seq // r, fill, 0)
    n_copy_tiles = sh_ref.shape[1] // r

    def do_chunk(ci, carry):
        c0 = pl.multiple_of(ci * chunk, chunk)

        def shift_tile(ti, carry2):
            t0 = pl.multiple_of(ti * r, r)
            win = gp_ref[pl.ds(c0 + t0, r + SUBLANES), :]
            for m in range(1, SUBLANES):
                sh_ref[m - 1, pl.ds(t0, r), :] = win[m:m + r]
            return carry2

        lax.fori_loop(0, n_copy_tiles, shift_tile, 0)

        def tile(ti, carry2):
            t0 = pl.multiple_of(ti * r, r)
            acc = None
            for k in range(n_taps):
                a8, m = (first + k) // SUBLANES * SUBLANES, (first + k) % SUBLANES
                src = gp_ref[pl.ds(c0 + t0 + a8, r), :] if m == 0 else sh_ref[m - 1, pl.ds(t0 + a8, r), :]
                term = w_ref[k:k + 1, :] * src
                acc = term if acc is None else acc + term
            z = acc + b_ref[...]
            mu = jnp.mean(z, axis=-1, keepdims=True)
            zc = z - mu
            var = jnp.mean(zc * zc, axis=-1, keepdims=True)
            zn = zc * lax.rsqrt(var + EPS) * g_ref[...] + beta_ref[...]
            o_ref[0, pl.ds(c0 + t0, r), :] = _silu(zn)
            return carry2

        lax.fori_loop(0, chunk // r, tile, 0)
        return carry

    lax.fori_loop(0, seq // chunk, do_chunk, 0)


def conformer_conv(ua, dw_w, dw_b, ln_g, ln_b):
    b, l, c2 = ua.shape
    c = MIX_CH
    k = dw_w.shape[0]
    chunk = min(l, CONV_CHUNK)
    copy_rows = -(-(chunk + CONV_PAD + k) // CONV_ROWS) * CONV_ROWS
    assert l % chunk == 0 and k - 1 <= 2 * CONV_PAD and copy_rows - chunk + SUBLANES <= CONV_PAD + CONV_TAIL
    vec = pl.BlockSpec((1, c), lambda bb: (0, 0))
    return pl.pallas_call(
        functools.partial(_conf_body, seq=l, n_taps=k, chunk=chunk),
        grid=(b,),
        in_specs=[pl.BlockSpec((1, l, c2), lambda bb: (bb, 0, 0)),
                  pl.BlockSpec((k, c), lambda bb: (0, 0)), vec, vec, vec],
        out_specs=pl.BlockSpec((1, l, c), lambda bb: (bb, 0, 0)),
        out_shape=jax.ShapeDtypeStruct((b, l, c), F32),
        scratch_shapes=[pltpu.VMEM((CONV_PAD + l + CONV_TAIL, c), F32),
                        pltpu.VMEM((SUBLANES - 1, copy_rows, c), F32)],
        compiler_params=_params("arbitrary"),
        name="conformer",
    )(ua, dw_w, dw_b.reshape(1, c), ln_g.reshape(1, c), ln_b.reshape(1, c))


SHORT_PAD = 8


def _short_body(u_ref, w_ref, b_ref, o_ref, xp_ref, *, seq, n_taps):
    c = MIX_CH
    r = CONV_ROWS
    half = (n_taps - 1) // 2
    zero = jnp.zeros((SHORT_PAD, c), F32)
    xp_ref[0:SHORT_PAD, :] = zero
    xp_ref[SHORT_PAD + seq:SHORT_PAD + seq + SHORT_PAD, :] = zero

    def fill(i, carry):
        r0 = pl.multiple_of(i * r, r)
        xp_ref[pl.ds(SHORT_PAD + r0, r), :] = u_ref[0, pl.ds(r0, r), :]
        return carry

    lax.fori_loop(0, seq // r, fill, 0)

    def tile(i, carry):
        r0 = pl.multiple_of(i * r, r)
        win = xp_ref[pl.ds(r0, r + 2 * SHORT_PAD), :]
        o_ref[0, pl.ds(r0, r), :] = _dw_tile(win, w_ref, n_taps, SHORT_PAD - half, r) + b_ref[...]
        return carry

    lax.fori_loop(0, seq // r, tile, 0)


def hyena_short_conv(ub, short_w, short_b):
    b, l, c3 = ub.shape
    c = MIX_CH
    k = short_w.shape[0]
    return pl.pallas_call(
        functools.partial(_short_body, seq=l, n_taps=k),
        grid=(b, c3 // c),
        in_specs=[pl.BlockSpec((1, l, c), lambda bb, j: (bb, 0, j)),
                  pl.BlockSpec((k, c), lambda bb, j: (0, j)),
                  pl.BlockSpec((1, c), lambda bb, j: (0, j))],
        out_specs=pl.BlockSpec((1, l, c), lambda bb, j: (bb, 0, j)),
        out_shape=jax.ShapeDtypeStruct((b, l, c3), F32),
        scratch_shapes=[pltpu.VMEM((l + 2 * SHORT_PAD, c), F32)],
        compiler_params=_params("arbitrary", "arbitrary"),
        name="hyena_short",
    )(ub, short_w, short_b.reshape(1, c3))


FFT_UNROLL = 8


def _fft_sizes(seq):
    n = 2 * seq
    n2 = 128 if n >= 4096 else 32
    return n, n // n2, n2


def _filter_stage1_tables(n, n1, n2):
    k1 = np.arange(n1)[None, :, None]
    i1 = np.arange(n1)[None, None, :]
    i2 = np.arange(n2)[:, None, None]
    th = 2.0 * np.pi * ((k1 * (i1 * n2 + i2)) % n) / n
    return np.concatenate([np.cos(th), -np.sin(th)], 1).astype(np.float32)


FFT_GROUP = SUBLANES


def _kron_matrices(n, n1, h):
    k1 = np.arange(n1)[:, None]
    i1 = np.arange(h)[None, :]
    th = 2.0 * np.pi * ((k1 * i1) % n1) / n1
    eye = np.eye(FFT_GROUP)
    cs, sn = np.kron(np.cos(th), eye), np.kron(np.sin(th), eye)
    fwd = np.block([[cs, sn], [-sn, cs]])
    inv = np.block([[cs.T, -sn.T], [sn.T, cs.T]]) / n
    return fwd.astype(np.float32), inv.astype(np.float32)


def _twiddle_tables(n, n1, n2):
    k1 = jnp.arange(n1, dtype=jnp.int32)[:, None]
    i2 = jnp.arange(n2, dtype=jnp.int32)[None, :]
    th = ((k1 * i2) % n).astype(F32) * (2.0 * math.pi / n)
    shape = (n1, n2 // FFT_GROUP, FFT_GROUP, LANES)
    full = lambda a: jnp.broadcast_to(a.reshape(shape[:3] + (1,)), shape)
    return full(jnp.cos(th)), full(jnp.sin(th))


def _mid_tables(n2):
    a = np.arange(n2)
    th = 2.0 * np.pi * ((a[:, None] * a[None, :]) % n2) / n2
    cs, sn = np.cos(th), np.sin(th)
    fwd = np.concatenate([np.concatenate([cs, sn], 1), np.concatenate([-sn, cs], 1)], 0)
    inv = np.concatenate([np.concatenate([cs, -sn], 1), np.concatenate([sn, cs], 1)], 0)
    return fwd.astype(np.float32), inv.astype(np.float32)


def _dot(a, b, precise):
    if precise:
        return jnp.dot(a, b, preferred_element_type=F32, precision=lax.Precision.HIGHEST)
    return jnp.dot(a.astype(BF16), b.astype(BF16), preferred_element_type=F32)


def _filter_stage1_body(x_ref, t_ref, o_ref, *, n1, n2):
    def step(i2, carry):
        x = x_ref[pl.ds(i2, n1, stride=n2), :]
        a = jnp.dot(t_ref[i2], x, preferred_element_type=F32, precision=lax.Precision.HIGHEST)
        o_ref[0, 0, pl.ds(i2, n1, stride=n2), :] = a[:n1]
        o_ref[0, 1, pl.ds(i2, n1, stride=n2), :] = a[n1:]
        return carry

    lax.fori_loop(0, n2, step, 0, unroll=FFT_UNROLL)


def filter_stage1(k, order, table, *, n1, n2):
    n = n1 * n2
    c = MIX_CH
    nc = c // LANES
    return pl.pallas_call(
        functools.partial(_filter_stage1_body, n1=n1, n2=n2),
        grid=(nc,),
        in_specs=[pl.BlockSpec((n, LANES), lambda j: (0, order * nc + j)),
                  pl.BlockSpec(table.shape, lambda j: (0, 0, 0))],
        out_specs=pl.BlockSpec((1, 2, n, LANES), lambda j: (0, 0, 0, j)),
        out_shape=jax.ShapeDtypeStruct((1, 2, n, c), F32),
        compiler_params=_params("arbitrary"),
        name="filter_stage1",
    )(k, table)


def _mid_body(a_ref, k_ref, f_ref, g_ref, o_ref, *, n2, kc):
    for j in range(kc):
        rows = slice(j * n2, (j + 1) * n2)
        blk = jnp.concatenate([a_ref[0, 0, rows, :], a_ref[0, 1, rows, :]], axis=0)
        s = _dot(f_ref[...], blk, False)
        sr, si = s[:n2], s[n2:]
        kr, ki = k_ref[0, rows, :], k_ref[1, rows, :]
        y = jnp.concatenate([sr * kr - si * ki, sr * ki + si * kr], axis=0)
        bb = _dot(g_ref[...], y, False)
        o_ref[0, 0, rows, :] = bb[:n2]
        o_ref[0, 1, rows, :] = bb[n2:]


def fft_mid(a, kf, f_fwd, f_inv, *, n1, n2):
    p = a.shape[0]
    c = MIX_CH
    kc = min(n1, 8)
    rows = kc * n2
    blk = pl.BlockSpec((1, 2, rows, c), lambda j, pp: (pp, 0, j, 0))
    mat = pl.BlockSpec(f_fwd.shape, lambda j, pp: (0, 0))
    return pl.pallas_call(
        functools.partial(_mid_body, n2=n2, kc=kc),
        grid=(n1 // kc, p),
        in_specs=[blk, pl.BlockSpec((2, rows, c), lambda j, pp: (0, j, 0)), mat, mat],
        out_specs=blk,
        out_shape=jax.ShapeDtypeStruct(a.shape, F32),
        compiler_params=_params("arbitrary", "arbitrary"),
        name="fft_mid",
    )(a, kf, f_fwd, f_inv)


def _filter_mid_body(a_ref, f_ref, sum_ref, o_ref, *, n2, kc):
    inv = 1.0 / (sum_ref[0:1, :] + EPS)
    for j in range(kc):
        rows = slice(j * n2, (j + 1) * n2)
        blk = jnp.concatenate([a_ref[0, 0, rows, :], a_ref[0, 1, rows, :]], axis=0)
        s = _dot(f_ref[...], blk, True) * inv
        o_ref[0, rows, :] = s[:n2]
        o_ref[1, rows, :] = s[n2:]


def filter_mid(a, f_fwd, abs_sum, order, *, n1, n2):
    c = MIX_CH
    kc = min(n1, 8)
    rows = kc * n2
    return pl.pallas_call(
        functools.partial(_filter_mid_body, n2=n2, kc=kc),
        grid=(n1 // kc,),
        in_specs=[pl.BlockSpec((1, 2, rows, c), lambda j: (0, 0, j, 0)),
                  pl.BlockSpec(f_fwd.shape, lambda j: (0, 0)),
                  pl.BlockSpec((SUBLANES, c), lambda j: (0, order))],
        out_specs=pl.BlockSpec((2, rows, c), lambda j: (0, j, 0)),
        out_shape=jax.ShapeDtypeStruct((2, n1 * n2, c), F32),
        compiler_params=_params("arbitrary"),
        name="filter_mid",
    )(a, f_fwd, abs_sum)


def _lanes(t, width):
    return t if width == LANES else jnp.concatenate([t] * (width // LANES), axis=-1)


def _kron1_body(zr_ref, zi_ref, m_ref, c_ref, s_ref, o_ref, *, n1, h, gs):
    j, c = FFT_GROUP, MIX_CH
    for s in range(gs):
        xr = zr_ref[0, :, s].reshape(h * j, c)
        xi = zi_ref[0, :, s].reshape(h * j, c)
        a = _dot(m_ref[...], jnp.concatenate([xr, xi], axis=0), False)
        ar = a[:n1 * j].reshape(n1, j, c)
        ai = a[n1 * j:].reshape(n1, j, c)
        cs, sn = _lanes(c_ref[:, s], c), _lanes(s_ref[:, s], c)
        o_ref[0, 0, :, s] = ar * cs + ai * sn
        o_ref[0, 1, :, s] = ai * cs - ar * sn


def _group_step(n_groups):
    return min(n_groups, 4)


def kron_stage1(z5, col, n_pairs, imag_offset, mat, twc, tws, *, n1, h):
    g, j, c = z5.shape[2], FFT_GROUP, MIX_CH
    gs = _group_step(g)
    tw = pl.BlockSpec((n1, gs, j, LANES), lambda gi, p: (0, gi, 0, 0))
    return pl.pallas_call(
        functools.partial(_kron1_body, n1=n1, h=h, gs=gs),
        grid=(g // gs, n_pairs),
        in_specs=[pl.BlockSpec((1, h, gs, j, c), lambda gi, p: (p, 0, gi, 0, col)),
                  pl.BlockSpec((1, h, gs, j, c), lambda gi, p: (p + imag_offset, 0, gi, 0, col)),
                  pl.BlockSpec(mat.shape, lambda gi, p: (0, 0)), tw, tw],
        out_specs=pl.BlockSpec((1, 2, n1, gs, j, c), lambda gi, p: (p, 0, 0, gi, 0, 0)),
        out_shape=jax.ShapeDtypeStruct((n_pairs, 2, n1, g, j, c), F32),
        compiler_params=_params("arbitrary", "arbitrary"),
        name="fft_kron1",
    )(z5, z5, mat, twc, tws)


def _kron3_body(b_ref, m_ref, c_ref, s_ref, zr_ref, zi_ref, gr_ref, gi_ref, d_ref, o_ref, *, n1, h, gs):
    j, c = FFT_GROUP, MIX_CH
    d = d_ref[...].reshape(1, 1, c)
    for s in range(gs):
        br, bi = b_ref[0, 0, :, s], b_ref[0, 1, :, s]
        cs, sn = _lanes(c_ref[:, s], c), _lanes(s_ref[:, s], c)
        xr = (br * cs - bi * sn).reshape(n1 * j, c)
        xi = (br * sn + bi * cs).reshape(n1 * j, c)
        y = _dot(m_ref[...], jnp.concatenate([xr, xi], axis=0), False)
        yr = y[:h * j].reshape(h, j, c)
        yi = y[h * j:].reshape(h, j, c)
        o_ref[0, 0, :, s] = gr_ref[0, :, s] * (yr + d * zr_ref[0, :, s])
        o_ref[1, 0, :, s] = gi_ref[0, :, s] * (yi + d * zi_ref[0, :, s])


def kron_stage3(bw, mat, twc, tws, z5, z_col, g5, g_col, d_vec, imag_offset, *, n1, h):
    p = bw.shape[0]
    g, j, c = bw.shape[3], FFT_GROUP, MIX_CH
    gs = _group_step(g)
    tw = pl.BlockSpec((n1, gs, j, LANES), lambda gi, pp: (0, gi, 0, 0))

    def src(col, off):
        return pl.BlockSpec((1, h, gs, j, c), lambda gi, pp: (pp + off, 0, gi, 0, col))

    return pl.pallas_call(
        functools.partial(_kron3_body, n1=n1, h=h, gs=gs),
        grid=(g // gs, p),
        in_specs=[pl.BlockSpec((1, 2, n1, gs, j, c), lambda gi, pp: (pp, 0, 0, gi, 0, 0)),
                  pl.BlockSpec(mat.shape, lambda gi, pp: (0, 0)), tw, tw,
                  src(z_col, 0), src(z_col, imag_offset), src(g_col, 0), src(g_col, imag_offset),
                  pl.BlockSpec((1, c), lambda gi, pp: (0, 0))],
        out_specs=pl.BlockSpec((2, 1, h, gs, j, c), lambda gi, pp: (0, pp, 0, gi, 0, 0)),
        out_shape=jax.ShapeDtypeStruct((2, p, h, g, j, c), F32),
        compiler_params=_params("arbitrary", "arbitrary"),
        name="fft_kron3",
    )(bw, mat, twc, tws, z5, z5, g5, g5, d_vec.reshape(1, c))


FILTER_TILE = 512


def _filter_body(fw1_ref, fb1_ref, fr1_ref, fw2_ref, fb2_ref, fr2_ref, fw3_ref, fb3_ref, bands_ref, decay_ref,
                 k_ref, sum_ref, *, seq):
    i = pl.program_id(0)
    c = MIX_CH
    hp = lax.Precision.HIGHEST
    n = i * FILTER_TILE + lax.broadcasted_iota(jnp.int32, (FILTER_TILE, 1), 0)
    pos = jnp.where(n <= seq, n, 2 * seq - n).astype(F32)
    t = pos * (1.0 / (seq - 1))
    ang = (pos * (2.0 * math.pi / seq)) * bands_ref[...]
    nb = bands_ref.shape[1]
    pre = (t * fw1_ref[0:1, :]
           + jnp.dot(jnp.cos(ang), fw1_ref[1:1 + nb, :], preferred_element_type=F32, precision=hp)
           - jnp.dot(jnp.sin(ang), fw1_ref[1 + nb:1 + 2 * nb, :], preferred_element_type=F32, precision=hp)
           + fb1_ref[...])
    h = jnp.sin(fr1_ref[...] * pre)
    h = jnp.sin(fr2_ref[...] * (jnp.dot(h, fw2_ref[...], preferred_element_type=F32, precision=hp) + fb2_ref[...]))
    h = jnp.dot(h, fw3_ref[...], preferred_element_type=F32, precision=hp) + fb3_ref[...]
    win = jnp.exp(-t * decay_ref[...]) + HYENA_SHIFT
    win = jnp.concatenate([win, win], axis=1)
    fwd, bwd = h[:, :2 * c], h[:, 2 * c:]
    k = jnp.where(n < seq, fwd, bwd) + jnp.where(n == 0, bwd, 0.0)
    k = jnp.where(n == seq, 0.0, k) * win
    k_ref[...] = k

    @pl.when(i == 0)
    def _():
        sum_ref[...] = jnp.zeros(sum_ref.shape, F32)

    sum_ref[...] = sum_ref[...] + jnp.sum(jnp.abs(k), axis=0, keepdims=True)


def hyena_filter_time(seq, fw1, fb1, fr1, fw2, fb2, fr2, fw3, fb3):
    c = MIX_CH
    n_bands = (fw1.shape[0] - 1) // 2
    bands = jnp.linspace(1e-4, n_bands - 1, n_bands, dtype=F32)[None, :]
    max_decay = math.log(HYENA_DECAY_TARGET) / HYENA_FAST_PCT
    min_decay = math.log(HYENA_DECAY_TARGET) / HYENA_SLOW_PCT
    decay = jnp.abs(jnp.linspace(min_decay, max_decay, c, dtype=F32))[None, :]
    assert (2 * seq) % FILTER_TILE == 0
    args = [fw1, fb1[None, :], fr1[None, :], fw2, fb2[None, :], fr2[None, :], fw3, fb3[None, :], bands, decay]
    return pl.pallas_call(
        functools.partial(_filter_body, seq=seq),
        grid=(2 * seq // FILTER_TILE,),
        in_specs=[pl.BlockSpec(a.shape, lambda i: (0, 0)) for a in args],
        out_specs=[pl.BlockSpec((FILTER_TILE, 2 * c), lambda i: (i, 0)),
                   pl.BlockSpec((SUBLANES, 2 * c), lambda i: (0, 0))],
        out_shape=[jax.ShapeDtypeStruct((2 * seq, 2 * c), F32), jax.ShapeDtypeStruct((SUBLANES, 2 * c), F32)],
        compiler_params=_params("arbitrary"),
        name="hyena_filter",
    )(*args)


def hyena_filter_spectrum(seq, filt):
    n, n1, n2 = _fft_sizes(seq)
    k, abs_sum = hyena_filter_time(seq, *filt)
    t1 = jnp.asarray(_filter_stage1_tables(n, n1, n2))
    f_fwd = jnp.asarray(_mid_tables(n2)[0])
    return [filter_mid(filter_stage1(k, o, t1, n1=n1, n2=n2), f_fwd, abs_sum, o, n1=n1, n2=n2) for o in range(2)]


def hyena_mixer(ub, short_w, short_b, hy_d, kf, tabs):
    b, l, _ = ub.shape
    n, n1, n2 = _fft_sizes(l)
    h = n1 // 2
    p = b // 2
    c = MIX_CH
    g = n2 // FFT_GROUP
    u = hyena_short_conv(ub, short_w, short_b)
    u5 = u.reshape(b, h, g, FFT_GROUP, u.shape[-1])
    m_fwd, m_inv, twc, tws, f_fwd, f_inv = tabs
    z5 = u5
    for o in range(2):
        a = kron_stage1(z5, 0, p, p, m_fwd, twc, tws, n1=n1, h=h)
        bw = fft_mid(a.reshape(p, 2, n, c), kf[o], f_fwd, f_inv, n1=n1, n2=n2)
        z = kron_stage3(bw.reshape(a.shape), m_inv, twc, tws, z5, 0, u5, 1 + o, hy_d[o], p, n1=n1, h=h)
        z5 = z.reshape(b, h, g, FFT_GROUP, c)
    return z5.reshape(b, l, c)


def hyena_tables(seq):
    n, n1, n2 = _fft_sizes(seq)
    m_fwd, m_inv = _kron_matrices(n, n1, n1 // 2)
    twc, tws = _twiddle_tables(n, n1, n2)
    f_fwd, f_inv = _mid_tables(n2)
    bf = lambda a: jnp.asarray(a).astype(BF16)
    return bf(m_fwd), bf(m_inv), twc, tws, bf(f_fwd), bf(f_inv)


def _fnet_body(cl_ref, sl_ref, x_ref, cc_ref, sc_ref, o_ref, *, scale):
    x = x_ref[0].astype(BF16)
    pr = jnp.dot(cl_ref[...], x, preferred_element_type=F32).astype(BF16)
    qr = jnp.dot(sl_ref[...], x, preferred_element_type=F32).astype(BF16)
    o_ref[0] = (jnp.dot(pr, cc_ref[...], preferred_element_type=F32)
                - jnp.dot(qr, sc_ref[...], preferred_element_type=F32)) * scale


def _dft_tables(n):
    a = jnp.arange(n, dtype=jnp.int32)
    th = ((a[:, None] * a[None, :]) % n).astype(F32) * (2.0 * math.pi / n)
    return jnp.cos(th).astype(BF16), jnp.sin(th).astype(BF16)


FNET_DIRECT_MAX = 1024


def _fnet1_body(x_ref, ccs_ref, m_ref, c_ref, s_ref, o_ref, *, n1, gs):
    j, c = FFT_GROUP, MIX_CH
    for s in range(gs):
        x = x_ref[0, :, s].reshape(n1 * j, c).astype(BF16)
        z = jnp.dot(x, ccs_ref[...], preferred_element_type=F32)
        a = _dot(m_ref[...], jnp.concatenate([z[:, :c], z[:, c:]], axis=0), False)
        ar = a[:n1 * j].reshape(n1, j, c)
        ai = a[n1 * j:].reshape(n1, j, c)
        cs, sn = _lanes(c_ref[:, s], c), _lanes(s_ref[:, s], c)
        o_ref[0, 0, :, s] = ar * cs + ai * sn
        o_ref[0, 1, :, s] = ai * cs - ar * sn


def _fnet2_body(a_ref, m_ref, o_ref, *, n2, scale):
    j, c = FFT_GROUP, MIX_CH
    x = a_ref[0].reshape(2 * j * n2, c)
    y = _dot(m_ref[...], x, False) * scale
    o_ref[0, :, 0] = y.reshape(n2, j, c)


def fnet_tables(seq):
    c = MIX_CH
    cc, sc = _dft_tables(c)
    if seq <= FNET_DIRECT_MAX:
        return _dft_tables(seq) + (cc, sc)
    n2 = 128
    n1 = seq // n2
    m_fwd, _ = _kron_matrices(seq, n1, n1)
    twc, tws = _twiddle_tables(seq, n1, n2)
    a = np.arange(n2)
    th = 2.0 * np.pi * ((a[:, None] * a[None, :]) % n2) / n2
    eye = np.eye(FFT_GROUP)
    m2 = np.concatenate([np.einsum('kn,ij->kijn', f, eye).reshape(n2 * FFT_GROUP, FFT_GROUP * n2)
                         for f in (np.cos(th), np.sin(th))], axis=1).astype(np.float32)
    ccs = jnp.concatenate([cc, -sc], axis=1)
    return ccs, jnp.asarray(m_fwd).astype(BF16), twc, tws, jnp.asarray(m2).astype(BF16)


def fnet_two_stage(uc, tables):
    b, l, c = uc.shape
    ccs, m_fwd, twc, tws, m2 = tables
    j = FFT_GROUP
    n2 = 128
    n1 = l // n2
    g = n2 // j
    gs = _group_step(g)
    tw = pl.BlockSpec((n1, gs, j, LANES), lambda gi, bb: (0, gi, 0, 0))
    a = pl.pallas_call(
        functools.partial(_fnet1_body, n1=n1, gs=gs),
        grid=(g // gs, b),
        in_specs=[pl.BlockSpec((1, n1, gs, j, c), lambda gi, bb: (bb, 0, gi, 0, 0)),
                  pl.BlockSpec(ccs.shape, lambda gi, bb: (0, 0)),
                  pl.BlockSpec(m_fwd.shape, lambda gi, bb: (0, 0)), tw, tw],
        out_specs=pl.BlockSpec((1, 2, n1, gs, j, c), lambda gi, bb: (bb, 0, 0, gi, 0, 0)),
        out_shape=jax.ShapeDtypeStruct((b, 2, n1, g, j, c), F32),
        compiler_params=_params("arbitrary", "arbitrary"),
        name="fnet_stage1",
    )(uc.reshape(b, n1, g, j, c), ccs, m_fwd, twc, tws)
    out = pl.pallas_call(
        functools.partial(_fnet2_body, n2=n2, scale=1.0 / math.sqrt(l * c)),
        grid=(b, n1 // j),
        in_specs=[pl.BlockSpec((1, 2, j, n2, c), lambda bb, q: (bb, 0, q, 0, 0)),
                  pl.BlockSpec(m2.shape, lambda bb, q: (0, 0))],
        out_specs=pl.BlockSpec((1, n2, 1, j, c), lambda bb, q: (bb, 0, q, 0, 0)),
        out_shape=jax.ShapeDtypeStruct((b, n2, n1 // j, j, c), F32),
        compiler_params=_params("arbitrary", "arbitrary"),
        name="fnet_stage2",
    )(a.reshape(b, 2, n1, n2, c), m2)
    return out.reshape(b, l, c)


def fnet_mixer(uc, tables):
    b, l, c = uc.shape
    if l > FNET_DIRECT_MAX:
        return fnet_two_stage(uc, tables)
    cl, sl, cc, sc = tables
    tm = min(l, 512)
    row = pl.BlockSpec((tm, l), lambda i, bb: (i, 0))
    sq = pl.BlockSpec((c, c), lambda i, bb: (0, 0))
    return pl.pallas_call(
        functools.partial(_fnet_body, scale=1.0 / math.sqrt(l * c)),
        grid=(l // tm, b),
        in_specs=[row, row, pl.BlockSpec((1, l, c), lambda i, bb: (bb, 0, 0)), sq, sq],
        out_specs=pl.BlockSpec((1, tm, c), lambda i, bb: (bb, i, 0)),
        out_shape=jax.ShapeDtypeStruct((b, l, c), F32),
        compiler_params=_params("arbitrary", "arbitrary"),
        name="fnet",
    )(cl, sl, uc, cc, sc)


def _heads_rows(x, g):
    h0 = Q_PER_KV * g
    return jnp.concatenate([x[:, (h0 + r) * HEAD_DIM:(h0 + r + 1) * HEAD_DIM] for r in range(Q_PER_KV)], axis=0)


def _qk(q, k):
    return lax.dot_general(q.astype(BF16), k.astype(BF16), (((1,), (1,)), ((), ())),
                           preferred_element_type=F32)


def _sink_col(sink_ref, g, rows):
    ridx = lax.broadcasted_iota(jnp.int32, (Q_PER_KV * rows, 1), 0)
    col = jnp.full((Q_PER_KV * rows, 1), sink_ref[Q_PER_KV * g], F32)
    for r in range(1, Q_PER_KV):
        col = jnp.where(ridx >= r * rows, sink_ref[Q_PER_KV * g + r], col)
    return col * LOG2E


def _lat_attn_body(sink_ref, q_ref, qr_ref, kp_ref, kc_ref, kn_ref, ck_ref, cv_ref, o_ref):
    i = pl.program_id(1)
    nblk = pl.num_programs(1)
    blk = ATT_BLOCK
    span = blk + 2 * WINDOW
    q = q_ref[0]
    qr = qr_ref[0]
    kv = jnp.concatenate([kp_ref[0], kc_ref[0], kn_ref[0]], axis=0)
    ck = ck_ref[0, 0]
    cv = cv_ref[0, 0]
    kvw = N_KV_HEADS * HEAD_DIM
    r = lax.broadcasted_iota(jnp.int32, (Q_PER_KV * blk, span), 0) % blk
    j = lax.broadcasted_iota(jnp.int32, (Q_PER_KV * blk, span), 1)
    ok = (j >= r) & (j <= r + 2 * WINDOW)
    ok = ok & ((i > 0) | (j >= WINDOW)) & ((i < nblk - 1) | (j < WINDOW + blk))
    outs = []
    for g in range(N_KV_HEADS):
        kl = kv[:, g * HEAD_DIM:(g + 1) * HEAD_DIM]
        vl = kv[:, kvw + g * HEAD_DIM:kvw + (g + 1) * HEAD_DIM]
        s_loc = jnp.where(ok, _qk(_heads_rows(qr, g), kl), NEG_INF)
        s_ctx = _qk(_heads_rows(q, g), ck[:, g * HEAD_DIM:(g + 1) * HEAD_DIM])
        sink = _sink_col(sink_ref, g, blk)
        m = jnp.maximum(jnp.maximum(jnp.max(s_loc, axis=-1, keepdims=True),
                                    jnp.max(s_ctx, axis=-1, keepdims=True)), sink)
        e_loc = jnp.exp2(s_loc - m)
        e_ctx = jnp.exp2(s_ctx - m)
        den = jnp.sum(e_loc, axis=-1, keepdims=True) + jnp.sum(e_ctx, axis=-1, keepdims=True) + jnp.exp2(sink - m)
        o = (jnp.dot(e_loc.astype(BF16), vl.astype(BF16), preferred_element_type=F32)
             + jnp.dot(e_ctx.astype(BF16), cv[:, g * HEAD_DIM:(g + 1) * HEAD_DIM].astype(BF16),
                       preferred_element_type=F32)) * (1.0 / den)
        outs += [o[rr * blk:(rr + 1) * blk] for rr in range(Q_PER_KV)]
    o_ref[0] = jnp.concatenate(outs, axis=1)


def latent_attention(uq, uqr, ukv, cache_k, cache_v, layer, sink):
    b, l, c = uq.shape
    p = cache_k.shape[2]
    blk = ATT_BLOCK
    nblk = l // blk
    qspec = pl.BlockSpec((1, blk, c), lambda bb, i: (bb, i, 0))
    cspec = pl.BlockSpec((1, 1, p, cache_k.shape[3]), lambda bb, i: (bb, layer, 0, 0))
    return pl.pallas_call(
        _lat_attn_body,
        grid=(b, nblk),
        in_specs=[pl.BlockSpec(memory_space=pltpu.SMEM), qspec, qspec,
                  pl.BlockSpec((1, blk, c), lambda bb, i: (bb, jnp.maximum(i - 1, 0), 0)),
                  qspec,
                  pl.BlockSpec((1, blk, c), lambda bb, i: (bb, jnp.minimum(i + 1, nblk - 1), 0)),
                  cspec, cspec],
        out_specs=qspec,
        out_shape=jax.ShapeDtypeStruct((b, l, c), F32),
        compiler_params=_params("arbitrary", "arbitrary"),
        name="latent_attention",
    )(sink, uq, uqr, ukv, ukv, ukv, cache_k, cache_v)


def _ctx_attn_body(sink_ref, q_ref, kv_ref, o_ref, *, seq):
    q = q_ref[0]
    kv = kv_ref[0]
    kvw = N_KV_HEADS * HEAD_DIM
    outs = []
    for g in range(N_KV_HEADS):
        kl = kv[:, g * HEAD_DIM:(g + 1) * HEAD_DIM]
        vl = kv[:, kvw + g * HEAD_DIM:kvw + (g + 1) * HEAD_DIM]
        s = _qk(_heads_rows(q, g), kl)
        sink = _sink_col(sink_ref, g, seq)
        m = jnp.maximum(jnp.max(s, axis=-1, keepdims=True), sink)
        e = jnp.exp2(s - m)
        den = jnp.sum(e, axis=-1, keepdims=True) + jnp.exp2(sink - m)
        o = jnp.dot(e.astype(BF16), vl.astype(BF16), preferred_element_type=F32) * (1.0 / den)
        outs += [o[rr * seq:(rr + 1) * seq] for rr in range(Q_PER_KV)]
    o_ref[0] = jnp.concatenate(outs, axis=1)


def context_attention(uq, ukv, sink):
    b, s, c = uq.shape
    spec = pl.BlockSpec((1, s, c), lambda bb: (bb, 0, 0))
    return pl.pallas_call(
        functools.partial(_ctx_attn_body, seq=s),
        grid=(b,),
        in_specs=[pl.BlockSpec(memory_space=pltpu.SMEM), spec, spec],
        out_specs=spec,
        out_shape=jax.ShapeDtypeStruct((b, s, c), F32),
        compiler_params=_params("arbitrary"),
        name="context_attention",
    )(sink, uq, ukv)


def _pack_bf16_pairs(hi_rounded):
    k = hi_rounded.shape[1] // 2
    bits = lax.bitcast_convert_type(hi_rounded, jnp.uint32)
    return bits[:, :k] | (bits[:, k:] >> 16)


def _unpack_bf16_pairs(packed):
    a = lax.bitcast_convert_type(packed & jnp.uint32(0xFFFF0000), F32)
    b = lax.bitcast_convert_type(packed << 16, F32)
    return jnp.concatenate([a, b], axis=1).astype(BF16)


def _out_body(ya_ref, yb_ref, yc_ref, yd_ref, x_ref, mod_ref, g_ref, w_ref, rw_ref, rb_ref,
              x1_ref, h_ref, route_ref):
    c = MIX_CH
    y = jnp.dot(ya_ref[0].astype(BF16), w_ref[0:c, :], preferred_element_type=F32)
    for j, ref in enumerate((yb_ref, yc_ref, yd_ref), start=1):
        y = y + jnp.dot(ref[0].astype(BF16), w_ref[j * c:(j + 1) * c, :], preferred_element_type=F32)
    x1 = x_ref[0] + mod_ref[0, 2:3, :] * y
    x1_ref[0] = x1
    h = _rmsnorm_mod(x1, g_ref[...], mod_ref[0, 4:5, :], mod_ref[0, 3:4, :])
    h_hi = h.astype(BF16)
    h_hi32 = h_hi.astype(F32)
    h_ref[0] = _pack_bf16_pairs(h_hi32)
    h_lo = (h - h_hi32).astype(BF16)
    tm = h.shape[0]
    prod = jnp.dot(jnp.concatenate([h_hi, h_lo], axis=0), rw_ref[...], preferred_element_type=F32)
    logits = (prod[:tm, :ROUTE_LANES] + prod[:tm, ROUTE_LANES:]
              + prod[tm:, :ROUTE_LANES] + prod[tm:, ROUTE_LANES:]) + rb_ref[...]
    lane = lax.broadcasted_iota(jnp.int32, logits.shape, 1)
    is_c = lane < N_GROUPS
    lc = jnp.where(is_c, logits, NEG_INF)
    mc = jnp.max(lc, axis=-1, keepdims=True)
    grp = jnp.min(jnp.where(lc == mc, lane, ROUTE_LANES), axis=-1, keepdims=True)
    pg = 1.0 / jnp.sum(jnp.where(is_c, jnp.exp(lc - mc), 0.0), axis=-1, keepdims=True)
    lo = N_GROUPS + grp * EXPERTS_PER_GROUP
    in_g = (lane >= lo) & (lane < lo + EXPERTS_PER_GROUP)
    lf = jnp.where(in_g, logits, NEG_INF)
    t1 = jnp.max(lf, axis=-1, keepdims=True)
    i1 = jnp.min(jnp.where(lf == t1, lane, ROUTE_LANES), axis=-1, keepdims=True)
    lf2 = jnp.where(lane == i1, NEG_INF, lf)
    t2 = jnp.max(lf2, axis=-1, keepdims=True)
    i2 = jnp.min(jnp.where(lf2 == t2, lane, ROUTE_LANES), axis=-1, keepdims=True)
    e2 = jnp.exp(t2 - t1)
    w1 = pg / (1.0 + e2)
    w2 = pg * e2 / (1.0 + e2)
    rec = jnp.where(lane == 0, (i1 - N_GROUPS).astype(F32),
                    jnp.where(lane == 1, (i2 - N_GROUPS).astype(F32),
                              jnp.where(lane == 2, w1, jnp.where(lane == 3, w2, 0.0))))
    route_ref[0] = rec


def out_projection(ys, x, mods, mod_row0, norm_g, w_out_bf, rw, rb, *, tm):
    b, l, d = x.shape
    c = MIX_CH
    row = (lambda bb: 0) if mod_row0 is None else (lambda bb: mod_row0 + bb)
    yspec = pl.BlockSpec((1, tm, c), lambda bb, i: (bb, i, 0))
    xspec = pl.BlockSpec((1, tm, d), lambda bb, i: (bb, i, 0))
    return pl.pallas_call(
        _out_body,
        grid=(b, l // tm),
        in_specs=[yspec] * 4 + [xspec,
                                pl.BlockSpec((1, N_MOD, d), lambda bb, i: (row(bb), 0, 0)),
                                pl.BlockSpec((1, d), lambda bb, i: (0, 0)),
                                pl.BlockSpec(w_out_bf.shape, lambda bb, i: (0, 0)),
                                pl.BlockSpec(rw.shape, lambda bb, i: (0, 0)),
                                pl.BlockSpec(rb.shape, lambda bb, i: (0, 0))],
        out_specs=[xspec, pl.BlockSpec((1, tm, d // 2), lambda bb, i: (bb, i, 0)),
                   pl.BlockSpec((1, tm, ROUTE_LANES), lambda bb, i: (bb, i, 0))],
        out_shape=[jax.ShapeDtypeStruct((b, l, d), F32), jax.ShapeDtypeStruct((b, l, d // 2), jnp.uint32),
                   jax.ShapeDtypeStruct((b, l, ROUTE_LANES), F32)],
        compiler_params=_params("arbitrary", "arbitrary"),
        name="out_proj",
    )(*ys, x, mods, norm_g.reshape(1, d), w_out_bf, rw, rb)


def _expert_body(be_ref, nv_ref, xs_ref, wg_ref, wu_ref, wd_ref, o_ref, wg_s, wu_s, wd_s):
    i = pl.program_id(0)
    prev = be_ref[jnp.maximum(i - 1, 0)]

    @pl.when((i == 0) | (be_ref[i] != prev))
    def _():
        wg_s[...] = wg_ref[0, 0].astype(BF16)
        wu_s[...] = wu_ref[0, 0].astype(BF16)
        wd_s[...] = wd_ref[0, 0].astype(BF16)

    @pl.when(nv_ref[i] > 0)
    def _():
        row = lax.broadcasted_iota(jnp.int32, xs_ref.shape, 0)
        x = _unpack_bf16_pairs(jnp.where(row < nv_ref[i], xs_ref[...], jnp.uint32(0)))
        g = jnp.dot(x, wg_s[...], preferred_element_type=F32)
        u = jnp.dot(x, wu_s[...], preferred_element_type=F32)
        a = (_silu(g) * u).astype(BF16)
        o_ref[...] = jnp.dot(a, wd_s[...], preferred_element_type=F32)

    @pl.when(nv_ref[i] <= 0)
    def _():
        o_ref[...] = jnp.zeros(o_ref.shape, F32)


def expert_ffn(xs, blk_e, n_valid, layer, e_gate, e_up, e_down):
    rows, dh = xs.shape
    d = 2 * dh
    nb = rows // MOE_BLOCK
    de = e_gate.shape[-1]
    grid_spec = pltpu.PrefetchScalarGridSpec(
        num_scalar_prefetch=2,
        grid=(nb,),
        in_specs=[pl.BlockSpec((MOE_BLOCK, dh), lambda i, be, nv: (i, 0)),
                  pl.BlockSpec((1, 1, d, de), lambda i, be, nv: (layer, be[i], 0, 0)),
                  pl.BlockSpec((1, 1, d, de), lambda i, be, nv: (layer, be[i], 0, 0)),
                  pl.BlockSpec((1, 1, de, d), lambda i, be, nv: (layer, be[i], 0, 0))],
        out_specs=pl.BlockSpec((MOE_BLOCK, d), lambda i, be, nv: (i, 0)),
        scratch_shapes=[pltpu.VMEM((d, de), BF16), pltpu.VMEM((d, de), BF16), pltpu.VMEM((de, d), BF16)],
    )
    return pl.pallas_call(
        _expert_body, grid_spec=grid_spec,
        out_shape=jax.ShapeDtypeStruct((rows, d), F32),
        compiler_params=_params("arbitrary"),
        name="expert_ffn",
    )(blk_e, n_valid, xs, e_gate, e_up, e_down)


RANK_TILE = 512


def _slot_body(route_ref, slot_ref, cnt_ref, carry_ref):
    phase, i = pl.program_id(0), pl.program_id(1)
    t = RANK_TILE
    rec = route_ref[...]
    lane = lax.broadcasted_iota(jnp.int32, rec.shape, 1)
    lanef = lane.astype(F32)
    e0 = (lanef == rec[:, 0:1]).astype(F32)
    e1 = (lanef == rec[:, 1:2]).astype(F32)
    both = e0 + e1

    @pl.when((phase == 0) & (i == 0))
    def _():
        carry_ref[...] = jnp.zeros(carry_ref.shape, F32)

    @pl.when(phase == 0)
    def _():
        carry_ref[...] = carry_ref[...] + jnp.sum(both, axis=0, keepdims=True)
        cnt_ref[...] = carry_ref[...]
        slot_ref[...] = jnp.zeros(slot_ref.shape, F32)

    @pl.when((phase == 1) & (i == 0))
    def _():
        cnt = carry_ref[...]
        padded = jnp.floor((cnt + (MOE_BLOCK - 1)) * (1.0 / MOE_BLOCK)) * MOE_BLOCK
        ln = lax.broadcasted_iota(jnp.int32, cnt.shape, 1)
        incl = padded
        sh = 1
        while sh < ROUTE_LANES:
            incl = incl + jnp.where(ln >= sh, pltpu.roll(incl, sh, axis=1), 0.0)
            sh *= 2
        carry_ref[...] = incl - padded

    @pl.when(phase == 1)
    def _():
        earlier = (lax.broadcasted_iota(jnp.int32, (t, t), 0) > lax.broadcasted_iota(jnp.int32, (t, t), 1))
        before = jnp.dot(earlier.astype(BF16), both.astype(BF16), preferred_element_type=F32) + carry_ref[0:1, :]
        s0 = jnp.sum(e0 * before, axis=-1, keepdims=True)
        s1 = jnp.sum(e1 * before, axis=-1, keepdims=True)
        slot_ref[...] = jnp.where(lane == 0, s0, jnp.where(lane == 1, s1, 0.0))
        carry_ref[...] = carry_ref[...] + jnp.sum(both, axis=0, keepdims=True)


def moe_slots(route):
    n = route.shape[0]
    t = RANK_TILE
    return pl.pallas_call(
        _slot_body,
        grid=(2, n // t),
        in_specs=[pl.BlockSpec((t, ROUTE_LANES), lambda ph, i: (i, 0))],
        out_specs=[pl.BlockSpec((t, ROUTE_LANES), lambda ph, i: (i * ph, 0)),
                   pl.BlockSpec((SUBLANES, ROUTE_LANES), lambda ph, i: (0, 0))],
        out_shape=[jax.ShapeDtypeStruct((n, ROUTE_LANES), F32),
                   jax.ShapeDtypeStruct((SUBLANES, ROUTE_LANES), F32)],
        scratch_shapes=[pltpu.VMEM((SUBLANES, ROUTE_LANES), F32)],
        compiler_params=_params("arbitrary", "arbitrary"),
        name="moe_slots",
    )(route)


def _sc_mesh():
    return plsc.VectorSubcoreMesh(core_axis_name="c", subcore_axis_name="s")


def _sc_worker():
    return lax.axis_index("s") * SC_CORES + lax.axis_index("c")


DISPATCH_ROWS = 64
COMBINE_ROWS = 32


def sc_dispatch(rows, dest, n_slots):
    n, w = rows.shape
    ch = DISPATCH_ROWS
    per_w = n // SC_WORKERS
    n_ch = per_w // ch

    @functools.partial(
        pl.kernel, mesh=_sc_mesh(),
        out_type=jax.ShapeDtypeStruct((n_slots, w), rows.dtype),
        scratch_types=[pltpu.VMEM((ch,), jnp.int32), pltpu.VMEM((ch, w), rows.dtype)],
    )
    def scatter_kernel(rows_hbm, dest_hbm, out_hbm, idx_v, rows_v):
        wid = _sc_worker()

        @pl.loop(0, n_ch)
        def _(j):
            chunk = wid * n_ch + j
            pltpu.sync_copy(rows_hbm.at[pl.ds(pl.multiple_of(chunk * ch, ch), ch)], rows_v)
            for k in range(2):
                pltpu.sync_copy(dest_hbm.at[k, chunk], idx_v)
                pltpu.sync_copy(rows_v, out_hbm.at[idx_v])

    return scatter_kernel(rows, dest)


def sc_gather_rows(table, idx):
    s, w = table.shape
    m = idx.shape[0]
    ch = COMBINE_ROWS
    per_w = m // SC_WORKERS
    n_ch = per_w // ch

    @functools.partial(
        pl.kernel, mesh=_sc_mesh(),
        out_type=jax.ShapeDtypeStruct((m, w), table.dtype),
        scratch_types=[pltpu.VMEM((ch,), jnp.int32), pltpu.VMEM((ch, w), table.dtype), pltpu.SemaphoreType.DMA],
    )
    def gather_kernel(table_hbm, idx_hbm, out_hbm, idx_v, rows_v, sem):
        wid = _sc_worker()

        @pl.loop(0, n_ch)
        def _(j):
            off = pl.multiple_of((wid * n_ch + j) * ch, ch)
            pltpu.sync_copy(idx_hbm.at[pl.ds(off, ch)], idx_v)
            pltpu.async_copy(table_hbm.at[idx_v], rows_v, sem).wait()
            pltpu.sync_copy(rows_v, out_hbm.at[pl.ds(off, ch)])

    return gather_kernel(table, idx)


def hier_moe(h_packed, route, layer, e_gate, e_up, e_down):
    b, l, dh = h_packed.shape
    n = b * l
    assert n % (SC_WORKERS * DISPATCH_ROWS) == 0 and (2 * n) % (SC_WORKERS * COMBINE_ROWS) == 0
    slots, cnt = moe_slots(route.reshape(n, ROUTE_LANES))
    counts = cnt[0, :N_EXPERTS].astype(jnp.int32)
    padded = (counts + MOE_BLOCK - 1) // MOE_BLOCK * MOE_BLOCK
    pend = jnp.cumsum(padded)
    nb = -(-2 * n // MOE_BLOCK) + N_EXPERTS
    blk0 = jnp.arange(nb, dtype=jnp.int32) * MOE_BLOCK
    owner = pend[None, :] <= blk0[:, None]
    blk_e = jnp.minimum(jnp.sum(owner, axis=1), N_EXPERTS - 1).astype(jnp.int32)
    run_end = jnp.sum(jnp.where(jnp.arange(N_EXPERTS)[None, :] == blk_e[:, None],
                                (pend - padded + counts)[None, :], 0), axis=1)
    n_valid = jnp.clip(run_end - blk0, 0, MOE_BLOCK).astype(jnp.int32)
    dest = slots[:, 0:2].astype(jnp.int32).T
    xs = sc_dispatch(h_packed.reshape(n, dh), dest.reshape(2, n // DISPATCH_ROWS, DISPATCH_ROWS), nb * MOE_BLOCK)
    y = expert_ffn(xs, blk_e, n_valid, layer, e_gate, e_up, e_down)
    return sc_gather_rows(y, dest.reshape(2 * n)).reshape(2, b, l, 2 * dh)


def _final_body(x_ref, y0_ref, y1_ref, route_ref, pmod_ref, g_ref, o_ref):
    x = _moe_residual(x_ref[0], y0_ref, y1_ref, route_ref, pmod_ref)
    ms = jnp.mean(x * x, axis=-1, keepdims=True)
    o_ref[0] = x * lax.rsqrt(ms + EPS) * g_ref[...]


def final_norm(x1, res, mod_row0, norm_g, *, tm):
    b, l, d = x1.shape
    row = (lambda bb: 0) if mod_row0 is None else (lambda bb: mod_row0 + bb)
    xspec = pl.BlockSpec((1, tm, d), lambda bb, i: (bb, i, 0))
    rargs, rspecs = _residual_specs(res, tm, d, row)
    return pl.pallas_call(
        _final_body,
        grid=(b, l // tm),
        in_specs=[xspec] + rspecs + [pl.BlockSpec((1, d), lambda bb, i: (0, 0))],
        out_specs=xspec,
        out_shape=jax.ShapeDtypeStruct((b, l, d), F32),
        compiler_params=_params("arbitrary", "arbitrary"),
        name="final_norm",
    )(x1, *rargs, norm_g.reshape(1, d))


def _rope_tables(seq):
    rows = seq // GRID_W
    row_pos = jnp.repeat(jnp.arange(rows, dtype=F32), GRID_W)
    col_pos = jnp.tile(jnp.arange(GRID_W, dtype=F32), rows)
    n_freq = HEAD_DIM // 4
    inv = ROPE_BASE ** (-jnp.arange(n_freq, dtype=F32) / n_freq)
    ang = jnp.concatenate([row_pos[:, None] * inv, col_pos[:, None] * inv], axis=-1)
    cs, sn = jnp.cos(ang), jnp.sin(ang)
    cos_f = jnp.tile(jnp.concatenate([cs, cs], axis=-1), (1, N_Q_HEADS))
    sin_s = jnp.tile(jnp.concatenate([-sn, sn], axis=-1), (1, N_Q_HEADS))
    return cos_f, sin_s


def kernel(x_prompt, x_sample, cache_k, cache_v, c, c_ctx, ada_w, ada_b, norm1_g, norm2_g, w_in, conv_dw_w, conv_dw_b, conv_ln_g, conv_ln_b, hy_short_w, hy_short_b, hy_fw1, hy_fb1, hy_freq1, hy_fw2, hy_fb2, hy_freq2, hy_fw3, hy_fb3, hy_d, attn_sink, w_out, router_coarse_w, router_coarse_b, router_fine_w, router_fine_b, exp_gate, exp_up, exp_down, norm_f_g):
    depth = ada_w.shape[0]
    bp, lp, d = x_prompt.shape
    bs, ls, _ = x_sample.shape
    assert bp % 2 == 0 and bs % 2 == 0 and ls % ATT_BLOCK == 0 and ls % GRID_W == 0

    n_rows = -(-(1 + bs) // SUBLANES) * SUBLANES
    cvec = jnp.concatenate([c_ctx[None, :], c, jnp.zeros((n_rows - 1 - bs, d), F32)], axis=0)
    mods = adaln_all(cvec, ada_w, ada_b)

    rope = _rope_tables(ls)
    fnet_tabs, hy_tabs = {}, {}
    for seq in {lp, ls}:
        fnet_tabs[seq] = fnet_tables(seq)
        hy_tabs[seq] = hyena_tables(seq)
    ck = cache_k.reshape(cache_k.shape[0], depth, cache_k.shape[2], -1)
    cv = cache_v.reshape(cache_v.shape[0], depth, cache_v.shape[2], -1)
    pad = ROUTE_LANES - N_GROUPS - N_EXPERTS

    tm_p = min(lp, 512)
    tm_s = min(ls, 512)
    xp, xs = x_prompt, x_sample
    res_p = res_s = None
    ks_out, vs_out = [], []
    for l in range(depth):
        w_in_bf = w_in[l].astype(BF16)
        w_out_bf = w_out[l].astype(BF16)
        rw = jnp.concatenate([router_coarse_w[l], router_fine_w[l], jnp.zeros((d, pad), F32)], axis=1)
        rw_hi = rw.astype(BF16)
        rw = jnp.concatenate([rw_hi, (rw - rw_hi.astype(F32)).astype(BF16)], axis=1)
        rb = jnp.concatenate([router_coarse_b[l], router_fine_b[l], jnp.zeros((pad,), F32)])[None, :]
        filt = (hy_fw1[l], hy_fb1[l], hy_freq1[l], hy_fw2[l], hy_fb2[l], hy_freq2[l], hy_fw3[l], hy_fb3[l])
        sink = attn_sink[l]

        def mixers(ua, ub, uc, yd, seq):
            ya = conformer_conv(ua, conv_dw_w[l], conv_dw_b[l], conv_ln_g[l], conv_ln_b[l])
            yb = hyena_mixer(ub, hy_short_w[l], hy_short_b[l], hy_d[l], hyena_filter_spectrum(seq, filt),
                             hy_tabs[seq])
            yc = fnet_mixer(uc, fnet_tabs[seq])
            return (ya, yb, yc, yd)

        outs = in_projection(xp, mods[l], None, norm1_g[l], w_in_bf, res=res_p, tm=tm_p)
        if res_p is not None:
            xp, outs = outs[0], outs[1:]
        ua, ub, uc, uq, ukv = outs
        kvw = N_KV_HEADS * HEAD_DIM
        ks_out.append(ukv[..., :kvw].reshape(bp, lp, N_KV_HEADS, HEAD_DIM))
        vs_out.append(ukv[..., kvw:].reshape(bp, lp, N_KV_HEADS, HEAD_DIM))
        ys = mixers(ua, ub, uc, context_attention(uq, ukv, sink), lp)
        x1p, hp, route = out_projection(ys, xp, mods[l], None, norm2_g[l], w_out_bf, rw, rb, tm=tm_p)
        res_p = (hier_moe(hp, route, l, exp_gate, exp_up, exp_down), route, mods[l])
        xp = x1p

        outs = in_projection(xs, mods[l], 1, norm1_g[l], w_in_bf, res=res_s, rope=rope, tm=tm_s)
        if res_s is not None:
            xs, outs = outs[0], outs[1:]
        ua, ub, uc, uq, uqr, ukv = outs
        ys = mixers(ua, ub, uc, latent_attention(uq, uqr, ukv, ck, cv, l, sink), ls)
        x1s, hs, route = out_projection(ys, xs, mods[l], 1, norm2_g[l], w_out_bf, rw, rb, tm=tm_s)
        res_s = (hier_moe(hs, route, l, exp_gate, exp_up, exp_down), route, mods[l])
        xs = x1s

    y_prompt = final_norm(xp, res_p, None, norm_f_g, tm=tm_p)
    y_sample = final_norm(xs, res_s, 1, norm_f_g, tm=tm_s)
    return (y_prompt, y_sample, jnp.stack(ks_out, axis=1), jnp.stack(vs_out, axis=1))
```

```python
import functools
import math

import numpy as np
import jax
import jax.numpy as jnp
from jax import lax
from jax.experimental import pallas as pl
from jax.experimental.pallas import tpu as pltpu
from jax.experimental.pallas import tpu_sc as plsc

F32 = jnp.float32
BF16 = jnp.bfloat16

HEAD_DIM = 64
LOG2E = math.log2(math.e)
ATT_SCALE = HEAD_DIM ** -0.5 * LOG2E
N_Q_HEADS = 4
N_KV_HEADS = 2
Q_PER_KV = N_Q_HEADS // N_KV_HEADS
WINDOW = 128
ATT_BLOCK = 128
GRID_W = 64
ROPE_BASE = 10000.0
N_GROUPS = 4
EXPERTS_PER_GROUP = 8
N_EXPERTS = N_GROUPS * EXPERTS_PER_GROUP
MOE_BLOCK = 256
N_MOD = 6
EPS = 1e-6
NEG_INF = -1e30
HYENA_DECAY_TARGET = 1e-2
HYENA_FAST_PCT = 0.3
HYENA_SLOW_PCT = 1.5
HYENA_SHIFT = 0.05

LANES = 128
SUBLANES = 8
VMEM_LIMIT = 56 * 1024 * 1024

MIX_CH = 256
ROUTE_LANES = 128
SC_CORES = 2
SC_WORKERS = SC_CORES * 16


def _params(*sem):
    return pltpu.CompilerParams(dimension_semantics=sem, vmem_limit_bytes=VMEM_LIMIT)


def _silu(x):
    return x * jax.nn.sigmoid(x)


def _ada_body(c_ref, w_ref, b_ref, o_ref):
    s = _silu(c_ref[...]).astype(BF16)
    o_ref[0] = jnp.dot(s, w_ref[0].astype(BF16), preferred_element_type=F32) + b_ref[0]


def adaln_all(cvec, ada_w, ada_b):
    depth, d, n6 = ada_w.shape
    r = cvec.shape[0]
    tn = n6 // 4
    out = pl.pallas_call(
        _ada_body,
        grid=(depth, n6 // tn),
        in_specs=[
            pl.BlockSpec((r, d), lambda l, j: (0, 0)),
            pl.BlockSpec((1, d, tn), lambda l, j: (l, 0, j)),
            pl.BlockSpec((1, 1, tn), lambda l, j: (l, 0, j)),
        ],
        out_specs=pl.BlockSpec((1, r, tn), lambda l, j: (l, 0, j)),
        out_shape=jax.ShapeDtypeStruct((depth, r, n6), F32),
        compiler_params=_params("arbitrary", "arbitrary"),
        name="adaln",
    )(cvec, ada_w, ada_b.reshape(depth, 1, n6))
    return out.reshape(depth, r, N_MOD, d)


def _swap_halves(x):
    pieces = []
    for j in range(x.shape[1] // LANES):
        xj = x[:, j * LANES:(j + 1) * LANES]
        fwd = pltpu.roll(xj, LANES - HEAD_DIM // 2, axis=1)
        bwd = pltpu.roll(xj, HEAD_DIM // 2, axis=1)
        lane = lax.broadcasted_iota(jnp.int32, xj.shape, 1)
        pieces.append(jnp.where((lane % HEAD_DIM) < HEAD_DIM // 2, fwd, bwd))
    return pieces[0] if len(pieces) == 1 else jnp.concatenate(pieces, axis=1)


def _rmsnorm_mod(x, g, scale, shift):
    ms = jnp.mean(x * x, axis=-1, keepdims=True)
    return (x * lax.rsqrt(ms + EPS) * g) * (1.0 + scale) + shift


def _moe_residual(x1, y0_ref, y1_ref, route_ref, pmod_ref):
    y0 = _unpack_bf16_pairs(y0_ref[0, 0], F32)
    y1 = _unpack_bf16_pairs(y1_ref[0, 0], F32)
    moe = route_ref[0, :, 2:3] * y0 + route_ref[0, :, 3:4] * y1
    return x1 + pmod_ref[0, 5:6, :] * moe


def _residual_specs(res, tm, d, row):
    pair, route, pmods = res
    args = [pair, pair, route, pmods]
    specs = [pl.BlockSpec((1, 1, tm, d // 2), lambda bb, i: (0, bb, i, 0)),
             pl.BlockSpec((1, 1, tm, d // 2), lambda bb, i: (1, bb, i, 0)),
             pl.BlockSpec((1, tm, ROUTE_LANES), lambda bb, i: (bb, i, 0)),
             pl.BlockSpec((1, N_MOD, d), lambda bb, i: (row(bb), 0, 0))]
    return args, specs


def _in_body(*refs, fuse_res, rope):
    it = iter(refs)
    x_ref = next(it)
    if fuse_res:
        res_refs = [next(it) for _ in range(4)]
    mod_ref = next(it)
    g_ref = next(it)
    w_ref = next(it)
    if rope:
        cos_ref = next(it)
        sin_ref = next(it)
    outs = list(it)
    x = x_ref[0]
    if fuse_res:
        x = _moe_residual(x, *res_refs)
        outs.pop(0)[0] = x
    h = _rmsnorm_mod(x, g_ref[...], mod_ref[0, 1:2, :], mod_ref[0, 0:1, :])
    u = jnp.dot(h.astype(BF16), w_ref[...], preferred_element_type=F32)
    c = MIX_CH
    ua_ref, ub_ref, uc_ref, uq_ref = outs[:4]
    ua_ref[0] = u[:, 0:2 * c]
    ub_ref[0] = u[:, 2 * c:5 * c]
    uc_ref[0] = u[:, 5 * c:6 * c]
    q = u[:, 6 * c:7 * c] * ATT_SCALE
    k = u[:, 7 * c:7 * c + c // 2]
    v = u[:, 7 * c + c // 2:8 * c]
    uq_ref[0] = q.astype(uq_ref.dtype)
    if rope:
        uqr_ref, ukv_ref = outs[4:]
        cs = cos_ref[...]
        sn = sin_ref[...]
        uqr_ref[0] = (q * cs + _swap_halves(q) * sn).astype(uqr_ref.dtype)
        kr = k * cs[:, :c // 2] + _swap_halves(k) * sn[:, :c // 2]
        ukv_ref[0] = jnp.concatenate([kr, v], axis=1).astype(ukv_ref.dtype)
    else:
        outs[4][0] = u[:, 7 * c:8 * c]


def in_projection(x, mods, mod_row0, norm_g, w_in_bf, *, res=None, rope=None, tm):
    b, l, d = x.shape
    c = MIX_CH
    grid = (b, l // tm)
    row = (lambda bb: 0) if mod_row0 is None else (lambda bb: mod_row0 + bb)
    xspec = pl.BlockSpec((1, tm, d), lambda bb, i: (bb, i, 0))
    mspec = pl.BlockSpec((1, N_MOD, d), lambda bb, i: (row(bb), 0, 0))
    args, specs = [x], [xspec]
    if res is not None:
        rargs, rspecs = _residual_specs(res, tm, d, row)
        args += rargs
        specs += rspecs
    args += [mods, norm_g.reshape(1, d), w_in_bf]
    specs += [mspec, pl.BlockSpec((1, d), lambda bb, i: (0, 0)),
              pl.BlockSpec(w_in_bf.shape, lambda bb, i: (0, 0))]
    if rope is not None:
        args += [rope[0], rope[1]]
        specs += [pl.BlockSpec((tm, c), lambda bb, i: (i, 0))] * 2

    def ospec(w):
        return pl.BlockSpec((1, tm, w), lambda bb, i: (bb, i, 0))

    out_shape, out_specs = [], []
    if res is not None:
        out_shape.append(jax.ShapeDtypeStruct((b, l, d), F32))
        out_specs.append(xspec)
    widths = [(2 * c, F32), (3 * c, F32), (c, F32), (c, BF16)] + ([(c, BF16), (c, BF16)] if rope is not None else [(c, F32)])
    for w, dt in widths:
        out_shape.append(jax.ShapeDtypeStruct((b, l, w), dt))
        out_specs.append(ospec(w))
    return pl.pallas_call(
        functools.partial(_in_body, fuse_res=res is not None, rope=rope is not None),
        grid=grid, in_specs=specs, out_specs=out_specs, out_shape=out_shape,
        compiler_params=_params("arbitrary", "arbitrary"),
        name="in_proj",
    )(*args)


def _dw_tile(win, w_ref, n_taps, first, rows):
    acc = w_ref[0:1, :] * win[first:first + rows]
    for k in range(1, n_taps):
        acc = acc + w_ref[k:k + 1, :] * win[first + k:first + k + rows]
    return acc


CONV_PAD = 16
CONV_ROWS = 64
CONV_CHUNK = 512
CONV_TAIL = CONV_PAD + CONV_ROWS + SUBLANES + SUBLANES


def _conf_body(u_ref, w_ref, b_ref, g_ref, beta_ref, o_ref, gp_ref, sh_ref, *, seq, n_taps, chunk):
    c = MIX_CH
    r = CONV_ROWS
    first = CONV_PAD - (n_taps - 1) // 2
    gp_ref[0:CONV_PAD, :] = jnp.zeros((CONV_PAD, c), F32)
    gp_ref[CONV_PAD + seq:CONV_PAD + seq + CONV_TAIL, :] = jnp.zeros((CONV_TAIL, c), F32)

    def fill(i, carry):
        r0 = pl.multiple_of(i * r, r)
        a = u_ref[0, pl.ds(r0, r), 0:c]
        g = u_ref[0, pl.ds(r0, r), c:2 * c]
        gp_ref[pl.ds(CONV_PAD + r0, r), :] = a * jax.nn.sigmoid(g)
        return carry

    lax.fori_loop(0, seq // r, fill, 0)
    n_copy_tiles = sh_ref.shape[1] // r

    def do_chunk(ci, carry):
        c0 = pl.multiple_of(ci * chunk, chunk)

        def shift_tile(ti, carry2):
            t0 = pl.multiple_of(ti * r, r)
            win = gp_ref[pl.ds(c0 + t0, r + SUBLANES), :]
            for m in range(1, SUBLANES):
                sh_ref[m - 1, pl.ds(t0, r), :] = win[m:m + r]
            return carry2

        lax.fori_loop(0, n_copy_tiles, shift_tile, 0)

        def tile(ti, carry2):
            t0 = pl.multiple_of(ti * r, r)
            acc = None
            for k in range(n_taps):
                a8, m = (first + k) // SUBLANES * SUBLANES, (first + k) % SUBLANES
                src = gp_ref[pl.ds(c0 + t0 + a8, r), :] if m == 0 else sh_ref[m - 1, pl.ds(t0 + a8, r), :]
                term = w_ref[k:k + 1, :] * src
                acc = term if acc is None else acc + term
            z = acc + b_ref[...]
            mu = jnp.mean(z, axis=-1, keepdims=True)
            zc = z - mu
            var = jnp.mean(zc * zc, axis=-1, keepdims=True)
            zn = zc * lax.rsqrt(var + EPS) * g_ref[...] + beta_ref[...]
            o_ref[0, pl.ds(c0 + t0, r), :] = _silu(zn).astype(o_ref.dtype)
            return carry2

        lax.fori_loop(0, chunk // r, tile, 0)
        return carry

    lax.fori_loop(0, seq // chunk, do_chunk, 0)


def conformer_conv(ua, dw_w, dw_b, ln_g, ln_b):
    b, l, c2 = ua.shape
    c = MIX_CH
    k = dw_w.shape[0]
    chunk = min(l, CONV_CHUNK)
    copy_rows = -(-(chunk + CONV_PAD + k) // CONV_ROWS) * CONV_ROWS
    assert l % chunk == 0 and k - 1 <= 2 * CONV_PAD and copy_rows - chunk + SUBLANES <= CONV_PAD + CONV_TAIL
    vec = pl.BlockSpec((1, c), lambda bb: (0, 0))
    return pl.pallas_call(
        functools.partial(_conf_body, seq=l, n_taps=k, chunk=chunk),
        grid=(b,),
        in_specs=[pl.BlockSpec((1, l, c2), lambda bb: (bb, 0, 0)),
                  pl.BlockSpec((k, c), lambda bb: (0, 0)), vec, vec, vec],
        out_specs=pl.BlockSpec((1, l, c), lambda bb: (bb, 0, 0)),
        out_shape=jax.ShapeDtypeStruct((b, l, c), BF16),
        scratch_shapes=[pltpu.VMEM((CONV_PAD + l + CONV_TAIL, c), F32),
                        pltpu.VMEM((SUBLANES - 1, copy_rows, c), F32)],
        compiler_params=_params("arbitrary"),
        name="conformer",
    )(ua, dw_w, dw_b.reshape(1, c), ln_g.reshape(1, c), ln_b.reshape(1, c))


SHORT_PAD = 8


def _short_body(u_ref, w_ref, b_ref, o_ref, xp_ref, *, seq, n_taps):
    c = MIX_CH
    r = CONV_ROWS
    half = (n_taps - 1) // 2
    zero = jnp.zeros((SHORT_PAD, c), F32)
    xp_ref[0:SHORT_PAD, :] = zero
    xp_ref[SHORT_PAD + seq:SHORT_PAD + seq + SHORT_PAD, :] = zero

    def fill(i, carry):
        r0 = pl.multiple_of(i * r, r)
        xp_ref[pl.ds(SHORT_PAD + r0, r), :] = u_ref[0, pl.ds(r0, r), :]
        return carry

    lax.fori_loop(0, seq // r, fill, 0)

    def tile(i, carry):
        r0 = pl.multiple_of(i * r, r)
        win = xp_ref[pl.ds(r0, r + 2 * SHORT_PAD), :]
        o_ref[0, pl.ds(r0, r), :] = _dw_tile(win, w_ref, n_taps, SHORT_PAD - half, r) + b_ref[...]
        return carry

    lax.fori_loop(0, seq // r, tile, 0)


def hyena_short_conv(ub, short_w, short_b):
    b, l, c3 = ub.shape
    c = MIX_CH
    k = short_w.shape[0]
    return pl.pallas_call(
        functools.partial(_short_body, seq=l, n_taps=k),
        grid=(b, c3 // c),
        in_specs=[pl.BlockSpec((1, l, c), lambda bb, j: (bb, 0, j)),
                  pl.BlockSpec((k, c), lambda bb, j: (0, j)),
                  pl.BlockSpec((1, c), lambda bb, j: (0, j))],
        out_specs=pl.BlockSpec((1, l, c), lambda bb, j: (bb, 0, j)),
        out_shape=jax.ShapeDtypeStruct((b, l, c3), F32),
        scratch_shapes=[pltpu.VMEM((l + 2 * SHORT_PAD, c), F32)],
        compiler_params=_params("arbitrary", "arbitrary"),
        name="hyena_short",
    )(ub, short_w, short_b.reshape(1, c3))


def _fft_sizes(seq):
    n = 2 * seq
    n2 = 128 if n >= 4096 else 32
    return n, n // n2, n2


FFT_GROUP = SUBLANES


def _kron_matrices(n, n1, h):
    k1 = np.arange(n1)[:, None]
    i1 = np.arange(h)[None, :]
    th = 2.0 * np.pi * ((k1 * i1) % n1) / n1
    eye = np.eye(FFT_GROUP)
    cs, sn = np.kron(np.cos(th), eye), np.kron(np.sin(th), eye)
    fwd = np.block([[cs, sn], [-sn, cs]])
    inv = np.block([[cs.T, -sn.T], [sn.T, cs.T]]) / n
    return fwd.astype(np.float32), inv.astype(np.float32)


def _twiddle_tables(n, n1, n2):
    k1 = jnp.arange(n1, dtype=jnp.int32)[:, None]
    i2 = jnp.arange(n2, dtype=jnp.int32)[None, :]
    th = ((k1 * i2) % n).astype(F32) * (2.0 * math.pi / n)
    shape = (n1, n2 // FFT_GROUP, FFT_GROUP, LANES)
    full = lambda a: jnp.broadcast_to(a.reshape(shape[:3] + (1,)), shape)
    return full(jnp.cos(th)), full(jnp.sin(th))


def _mid_tables(n2):
    a = np.arange(n2)
    th = 2.0 * np.pi * ((a[:, None] * a[None, :]) % n2) / n2
    cs, sn = np.cos(th), np.sin(th)
    fwd = np.concatenate([np.concatenate([cs, sn], 1), np.concatenate([-sn, cs], 1)], 0)
    inv = np.concatenate([np.concatenate([cs, -sn], 1), np.concatenate([sn, cs], 1)], 0)
    return fwd.astype(np.float32), inv.astype(np.float32)


def _dot(a, b):
    return jnp.dot(a.astype(BF16), b.astype(BF16), preferred_element_type=F32)


def _filter_stage1_body(x_ref, m_ref, c_ref, s_ref, o_ref, *, n1, gs):
    j, c = FFT_GROUP, MIX_CH
    for s in range(gs):
        a = _dot(m_ref[...], x_ref[:, s].reshape(n1 * j, c))
        ar = a[:n1 * j].reshape(n1, j, c)
        ai = a[n1 * j:].reshape(n1, j, c)
        cs, sn = _lanes(c_ref[:, s], c), _lanes(s_ref[:, s], c)
        o_ref[0, 0, :, s] = ar * cs + ai * sn
        o_ref[0, 1, :, s] = ai * cs - ar * sn


def filter_stage1(k, order, mat, twc, tws, *, n1, n2):
    g, j, c = n2 // FFT_GROUP, FFT_GROUP, MIX_CH
    gs = _group_step(g)
    tw = pl.BlockSpec((n1, gs, j, LANES), lambda gi: (0, gi, 0, 0))
    out = pl.pallas_call(
        functools.partial(_filter_stage1_body, n1=n1, gs=gs),
        grid=(g // gs,),
        in_specs=[pl.BlockSpec((n1, gs, j, c), lambda gi: (0, gi, 0, order)),
                  pl.BlockSpec(mat.shape, lambda gi: (0, 0)), tw, tw],
        out_specs=pl.BlockSpec((1, 2, n1, gs, j, c), lambda gi: (0, 0, 0, gi, 0, 0)),
        out_shape=jax.ShapeDtypeStruct((1, 2, n1, g, j, c), F32),
        compiler_params=_params("arbitrary"),
        name="filter_stage1",
    )(k.reshape(n1, g, j, k.shape[-1]), mat, twc, tws)
    return out.reshape(1, 2, n1 * n2, c)


def _mid_body(a_ref, k_ref, f_ref, g_ref, o_ref, *, n2, kc):
    for j in range(kc):
        rows = slice(j * n2, (j + 1) * n2)
        blk = jnp.concatenate([a_ref[0, 0, rows, :], a_ref[0, 1, rows, :]], axis=0)
        s = _dot(f_ref[...], blk)
        sr, si = s[:n2], s[n2:]
        kr, ki = k_ref[0, rows, :], k_ref[1, rows, :]
        y = jnp.concatenate([sr * kr - si * ki, sr * ki + si * kr], axis=0)
        bb = _dot(g_ref[...], y)
        o_ref[0, 0, rows, :] = bb[:n2]
        o_ref[0, 1, rows, :] = bb[n2:]


def fft_mid(a, kf, f_fwd, f_inv, *, n1, n2):
    p = a.shape[0]
    c = MIX_CH
    kc = min(n1, 8)
    rows = kc * n2
    blk = pl.BlockSpec((1, 2, rows, c), lambda j, pp: (pp, 0, j, 0))
    mat = pl.BlockSpec(f_fwd.shape, lambda j, pp: (0, 0))
    return pl.pallas_call(
        functools.partial(_mid_body, n2=n2, kc=kc),
        grid=(n1 // kc, p),
        in_specs=[blk, pl.BlockSpec((2, rows, c), lambda j, pp: (0, j, 0)), mat, mat],
        out_specs=blk,
        out_shape=jax.ShapeDtypeStruct(a.shape, F32),
        compiler_params=_params("arbitrary", "arbitrary"),
        name="fft_mid",
    )(a, kf, f_fwd, f_inv)


def _filter_mid_body(a_ref, f_ref, sum_ref, o_ref, *, n2, kc):
    inv = 1.0 / (sum_ref[0:1, :] + EPS)
    for j in range(kc):
        rows = slice(j * n2, (j + 1) * n2)
        blk = jnp.concatenate([a_ref[0, 0, rows, :], a_ref[0, 1, rows, :]], axis=0)
        s = _dot(f_ref[...], blk) * inv
        o_ref[0, rows, :] = s[:n2]
        o_ref[1, rows, :] = s[n2:]


def filter_mid(a, f_fwd, abs_sum, order, *, n1, n2):
    c = MIX_CH
    kc = min(n1, 8)
    rows = kc * n2
    return pl.pallas_call(
        functools.partial(_filter_mid_body, n2=n2, kc=kc),
        grid=(n1 // kc,),
        in_specs=[pl.BlockSpec((1, 2, rows, c), lambda j: (0, 0, j, 0)),
                  pl.BlockSpec(f_fwd.shape, lambda j: (0, 0)),
                  pl.BlockSpec((SUBLANES, c), lambda j: (0, order))],
        out_specs=pl.BlockSpec((2, rows, c), lambda j: (0, j, 0)),
        out_shape=jax.ShapeDtypeStruct((2, n1 * n2, c), F32),
        compiler_params=_params("arbitrary"),
        name="filter_mid",
    )(a, f_fwd, abs_sum)


def _lanes(t, width):
    return t if width == LANES else jnp.concatenate([t] * (width // LANES), axis=-1)


def _kron1_body(zr_ref, zi_ref, m_ref, c_ref, s_ref, o_ref, *, n1, h, gs):
    j, c = FFT_GROUP, MIX_CH
    for s in range(gs):
        xr = zr_ref[0, :, s].reshape(h * j, c)
        xi = zi_ref[0, :, s].reshape(h * j, c)
        a = _dot(m_ref[...], jnp.concatenate([xr, xi], axis=0))
        ar = a[:n1 * j].reshape(n1, j, c)
        ai = a[n1 * j:].reshape(n1, j, c)
        cs, sn = _lanes(c_ref[:, s], c), _lanes(s_ref[:, s], c)
        o_ref[0, 0, :, s] = ar * cs + ai * sn
        o_ref[0, 1, :, s] = ai * cs - ar * sn


def _group_step(n_groups):
    return min(n_groups, 4)


def kron_stage1(z5, col, n_pairs, imag_offset, mat, twc, tws, *, n1, h):
    g, j, c = z5.shape[2], FFT_GROUP, MIX_CH
    gs = _group_step(g)
    tw = pl.BlockSpec((n1, gs, j, LANES), lambda gi, p: (0, gi, 0, 0))
    return pl.pallas_call(
        functools.partial(_kron1_body, n1=n1, h=h, gs=gs),
        grid=(g // gs, n_pairs),
        in_specs=[pl.BlockSpec((1, h, gs, j, c), lambda gi, p: (p, 0, gi, 0, col)),
                  pl.BlockSpec((1, h, gs, j, c), lambda gi, p: (p + imag_offset, 0, gi, 0, col)),
                  pl.BlockSpec(mat.shape, lambda gi, p: (0, 0)), tw, tw],
        out_specs=pl.BlockSpec((1, 2, n1, gs, j, c), lambda gi, p: (p, 0, 0, gi, 0, 0)),
        out_shape=jax.ShapeDtypeStruct((n_pairs, 2, n1, g, j, c), F32),
        compiler_params=_params("arbitrary", "arbitrary"),
        name="fft_kron1",
    )(z5, z5, mat, twc, tws)


def _kron3_body(b_ref, m_ref, c_ref, s_ref, zr_ref, zi_ref, gr_ref, gi_ref, d_ref, o_ref, *, n1, h, gs):
    j, c = FFT_GROUP, MIX_CH
    d = d_ref[...].reshape(1, 1, c)
    for s in range(gs):
        br, bi = b_ref[0, 0, :, s], b_ref[0, 1, :, s]
        cs, sn = _lanes(c_ref[:, s], c), _lanes(s_ref[:, s], c)
        xr = (br * cs - bi * sn).reshape(n1 * j, c)
        xi = (br * sn + bi * cs).reshape(n1 * j, c)
        y = _dot(m_ref[...], jnp.concatenate([xr, xi], axis=0))
        yr = y[:h * j].reshape(h, j, c)
        yi = y[h * j:].reshape(h, j, c)
        o_ref[0, 0, :, s] = gr_ref[0, :, s] * (yr + d * zr_ref[0, :, s])
        o_ref[1, 0, :, s] = gi_ref[0, :, s] * (yi + d * zi_ref[0, :, s])


def kron_stage3(bw, mat, twc, tws, z5, z_col, g5, g_col, d_vec, imag_offset, *, n1, h):
    p = bw.shape[0]
    g, j, c = bw.shape[3], FFT_GROUP, MIX_CH
    gs = _group_step(g)
    tw = pl.BlockSpec((n1, gs, j, LANES), lambda gi, pp: (0, gi, 0, 0))

    def src(col, off):
        return pl.BlockSpec((1, h, gs, j, c), lambda gi, pp: (pp + off, 0, gi, 0, col))

    return pl.pallas_call(
        functools.partial(_kron3_body, n1=n1, h=h, gs=gs),
        grid=(g // gs, p),
        in_specs=[pl.BlockSpec((1, 2, n1, gs, j, c), lambda gi, pp: (pp, 0, 0, gi, 0, 0)),
                  pl.BlockSpec(mat.shape, lambda gi, pp: (0, 0)), tw, tw,
                  src(z_col, 0), src(z_col, imag_offset), src(g_col, 0), src(g_col, imag_offset),
                  pl.BlockSpec((1, c), lambda gi, pp: (0, 0))],
        out_specs=pl.BlockSpec((2, 1, h, gs, j, c), lambda gi, pp: (0, pp, 0, gi, 0, 0)),
        out_shape=jax.ShapeDtypeStruct((2, p, h, g, j, c), F32),
        compiler_params=_params("arbitrary", "arbitrary"),
        name="fft_kron3",
    )(bw, mat, twc, tws, z5, z5, g5, g5, d_vec.reshape(1, c))


FILTER_TILE = 512


def _filter_body(fw1_ref, fb1_ref, fr1_ref, fw2_ref, fb2_ref, fr2_ref, fw3_ref, fb3_ref, bands_ref, decay_ref,
                 k_ref, sum_ref, *, seq):
    i = pl.program_id(0)
    c = MIX_CH
    hp = lax.Precision.HIGHEST
    n = i * FILTER_TILE + lax.broadcasted_iota(jnp.int32, (FILTER_TILE, 1), 0)
    pos = jnp.where(n <= seq, n, 2 * seq - n).astype(F32)
    t = pos * (1.0 / (seq - 1))
    ang = (pos * (2.0 * math.pi / seq)) * bands_ref[...]
    nb = bands_ref.shape[1]
    pre = (t * fw1_ref[0:1, :]
           + jnp.dot(jnp.cos(ang), fw1_ref[1:1 + nb, :], preferred_element_type=F32, precision=hp)
           - jnp.dot(jnp.sin(ang), fw1_ref[1 + nb:1 + 2 * nb, :], preferred_element_type=F32, precision=hp)
           + fb1_ref[...])
    h = jnp.sin(fr1_ref[...] * pre)
    h = jnp.sin(fr2_ref[...] * (jnp.dot(h, fw2_ref[...], preferred_element_type=F32, precision=hp) + fb2_ref[...]))
    h = jnp.dot(h, fw3_ref[...], preferred_element_type=F32, precision=hp) + fb3_ref[...]
    win = jnp.exp(-t * decay_ref[...]) + HYENA_SHIFT
    win = jnp.concatenate([win, win], axis=1)
    fwd, bwd = h[:, :2 * c], h[:, 2 * c:]
    k = jnp.where(n < seq, fwd, bwd) + jnp.where(n == 0, bwd, 0.0)
    k = jnp.where(n == seq, 0.0, k) * win
    k_ref[...] = k

    @pl.when(i == 0)
    def _():
        sum_ref[...] = jnp.zeros(sum_ref.shape, F32)

    sum_ref[...] = sum_ref[...] + jnp.sum(jnp.abs(k), axis=0, keepdims=True)


def hyena_filter_time(seq, fw1, fb1, fr1, fw2, fb2, fr2, fw3, fb3):
    c = MIX_CH
    n_bands = (fw1.shape[0] - 1) // 2
    bands = jnp.linspace(1e-4, n_bands - 1, n_bands, dtype=F32)[None, :]
    max_decay = math.log(HYENA_DECAY_TARGET) / HYENA_FAST_PCT
    min_decay = math.log(HYENA_DECAY_TARGET) / HYENA_SLOW_PCT
    decay = jnp.abs(jnp.linspace(min_decay, max_decay, c, dtype=F32))[None, :]
    assert (2 * seq) % FILTER_TILE == 0
    args = [fw1, fb1[None, :], fr1[None, :], fw2, fb2[None, :], fr2[None, :], fw3, fb3[None, :], bands, decay]
    return pl.pallas_call(
        functools.partial(_filter_body, seq=seq),
        grid=(2 * seq // FILTER_TILE,),
        in_specs=[pl.BlockSpec(a.shape, lambda i: (0, 0)) for a in args],
        out_specs=[pl.BlockSpec((FILTER_TILE, 2 * c), lambda i: (i, 0)),
                   pl.BlockSpec((SUBLANES, 2 * c), lambda i: (0, 0))],
        out_shape=[jax.ShapeDtypeStruct((2 * seq, 2 * c), F32), jax.ShapeDtypeStruct((SUBLANES, 2 * c), F32)],
        compiler_params=_params("arbitrary"),
        name="hyena_filter",
    )(*args)


def hyena_filter_spectrum(seq, filt, tabs):
    n, n1, n2 = _fft_sizes(seq)
    k, abs_sum = hyena_filter_time(seq, *filt)
    twc, tws, f_fwd, m_real = tabs[2], tabs[3], tabs[4], tabs[6]
    return [filter_mid(filter_stage1(k, o, m_real, twc, tws, n1=n1, n2=n2), f_fwd, abs_sum, o, n1=n1, n2=n2)
            for o in range(2)]


def hyena_mixer(ub, short_w, short_b, hy_d, kf, tabs):
    b, l, _ = ub.shape
    n, n1, n2 = _fft_sizes(l)
    h = n1 // 2
    p = b // 2
    c = MIX_CH
    g = n2 // FFT_GROUP
    u = hyena_short_conv(ub, short_w, short_b)
    u5 = u.reshape(b, h, g, FFT_GROUP, u.shape[-1])
    m_fwd, m_inv, twc, tws, f_fwd, f_inv = tabs[:6]
    z5 = u5
    for o in range(2):
        a = kron_stage1(z5, 0, p, p, m_fwd, twc, tws, n1=n1, h=h)
        bw = fft_mid(a.reshape(p, 2, n, c), kf[o], f_fwd, f_inv, n1=n1, n2=n2)
        z = kron_stage3(bw.reshape(a.shape), m_inv, twc, tws, z5, 0, u5, 1 + o, hy_d[o], p, n1=n1, h=h)
        z5 = z.reshape(b, h, g, FFT_GROUP, c)
    return z5.reshape(b, l, c)


def hyena_tables(seq):
    n, n1, n2 = _fft_sizes(seq)
    m_fwd, m_inv = _kron_matrices(n, n1, n1 // 2)
    m_real = _kron_matrices(n, n1, n1)[0][:, :n1 * FFT_GROUP]
    twc, tws = _twiddle_tables(n, n1, n2)
    f_fwd, f_inv = _mid_tables(n2)
    bf = lambda a: jnp.asarray(a).astype(BF16)
    return bf(m_fwd), bf(m_inv), twc, tws, bf(f_fwd), bf(f_inv), bf(m_real)


def _fnet_body(cl_ref, sl_ref, x_ref, cc_ref, sc_ref, o_ref, *, scale):
    x = x_ref[0].astype(BF16)
    pr = jnp.dot(cl_ref[...], x, preferred_element_type=F32).astype(BF16)
    qr = jnp.dot(sl_ref[...], x, preferred_element_type=F32).astype(BF16)
    o_ref[0] = (jnp.dot(pr, cc_ref[...], preferred_element_type=F32)
                - jnp.dot(qr, sc_ref[...], preferred_element_type=F32)) * scale


def _dft_tables(n):
    a = jnp.arange(n, dtype=jnp.int32)
    th = ((a[:, None] * a[None, :]) % n).astype(F32) * (2.0 * math.pi / n)
    return jnp.cos(th).astype(BF16), jnp.sin(th).astype(BF16)


FNET_DIRECT_MAX = 1024


def _fnet1_body(x_ref, ccs_ref, m_ref, c_ref, s_ref, o_ref, *, n1, gs):
    j, c = FFT_GROUP, MIX_CH
    for s in range(gs):
        x = x_ref[0, :, s].reshape(n1 * j, c).astype(BF16)
        z = jnp.dot(x, ccs_ref[...], preferred_element_type=F32)
        a = _dot(m_ref[...], jnp.concatenate([z[:, :c], z[:, c:]], axis=0))
        ar = a[:n1 * j].reshape(n1, j, c)
        ai = a[n1 * j:].reshape(n1, j, c)
        cs, sn = _lanes(c_ref[:, s], c), _lanes(s_ref[:, s], c)
        o_ref[0, 0, :, s] = ar * cs + ai * sn
        o_ref[0, 1, :, s] = ai * cs - ar * sn


def _fnet2_body(a_ref, m_ref, o_ref, *, n2, scale):
    j, c = FFT_GROUP, MIX_CH
    x = a_ref[0].reshape(2 * j * n2, c)
    y = _dot(m_ref[...], x) * scale
    o_ref[0, :, 0] = y.reshape(n2, j, c)


def fnet_tables(seq):
    c = MIX_CH
    cc, sc = _dft_tables(c)
    if seq <= FNET_DIRECT_MAX:
        return _dft_tables(seq) + (cc, sc)
    n2 = 128
    n1 = seq // n2
    m_fwd, _ = _kron_matrices(seq, n1, n1)
    twc, tws = _twiddle_tables(seq, n1, n2)
    a = np.arange(n2)
    th = 2.0 * np.pi * ((a[:, None] * a[None, :]) % n2) / n2
    eye = np.eye(FFT_GROUP)
    m2 = np.concatenate([np.einsum('kn,ij->kijn', f, eye).reshape(n2 * FFT_GROUP, FFT_GROUP * n2)
                         for f in (np.cos(th), np.sin(th))], axis=1).astype(np.float32)
    ccs = jnp.concatenate([cc, -sc], axis=1)
    return ccs, jnp.asarray(m_fwd).astype(BF16), twc, tws, jnp.asarray(m2).astype(BF16)


def fnet_two_stage(uc, tables):
    b, l, c = uc.shape
    ccs, m_fwd, twc, tws, m2 = tables
    j = FFT_GROUP
    n2 = 128
    n1 = l // n2
    g = n2 // j
    gs = _group_step(g)
    tw = pl.BlockSpec((n1, gs, j, LANES), lambda gi, bb: (0, gi, 0, 0))
    a = pl.pallas_call(
        functools.partial(_fnet1_body, n1=n1, gs=gs),
        grid=(g // gs, b),
        in_specs=[pl.BlockSpec((1, n1, gs, j, c), lambda gi, bb: (bb, 0, gi, 0, 0)),
                  pl.BlockSpec(ccs.shape, lambda gi, bb: (0, 0)),
                  pl.BlockSpec(m_fwd.shape, lambda gi, bb: (0, 0)), tw, tw],
        out_specs=pl.BlockSpec((1, 2, n1, gs, j, c), lambda gi, bb: (bb, 0, 0, gi, 0, 0)),
        out_shape=jax.ShapeDtypeStruct((b, 2, n1, g, j, c), F32),
        compiler_params=_params("arbitrary", "arbitrary"),
        name="fnet_stage1",
    )(uc.reshape(b, n1, g, j, c), ccs, m_fwd, twc, tws)
    out = pl.pallas_call(
        functools.partial(_fnet2_body, n2=n2, scale=1.0 / math.sqrt(l * c)),
        grid=(b, n1 // j),
        in_specs=[pl.BlockSpec((1, 2, j, n2, c), lambda bb, q: (bb, 0, q, 0, 0)),
                  pl.BlockSpec(m2.shape, lambda bb, q: (0, 0))],
        out_specs=pl.BlockSpec((1, n2, 1, j, c), lambda bb, q: (bb, 0, q, 0, 0)),
        out_shape=jax.ShapeDtypeStruct((b, n2, n1 // j, j, c), F32),
        compiler_params=_params("arbitrary", "arbitrary"),
        name="fnet_stage2",
    )(a.reshape(b, 2, n1, n2, c), m2)
    return out.reshape(b, l, c)


def fnet_mixer(uc, tables):
    b, l, c = uc.shape
    if l > FNET_DIRECT_MAX:
        return fnet_two_stage(uc, tables)
    cl, sl, cc, sc = tables
    tm = min(l, 512)
    row = pl.BlockSpec((tm, l), lambda i, bb: (i, 0))
    sq = pl.BlockSpec((c, c), lambda i, bb: (0, 0))
    return pl.pallas_call(
        functools.partial(_fnet_body, scale=1.0 / math.sqrt(l * c)),
        grid=(l // tm, b),
        in_specs=[row, row, pl.BlockSpec((1, l, c), lambda i, bb: (bb, 0, 0)), sq, sq],
        out_specs=pl.BlockSpec((1, tm, c), lambda i, bb: (bb, i, 0)),
        out_shape=jax.ShapeDtypeStruct((b, l, c), F32),
        compiler_params=_params("arbitrary", "arbitrary"),
        name="fnet",
    )(cl, sl, uc, cc, sc)


def _heads_rows(x, g):
    h0 = Q_PER_KV * g
    return jnp.concatenate([x[:, (h0 + r) * HEAD_DIM:(h0 + r + 1) * HEAD_DIM] for r in range(Q_PER_KV)], axis=0)


def _qk(q, k):
    return lax.dot_general(q.astype(BF16), k.astype(BF16), (((1,), (1,)), ((), ())),
                           preferred_element_type=F32)


def _sink_col(sink_ref, g, rows):
    ridx = lax.broadcasted_iota(jnp.int32, (Q_PER_KV * rows, 1), 0)
    col = jnp.full((Q_PER_KV * rows, 1), sink_ref[Q_PER_KV * g], F32)
    for r in range(1, Q_PER_KV):
        col = jnp.where(ridx >= r * rows, sink_ref[Q_PER_KV * g + r], col)
    return col * LOG2E


def _lat_attn_body(sink_ref, q_ref, qr_ref, kp_ref, kc_ref, kn_ref, ck_ref, cv_ref, o_ref, *, sub):
    i = pl.program_id(1)
    n_qblk = pl.num_programs(1) * sub
    blk = ATT_BLOCK
    span = blk + 2 * WINDOW
    kv = jnp.concatenate([kp_ref[0], kc_ref[0], kn_ref[0]], axis=0)
    ck = ck_ref[0, 0].astype(BF16)
    cv = cv_ref[0, 0].astype(BF16)
    kvw = N_KV_HEADS * HEAD_DIM
    r = lax.broadcasted_iota(jnp.int32, (Q_PER_KV * blk, span), 0) % blk
    j = lax.broadcasted_iota(jnp.int32, (Q_PER_KV * blk, span), 1)
    band = (j >= r) & (j <= r + 2 * WINDOW)
    for s in range(sub):
        qi = i * sub + s
        ok = band & ((qi > 0) | (j >= WINDOW)) & ((qi < n_qblk - 1) | (j < WINDOW + blk))
        q = q_ref[0, s * blk:(s + 1) * blk, :]
        qr = qr_ref[0, s * blk:(s + 1) * blk, :]
        outs = []
        for g in range(N_KV_HEADS):
            kl = kv[s * blk:s * blk + span, g * HEAD_DIM:(g + 1) * HEAD_DIM]
            vl = kv[s * blk:s * blk + span, kvw + g * HEAD_DIM:kvw + (g + 1) * HEAD_DIM]
            s_loc = jnp.where(ok, _qk(_heads_rows(qr, g), kl), NEG_INF)
            s_ctx = _qk(_heads_rows(q, g), ck[:, g * HEAD_DIM:(g + 1) * HEAD_DIM])
            sink = _sink_col(sink_ref, g, blk)
            m = jnp.maximum(jnp.maximum(jnp.max(s_loc, axis=-1, keepdims=True),
                                        jnp.max(s_ctx, axis=-1, keepdims=True)), sink)
            e_loc = jnp.exp2(s_loc - m)
            e_ctx = jnp.exp2(s_ctx - m)
            den = (jnp.sum(e_loc, axis=-1, keepdims=True) + jnp.sum(e_ctx, axis=-1, keepdims=True)
                   + jnp.exp2(sink - m))
            o = (jnp.dot(e_loc.astype(BF16), vl, preferred_element_type=F32)
                 + jnp.dot(e_ctx.astype(BF16), cv[:, g * HEAD_DIM:(g + 1) * HEAD_DIM],
                           preferred_element_type=F32)) * (1.0 / den)
            outs += [o[rr * blk:(rr + 1) * blk] for rr in range(Q_PER_KV)]
        o_ref[0, s * blk:(s + 1) * blk, :] = jnp.concatenate(outs, axis=1).astype(o_ref.dtype)


ATT_SUB = 4


def latent_attention(uq, uqr, ukv, cache_k, cache_v, layer, sink):
    b, l, c = uq.shape
    p = cache_k.shape[2]
    blk = ATT_BLOCK
    nblk = l // blk
    sub = math.gcd(ATT_SUB, nblk)
    rows = sub * blk
    qspec = pl.BlockSpec((1, rows, c), lambda bb, i: (bb, i, 0))
    cspec = pl.BlockSpec((1, 1, p, cache_k.shape[3]), lambda bb, i: (bb, layer, 0, 0))
    return pl.pallas_call(
        functools.partial(_lat_attn_body, sub=sub),
        grid=(b, nblk // sub),
        in_specs=[pl.BlockSpec(memory_space=pltpu.SMEM), qspec, qspec,
                  pl.BlockSpec((1, blk, c), lambda bb, i: (bb, jnp.maximum(i * sub - 1, 0), 0)),
                  qspec,
                  pl.BlockSpec((1, blk, c), lambda bb, i: (bb, jnp.minimum((i + 1) * sub, nblk - 1), 0)),
                  cspec, cspec],
        out_specs=qspec,
        out_shape=jax.ShapeDtypeStruct((b, l, c), BF16),
        compiler_params=_params("arbitrary", "arbitrary"),
        name="latent_attention",
    )(sink, uq, uqr, ukv, ukv, ukv, cache_k, cache_v)


def _ctx_attn_body(sink_ref, q_ref, kv_ref, o_ref, *, seq):
    q = q_ref[0]
    kv = kv_ref[0]
    kvw = N_KV_HEADS * HEAD_DIM
    outs = []
    for g in range(N_KV_HEADS):
        kl = kv[:, g * HEAD_DIM:(g + 1) * HEAD_DIM]
        vl = kv[:, kvw + g * HEAD_DIM:kvw + (g + 1) * HEAD_DIM]
        s = _qk(_heads_rows(q, g), kl)
        sink = _sink_col(sink_ref, g, seq)
        m = jnp.maximum(jnp.max(s, axis=-1, keepdims=True), sink)
        e = jnp.exp2(s - m)
        den = jnp.sum(e, axis=-1, keepdims=True) + jnp.exp2(sink - m)
        o = jnp.dot(e.astype(BF16), vl.astype(BF16), preferred_element_type=F32) * (1.0 / den)
        outs += [o[rr * seq:(rr + 1) * seq] for rr in range(Q_PER_KV)]
    o_ref[0] = jnp.concatenate(outs, axis=1)


def context_attention(uq, ukv, sink):
    b, s, c = uq.shape
    spec = pl.BlockSpec((1, s, c), lambda bb: (bb, 0, 0))
    return pl.pallas_call(
        functools.partial(_ctx_attn_body, seq=s),
        grid=(b,),
        in_specs=[pl.BlockSpec(memory_space=pltpu.SMEM), spec, spec],
        out_specs=spec,
        out_shape=jax.ShapeDtypeStruct((b, s, c), F32),
        compiler_params=_params("arbitrary"),
        name="context_attention",
    )(sink, uq, ukv)


def _pack_bf16_pairs(hi_rounded):
    k = hi_rounded.shape[1] // 2
    bits = lax.bitcast_convert_type(hi_rounded, jnp.uint32)
    return bits[:, :k] | (bits[:, k:] >> 16)


def _unpack_bf16_pairs(packed, dtype=BF16):
    a = lax.bitcast_convert_type(packed & jnp.uint32(0xFFFF0000), F32)
    b = lax.bitcast_convert_type(packed << 16, F32)
    return jnp.concatenate([a, b], axis=1).astype(dtype)


def _out_body(ya_ref, yb_ref, yc_ref, yd_ref, x_ref, mod_ref, g_ref, w_ref, rw_ref, rb_ref,
              x1_ref, h_ref, route_ref, cnt_ref):
    c = MIX_CH
    y = jnp.dot(ya_ref[0].astype(BF16), w_ref[0:c, :], preferred_element_type=F32)
    for j, ref in enumerate((yb_ref, yc_ref, yd_ref), start=1):
        y = y + jnp.dot(ref[0].astype(BF16), w_ref[j * c:(j + 1) * c, :], preferred_element_type=F32)
    x1 = x_ref[0] + mod_ref[0, 2:3, :] * y
    x1_ref[0] = x1
    h = _rmsnorm_mod(x1, g_ref[...], mod_ref[0, 4:5, :], mod_ref[0, 3:4, :])
    h_hi = h.astype(BF16)
    h_hi32 = h_hi.astype(F32)
    h_ref[0] = _pack_bf16_pairs(h_hi32)
    h_lo = (h - h_hi32).astype(BF16)
    tm = h.shape[0]
    prod = jnp.dot(jnp.concatenate([h_hi, h_lo], axis=0), rw_ref[...], preferred_element_type=F32)
    logits = (prod[:tm, :ROUTE_LANES] + prod[:tm, ROUTE_LANES:]
              + prod[tm:, :ROUTE_LANES] + prod[tm:, ROUTE_LANES:]) + rb_ref[...]
    lane = lax.broadcasted_iota(jnp.int32, logits.shape, 1)
    is_c = lane < N_GROUPS
    lc = jnp.where(is_c, logits, NEG_INF)
    mc = jnp.max(lc, axis=-1, keepdims=True)
    grp = jnp.min(jnp.where(lc == mc, lane, ROUTE_LANES), axis=-1, keepdims=True)
    pg = 1.0 / jnp.sum(jnp.where(is_c, jnp.exp(lc - mc), 0.0), axis=-1, keepdims=True)
    lo = N_GROUPS + grp * EXPERTS_PER_GROUP
    in_g = (lane >= lo) & (lane < lo + EXPERTS_PER_GROUP)
    lf = jnp.where(in_g, logits, NEG_INF)
    t1 = jnp.max(lf, axis=-1, keepdims=True)
    i1 = jnp.min(jnp.where(lf == t1, lane, ROUTE_LANES), axis=-1, keepdims=True)
    lf2 = jnp.where(lane == i1, NEG_INF, lf)
    t2 = jnp.max(lf2, axis=-1, keepdims=True)
    i2 = jnp.min(jnp.where(lf2 == t2, lane, ROUTE_LANES), axis=-1, keepdims=True)
    e2 = jnp.exp(t2 - t1)
    w1 = pg / (1.0 + e2)
    w2 = pg * e2 / (1.0 + e2)
    rec = jnp.where(lane == 0, (i1 - N_GROUPS).astype(F32),
                    jnp.where(lane == 1, (i2 - N_GROUPS).astype(F32),
                              jnp.where(lane == 2, w1, jnp.where(lane == 3, w2, 0.0))))
    route_ref[0] = rec

    @pl.when((pl.program_id(0) == 0) & (pl.program_id(1) == 0))
    def _():
        cnt_ref[...] = jnp.zeros(cnt_ref.shape, F32)

    e0, e1 = _choice_onehots(rec)
    cnt_ref[...] = cnt_ref[...] + jnp.sum(e0 + e1, axis=0, keepdims=True)


def out_projection(ys, x, mods, mod_row0, norm_g, w_out_bf, rw, rb, *, tm):
    b, l, d = x.shape
    c = MIX_CH
    row = (lambda bb: 0) if mod_row0 is None else (lambda bb: mod_row0 + bb)
    yspec = pl.BlockSpec((1, tm, c), lambda bb, i: (bb, i, 0))
    xspec = pl.BlockSpec((1, tm, d), lambda bb, i: (bb, i, 0))
    return pl.pallas_call(
        _out_body,
        grid=(b, l // tm),
        in_specs=[yspec] * 4 + [xspec,
                                pl.BlockSpec((1, N_MOD, d), lambda bb, i: (row(bb), 0, 0)),
                                pl.BlockSpec((1, d), lambda bb, i: (0, 0)),
                                pl.BlockSpec(w_out_bf.shape, lambda bb, i: (0, 0)),
                                pl.BlockSpec(rw.shape, lambda bb, i: (0, 0)),
                                pl.BlockSpec(rb.shape, lambda bb, i: (0, 0))],
        out_specs=[xspec, pl.BlockSpec((1, tm, d // 2), lambda bb, i: (bb, i, 0)),
                   pl.BlockSpec((1, tm, ROUTE_LANES), lambda bb, i: (bb, i, 0)),
                   pl.BlockSpec((SUBLANES, ROUTE_LANES), lambda bb, i: (0, 0))],
        out_shape=[jax.ShapeDtypeStruct((b, l, d), F32), jax.ShapeDtypeStruct((b, l, d // 2), jnp.uint32),
                   jax.ShapeDtypeStruct((b, l, ROUTE_LANES), F32),
                   jax.ShapeDtypeStruct((SUBLANES, ROUTE_LANES), F32)],
        compiler_params=_params("arbitrary", "arbitrary"),
        name="out_proj",
    )(*ys, x, mods, norm_g.reshape(1, d), w_out_bf, rw, rb)


def _expert_body(be_ref, nv_ref, xs_ref, wg_ref, wu_ref, wd_ref, o_ref, wg_s, wu_s, wd_s):
    i = pl.program_id(0)
    prev = be_ref[jnp.maximum(i - 1, 0)]

    @pl.when((i == 0) | (be_ref[i] != prev))
    def _():
        wg_s[...] = wg_ref[0, 0].astype(BF16)
        wu_s[...] = wu_ref[0, 0].astype(BF16)
        wd_s[...] = wd_ref[0, 0].astype(BF16)

    @pl.when(nv_ref[i] > 0)
    def _():
        row = lax.broadcasted_iota(jnp.int32, xs_ref.shape, 0)
        x = _unpack_bf16_pairs(jnp.where(row < nv_ref[i], xs_ref[...], jnp.uint32(0)))
        g = jnp.dot(x, wg_s[...], preferred_element_type=F32)
        u = jnp.dot(x, wu_s[...], preferred_element_type=F32)
        a = (_silu(g) * u).astype(BF16)
        y = jnp.dot(a, wd_s[...], preferred_element_type=F32)
        o_ref[...] = _pack_bf16_pairs(y.astype(BF16).astype(F32))

    @pl.when(nv_ref[i] <= 0)
    def _():
        o_ref[...] = jnp.zeros(o_ref.shape, jnp.uint32)


def expert_ffn(xs, blk_e, n_valid, layer, e_gate, e_up, e_down):
    rows, dh = xs.shape
    d = 2 * dh
    nb = rows // MOE_BLOCK
    de = e_gate.shape[-1]
    grid_spec = pltpu.PrefetchScalarGridSpec(
        num_scalar_prefetch=2,
        grid=(nb,),
        in_specs=[pl.BlockSpec((MOE_BLOCK, dh), lambda i, be, nv: (i, 0)),
                  pl.BlockSpec((1, 1, d, de), lambda i, be, nv: (layer, be[i], 0, 0)),
                  pl.BlockSpec((1, 1, d, de), lambda i, be, nv: (layer, be[i], 0, 0)),
                  pl.BlockSpec((1, 1, de, d), lambda i, be, nv: (layer, be[i], 0, 0))],
        out_specs=pl.BlockSpec((MOE_BLOCK, dh), lambda i, be, nv: (i, 0)),
        scratch_shapes=[pltpu.VMEM((d, de), BF16), pltpu.VMEM((d, de), BF16), pltpu.VMEM((de, d), BF16)],
    )
    return pl.pallas_call(
        _expert_body, grid_spec=grid_spec,
        out_shape=jax.ShapeDtypeStruct((rows, dh), jnp.uint32),
        compiler_params=_params("arbitrary"),
        name="expert_ffn",
    )(blk_e, n_valid, xs, e_gate, e_up, e_down)


RANK_TILE = 512


def _choice_onehots(rec):
    lanef = lax.broadcasted_iota(jnp.int32, rec.shape, 1).astype(F32)
    return (lanef == rec[:, 0:1]).astype(F32), (lanef == rec[:, 1:2]).astype(F32)


def _slot_body(route_ref, cnt_ref, slot_ref, carry_ref):
    t = RANK_TILE
    rec = route_ref[...]
    lane = lax.broadcasted_iota(jnp.int32, rec.shape, 1)
    e0, e1 = _choice_onehots(rec)
    both = e0 + e1

    @pl.when(pl.program_id(0) == 0)
    def _():
        cnt = cnt_ref[...]
        padded = jnp.floor((cnt + (MOE_BLOCK - 1)) * (1.0 / MOE_BLOCK)) * MOE_BLOCK
        ln = lax.broadcasted_iota(jnp.int32, cnt.shape, 1)
        incl = padded
        sh = 1
        while sh < ROUTE_LANES:
            incl = incl + jnp.where(ln >= sh, pltpu.roll(incl, sh, axis=1), 0.0)
            sh *= 2
        carry_ref[...] = incl - padded

    earlier = (lax.broadcasted_iota(jnp.int32, (t, t), 0) > lax.broadcasted_iota(jnp.int32, (t, t), 1))
    before = jnp.dot(earlier.astype(BF16), both.astype(BF16), preferred_element_type=F32) + carry_ref[0:1, :]
    s0 = jnp.sum(e0 * before, axis=-1, keepdims=True)
    s1 = jnp.sum(e1 * before, axis=-1, keepdims=True)
    slot_ref[...] = jnp.where(lane == 0, s0, jnp.where(lane == 1, s1, 0.0))
    carry_ref[...] = carry_ref[...] + jnp.sum(both, axis=0, keepdims=True)


def moe_slots(route, counts):
    n = route.shape[0]
    t = RANK_TILE
    return pl.pallas_call(
        _slot_body,
        grid=(n // t,),
        in_specs=[pl.BlockSpec((t, ROUTE_LANES), lambda i: (i, 0)),
                  pl.BlockSpec((SUBLANES, ROUTE_LANES), lambda i: (0, 0))],
        out_specs=pl.BlockSpec((t, ROUTE_LANES), lambda i: (i, 0)),
        out_shape=jax.ShapeDtypeStruct((n, ROUTE_LANES), F32),
        scratch_shapes=[pltpu.VMEM((SUBLANES, ROUTE_LANES), F32)],
        compiler_params=_params("arbitrary"),
        name="moe_slots",
    )(route, counts)


def _sc_mesh():
    return plsc.VectorSubcoreMesh(core_axis_name="c", subcore_axis_name="s")


def _sc_worker():
    return lax.axis_index("s") * SC_CORES + lax.axis_index("c")


DISPATCH_ROWS = 64
COMBINE_ROWS = 64


def sc_dispatch(rows, dest, n_slots):
    n, w = rows.shape
    ch = DISPATCH_ROWS
    per_w = n // SC_WORKERS
    n_ch = per_w // ch

    @functools.partial(
        pl.kernel, mesh=_sc_mesh(),
        out_type=jax.ShapeDtypeStruct((n_slots, w), rows.dtype),
        scratch_types=[pltpu.VMEM((ch,), jnp.int32), pltpu.VMEM((ch, w), rows.dtype)],
    )
    def scatter_kernel(rows_hbm, dest_hbm, out_hbm, idx_v, rows_v):
        wid = _sc_worker()

        @pl.loop(0, n_ch)
        def _(j):
            chunk = wid * n_ch + j
            pltpu.sync_copy(rows_hbm.at[pl.ds(pl.multiple_of(chunk * ch, ch), ch)], rows_v)
            for k in range(2):
                pltpu.sync_copy(dest_hbm.at[k, chunk], idx_v)
                pltpu.sync_copy(rows_v, out_hbm.at[idx_v])

    return scatter_kernel(rows, dest)


def sc_gather_rows(table, idx):
    s, w = table.shape
    m = idx.shape[0]
    ch = COMBINE_ROWS
    per_w = m // SC_WORKERS
    n_ch = per_w // ch

    @functools.partial(
        pl.kernel, mesh=_sc_mesh(),
        out_type=jax.ShapeDtypeStruct((m, w), table.dtype),
        scratch_types=[pltpu.VMEM((ch,), jnp.int32), pltpu.VMEM((ch, w), table.dtype), pltpu.SemaphoreType.DMA],
    )
    def gather_kernel(table_hbm, idx_hbm, out_hbm, idx_v, rows_v, sem):
        wid = _sc_worker()

        @pl.loop(0, n_ch)
        def _(j):
            off = pl.multiple_of((wid * n_ch + j) * ch, ch)
            pltpu.sync_copy(idx_hbm.at[pl.ds(off, ch)], idx_v)
            pltpu.async_copy(table_hbm.at[idx_v], rows_v, sem).wait()
            pltpu.sync_copy(rows_v, out_hbm.at[pl.ds(off, ch)])

    return gather_kernel(table, idx)


def hier_moe(h_packed, route, cnt, layer, e_gate, e_up, e_down):
    b, l, dh = h_packed.shape
    n = b * l
    assert n % (SC_WORKERS * DISPATCH_ROWS) == 0 and (2 * n) % (SC_WORKERS * COMBINE_ROWS) == 0
    slots = moe_slots(route.reshape(n, ROUTE_LANES), cnt)
    counts = cnt[0, :N_EXPERTS].astype(jnp.int32)
    padded = (counts + MOE_BLOCK - 1) // MOE_BLOCK * MOE_BLOCK
    pend = jnp.cumsum(padded)
    nb = -(-2 * n // MOE_BLOCK) + N_EXPERTS
    blk0 = jnp.arange(nb, dtype=jnp.int32) * MOE_BLOCK
    owner = pend[None, :] <= blk0[:, None]
    blk_e = jnp.minimum(jnp.sum(owner, axis=1), N_EXPERTS - 1).astype(jnp.int32)
    run_end = jnp.sum(jnp.where(jnp.arange(N_EXPERTS)[None, :] == blk_e[:, None],
                                (pend - padded + counts)[None, :], 0), axis=1)
    n_valid = jnp.clip(run_end - blk0, 0, MOE_BLOCK).astype(jnp.int32)
    dest = slots[:, 0:2].astype(jnp.int32).T
    xs = sc_dispatch(h_packed.reshape(n, dh), dest.reshape(2, n // DISPATCH_ROWS, DISPATCH_ROWS), nb * MOE_BLOCK)
    y = expert_ffn(xs, blk_e, n_valid, layer, e_gate, e_up, e_down)
    return sc_gather_rows(y, dest.reshape(2 * n)).reshape(2, b, l, dh)


def _final_body(x_ref, y0_ref, y1_ref, route_ref, pmod_ref, g_ref, o_ref):
    x = _moe_residual(x_ref[0], y0_ref, y1_ref, route_ref, pmod_ref)
    ms = jnp.mean(x * x, axis=-1, keepdims=True)
    o_ref[0] = x * lax.rsqrt(ms + EPS) * g_ref[...]


def final_norm(x1, res, mod_row0, norm_g, *, tm):
    b, l, d = x1.shape
    row = (lambda bb: 0) if mod_row0 is None else (lambda bb: mod_row0 + bb)
    xspec = pl.BlockSpec((1, tm, d), lambda bb, i: (bb, i, 0))
    rargs, rspecs = _residual_specs(res, tm, d, row)
    return pl.pallas_call(
        _final_body,
        grid=(b, l // tm),
        in_specs=[xspec] + rspecs + [pl.BlockSpec((1, d), lambda bb, i: (0, 0))],
        out_specs=xspec,
        out_shape=jax.ShapeDtypeStruct((b, l, d), F32),
        compiler_params=_params("arbitrary", "arbitrary"),
        name="final_norm",
    )(x1, *rargs, norm_g.reshape(1, d))


def _rope_tables(seq):
    rows = seq // GRID_W
    row_pos = jnp.repeat(jnp.arange(rows, dtype=F32), GRID_W)
    col_pos = jnp.tile(jnp.arange(GRID_W, dtype=F32), rows)
    n_freq = HEAD_DIM // 4
    inv = ROPE_BASE ** (-jnp.arange(n_freq, dtype=F32) / n_freq)
    ang = jnp.concatenate([row_pos[:, None] * inv, col_pos[:, None] * inv], axis=-1)
    cs, sn = jnp.cos(ang), jnp.sin(ang)
    cos_f = jnp.tile(jnp.concatenate([cs, cs], axis=-1), (1, N_Q_HEADS))
    sin_s = jnp.tile(jnp.concatenate([-sn, sn], axis=-1), (1, N_Q_HEADS))
    return cos_f, sin_s


def kernel(x_prompt, x_sample, cache_k, cache_v, c, c_ctx, ada_w, ada_b, norm1_g, norm2_g, w_in, conv_dw_w, conv_dw_b, conv_ln_g, conv_ln_b, hy_short_w, hy_short_b, hy_fw1, hy_fb1, hy_freq1, hy_fw2, hy_fb2, hy_freq2, hy_fw3, hy_fb3, hy_d, attn_sink, w_out, router_coarse_w, router_coarse_b, router_fine_w, router_fine_b, exp_gate, exp_up, exp_down, norm_f_g):
    depth = ada_w.shape[0]
    bp, lp, d = x_prompt.shape
    bs, ls, _ = x_sample.shape
    assert bp % 2 == 0 and bs % 2 == 0 and ls % ATT_BLOCK == 0 and ls % GRID_W == 0

    n_rows = -(-(1 + bs) // SUBLANES) * SUBLANES
    cvec = jnp.concatenate([c_ctx[None, :], c, jnp.zeros((n_rows - 1 - bs, d), F32)], axis=0)
    mods = adaln_all(cvec, ada_w, ada_b)

    rope = _rope_tables(ls)
    fnet_tabs, hy_tabs = {}, {}
    for seq in {lp, ls}:
        fnet_tabs[seq] = fnet_tables(seq)
        hy_tabs[seq] = hyena_tables(seq)
    ck = cache_k.reshape(cache_k.shape[0], depth, cache_k.shape[2], -1)
    cv = cache_v.reshape(cache_v.shape[0], depth, cache_v.shape[2], -1)
    pad = ROUTE_LANES - N_GROUPS - N_EXPERTS

    tm_p = min(lp, 512)
    tm_s = min(ls, 512)
    xp, xs = x_prompt, x_sample
    res_p = res_s = None
    ks_out, vs_out = [], []
    for l in range(depth):
        w_in_bf = w_in[l].astype(BF16)
        w_out_bf = w_out[l].astype(BF16)
        rw = jnp.concatenate([router_coarse_w[l], router_fine_w[l], jnp.zeros((d, pad), F32)], axis=1)
        rw_hi = rw.astype(BF16)
        rw = jnp.concatenate([rw_hi, (rw - rw_hi.astype(F32)).astype(BF16)], axis=1)
        rb = jnp.concatenate([router_coarse_b[l], router_fine_b[l], jnp.zeros((pad,), F32)])[None, :]
        filt = (hy_fw1[l], hy_fb1[l], hy_freq1[l], hy_fw2[l], hy_fb2[l], hy_freq2[l], hy_fw3[l], hy_fb3[l])
        sink = attn_sink[l]

        def mixers(ua, ub, uc, yd, seq):
            ya = conformer_conv(ua, conv_dw_w[l], conv_dw_b[l], conv_ln_g[l], conv_ln_b[l])
            yb = hyena_mixer(ub, hy_short_w[l], hy_short_b[l], hy_d[l], hyena_filter_spectrum(seq, filt, hy_tabs[seq]),
                             hy_tabs[seq])
            yc = fnet_mixer(uc, fnet_tabs[seq])
            return (ya, yb, yc, yd)

        outs = in_projection(xp, mods[l], None, norm1_g[l], w_in_bf, res=res_p, tm=tm_p)
        if res_p is not None:
            xp, outs = outs[0], outs[1:]
        ua, ub, uc, uq, ukv = outs
        kvw = N_KV_HEADS * HEAD_DIM
        ks_out.append(ukv[..., :kvw].reshape(bp, lp, N_KV_HEADS, HEAD_DIM))
        vs_out.append(ukv[..., kvw:].reshape(bp, lp, N_KV_HEADS, HEAD_DIM))
        ys = mixers(ua, ub, uc, context_attention(uq, ukv, sink), lp)
        x1p, hp, route, cnt = out_projection(ys, xp, mods[l], None, norm2_g[l], w_out_bf, rw, rb, tm=tm_p)
        res_p = (hier_moe(hp, route, cnt, l, exp_gate, exp_up, exp_down), route, mods[l])
        xp = x1p

        outs = in_projection(xs, mods[l], 1, norm1_g[l], w_in_bf, res=res_s, rope=rope, tm=tm_s)
        if res_s is not None:
            xs, outs = outs[0], outs[1:]
        ua, ub, uc, uq, uqr, ukv = outs
        ys = mixers(ua, ub, uc, latent_attention(uq, uqr, ukv, ck, cv, l, sink), ls)
        x1s, hs, route, cnt = out_projection(ys, xs, mods[l], 1, norm2_g[l], w_out_bf, rw, rb, tm=tm_s)
        res_s = (hier_moe(hs, route, cnt, l, exp_gate, exp_up, exp_down), route, mods[l])
        xs = x1s

    y_prompt = final_norm(xp, res_p, None, norm_f_g, tm=tm_p)
    y_sample = final_norm(xs, res_s, 1, norm_f_g, tm=tm_s)
    return (y_prompt, y_sample, jnp.stack(ks_out, axis=1), jnp.stack(vs_out, axis=1))
```

```python
import functools
import math

import numpy as np
import jax
import jax.numpy as jnp
from jax import lax
from jax.experimental import pallas as pl
from jax.experimental.pallas import tpu as pltpu
from jax.experimental.pallas import tpu_sc as plsc

F32 = jnp.float32
BF16 = jnp.bfloat16

HEAD_DIM = 64
LOG2E = math.log2(math.e)
ATT_SCALE = HEAD_DIM ** -0.5 * LOG2E
N_Q_HEADS = 4
N_KV_HEADS = 2
Q_PER_KV = N_Q_HEADS // N_KV_HEADS
WINDOW = 128
ATT_BLOCK = 128
GRID_W = 64
ROPE_BASE = 10000.0
N_GROUPS = 4
EXPERTS_PER_GROUP = 8
N_EXPERTS = N_GROUPS * EXPERTS_PER_GROUP
MOE_BLOCK = 512
N_MOD = 6
EPS = 1e-6
NEG_INF = -1e30
HYENA_DECAY_TARGET = 1e-2
HYENA_FAST_PCT = 0.3
HYENA_SLOW_PCT = 1.5
HYENA_SHIFT = 0.05

LANES = 128
SUBLANES = 8
VMEM_LIMIT = 56 * 1024 * 1024

MIX_CH = 256
ROUTE_LANES = 128
SC_CORES = 2
SC_WORKERS = SC_CORES * 16


def _params(*sem):
    return pltpu.CompilerParams(dimension_semantics=sem, vmem_limit_bytes=VMEM_LIMIT)


def _silu(x):
    return x * jax.nn.sigmoid(x)


def _ada_body(c_ref, w_ref, b_ref, o_ref):
    s = _silu(c_ref[...]).astype(BF16)
    o_ref[0] = jnp.dot(s, w_ref[0].astype(BF16), preferred_element_type=F32) + b_ref[0]


def adaln_all(cvec, ada_w, ada_b):
    depth, d, n6 = ada_w.shape
    r = cvec.shape[0]
    tn = n6 // 4
    out = pl.pallas_call(
        _ada_body,
        grid=(depth, n6 // tn),
        in_specs=[
            pl.BlockSpec((r, d), lambda l, j: (0, 0)),
            pl.BlockSpec((1, d, tn), lambda l, j: (l, 0, j)),
            pl.BlockSpec((1, 1, tn), lambda l, j: (l, 0, j)),
        ],
        out_specs=pl.BlockSpec((1, r, tn), lambda l, j: (l, 0, j)),
        out_shape=jax.ShapeDtypeStruct((depth, r, n6), F32),
        compiler_params=_params("arbitrary", "arbitrary"),
        name="adaln",
    )(cvec, ada_w, ada_b.reshape(depth, 1, n6))
    return out.reshape(depth, r, N_MOD, d)


def _swap_halves(x):
    pieces = []
    for j in range(x.shape[1] // LANES):
        xj = x[:, j * LANES:(j + 1) * LANES]
        fwd = pltpu.roll(xj, LANES - HEAD_DIM // 2, axis=1)
        bwd = pltpu.roll(xj, HEAD_DIM // 2, axis=1)
        lane = lax.broadcasted_iota(jnp.int32, xj.shape, 1)
        pieces.append(jnp.where((lane % HEAD_DIM) < HEAD_DIM // 2, fwd, bwd))
    return pieces[0] if len(pieces) == 1 else jnp.concatenate(pieces, axis=1)


def _rmsnorm_mod(x, g, scale, shift):
    ms = jnp.mean(x * x, axis=-1, keepdims=True)
    return (x * lax.rsqrt(ms + EPS) * g) * (1.0 + scale) + shift


def _moe_residual(x1, y0_ref, y1_ref, route_ref, pmod_ref, rows=slice(None)):
    y0 = _unpack_bf16_pairs(y0_ref[0, 0, rows, :], F32)
    y1 = _unpack_bf16_pairs(y1_ref[0, 0, rows, :], F32)
    moe = route_ref[0, rows, 2:3] * y0 + route_ref[0, rows, 3:4] * y1
    return x1 + pmod_ref[0, 5:6, :] * moe


def _residual_specs(res, tm, d, row):
    pair, route, pmods = res
    args = [pair, pair, route, pmods]
    specs = [pl.BlockSpec((1, 1, tm, d // 2), lambda bb, i: (0, bb, i, 0)),
             pl.BlockSpec((1, 1, tm, d // 2), lambda bb, i: (1, bb, i, 0)),
             pl.BlockSpec((1, tm, ROUTE_LANES), lambda bb, i: (bb, i, 0)),
             pl.BlockSpec((1, N_MOD, d), lambda bb, i: (row(bb), 0, 0))]
    return args, specs


IN_SPLIT = 2


def _in_body(*refs, fuse_res, rope):
    it = iter(refs)
    x_ref = next(it)
    if fuse_res:
        res_refs = [next(it) for _ in range(4)]
    mod_ref = next(it)
    g_ref = next(it)
    w_ref = next(it)
    if rope:
        cos_ref = next(it)
        sin_ref = next(it)
    outs = list(it)
    xo_ref = outs.pop(0) if fuse_res else None
    c = MIX_CH
    part = x_ref.shape[1] // IN_SPLIT
    hs = []
    for p in range(IN_SPLIT):
        rows = slice(p * part, (p + 1) * part)
        x = x_ref[0, rows, :]
        if fuse_res:
            x = _moe_residual(x, *res_refs, rows)
            xo_ref[0, rows, :] = x
        hs.append(_rmsnorm_mod(x, g_ref[...], mod_ref[0, 1:2, :], mod_ref[0, 0:1, :]).astype(BF16))
    ua_ref, ub_ref, uc_ref, uq_ref = outs[:4]
    for p in range(IN_SPLIT):
        rows = slice(p * part, (p + 1) * part)
        u = jnp.dot(hs[p], w_ref[...], preferred_element_type=F32)
        ua_ref[0, rows, :] = u[:, 0:2 * c]
        ub_ref[0, rows, :] = u[:, 2 * c:5 * c]
        uc_ref[0, rows, :] = u[:, 5 * c:6 * c]
        q = u[:, 6 * c:7 * c] * ATT_SCALE
        k = u[:, 7 * c:7 * c + c // 2]
        v = u[:, 7 * c + c // 2:8 * c]
        uq_ref[0, rows, :] = q.astype(uq_ref.dtype)
        if rope:
            uqr_ref, ukv_ref = outs[4:]
            cs = cos_ref[rows, :]
            sn = sin_ref[rows, :]
            uqr_ref[0, rows, :] = (q * cs + _swap_halves(q) * sn).astype(uqr_ref.dtype)
            kr = k * cs[:, :c // 2] + _swap_halves(k) * sn[:, :c // 2]
            ukv_ref[0, rows, :] = jnp.concatenate([kr, v], axis=1).astype(ukv_ref.dtype)
        else:
            outs[4][0, rows, :] = u[:, 7 * c:8 * c]


def in_projection(x, mods, mod_row0, norm_g, w_in_bf, *, res=None, rope=None, tm):
    b, l, d = x.shape
    c = MIX_CH
    grid = (b, l // tm)
    row = (lambda bb: 0) if mod_row0 is None else (lambda bb: mod_row0 + bb)
    xspec = pl.BlockSpec((1, tm, d), lambda bb, i: (bb, i, 0))
    mspec = pl.BlockSpec((1, N_MOD, d), lambda bb, i: (row(bb), 0, 0))
    args, specs = [x], [xspec]
    if res is not None:
        rargs, rspecs = _residual_specs(res, tm, d, row)
        args += rargs
        specs += rspecs
    args += [mods, norm_g.reshape(1, d), w_in_bf]
    specs += [mspec, pl.BlockSpec((1, d), lambda bb, i: (0, 0)),
              pl.BlockSpec(w_in_bf.shape, lambda bb, i: (0, 0))]
    if rope is not None:
        args += [rope[0], rope[1]]
        specs += [pl.BlockSpec((tm, c), lambda bb, i: (i, 0))] * 2

    def ospec(w):
        return pl.BlockSpec((1, tm, w), lambda bb, i: (bb, i, 0))

    out_shape, out_specs = [], []
    if res is not None:
        out_shape.append(jax.ShapeDtypeStruct((b, l, d), F32))
        out_specs.append(xspec)
    widths = [(2 * c, F32), (3 * c, F32), (c, F32), (c, BF16)] + ([(c, BF16), (c, BF16)] if rope is not None else [(c, F32)])
    for w, dt in widths:
        out_shape.append(jax.ShapeDtypeStruct((b, l, w), dt))
        out_specs.append(ospec(w))
    return pl.pallas_call(
        functools.partial(_in_body, fuse_res=res is not None, rope=rope is not None),
        grid=grid, in_specs=specs, out_specs=out_specs, out_shape=out_shape,
        compiler_params=_params("arbitrary", "arbitrary"),
        name="in_proj",
    )(*args)


def _dw_tile(win, w_ref, n_taps, first, rows):
    acc = w_ref[0:1, :] * win[first:first + rows]
    for k in range(1, n_taps):
        acc = acc + w_ref[k:k + 1, :] * win[first + k:first + k + rows]
    return acc


CONV_PAD = 16
CONV_ROWS = 256
CONV_CHUNK = 512
CONV_TAIL = CONV_PAD + CONV_ROWS + SUBLANES + SUBLANES


def _conf_body(u_ref, w_ref, b_ref, g_ref, beta_ref, o_ref, gp_ref, sh_ref, *, seq, n_taps, chunk):
    c = MIX_CH
    r = CONV_ROWS
    first = CONV_PAD - (n_taps - 1) // 2
    gp_ref[0:CONV_PAD, :] = jnp.zeros((CONV_PAD, c), F32)
    gp_ref[CONV_PAD + seq:CONV_PAD + seq + CONV_TAIL, :] = jnp.zeros((CONV_TAIL, c), F32)

    def fill(i, carry):
        r0 = pl.multiple_of(i * r, r)
        a = u_ref[0, pl.ds(r0, r), 0:c]
        g = u_ref[0, pl.ds(r0, r), c:2 * c]
        gp_ref[pl.ds(CONV_PAD + r0, r), :] = a * jax.nn.sigmoid(g)
        return carry

    lax.fori_loop(0, seq // r, fill, 0)
    n_copy_tiles = sh_ref.shape[1] // r

    def do_chunk(ci, carry):
        c0 = pl.multiple_of(ci * chunk, chunk)

        def shift_tile(ti, carry2):
            t0 = pl.multiple_of(ti * r, r)
            win = gp_ref[pl.ds(c0 + t0, r + SUBLANES), :]
            for m in range(1, SUBLANES):
                sh_ref[m - 1, pl.ds(t0, r), :] = win[m:m + r]
            return carry2

        lax.fori_loop(0, n_copy_tiles, shift_tile, 0)

        def tile(ti, carry2):
            t0 = pl.multiple_of(ti * r, r)
            acc = None
            for k in range(n_taps):
                a8, m = (first + k) // SUBLANES * SUBLANES, (first + k) % SUBLANES
                src = gp_ref[pl.ds(c0 + t0 + a8, r), :] if m == 0 else sh_ref[m - 1, pl.ds(t0 + a8, r), :]
                term = w_ref[k:k + 1, :] * src
                acc = term if acc is None else acc + term
            z = acc + b_ref[...]
            mu = jnp.mean(z, axis=-1, keepdims=True)
            zc = z - mu
            var = jnp.mean(zc * zc, axis=-1, keepdims=True)
            zn = zc * lax.rsqrt(var + EPS) * g_ref[...] + beta_ref[...]
            o_ref[0, pl.ds(c0 + t0, r), :] = _silu(zn).astype(o_ref.dtype)
            return carry2

        lax.fori_loop(0, chunk // r, tile, 0)
        return carry

    lax.fori_loop(0, seq // chunk, do_chunk, 0)


def conformer_conv(ua, dw_w, dw_b, ln_g, ln_b):
    b, l, c2 = ua.shape
    c = MIX_CH
    k = dw_w.shape[0]
    chunk = min(l, CONV_CHUNK)
    copy_rows = -(-(chunk + CONV_PAD + k) // CONV_ROWS) * CONV_ROWS
    assert l % chunk == 0 and k - 1 <= 2 * CONV_PAD and copy_rows - chunk + SUBLANES <= CONV_PAD + CONV_TAIL
    vec = pl.BlockSpec((1, c), lambda bb: (0, 0))
    return pl.pallas_call(
        functools.partial(_conf_body, seq=l, n_taps=k, chunk=chunk),
        grid=(b,),
        in_specs=[pl.BlockSpec((1, l, c2), lambda bb: (bb, 0, 0)),
                  pl.BlockSpec((k, c), lambda bb: (0, 0)), vec, vec, vec],
        out_specs=pl.BlockSpec((1, l, c), lambda bb: (bb, 0, 0)),
        out_shape=jax.ShapeDtypeStruct((b, l, c), BF16),
        scratch_shapes=[pltpu.VMEM((CONV_PAD + l + CONV_TAIL, c), F32),
                        pltpu.VMEM((SUBLANES - 1, copy_rows, c), F32)],
        compiler_params=_params("arbitrary"),
        name="conformer",
    )(ua, dw_w, dw_b.reshape(1, c), ln_g.reshape(1, c), ln_b.reshape(1, c))


SHORT_PAD = 8
SHORT_ROWS = 64


def _short_body(u_ref, w_ref, b_ref, o_ref, xp_ref, *, seq, n_taps):
    c = MIX_CH
    r = SHORT_ROWS
    half = (n_taps - 1) // 2
    zero = jnp.zeros((SHORT_PAD, c), F32)
    xp_ref[0:SHORT_PAD, :] = zero
    xp_ref[SHORT_PAD + seq:SHORT_PAD + seq + SHORT_PAD, :] = zero

    def fill(i, carry):
        r0 = pl.multiple_of(i * r, r)
        xp_ref[pl.ds(SHORT_PAD + r0, r), :] = u_ref[0, pl.ds(r0, r), :]
        return carry

    lax.fori_loop(0, seq // r, fill, 0)

    def tile(i, carry):
        r0 = pl.multiple_of(i * r, r)
        win = xp_ref[pl.ds(r0, r + 2 * SHORT_PAD), :]
        o_ref[0, pl.ds(r0, r), :] = _dw_tile(win, w_ref, n_taps, SHORT_PAD - half, r) + b_ref[...]
        return carry

    lax.fori_loop(0, seq // r, tile, 0)


def hyena_short_conv(ub, short_w, short_b):
    b, l, c3 = ub.shape
    c = MIX_CH
    k = short_w.shape[0]
    return pl.pallas_call(
        functools.partial(_short_body, seq=l, n_taps=k),
        grid=(b, c3 // c),
        in_specs=[pl.BlockSpec((1, l, c), lambda bb, j: (bb, 0, j)),
                  pl.BlockSpec((k, c), lambda bb, j: (0, j)),
                  pl.BlockSpec((1, c), lambda bb, j: (0, j))],
        out_specs=pl.BlockSpec((1, l, c), lambda bb, j: (bb, 0, j)),
        out_shape=jax.ShapeDtypeStruct((b, l, c3), F32),
        scratch_shapes=[pltpu.VMEM((l + 2 * SHORT_PAD, c), F32)],
        compiler_params=_params("arbitrary", "arbitrary"),
        name="hyena_short",
    )(ub, short_w, short_b.reshape(1, c3))


def _fft_sizes(seq):
    n = 2 * seq
    n2 = 128 if n >= 4096 else 32
    return n, n // n2, n2


FFT_GROUP = SUBLANES


def _kron_matrices(n, n1, h):
    k1 = np.arange(n1)[:, None]
    i1 = np.arange(h)[None, :]
    th = 2.0 * np.pi * ((k1 * i1) % n1) / n1
    eye = np.eye(FFT_GROUP)
    cs, sn = np.kron(np.cos(th), eye), np.kron(np.sin(th), eye)
    fwd = np.block([[cs, sn], [-sn, cs]])
    inv = np.block([[cs.T, -sn.T], [sn.T, cs.T]]) / n
    return fwd.astype(np.float32), inv.astype(np.float32)


def _twiddle_tables(n, n1, n2):
    k1 = jnp.arange(n1, dtype=jnp.int32)[:, None]
    i2 = jnp.arange(n2, dtype=jnp.int32)[None, :]
    th = ((k1 * i2) % n).astype(F32) * (2.0 * math.pi / n)
    shape = (n1, n2 // FFT_GROUP, FFT_GROUP, LANES)
    full = lambda a: jnp.broadcast_to(a.reshape(shape[:3] + (1,)), shape)
    return full(jnp.cos(th)), full(jnp.sin(th))


def _mid_tables(n2):
    a = np.arange(n2)
    th = 2.0 * np.pi * ((a[:, None] * a[None, :]) % n2) / n2
    cs, sn = np.cos(th), np.sin(th)
    fwd = np.concatenate([np.concatenate([cs, sn], 1), np.concatenate([-sn, cs], 1)], 0)
    inv = np.concatenate([np.concatenate([cs, -sn], 1), np.concatenate([sn, cs], 1)], 0)
    return fwd.astype(np.float32), inv.astype(np.float32)


def _dot(a, b):
    return jnp.dot(a.astype(BF16), b.astype(BF16), preferred_element_type=F32)


def _filter_stage1_body(x_ref, m_ref, c_ref, s_ref, o_ref, *, n1, gs):
    j, c = FFT_GROUP, MIX_CH
    for s in range(gs):
        a = _dot(m_ref[...], x_ref[:, s].reshape(n1 * j, c))
        ar = a[:n1 * j].reshape(n1, j, c)
        ai = a[n1 * j:].reshape(n1, j, c)
        cs, sn = _lanes(c_ref[:, s], c), _lanes(s_ref[:, s], c)
        o_ref[0, 0, :, s] = ar * cs + ai * sn
        o_ref[0, 1, :, s] = ai * cs - ar * sn


def filter_stage1(k, order, mat, twc, tws, *, n1, n2):
    g, j, c = n2 // FFT_GROUP, FFT_GROUP, MIX_CH
    gs = _group_step(g)
    tw = pl.BlockSpec((n1, gs, j, LANES), lambda gi: (0, gi, 0, 0))
    out = pl.pallas_call(
        functools.partial(_filter_stage1_body, n1=n1, gs=gs),
        grid=(g // gs,),
        in_specs=[pl.BlockSpec((n1, gs, j, c), lambda gi: (0, gi, 0, order)),
                  pl.BlockSpec(mat.shape, lambda gi: (0, 0)), tw, tw],
        out_specs=pl.BlockSpec((1, 2, n1, gs, j, c), lambda gi: (0, 0, 0, gi, 0, 0)),
        out_shape=jax.ShapeDtypeStruct((1, 2, n1, g, j, c), F32),
        compiler_params=_params("arbitrary"),
        name="filter_stage1",
    )(k.reshape(n1, g, j, k.shape[-1]), mat, twc, tws)
    return out.reshape(1, 2, n1 * n2, c)


def _mid_body(a_ref, k_ref, f_ref, g_ref, o_ref, *, n2, kc):
    for j in range(kc):
        rows = slice(j * n2, (j + 1) * n2)
        blk = jnp.concatenate([a_ref[0, 0, rows, :], a_ref[0, 1, rows, :]], axis=0)
        s = _dot(f_ref[...], blk)
        sr, si = s[:n2], s[n2:]
        kr, ki = k_ref[0, rows, :], k_ref[1, rows, :]
        y = jnp.concatenate([sr * kr - si * ki, sr * ki + si * kr], axis=0)
        bb = _dot(g_ref[...], y)
        o_ref[0, 0, rows, :] = bb[:n2]
        o_ref[0, 1, rows, :] = bb[n2:]


def fft_mid(a, kf, f_fwd, f_inv, *, n1, n2):
    p = a.shape[0]
    c = MIX_CH
    kc = min(n1, 8)
    rows = kc * n2
    blk = pl.BlockSpec((1, 2, rows, c), lambda j, pp: (pp, 0, j, 0))
    mat = pl.BlockSpec(f_fwd.shape, lambda j, pp: (0, 0))
    return pl.pallas_call(
        functools.partial(_mid_body, n2=n2, kc=kc),
        grid=(n1 // kc, p),
        in_specs=[blk, pl.BlockSpec((2, rows, c), lambda j, pp: (0, j, 0)), mat, mat],
        out_specs=blk,
        out_shape=jax.ShapeDtypeStruct(a.shape, F32),
        compiler_params=_params("arbitrary", "arbitrary"),
        name="fft_mid",
    )(a, kf, f_fwd, f_inv)


def _filter_mid_body(a_ref, f_ref, sum_ref, o_ref, *, n2, kc):
    inv = 1.0 / (sum_ref[0:1, :] + EPS)
    for j in range(kc):
        rows = slice(j * n2, (j + 1) * n2)
        blk = jnp.concatenate([a_ref[0, 0, rows, :], a_ref[0, 1, rows, :]], axis=0)
        s = _dot(f_ref[...], blk) * inv
        o_ref[0, rows, :] = s[:n2]
        o_ref[1, rows, :] = s[n2:]


def filter_mid(a, f_fwd, abs_sum, order, *, n1, n2):
    c = MIX_CH
    kc = min(n1, 8)
    rows = kc * n2
    return pl.pallas_call(
        functools.partial(_filter_mid_body, n2=n2, kc=kc),
        grid=(n1 // kc,),
        in_specs=[pl.BlockSpec((1, 2, rows, c), lambda j: (0, 0, j, 0)),
                  pl.BlockSpec(f_fwd.shape, lambda j: (0, 0)),
                  pl.BlockSpec((SUBLANES, c), lambda j: (0, order))],
        out_specs=pl.BlockSpec((2, rows, c), lambda j: (0, j, 0)),
        out_shape=jax.ShapeDtypeStruct((2, n1 * n2, c), F32),
        compiler_params=_params("arbitrary"),
        name="filter_mid",
    )(a, f_fwd, abs_sum)


def _lanes(t, width):
    return t if width == LANES else jnp.concatenate([t] * (width // LANES), axis=-1)


def _kron1_body(zr_ref, zi_ref, m_ref, c_ref, s_ref, o_ref, *, n1, h, gs):
    j, c = FFT_GROUP, MIX_CH
    for s in range(gs):
        xr = zr_ref[0, :, s].reshape(h * j, c)
        xi = zi_ref[0, :, s].reshape(h * j, c)
        a = _dot(m_ref[...], jnp.concatenate([xr, xi], axis=0))
        ar = a[:n1 * j].reshape(n1, j, c)
        ai = a[n1 * j:].reshape(n1, j, c)
        cs, sn = _lanes(c_ref[:, s], c), _lanes(s_ref[:, s], c)
        o_ref[0, 0, :, s] = ar * cs + ai * sn
        o_ref[0, 1, :, s] = ai * cs - ar * sn


def _group_step(n_groups):
    return min(n_groups, 4)


def kron_stage1(z5, col, n_pairs, imag_offset, mat, twc, tws, *, n1, h):
    g, j, c = z5.shape[2], FFT_GROUP, MIX_CH
    gs = _group_step(g)
    tw = pl.BlockSpec((n1, gs, j, LANES), lambda gi, p: (0, gi, 0, 0))
    return pl.pallas_call(
        functools.partial(_kron1_body, n1=n1, h=h, gs=gs),
        grid=(g // gs, n_pairs),
        in_specs=[pl.BlockSpec((1, h, gs, j, c), lambda gi, p: (p, 0, gi, 0, col)),
                  pl.BlockSpec((1, h, gs, j, c), lambda gi, p: (p + imag_offset, 0, gi, 0, col)),
                  pl.BlockSpec(mat.shape, lambda gi, p: (0, 0)), tw, tw],
        out_specs=pl.BlockSpec((1, 2, n1, gs, j, c), lambda gi, p: (p, 0, 0, gi, 0, 0)),
        out_shape=jax.ShapeDtypeStruct((n_pairs, 2, n1, g, j, c), F32),
        compiler_params=_params("arbitrary", "arbitrary"),
        name="fft_kron1",
    )(z5, z5, mat, twc, tws)


def _kron3_body(b_ref, m_ref, c_ref, s_ref, zr_ref, zi_ref, gr_ref, gi_ref, d_ref, o_ref, *, n1, h, gs):
    j, c = FFT_GROUP, MIX_CH
    d = d_ref[...].reshape(1, 1, c)
    for s in range(gs):
        br, bi = b_ref[0, 0, :, s], b_ref[0, 1, :, s]
        cs, sn = _lanes(c_ref[:, s], c), _lanes(s_ref[:, s], c)
        xr = (br * cs - bi * sn).reshape(n1 * j, c)
        xi = (br * sn + bi * cs).reshape(n1 * j, c)
        y = _dot(m_ref[...], jnp.concatenate([xr, xi], axis=0))
        yr = y[:h * j].reshape(h, j, c)
        yi = y[h * j:].reshape(h, j, c)
        o_ref[0, 0, :, s] = gr_ref[0, :, s] * (yr + d * zr_ref[0, :, s])
        o_ref[1, 0, :, s] = gi_ref[0, :, s] * (yi + d * zi_ref[0, :, s])


def kron_stage3(bw, mat, twc, tws, z5, z_col, g5, g_col, d_vec, imag_offset, *, n1, h):
    p = bw.shape[0]
    g, j, c = bw.shape[3], FFT_GROUP, MIX_CH
    gs = _group_step(g)
    tw = pl.BlockSpec((n1, gs, j, LANES), lambda gi, pp: (0, gi, 0, 0))

    def src(col, off):
        return pl.BlockSpec((1, h, gs, j, c), lambda gi, pp: (pp + off, 0, gi, 0, col))

    return pl.pallas_call(
        functools.partial(_kron3_body, n1=n1, h=h, gs=gs),
        grid=(g // gs, p),
        in_specs=[pl.BlockSpec((1, 2, n1, gs, j, c), lambda gi, pp: (pp, 0, 0, gi, 0, 0)),
                  pl.BlockSpec(mat.shape, lambda gi, pp: (0, 0)), tw, tw,
                  src(z_col, 0), src(z_col, imag_offset), src(g_col, 0), src(g_col, imag_offset),
                  pl.BlockSpec((1, c), lambda gi, pp: (0, 0))],
        out_specs=pl.BlockSpec((2, 1, h, gs, j, c), lambda gi, pp: (0, pp, 0, gi, 0, 0)),
        out_shape=jax.ShapeDtypeStruct((2, p, h, g, j, c), F32),
        compiler_params=_params("arbitrary", "arbitrary"),
        name="fft_kron3",
    )(bw, mat, twc, tws, z5, z5, g5, g5, d_vec.reshape(1, c))


FILTER_TILE = 512


def _lane_dense(fn, a):
    r, w = a.shape
    f = LANES // w
    rows = r // f
    dense = jnp.concatenate([a[q * rows:(q + 1) * rows] for q in range(f)], axis=1)
    out = fn(dense)
    return jnp.concatenate([out[:, q * w:(q + 1) * w] for q in range(f)], axis=0)


def _filter_body(fw1_ref, fb1_ref, fr1_ref, fw2_ref, fb2_ref, fr2_ref, fw3_ref, fb3_ref, bands_ref, decay_ref,
                 k_ref, sum_ref, *, seq):
    i = pl.program_id(0)
    c = MIX_CH
    hp = lax.Precision.HIGHEST
    n = i * FILTER_TILE + lax.broadcasted_iota(jnp.int32, (FILTER_TILE, 1), 0)
    pos = jnp.where(n <= seq, n, 2 * seq - n).astype(F32)
    t = pos * (1.0 / (seq - 1))
    ang = (pos * (2.0 * math.pi / seq)) * bands_ref[...]
    nb = bands_ref.shape[1]
    pre = (t * fw1_ref[0:1, :]
           + jnp.dot(_lane_dense(jnp.cos, ang), fw1_ref[1:1 + nb, :], preferred_element_type=F32, precision=hp)
           - jnp.dot(_lane_dense(jnp.sin, ang), fw1_ref[1 + nb:1 + 2 * nb, :], preferred_element_type=F32,
                     precision=hp)
           + fb1_ref[...])
    h = _lane_dense(jnp.sin, fr1_ref[...] * pre)
    h = _lane_dense(jnp.sin, fr2_ref[...] * (jnp.dot(h, fw2_ref[...], preferred_element_type=F32, precision=hp)
                                             + fb2_ref[...]))
    h = jnp.dot(h, fw3_ref[...], preferred_element_type=F32, precision=hp) + fb3_ref[...]
    win = jnp.exp(-t * decay_ref[...]) + HYENA_SHIFT
    win = jnp.concatenate([win, win], axis=1)
    fwd, bwd = h[:, :2 * c], h[:, 2 * c:]
    k = jnp.where(n < seq, fwd, bwd) + jnp.where(n == 0, bwd, 0.0)
    k = jnp.where(n == seq, 0.0, k) * win
    k_ref[...] = k

    @pl.when(i == 0)
    def _():
        sum_ref[...] = jnp.zeros(sum_ref.shape, F32)

    sum_ref[...] = sum_ref[...] + jnp.sum(jnp.abs(k), axis=0, keepdims=True)


def hyena_filter_time(seq, fw1, fb1, fr1, fw2, fb2, fr2, fw3, fb3):
    c = MIX_CH
    n_bands = (fw1.shape[0] - 1) // 2
    bands = jnp.linspace(1e-4, n_bands - 1, n_bands, dtype=F32)[None, :]
    max_decay = math.log(HYENA_DECAY_TARGET) / HYENA_FAST_PCT
    min_decay = math.log(HYENA_DECAY_TARGET) / HYENA_SLOW_PCT
    decay = jnp.abs(jnp.linspace(min_decay, max_decay, c, dtype=F32))[None, :]
    assert (2 * seq) % FILTER_TILE == 0
    args = [fw1, fb1[None, :], fr1[None, :], fw2, fb2[None, :], fr2[None, :], fw3, fb3[None, :], bands, decay]
    return pl.pallas_call(
        functools.partial(_filter_body, seq=seq),
        grid=(2 * seq // FILTER_TILE,),
        in_specs=[pl.BlockSpec(a.shape, lambda i: (0, 0)) for a in args],
        out_specs=[pl.BlockSpec((FILTER_TILE, 2 * c), lambda i: (i, 0)),
                   pl.BlockSpec((SUBLANES, 2 * c), lambda i: (0, 0))],
        out_shape=[jax.ShapeDtypeStruct((2 * seq, 2 * c), F32), jax.ShapeDtypeStruct((SUBLANES, 2 * c), F32)],
        compiler_params=_params("arbitrary"),
        name="hyena_filter",
    )(*args)


def hyena_filter_spectrum(seq, filt, tabs):
    n, n1, n2 = _fft_sizes(seq)
    k, abs_sum = hyena_filter_time(seq, *filt)
    twc, tws, f_fwd, m_real = tabs[2], tabs[3], tabs[4], tabs[6]
    return [filter_mid(filter_stage1(k, o, m_real, twc, tws, n1=n1, n2=n2), f_fwd, abs_sum, o, n1=n1, n2=n2)
            for o in range(2)]


def hyena_mixer(ub, short_w, short_b, hy_d, kf, tabs):
    b, l, _ = ub.shape
    n, n1, n2 = _fft_sizes(l)
    h = n1 // 2
    p = b // 2
    c = MIX_CH
    g = n2 // FFT_GROUP
    u = hyena_short_conv(ub, short_w, short_b)
    u5 = u.reshape(b, h, g, FFT_GROUP, u.shape[-1])
    m_fwd, m_inv, twc, tws, f_fwd, f_inv = tabs[:6]
    z5 = u5
    for o in range(2):
        a = kron_stage1(z5, 0, p, p, m_fwd, twc, tws, n1=n1, h=h)
        bw = fft_mid(a.reshape(p, 2, n, c), kf[o], f_fwd, f_inv, n1=n1, n2=n2)
        z = kron_stage3(bw.reshape(a.shape), m_inv, twc, tws, z5, 0, u5, 1 + o, hy_d[o], p, n1=n1, h=h)
        z5 = z.reshape(b, h, g, FFT_GROUP, c)
    return z5.reshape(b, l, c)


def hyena_tables(seq):
    n, n1, n2 = _fft_sizes(seq)
    m_fwd, m_inv = _kron_matrices(n, n1, n1 // 2)
    m_real = _kron_matrices(n, n1, n1)[0][:, :n1 * FFT_GROUP]
    twc, tws = _twiddle_tables(n, n1, n2)
    f_fwd, f_inv = _mid_tables(n2)
    bf = lambda a: jnp.asarray(a).astype(BF16)
    return bf(m_fwd), bf(m_inv), twc, tws, bf(f_fwd), bf(f_inv), bf(m_real)


def _fnet_body(cl_ref, sl_ref, x_ref, cc_ref, sc_ref, o_ref, *, scale):
    x = x_ref[0].astype(BF16)
    pr = jnp.dot(cl_ref[...], x, preferred_element_type=F32).astype(BF16)
    qr = jnp.dot(sl_ref[...], x, preferred_element_type=F32).astype(BF16)
    o_ref[0] = (jnp.dot(pr, cc_ref[...], preferred_element_type=F32)
                - jnp.dot(qr, sc_ref[...], preferred_element_type=F32)) * scale


def _dft_tables(n):
    a = jnp.arange(n, dtype=jnp.int32)
    th = ((a[:, None] * a[None, :]) % n).astype(F32) * (2.0 * math.pi / n)
    return jnp.cos(th).astype(BF16), jnp.sin(th).astype(BF16)


FNET_DIRECT_MAX = 1024


def _fnet1_body(x_ref, ccs_ref, m_ref, c_ref, s_ref, o_ref, *, n1, gs):
    j, c = FFT_GROUP, MIX_CH
    for s in range(gs):
        x = x_ref[0, :, s].reshape(n1 * j, c).astype(BF16)
        z = jnp.dot(x, ccs_ref[...], preferred_element_type=F32)
        a = _dot(m_ref[...], jnp.concatenate([z[:, :c], z[:, c:]], axis=0))
        ar = a[:n1 * j].reshape(n1, j, c)
        ai = a[n1 * j:].reshape(n1, j, c)
        cs, sn = _lanes(c_ref[:, s], c), _lanes(s_ref[:, s], c)
        o_ref[0, 0, :, s] = ar * cs + ai * sn
        o_ref[0, 1, :, s] = ai * cs - ar * sn


def _fnet2_body(a_ref, m_ref, o_ref, *, n2, scale):
    j, c = FFT_GROUP, MIX_CH
    x = a_ref[0].reshape(2 * j * n2, c)
    y = _dot(m_ref[...], x) * scale
    o_ref[0, :, 0] = y.reshape(n2, j, c)


def fnet_tables(seq):
    c = MIX_CH
    cc, sc = _dft_tables(c)
    if seq <= FNET_DIRECT_MAX:
        return _dft_tables(seq) + (cc, sc)
    n2 = 128
    n1 = seq // n2
    m_fwd, _ = _kron_matrices(seq, n1, n1)
    twc, tws = _twiddle_tables(seq, n1, n2)
    a = np.arange(n2)
    th = 2.0 * np.pi * ((a[:, None] * a[None, :]) % n2) / n2
    eye = np.eye(FFT_GROUP)
    m2 = np.concatenate([np.einsum('kn,ij->kijn', f, eye).reshape(n2 * FFT_GROUP, FFT_GROUP * n2)
                         for f in (np.cos(th), np.sin(th))], axis=1).astype(np.float32)
    ccs = jnp.concatenate([cc, -sc], axis=1)
    return ccs, jnp.asarray(m_fwd).astype(BF16), twc, tws, jnp.asarray(m2).astype(BF16)


def fnet_two_stage(uc, tables):
    b, l, c = uc.shape
    ccs, m_fwd, twc, tws, m2 = tables
    j = FFT_GROUP
    n2 = 128
    n1 = l // n2
    g = n2 // j
    gs = _group_step(g)
    tw = pl.BlockSpec((n1, gs, j, LANES), lambda gi, bb: (0, gi, 0, 0))
    a = pl.pallas_call(
        functools.partial(_fnet1_body, n1=n1, gs=gs),
        grid=(g // gs, b),
        in_specs=[pl.BlockSpec((1, n1, gs, j, c), lambda gi, bb: (bb, 0, gi, 0, 0)),
                  pl.BlockSpec(ccs.shape, lambda gi, bb: (0, 0)),
                  pl.BlockSpec(m_fwd.shape, lambda gi, bb: (0, 0)), tw, tw],
        out_specs=pl.BlockSpec((1, 2, n1, gs, j, c), lambda gi, bb: (bb, 0, 0, gi, 0, 0)),
        out_shape=jax.ShapeDtypeStruct((b, 2, n1, g, j, c), F32),
        compiler_params=_params("arbitrary", "arbitrary"),
        name="fnet_stage1",
    )(uc.reshape(b, n1, g, j, c), ccs, m_fwd, twc, tws)
    out = pl.pallas_call(
        functools.partial(_fnet2_body, n2=n2, scale=1.0 / math.sqrt(l * c)),
        grid=(b, n1 // j),
        in_specs=[pl.BlockSpec((1, 2, j, n2, c), lambda bb, q: (bb, 0, q, 0, 0)),
                  pl.BlockSpec(m2.shape, lambda bb, q: (0, 0))],
        out_specs=pl.BlockSpec((1, n2, 1, j, c), lambda bb, q: (bb, 0, q, 0, 0)),
        out_shape=jax.ShapeDtypeStruct((b, n2, n1 // j, j, c), F32),
        compiler_params=_params("arbitrary", "arbitrary"),
        name="fnet_stage2",
    )(a.reshape(b, 2, n1, n2, c), m2)
    return out.reshape(b, l, c)


def fnet_mixer(uc, tables):
    b, l, c = uc.shape
    if l > FNET_DIRECT_MAX:
        return fnet_two_stage(uc, tables)
    cl, sl, cc, sc = tables
    tm = min(l, 512)
    row = pl.BlockSpec((tm, l), lambda i, bb: (i, 0))
    sq = pl.BlockSpec((c, c), lambda i, bb: (0, 0))
    return pl.pallas_call(
        functools.partial(_fnet_body, scale=1.0 / math.sqrt(l * c)),
        grid=(l // tm, b),
        in_specs=[row, row, pl.BlockSpec((1, l, c), lambda i, bb: (bb, 0, 0)), sq, sq],
        out_specs=pl.BlockSpec((1, tm, c), lambda i, bb: (bb, i, 0)),
        out_shape=jax.ShapeDtypeStruct((b, l, c), F32),
        compiler_params=_params("arbitrary", "arbitrary"),
        name="fnet",
    )(cl, sl, uc, cc, sc)


def _heads_rows(x, g):
    h0 = Q_PER_KV * g
    return jnp.concatenate([x[:, (h0 + r) * HEAD_DIM:(h0 + r + 1) * HEAD_DIM] for r in range(Q_PER_KV)], axis=0)


def _qk(q, k):
    return lax.dot_general(q.astype(BF16), k.astype(BF16), (((1,), (1,)), ((), ())),
                           preferred_element_type=F32)


def _sink_col(sink_ref, g, rows):
    ridx = lax.broadcasted_iota(jnp.int32, (Q_PER_KV * rows, 1), 0)
    col = jnp.full((Q_PER_KV * rows, 1), sink_ref[Q_PER_KV * g], F32)
    for r in range(1, Q_PER_KV):
        col = jnp.where(ridx >= r * rows, sink_ref[Q_PER_KV * g + r], col)
    return col * LOG2E


def _lat_attn_body(sink_ref, q_ref, qr_ref, kp_ref, kc_ref, kn_ref, ck_ref, cv_ref, o_ref, *, sub):
    i = pl.program_id(1)
    n_qblk = pl.num_programs(1) * sub
    blk = ATT_BLOCK
    span = blk + 2 * WINDOW
    kv = jnp.concatenate([kp_ref[0], kc_ref[0], kn_ref[0]], axis=0)
    ck = ck_ref[0, 0].astype(BF16)
    cv = cv_ref[0, 0].astype(BF16)
    kvw = N_KV_HEADS * HEAD_DIM
    r = lax.broadcasted_iota(jnp.int32, (Q_PER_KV * blk, span), 0) % blk
    j = lax.broadcasted_iota(jnp.int32, (Q_PER_KV * blk, span), 1)
    band = (j >= r) & (j <= r + 2 * WINDOW)
    for s in range(sub):
        qi = i * sub + s
        ok = band & ((qi > 0) | (j >= WINDOW)) & ((qi < n_qblk - 1) | (j < WINDOW + blk))
        q = q_ref[0, s * blk:(s + 1) * blk, :]
        qr = qr_ref[0, s * blk:(s + 1) * blk, :]
        outs = []
        for g in range(N_KV_HEADS):
            kl = kv[s * blk:s * blk + span, g * HEAD_DIM:(g + 1) * HEAD_DIM]
            vl = kv[s * blk:s * blk + span, kvw + g * HEAD_DIM:kvw + (g + 1) * HEAD_DIM]
            s_loc = jnp.where(ok, _qk(_heads_rows(qr, g), kl), NEG_INF)
            s_ctx = _qk(_heads_rows(q, g), ck[:, g * HEAD_DIM:(g + 1) * HEAD_DIM])
            sink = _sink_col(sink_ref, g, blk)
            m = jnp.maximum(jnp.maximum(jnp.max(s_loc, axis=-1, keepdims=True),
                                        jnp.max(s_ctx, axis=-1, keepdims=True)), sink)
            e_loc = jnp.exp2(s_loc - m)
            e_ctx = jnp.exp2(s_ctx - m)
            den = (jnp.sum(e_loc, axis=-1, keepdims=True) + jnp.sum(e_ctx, axis=-1, keepdims=True)
                   + jnp.exp2(sink - m))
            o = (jnp.dot(e_loc.astype(BF16), vl, preferred_element_type=F32)
                 + jnp.dot(e_ctx.astype(BF16), cv[:, g * HEAD_DIM:(g + 1) * HEAD_DIM],
                           preferred_element_type=F32)) * (1.0 / den)
            outs += [o[rr * blk:(rr + 1) * blk] for rr in range(Q_PER_KV)]
        o_ref[0, s * blk:(s + 1) * blk, :] = jnp.concatenate(outs, axis=1).astype(o_ref.dtype)


ATT_SUB = 4


def latent_attention(uq, uqr, ukv, cache_k, cache_v, layer, sink):
    b, l, c = uq.shape
    p = cache_k.shape[2]
    blk = ATT_BLOCK
    nblk = l // blk
    sub = math.gcd(ATT_SUB, nblk)
    rows = sub * blk
    qspec = pl.BlockSpec((1, rows, c), lambda bb, i: (bb, i, 0))
    cspec = pl.BlockSpec((1, 1, p, cache_k.shape[3]), lambda bb, i: (bb, layer, 0, 0))
    return pl.pallas_call(
        functools.partial(_lat_attn_body, sub=sub),
        grid=(b, nblk // sub),
        in_specs=[pl.BlockSpec(memory_space=pltpu.SMEM), qspec, qspec,
                  pl.BlockSpec((1, blk, c), lambda bb, i: (bb, jnp.maximum(i * sub - 1, 0), 0)),
                  qspec,
                  pl.BlockSpec((1, blk, c), lambda bb, i: (bb, jnp.minimum((i + 1) * sub, nblk - 1), 0)),
                  cspec, cspec],
        out_specs=qspec,
        out_shape=jax.ShapeDtypeStruct((b, l, c), BF16),
        compiler_params=_params("arbitrary", "arbitrary"),
        name="latent_attention",
    )(sink, uq, uqr, ukv, ukv, ukv, cache_k, cache_v)


def _ctx_attn_body(sink_ref, q_ref, kv_ref, o_ref, *, seq):
    q = q_ref[0]
    kv = kv_ref[0]
    kvw = N_KV_HEADS * HEAD_DIM
    outs = []
    for g in range(N_KV_HEADS):
        kl = kv[:, g * HEAD_DIM:(g + 1) * HEAD_DIM]
        vl = kv[:, kvw + g * HEAD_DIM:kvw + (g + 1) * HEAD_DIM]
        s = _qk(_heads_rows(q, g), kl)
        sink = _sink_col(sink_ref, g, seq)
        m = jnp.maximum(jnp.max(s, axis=-1, keepdims=True), sink)
        e = jnp.exp2(s - m)
        den = jnp.sum(e, axis=-1, keepdims=True) + jnp.exp2(sink - m)
        o = jnp.dot(e.astype(BF16), vl.astype(BF16), preferred_element_type=F32) * (1.0 / den)
        outs += [o[rr * seq:(rr + 1) * seq] for rr in range(Q_PER_KV)]
    o_ref[0] = jnp.concatenate(outs, axis=1)


def context_attention(uq, ukv, sink):
    b, s, c = uq.shape
    spec = pl.BlockSpec((1, s, c), lambda bb: (bb, 0, 0))
    return pl.pallas_call(
        functools.partial(_ctx_attn_body, seq=s),
        grid=(b,),
        in_specs=[pl.BlockSpec(memory_space=pltpu.SMEM), spec, spec],
        out_specs=spec,
        out_shape=jax.ShapeDtypeStruct((b, s, c), F32),
        compiler_params=_params("arbitrary"),
        name="context_attention",
    )(sink, uq, ukv)


def _pack_bf16_pairs(hi_rounded):
    k = hi_rounded.shape[1] // 2
    bits = lax.bitcast_convert_type(hi_rounded, jnp.uint32)
    return bits[:, :k] | (bits[:, k:] >> 16)


def _unpack_bf16_pairs(packed, dtype=BF16):
    a = lax.bitcast_convert_type(packed & jnp.uint32(0xFFFF0000), F32)
    b = lax.bitcast_convert_type(packed << 16, F32)
    return jnp.concatenate([a, b], axis=1).astype(dtype)


def _out_body(ya_ref, yb_ref, yc_ref, yd_ref, x_ref, mod_ref, g_ref, w_ref, rw_ref, rb_ref,
              x1_ref, h_ref, route_ref, cnt_ref):
    c = MIX_CH
    y = jnp.dot(ya_ref[0].astype(BF16), w_ref[0:c, :], preferred_element_type=F32)
    for j, ref in enumerate((yb_ref, yc_ref, yd_ref), start=1):
        y = y + jnp.dot(ref[0].astype(BF16), w_ref[j * c:(j + 1) * c, :], preferred_element_type=F32)
    x1 = x_ref[0] + mod_ref[0, 2:3, :] * y
    x1_ref[0] = x1
    h = _rmsnorm_mod(x1, g_ref[...], mod_ref[0, 4:5, :], mod_ref[0, 3:4, :])
    h_hi = h.astype(BF16)
    h_hi32 = h_hi.astype(F32)
    h_ref[0] = _pack_bf16_pairs(h_hi32)
    h_lo = (h - h_hi32).astype(BF16)
    tm = h.shape[0]
    prod = jnp.dot(jnp.concatenate([h_hi, h_lo], axis=0), rw_ref[...], preferred_element_type=F32)
    logits = (prod[:tm, :ROUTE_LANES] + prod[:tm, ROUTE_LANES:]
              + prod[tm:, :ROUTE_LANES] + prod[tm:, ROUTE_LANES:]) + rb_ref[...]
    lane = lax.broadcasted_iota(jnp.int32, logits.shape, 1)
    is_c = lane < N_GROUPS
    lc = jnp.where(is_c, logits, NEG_INF)
    mc = jnp.max(lc, axis=-1, keepdims=True)
    grp = jnp.min(jnp.where(lc == mc, lane, ROUTE_LANES), axis=-1, keepdims=True)
    pg = 1.0 / jnp.sum(jnp.where(is_c, jnp.exp(lc - mc), 0.0), axis=-1, keepdims=True)
    lo = N_GROUPS + grp * EXPERTS_PER_GROUP
    in_g = (lane >= lo) & (lane < lo + EXPERTS_PER_GROUP)
    lf = jnp.where(in_g, logits, NEG_INF)
    t1 = jnp.max(lf, axis=-1, keepdims=True)
    i1 = jnp.min(jnp.where(lf == t1, lane, ROUTE_LANES), axis=-1, keepdims=True)
    lf2 = jnp.where(lane == i1, NEG_INF, lf)
    t2 = jnp.max(lf2, axis=-1, keepdims=True)
    i2 = jnp.min(jnp.where(lf2 == t2, lane, ROUTE_LANES), axis=-1, keepdims=True)
    e2 = jnp.exp(t2 - t1)
    w1 = pg / (1.0 + e2)
    w2 = pg * e2 / (1.0 + e2)
    rec = jnp.where(lane == 0, (i1 - N_GROUPS).astype(F32),
                    jnp.where(lane == 1, (i2 - N_GROUPS).astype(F32),
                              jnp.where(lane == 2, w1, jnp.where(lane == 3, w2, 0.0))))
    route_ref[0] = rec

    @pl.when((pl.program_id(0) == 0) & (pl.program_id(1) == 0))
    def _():
        cnt_ref[...] = jnp.zeros(cnt_ref.shape, F32)

    e0, e1 = _choice_onehots(rec)
    cnt_ref[...] = cnt_ref[...] + jnp.sum(e0 + e1, axis=0, keepdims=True)


def out_projection(ys, x, mods, mod_row0, norm_g, w_out_bf, rw, rb, *, tm):
    b, l, d = x.shape
    c = MIX_CH
    row = (lambda bb: 0) if mod_row0 is None else (lambda bb: mod_row0 + bb)
    yspec = pl.BlockSpec((1, tm, c), lambda bb, i: (bb, i, 0))
    xspec = pl.BlockSpec((1, tm, d), lambda bb, i: (bb, i, 0))
    return pl.pallas_call(
        _out_body,
        grid=(b, l // tm),
        in_specs=[yspec] * 4 + [xspec,
                                pl.BlockSpec((1, N_MOD, d), lambda bb, i: (row(bb), 0, 0)),
                                pl.BlockSpec((1, d), lambda bb, i: (0, 0)),
                                pl.BlockSpec(w_out_bf.shape, lambda bb, i: (0, 0)),
                                pl.BlockSpec(rw.shape, lambda bb, i: (0, 0)),
                                pl.BlockSpec(rb.shape, lambda bb, i: (0, 0))],
        out_specs=[xspec, pl.BlockSpec((1, tm, d // 2), lambda bb, i: (bb, i, 0)),
                   pl.BlockSpec((1, tm, ROUTE_LANES), lambda bb, i: (bb, i, 0)),
                   pl.BlockSpec((SUBLANES, ROUTE_LANES), lambda bb, i: (0, 0))],
        out_shape=[jax.ShapeDtypeStruct((b, l, d), F32), jax.ShapeDtypeStruct((b, l, d // 2), jnp.uint32),
                   jax.ShapeDtypeStruct((b, l, ROUTE_LANES), F32),
                   jax.ShapeDtypeStruct((SUBLANES, ROUTE_LANES), F32)],
        compiler_params=_params("arbitrary", "arbitrary"),
        name="out_proj",
    )(*ys, x, mods, norm_g.reshape(1, d), w_out_bf, rw, rb)


def _expert_body(be_ref, nv_ref, xs_ref, wg_ref, wu_ref, wd_ref, o_ref, wg_s, wu_s, wd_s):
    i = pl.program_id(0)
    prev = be_ref[jnp.maximum(i - 1, 0)]

    @pl.when((i == 0) | (be_ref[i] != prev))
    def _():
        wg_s[...] = wg_ref[0, 0].astype(BF16)
        wu_s[...] = wu_ref[0, 0].astype(BF16)
        wd_s[...] = wd_ref[0, 0].astype(BF16)

    @pl.when(nv_ref[i] > 0)
    def _():
        row = lax.broadcasted_iota(jnp.int32, xs_ref.shape, 0)
        x = _unpack_bf16_pairs(jnp.where(row < nv_ref[i], xs_ref[...], jnp.uint32(0)))
        g = jnp.dot(x, wg_s[...], preferred_element_type=F32)
        u = jnp.dot(x, wu_s[...], preferred_element_type=F32)
        a = (_silu(g) * u).astype(BF16)
        y = jnp.dot(a, wd_s[...], preferred_element_type=F32)
        o_ref[...] = _pack_bf16_pairs(y.astype(BF16).astype(F32))

    @pl.when(nv_ref[i] <= 0)
    def _():
        o_ref[...] = jnp.zeros(o_ref.shape, jnp.uint32)


def expert_ffn(xs, blk_e, n_valid, layer, e_gate, e_up, e_down):
    rows, dh = xs.shape
    d = 2 * dh
    nb = rows // MOE_BLOCK
    de = e_gate.shape[-1]
    grid_spec = pltpu.PrefetchScalarGridSpec(
        num_scalar_prefetch=2,
        grid=(nb,),
        in_specs=[pl.BlockSpec((MOE_BLOCK, dh), lambda i, be, nv: (i, 0)),
                  pl.BlockSpec((1, 1, d, de), lambda i, be, nv: (layer, be[i], 0, 0)),
                  pl.BlockSpec((1, 1, d, de), lambda i, be, nv: (layer, be[i], 0, 0)),
                  pl.BlockSpec((1, 1, de, d), lambda i, be, nv: (layer, be[i], 0, 0))],
        out_specs=pl.BlockSpec((MOE_BLOCK, dh), lambda i, be, nv: (i, 0)),
        scratch_shapes=[pltpu.VMEM((d, de), BF16), pltpu.VMEM((d, de), BF16), pltpu.VMEM((de, d), BF16)],
    )
    return pl.pallas_call(
        _expert_body, grid_spec=grid_spec,
        out_shape=jax.ShapeDtypeStruct((rows, dh), jnp.uint32),
        compiler_params=_params("arbitrary"),
        name="expert_ffn",
    )(blk_e, n_valid, xs, e_gate, e_up, e_down)


RANK_TILE = 512


def _choice_onehots(rec):
    lanef = lax.broadcasted_iota(jnp.int32, rec.shape, 1).astype(F32)
    return (lanef == rec[:, 0:1]).astype(F32), (lanef == rec[:, 1:2]).astype(F32)


def _slot_body(route_ref, cnt_ref, slot_ref, carry_ref):
    t = RANK_TILE
    rec = route_ref[...]
    lane = lax.broadcasted_iota(jnp.int32, rec.shape, 1)
    e0, e1 = _choice_onehots(rec)
    both = e0 + e1

    @pl.when(pl.program_id(0) == 0)
    def _():
        cnt = cnt_ref[...]
        padded = jnp.floor((cnt + (MOE_BLOCK - 1)) * (1.0 / MOE_BLOCK)) * MOE_BLOCK
        ln = lax.broadcasted_iota(jnp.int32, cnt.shape, 1)
        incl = padded
        sh = 1
        while sh < ROUTE_LANES:
            incl = incl + jnp.where(ln >= sh, pltpu.roll(incl, sh, axis=1), 0.0)
            sh *= 2
        carry_ref[...] = incl - padded

    earlier = (lax.broadcasted_iota(jnp.int32, (t, t), 0) > lax.broadcasted_iota(jnp.int32, (t, t), 1))
    before = jnp.dot(earlier.astype(BF16), both.astype(BF16), preferred_element_type=F32) + carry_ref[0:1, :]
    s0 = jnp.sum(e0 * before, axis=-1, keepdims=True)
    s1 = jnp.sum(e1 * before, axis=-1, keepdims=True)
    slot_ref[...] = jnp.where(lane == 0, s0, jnp.where(lane == 1, s1, 0.0))
    carry_ref[...] = carry_ref[...] + jnp.sum(both, axis=0, keepdims=True)


def moe_slots(route, counts):
    n = route.shape[0]
    t = RANK_TILE
    return pl.pallas_call(
        _slot_body,
        grid=(n // t,),
        in_specs=[pl.BlockSpec((t, ROUTE_LANES), lambda i: (i, 0)),
                  pl.BlockSpec((SUBLANES, ROUTE_LANES), lambda i: (0, 0))],
        out_specs=pl.BlockSpec((t, ROUTE_LANES), lambda i: (i, 0)),
        out_shape=jax.ShapeDtypeStruct((n, ROUTE_LANES), F32),
        scratch_shapes=[pltpu.VMEM((SUBLANES, ROUTE_LANES), F32)],
        compiler_params=_params("arbitrary"),
        name="moe_slots",
    )(route, counts)


def _sc_mesh():
    return plsc.VectorSubcoreMesh(core_axis_name="c", subcore_axis_name="s")


def _sc_worker():
    return lax.axis_index("s") * SC_CORES + lax.axis_index("c")


DISPATCH_ROWS = 64
COMBINE_ROWS = 64


def sc_dispatch(rows, dest, n_slots):
    n, w = rows.shape
    ch = DISPATCH_ROWS
    per_w = n // SC_WORKERS
    n_ch = per_w // ch

    @functools.partial(
        pl.kernel, mesh=_sc_mesh(),
        out_type=jax.ShapeDtypeStruct((n_slots, w), rows.dtype),
        scratch_types=[pltpu.VMEM((ch,), jnp.int32), pltpu.VMEM((ch, w), rows.dtype)],
    )
    def scatter_kernel(rows_hbm, dest_hbm, out_hbm, idx_v, rows_v):
        wid = _sc_worker()

        @pl.loop(0, n_ch)
        def _(j):
            chunk = wid * n_ch + j
            pltpu.sync_copy(rows_hbm.at[pl.ds(pl.multiple_of(chunk * ch, ch), ch)], rows_v)
            for k in range(2):
                pltpu.sync_copy(dest_hbm.at[k, chunk], idx_v)
                pltpu.sync_copy(rows_v, out_hbm.at[idx_v])

    return scatter_kernel(rows, dest)


def sc_gather_rows(table, idx):
    s, w = table.shape
    m = idx.shape[0]
    ch = COMBINE_ROWS
    per_w = m // SC_WORKERS
    n_ch = per_w // ch

    @functools.partial(
        pl.kernel, mesh=_sc_mesh(),
        out_type=jax.ShapeDtypeStruct((m, w), table.dtype),
        scratch_types=[pltpu.VMEM((ch,), jnp.int32), pltpu.VMEM((ch, w), table.dtype), pltpu.SemaphoreType.DMA],
    )
    def gather_kernel(table_hbm, idx_hbm, out_hbm, idx_v, rows_v, sem):
        wid = _sc_worker()

        @pl.loop(0, n_ch)
        def _(j):
            off = pl.multiple_of((wid * n_ch + j) * ch, ch)
            pltpu.sync_copy(idx_hbm.at[pl.ds(off, ch)], idx_v)
            pltpu.async_copy(table_hbm.at[idx_v], rows_v, sem).wait()
            pltpu.sync_copy(rows_v, out_hbm.at[pl.ds(off, ch)])

    return gather_kernel(table, idx)


def hier_moe(h_packed, route, cnt, layer, e_gate, e_up, e_down):
    b, l, dh = h_packed.shape
    n = b * l
    assert n % (SC_WORKERS * DISPATCH_ROWS) == 0 and (2 * n) % (SC_WORKERS * COMBINE_ROWS) == 0
    slots = moe_slots(route.reshape(n, ROUTE_LANES), cnt)
    counts = cnt[0, :N_EXPERTS].astype(jnp.int32)
    padded = (counts + MOE_BLOCK - 1) // MOE_BLOCK * MOE_BLOCK
    pend = jnp.cumsum(padded)
    nb = -(-2 * n // MOE_BLOCK) + N_EXPERTS
    blk0 = jnp.arange(nb, dtype=jnp.int32) * MOE_BLOCK
    owner = pend[None, :] <= blk0[:, None]
    blk_e = jnp.minimum(jnp.sum(owner, axis=1), N_EXPERTS - 1).astype(jnp.int32)
    run_end = jnp.sum(jnp.where(jnp.arange(N_EXPERTS)[None, :] == blk_e[:, None],
                                (pend - padded + counts)[None, :], 0), axis=1)
    n_valid = jnp.clip(run_end - blk0, 0, MOE_BLOCK).astype(jnp.int32)
    dest = slots[:, 0:2].astype(jnp.int32).T
    xs = sc_dispatch(h_packed.reshape(n, dh), dest.reshape(2, n // DISPATCH_ROWS, DISPATCH_ROWS), nb * MOE_BLOCK)
    y = expert_ffn(xs, blk_e, n_valid, layer, e_gate, e_up, e_down)
    return sc_gather_rows(y, dest.reshape(2 * n)).reshape(2, b, l, dh)


def _final_body(x_ref, y0_ref, y1_ref, route_ref, pmod_ref, g_ref, o_ref):
    x = _moe_residual(x_ref[0], y0_ref, y1_ref, route_ref, pmod_ref)
    ms = jnp.mean(x * x, axis=-1, keepdims=True)
    o_ref[0] = x * lax.rsqrt(ms + EPS) * g_ref[...]


def final_norm(x1, res, mod_row0, norm_g, *, tm):
    b, l, d = x1.shape
    row = (lambda bb: 0) if mod_row0 is None else (lambda bb: mod_row0 + bb)
    xspec = pl.BlockSpec((1, tm, d), lambda bb, i: (bb, i, 0))
    rargs, rspecs = _residual_specs(res, tm, d, row)
    return pl.pallas_call(
        _final_body,
        grid=(b, l // tm),
        in_specs=[xspec] + rspecs + [pl.BlockSpec((1, d), lambda bb, i: (0, 0))],
        out_specs=xspec,
        out_shape=jax.ShapeDtypeStruct((b, l, d), F32),
        compiler_params=_params("arbitrary", "arbitrary"),
        name="final_norm",
    )(x1, *rargs, norm_g.reshape(1, d))


def _rope_tables(seq):
    rows = seq // GRID_W
    row_pos = jnp.repeat(jnp.arange(rows, dtype=F32), GRID_W)
    col_pos = jnp.tile(jnp.arange(GRID_W, dtype=F32), rows)
    n_freq = HEAD_DIM // 4
    inv = ROPE_BASE ** (-jnp.arange(n_freq, dtype=F32) / n_freq)
    ang = jnp.concatenate([row_pos[:, None] * inv, col_pos[:, None] * inv], axis=-1)
    cs, sn = jnp.cos(ang), jnp.sin(ang)
    cos_f = jnp.tile(jnp.concatenate([cs, cs], axis=-1), (1, N_Q_HEADS))
    sin_s = jnp.tile(jnp.concatenate([-sn, sn], axis=-1), (1, N_Q_HEADS))
    return cos_f, sin_s


def kernel(x_prompt, x_sample, cache_k, cache_v, c, c_ctx, ada_w, ada_b, norm1_g, norm2_g, w_in, conv_dw_w, conv_dw_b, conv_ln_g, conv_ln_b, hy_short_w, hy_short_b, hy_fw1, hy_fb1, hy_freq1, hy_fw2, hy_fb2, hy_freq2, hy_fw3, hy_fb3, hy_d, attn_sink, w_out, router_coarse_w, router_coarse_b, router_fine_w, router_fine_b, exp_gate, exp_up, exp_down, norm_f_g):
    depth = ada_w.shape[0]
    bp, lp, d = x_prompt.shape
    bs, ls, _ = x_sample.shape
    assert bp % 2 == 0 and bs % 2 == 0 and ls % ATT_BLOCK == 0 and ls % GRID_W == 0

    n_rows = -(-(1 + bs) // SUBLANES) * SUBLANES
    cvec = jnp.concatenate([c_ctx[None, :], c, jnp.zeros((n_rows - 1 - bs, d), F32)], axis=0)
    mods = adaln_all(cvec, ada_w, ada_b)

    rope = _rope_tables(ls)
    fnet_tabs, hy_tabs = {}, {}
    for seq in {lp, ls}:
        fnet_tabs[seq] = fnet_tables(seq)
        hy_tabs[seq] = hyena_tables(seq)
    ck = cache_k.reshape(cache_k.shape[0], depth, cache_k.shape[2], -1)
    cv = cache_v.reshape(cache_v.shape[0], depth, cache_v.shape[2], -1)
    pad = ROUTE_LANES - N_GROUPS - N_EXPERTS

    tm_p = min(lp, 512)
    tm_s = min(ls, 512)
    xp, xs = x_prompt, x_sample
    res_p = res_s = None
    ks_out, vs_out = [], []
    for l in range(depth):
        w_in_bf = w_in[l].astype(BF16)
        w_out_bf = w_out[l].astype(BF16)
        rw = jnp.concatenate([router_coarse_w[l], router_fine_w[l], jnp.zeros((d, pad), F32)], axis=1)
        rw_hi = rw.astype(BF16)
        rw = jnp.concatenate([rw_hi, (rw - rw_hi.astype(F32)).astype(BF16)], axis=1)
        rb = jnp.concatenate([router_coarse_b[l], router_fine_b[l], jnp.zeros((pad,), F32)])[None, :]
        filt = (hy_fw1[l], hy_fb1[l], hy_freq1[l], hy_fw2[l], hy_fb2[l], hy_freq2[l], hy_fw3[l], hy_fb3[l])
        sink = attn_sink[l]

        def mixers(ua, ub, uc, yd, seq):
            ya = conformer_conv(ua, conv_dw_w[l], conv_dw_b[l], conv_ln_g[l], conv_ln_b[l])
            yb = hyena_mixer(ub, hy_short_w[l], hy_short_b[l], hy_d[l], hyena_filter_spectrum(seq, filt, hy_tabs[seq]),
                             hy_tabs[seq])
            yc = fnet_mixer(uc, fnet_tabs[seq])
            return (ya, yb, yc, yd)

        outs = in_projection(xp, mods[l], None, norm1_g[l], w_in_bf, res=res_p, tm=tm_p)
        if res_p is not None:
            xp, outs = outs[0], outs[1:]
        ua, ub, uc, uq, ukv = outs
        kvw = N_KV_HEADS * HEAD_DIM
        ks_out.append(ukv[..., :kvw].reshape(bp, lp, N_KV_HEADS, HEAD_DIM))
        vs_out.append(ukv[..., kvw:].reshape(bp, lp, N_KV_HEADS, HEAD_DIM))
        ys = mixers(ua, ub, uc, context_attention(uq, ukv, sink), lp)
        x1p, hp, route, cnt = out_projection(ys, xp, mods[l], None, norm2_g[l], w_out_bf, rw, rb, tm=tm_p)
        res_p = (hier_moe(hp, route, cnt, l, exp_gate, exp_up, exp_down), route, mods[l])
        xp = x1p

        outs = in_projection(xs, mods[l], 1, norm1_g[l], w_in_bf, res=res_s, rope=rope, tm=tm_s)
        if res_s is not None:
            xs, outs = outs[0], outs[1:]
        ua, ub, uc, uq, uqr, ukv = outs
        ys = mixers(ua, ub, uc, latent_attention(uq, uqr, ukv, ck, cv, l, sink), ls)
        x1s, hs, route, cnt = out_projection(ys, xs, mods[l], 1, norm2_g[l], w_out_bf, rw, rb, tm=tm_s)
        res_s = (hier_moe(hs, route, cnt, l, exp_gate, exp_up, exp_down), route, mods[l])
        xs = x1s

    y_prompt = final_norm(xp, res_p, None, norm_f_g, tm=tm_p)
    y_sample = final_norm(xs, res_s, 1, norm_f_g, tm=tm_s)
    return (y_prompt, y_sample, jnp.stack(ks_out, axis=1), jnp.stack(vs_out, axis=1))
```

```python
import functools
import math

import numpy as np
import jax
import jax.numpy as jnp
from jax import lax
from jax.experimental import pallas as pl
from jax.experimental.pallas import tpu as pltpu
from jax.experimental.pallas import tpu_sc as plsc

F32 = jnp.float32
BF16 = jnp.bfloat16

HEAD_DIM = 64
LOG2E = math.log2(math.e)
ATT_SCALE = HEAD_DIM ** -0.5 * LOG2E
N_Q_HEADS = 4
N_KV_HEADS = 2
Q_PER_KV = N_Q_HEADS // N_KV_HEADS
WINDOW = 128
ATT_BLOCK = 128
GRID_W = 64
ROPE_BASE = 10000.0
N_GROUPS = 4
EXPERTS_PER_GROUP = 8
N_EXPERTS = N_GROUPS * EXPERTS_PER_GROUP
MOE_BLOCK = 512
N_MOD = 6
EPS = 1e-6
NEG_INF = -1e30
HYENA_DECAY_TARGET = 1e-2
HYENA_FAST_PCT = 0.3
HYENA_SLOW_PCT = 1.5
HYENA_SHIFT = 0.05

LANES = 128
SUBLANES = 8
VMEM_LIMIT = 56 * 1024 * 1024

MIX_CH = 256
ROUTE_LANES = 128
SC_CORES = 2
SC_WORKERS = SC_CORES * 16


def _params(*sem):
    return pltpu.CompilerParams(dimension_semantics=sem, vmem_limit_bytes=VMEM_LIMIT)


def _silu(x):
    return x * jax.nn.sigmoid(x)


def _ada_body(c_ref, w_ref, b_ref, o_ref):
    s = _silu(c_ref[...]).astype(BF16)
    o_ref[0] = jnp.dot(s, w_ref[0].astype(BF16), preferred_element_type=F32) + b_ref[0]


def adaln_all(cvec, ada_w, ada_b):
    depth, d, n6 = ada_w.shape
    r = cvec.shape[0]
    tn = n6 // 4
    out = pl.pallas_call(
        _ada_body,
        grid=(depth, n6 // tn),
        in_specs=[
            pl.BlockSpec((r, d), lambda l, j: (0, 0)),
            pl.BlockSpec((1, d, tn), lambda l, j: (l, 0, j)),
            pl.BlockSpec((1, 1, tn), lambda l, j: (l, 0, j)),
        ],
        out_specs=pl.BlockSpec((1, r, tn), lambda l, j: (l, 0, j)),
        out_shape=jax.ShapeDtypeStruct((depth, r, n6), F32),
        compiler_params=_params("arbitrary", "arbitrary"),
        name="adaln",
    )(cvec, ada_w, ada_b.reshape(depth, 1, n6))
    return out.reshape(depth, r, N_MOD, d)


def _swap_halves(x):
    pieces = []
    for j in range(x.shape[1] // LANES):
        xj = x[:, j * LANES:(j + 1) * LANES]
        fwd = pltpu.roll(xj, LANES - HEAD_DIM // 2, axis=1)
        bwd = pltpu.roll(xj, HEAD_DIM // 2, axis=1)
        lane = lax.broadcasted_iota(jnp.int32, xj.shape, 1)
        pieces.append(jnp.where((lane % HEAD_DIM) < HEAD_DIM // 2, fwd, bwd))
    return pieces[0] if len(pieces) == 1 else jnp.concatenate(pieces, axis=1)


def _rmsnorm_mod(x, g, scale, shift):
    ms = jnp.mean(x * x, axis=-1, keepdims=True)
    return (x * lax.rsqrt(ms + EPS)) * (g * (1.0 + scale)) + shift


def _moe_residual(x1, y0_ref, y1_ref, route_ref, pmod_ref, rows=slice(None)):
    y0 = _unpack_bf16_pairs(y0_ref[0, 0, rows, :], F32)
    y1 = _unpack_bf16_pairs(y1_ref[0, 0, rows, :], F32)
    moe = route_ref[0, rows, 2:3] * y0 + route_ref[0, rows, 3:4] * y1
    return x1 + pmod_ref[0, 5:6, :] * moe


def _residual_specs(res, tm, d, row):
    pair, route, pmods = res
    args = [pair, pair, route, pmods]
    specs = [pl.BlockSpec((1, 1, tm, d // 2), lambda bb, i: (0, bb, i, 0)),
             pl.BlockSpec((1, 1, tm, d // 2), lambda bb, i: (1, bb, i, 0)),
             pl.BlockSpec((1, tm, ROUTE_LANES), lambda bb, i: (bb, i, 0)),
             pl.BlockSpec((1, N_MOD, d), lambda bb, i: (row(bb), 0, 0))]
    return args, specs


IN_SPLIT = 2


def _in_body(*refs, fuse_res, rope):
    it = iter(refs)
    x_ref = next(it)
    if fuse_res:
        res_refs = [next(it) for _ in range(4)]
    mod_ref = next(it)
    g_ref = next(it)
    w_ref = next(it)
    if rope:
        cos_ref = next(it)
        sin_ref = next(it)
    outs = list(it)
    xo_ref = outs.pop(0) if fuse_res else None
    c = MIX_CH
    part = x_ref.shape[1] // IN_SPLIT
    hs = []
    for p in range(IN_SPLIT):
        rows = slice(p * part, (p + 1) * part)
        x = x_ref[0, rows, :]
        if fuse_res:
            x = _moe_residual(x, *res_refs, rows)
            xo_ref[0, rows, :] = x
        hs.append(_rmsnorm_mod(x, g_ref[...], mod_ref[0, 1:2, :], mod_ref[0, 0:1, :]).astype(BF16))
    ua_ref, ub_ref, uc_ref, uq_ref = outs[:4]
    for p in range(IN_SPLIT):
        rows = slice(p * part, (p + 1) * part)
        u = jnp.dot(hs[p], w_ref[...], preferred_element_type=F32)
        ua_ref[0, rows, :] = u[:, 0:2 * c]
        ub_ref[0, rows, :] = u[:, 2 * c:5 * c]
        uc_ref[0, rows, :] = u[:, 5 * c:6 * c]
        q = u[:, 6 * c:7 * c] * ATT_SCALE
        k = u[:, 7 * c:7 * c + c // 2]
        v = u[:, 7 * c + c // 2:8 * c]
        uq_ref[0, rows, :] = q.astype(uq_ref.dtype)
        if rope:
            uqr_ref, ukv_ref = outs[4:]
            cs = cos_ref[rows, :]
            sn = sin_ref[rows, :]
            uqr_ref[0, rows, :] = (q * cs + _swap_halves(q) * sn).astype(uqr_ref.dtype)
            kr = k * cs[:, :c // 2] + _swap_halves(k) * sn[:, :c // 2]
            ukv_ref[0, rows, :] = jnp.concatenate([kr, v], axis=1).astype(ukv_ref.dtype)
        else:
            outs[4][0, rows, :] = u[:, 7 * c:8 * c]


def in_projection(x, mods, mod_row0, norm_g, w_in_bf, *, res=None, rope=None, tm):
    b, l, d = x.shape
    c = MIX_CH
    grid = (b, l // tm)
    row = (lambda bb: 0) if mod_row0 is None else (lambda bb: mod_row0 + bb)
    xspec = pl.BlockSpec((1, tm, d), lambda bb, i: (bb, i, 0))
    mspec = pl.BlockSpec((1, N_MOD, d), lambda bb, i: (row(bb), 0, 0))
    args, specs = [x], [xspec]
    if res is not None:
        rargs, rspecs = _residual_specs(res, tm, d, row)
        args += rargs
        specs += rspecs
    args += [mods, norm_g.reshape(1, d), w_in_bf]
    specs += [mspec, pl.BlockSpec((1, d), lambda bb, i: (0, 0)),
              pl.BlockSpec(w_in_bf.shape, lambda bb, i: (0, 0))]
    if rope is not None:
        args += [rope[0], rope[1]]
        specs += [pl.BlockSpec((tm, c), lambda bb, i: (i, 0))] * 2

    def ospec(w):
        return pl.BlockSpec((1, tm, w), lambda bb, i: (bb, i, 0))

    out_shape, out_specs = [], []
    if res is not None:
        out_shape.append(jax.ShapeDtypeStruct((b, l, d), F32))
        out_specs.append(xspec)
    widths = [(2 * c, F32), (3 * c, F32), (c, F32), (c, BF16)] + ([(c, BF16), (c, BF16)] if rope is not None else [(c, F32)])
    for w, dt in widths:
        out_shape.append(jax.ShapeDtypeStruct((b, l, w), dt))
        out_specs.append(ospec(w))
    return pl.pallas_call(
        functools.partial(_in_body, fuse_res=res is not None, rope=rope is not None),
        grid=grid, in_specs=specs, out_specs=out_specs, out_shape=out_shape,
        compiler_params=_params("arbitrary", "arbitrary"),
        name="in_proj",
    )(*args)


def _dw_tile(win, w_ref, n_taps, first, rows):
    acc = w_ref[0:1, :] * win[first:first + rows]
    for k in range(1, n_taps):
        acc = acc + w_ref[k:k + 1, :] * win[first + k:first + k + rows]
    return acc


CONV_PAD = 16
CONV_ROWS = 256
CONV_CHUNK = 512
CONV_TAIL = CONV_PAD + CONV_ROWS + SUBLANES + SUBLANES


def _conf_body(u_ref, w_ref, b_ref, g_ref, beta_ref, o_ref, gp_ref, sh_ref, *, seq, n_taps, chunk):
    c = MIX_CH
    r = CONV_ROWS
    first = CONV_PAD - (n_taps - 1) // 2
    gp_ref[0:CONV_PAD, :] = jnp.zeros((CONV_PAD, c), F32)
    gp_ref[CONV_PAD + seq:CONV_PAD + seq + CONV_TAIL, :] = jnp.zeros((CONV_TAIL, c), F32)

    def fill(i, carry):
        r0 = pl.multiple_of(i * r, r)
        a = u_ref[0, pl.ds(r0, r), 0:c]
        g = u_ref[0, pl.ds(r0, r), c:2 * c]
        gp_ref[pl.ds(CONV_PAD + r0, r), :] = a * jax.nn.sigmoid(g)
        return carry

    lax.fori_loop(0, seq // r, fill, 0)
    n_copy_tiles = sh_ref.shape[1] // r

    def do_chunk(ci, carry):
        c0 = pl.multiple_of(ci * chunk, chunk)

        def shift_tile(ti, carry2):
            t0 = pl.multiple_of(ti * r, r)
            win = gp_ref[pl.ds(c0 + t0, r + SUBLANES), :]
            for m in range(1, SUBLANES):
                sh_ref[m - 1, pl.ds(t0, r), :] = win[m:m + r]
            return carry2

        lax.fori_loop(0, n_copy_tiles, shift_tile, 0)

        def tile(ti, carry2):
            t0 = pl.multiple_of(ti * r, r)
            acc = None
            for k in range(n_taps):
                a8, m = (first + k) // SUBLANES * SUBLANES, (first + k) % SUBLANES
                src = gp_ref[pl.ds(c0 + t0 + a8, r), :] if m == 0 else sh_ref[m - 1, pl.ds(t0 + a8, r), :]
                term = w_ref[k:k + 1, :] * src
                acc = term if acc is None else acc + term
            z = acc + b_ref[...]
            mu = jnp.mean(z, axis=-1, keepdims=True)
            zc = z - mu
            var = jnp.mean(zc * zc, axis=-1, keepdims=True)
            zn = zc * lax.rsqrt(var + EPS) * g_ref[...] + beta_ref[...]
            o_ref[0, pl.ds(c0 + t0, r), :] = _silu(zn).astype(o_ref.dtype)
            return carry2

        lax.fori_loop(0, chunk // r, tile, 0)
        return carry

    lax.fori_loop(0, seq // chunk, do_chunk, 0)


def conformer_conv(ua, dw_w, dw_b, ln_g, ln_b):
    b, l, c2 = ua.shape
    c = MIX_CH
    k = dw_w.shape[0]
    chunk = min(l, CONV_CHUNK)
    copy_rows = -(-(chunk + CONV_PAD + k) // CONV_ROWS) * CONV_ROWS
    assert l % chunk == 0 and k - 1 <= 2 * CONV_PAD and copy_rows - chunk + SUBLANES <= CONV_PAD + CONV_TAIL
    vec = pl.BlockSpec((1, c), lambda bb: (0, 0))
    return pl.pallas_call(
        functools.partial(_conf_body, seq=l, n_taps=k, chunk=chunk),
        grid=(b,),
        in_specs=[pl.BlockSpec((1, l, c2), lambda bb: (bb, 0, 0)),
                  pl.BlockSpec((k, c), lambda bb: (0, 0)), vec, vec, vec],
        out_specs=pl.BlockSpec((1, l, c), lambda bb: (bb, 0, 0)),
        out_shape=jax.ShapeDtypeStruct((b, l, c), BF16),
        scratch_shapes=[pltpu.VMEM((CONV_PAD + l + CONV_TAIL, c), F32),
                        pltpu.VMEM((SUBLANES - 1, copy_rows, c), F32)],
        compiler_params=_params("arbitrary"),
        name="conformer",
    )(ua, dw_w, dw_b.reshape(1, c), ln_g.reshape(1, c), ln_b.reshape(1, c))


SHORT_PAD = 8
SHORT_ROWS = 64


def _short_body(u_ref, w_ref, b_ref, o_ref, xp_ref, *, seq, n_taps):
    c = MIX_CH
    r = SHORT_ROWS
    half = (n_taps - 1) // 2
    zero = jnp.zeros((SHORT_PAD, c), F32)
    xp_ref[0:SHORT_PAD, :] = zero
    xp_ref[SHORT_PAD + seq:SHORT_PAD + seq + SHORT_PAD, :] = zero
    for q in range(u_ref.shape[0]):
        def fill(i, carry):
            r0 = pl.multiple_of(i * r, r)
            xp_ref[pl.ds(SHORT_PAD + r0, r), :] = u_ref[q, pl.ds(r0, r), :]
            return carry

        lax.fori_loop(0, seq // r, fill, 0)

        def tile(i, carry):
            r0 = pl.multiple_of(i * r, r)
            win = xp_ref[pl.ds(r0, r + 2 * SHORT_PAD), :]
            o_ref[q, pl.ds(r0, r), :] = _dw_tile(win, w_ref, n_taps, SHORT_PAD - half, r) + b_ref[...]
            return carry

        lax.fori_loop(0, seq // r, tile, 0)


SHORT_BLOCK_ROWS = 1024


def hyena_short_conv(ub, short_w, short_b):
    b, l, c3 = ub.shape
    c = MIX_CH
    k = short_w.shape[0]
    nbat = math.gcd(b, max(1, SHORT_BLOCK_ROWS // l))
    return pl.pallas_call(
        functools.partial(_short_body, seq=l, n_taps=k),
        grid=(b // nbat, c3 // c),
        in_specs=[pl.BlockSpec((nbat, l, c), lambda bb, j: (bb, 0, j)),
                  pl.BlockSpec((k, c), lambda bb, j: (0, j)),
                  pl.BlockSpec((1, c), lambda bb, j: (0, j))],
        out_specs=pl.BlockSpec((nbat, l, c), lambda bb, j: (bb, 0, j)),
        out_shape=jax.ShapeDtypeStruct((b, l, c3), F32),
        scratch_shapes=[pltpu.VMEM((l + 2 * SHORT_PAD, c), F32)],
        compiler_params=_params("arbitrary", "arbitrary"),
        name="hyena_short",
    )(ub, short_w, short_b.reshape(1, c3))


def _fft_sizes(seq):
    n = 2 * seq
    n2 = 128 if n >= 4096 else 32
    return n, n // n2, n2


FFT_GROUP = SUBLANES


def _kron_matrices(n, n1, h):
    k1 = np.arange(n1)[:, None]
    i1 = np.arange(h)[None, :]
    th = 2.0 * np.pi * ((k1 * i1) % n1) / n1
    eye = np.eye(FFT_GROUP)
    cs, sn = np.kron(np.cos(th), eye), np.kron(np.sin(th), eye)
    fwd = np.block([[cs, sn], [-sn, cs]])
    inv = np.block([[cs.T, -sn.T], [sn.T, cs.T]]) / n
    return fwd.astype(np.float32), inv.astype(np.float32)


def _twiddle_tables(n, n1, n2):
    k1 = jnp.arange(n1, dtype=jnp.int32)[:, None]
    i2 = jnp.arange(n2, dtype=jnp.int32)[None, :]
    th = ((k1 * i2) % n).astype(F32) * (2.0 * math.pi / n)
    shape = (n1, n2 // FFT_GROUP, FFT_GROUP, LANES)
    full = lambda a: jnp.broadcast_to(a.reshape(shape[:3] + (1,)), shape)
    return full(jnp.cos(th)), full(jnp.sin(th))


def _mid_tables(n2):
    a = np.arange(n2)
    th = 2.0 * np.pi * ((a[:, None] * a[None, :]) % n2) / n2
    cs, sn = np.cos(th), np.sin(th)
    fwd = np.concatenate([np.concatenate([cs, sn], 1), np.concatenate([-sn, cs], 1)], 0)
    inv = np.concatenate([np.concatenate([cs, -sn], 1), np.concatenate([sn, cs], 1)], 0)
    return fwd.astype(np.float32), inv.astype(np.float32)


MID_CHUNK = 16


def _dot(a, b):
    return jnp.dot(a.astype(BF16), b.astype(BF16), preferred_element_type=F32)


def _filter_stage1_body(x_ref, m_ref, c_ref, s_ref, o_ref, *, n1, gs):
    j, c = FFT_GROUP, MIX_CH
    for s in range(gs):
        a = _dot(m_ref[...], x_ref[:, s].reshape(n1 * j, c))
        ar = a[:n1 * j].reshape(n1, j, c)
        ai = a[n1 * j:].reshape(n1, j, c)
        cs, sn = _lanes(c_ref[:, s], c), _lanes(s_ref[:, s], c)
        o_ref[0, 0, :, s] = ar * cs + ai * sn
        o_ref[0, 1, :, s] = ai * cs - ar * sn


def filter_stage1(k, order, mat, twc, tws, *, n1, n2):
    g, j, c = n2 // FFT_GROUP, FFT_GROUP, MIX_CH
    gs = _group_step(g)
    tw = pl.BlockSpec((n1, gs, j, LANES), lambda gi: (0, gi, 0, 0))
    out = pl.pallas_call(
        functools.partial(_filter_stage1_body, n1=n1, gs=gs),
        grid=(g // gs,),
        in_specs=[pl.BlockSpec((n1, gs, j, c), lambda gi: (0, gi, 0, order)),
                  pl.BlockSpec(mat.shape, lambda gi: (0, 0)), tw, tw],
        out_specs=pl.BlockSpec((1, 2, n1, gs, j, c), lambda gi: (0, 0, 0, gi, 0, 0)),
        out_shape=jax.ShapeDtypeStruct((1, 2, n1, g, j, c), F32),
        compiler_params=_params("arbitrary"),
        name="filter_stage1",
    )(k.reshape(n1, g, j, k.shape[-1]), mat, twc, tws)
    return out.reshape(1, 2, n1 * n2, c)


def _mid_body(a_ref, k_ref, f_ref, g_ref, o_ref, *, n2, kc):
    for j in range(kc):
        rows = slice(j * n2, (j + 1) * n2)
        blk = jnp.concatenate([a_ref[0, 0, rows, :], a_ref[0, 1, rows, :]], axis=0)
        s = _dot(f_ref[...], blk)
        sr, si = s[:n2], s[n2:]
        kr, ki = k_ref[0, rows, :], k_ref[1, rows, :]
        y = jnp.concatenate([sr * kr - si * ki, sr * ki + si * kr], axis=0)
        bb = _dot(g_ref[...], y)
        o_ref[0, 0, rows, :] = bb[:n2]
        o_ref[0, 1, rows, :] = bb[n2:]


def fft_mid(a, kf, f_fwd, f_inv, *, n1, n2):
    p = a.shape[0]
    c = MIX_CH
    kc = min(n1, MID_CHUNK)
    rows = kc * n2
    blk = pl.BlockSpec((1, 2, rows, c), lambda j, pp: (pp, 0, j, 0))
    mat = pl.BlockSpec(f_fwd.shape, lambda j, pp: (0, 0))
    return pl.pallas_call(
        functools.partial(_mid_body, n2=n2, kc=kc),
        grid=(n1 // kc, p),
        in_specs=[blk, pl.BlockSpec((2, rows, c), lambda j, pp: (0, j, 0)), mat, mat],
        out_specs=blk,
        out_shape=jax.ShapeDtypeStruct(a.shape, F32),
        compiler_params=_params("arbitrary", "arbitrary"),
        name="fft_mid",
    )(a, kf, f_fwd, f_inv)


def _filter_mid_body(a_ref, f_ref, sum_ref, o_ref, *, n2, kc):
    inv = 1.0 / (sum_ref[0:1, :] + EPS)
    for j in range(kc):
        rows = slice(j * n2, (j + 1) * n2)
        blk = jnp.concatenate([a_ref[0, 0, rows, :], a_ref[0, 1, rows, :]], axis=0)
        s = _dot(f_ref[...], blk) * inv
        o_ref[0, rows, :] = s[:n2]
        o_ref[1, rows, :] = s[n2:]


def filter_mid(a, f_fwd, abs_sum, order, *, n1, n2):
    c = MIX_CH
    kc = min(n1, MID_CHUNK)
    rows = kc * n2
    return pl.pallas_call(
        functools.partial(_filter_mid_body, n2=n2, kc=kc),
        grid=(n1 // kc,),
        in_specs=[pl.BlockSpec((1, 2, rows, c), lambda j: (0, 0, j, 0)),
                  pl.BlockSpec(f_fwd.shape, lambda j: (0, 0)),
                  pl.BlockSpec((SUBLANES, c), lambda j: (0, order))],
        out_specs=pl.BlockSpec((2, rows, c), lambda j: (0, j, 0)),
        out_shape=jax.ShapeDtypeStruct((2, n1 * n2, c), F32),
        compiler_params=_params("arbitrary"),
        name="filter_mid",
    )(a, f_fwd, abs_sum)


def _lanes(t, width):
    return t if width == LANES else jnp.concatenate([t] * (width // LANES), axis=-1)


def _kron1_body(zr_ref, zi_ref, m_ref, c_ref, s_ref, o_ref, *, n1, h, gs):
    j, c = FFT_GROUP, MIX_CH
    for s in range(gs):
        xr = zr_ref[0, :, s].reshape(h * j, c)
        xi = zi_ref[0, :, s].reshape(h * j, c)
        a = _dot(m_ref[...], jnp.concatenate([xr, xi], axis=0))
        ar = a[:n1 * j].reshape(n1, j, c)
        ai = a[n1 * j:].reshape(n1, j, c)
        cs, sn = _lanes(c_ref[:, s], c), _lanes(s_ref[:, s], c)
        o_ref[0, 0, :, s] = ar * cs + ai * sn
        o_ref[0, 1, :, s] = ai * cs - ar * sn


def _group_step(n_groups):
    return min(n_groups, 4)


def kron_stage1(z5, col, n_pairs, imag_offset, mat, twc, tws, *, n1, h):
    g, j, c = z5.shape[2], FFT_GROUP, MIX_CH
    gs = _group_step(g)
    tw = pl.BlockSpec((n1, gs, j, LANES), lambda gi, p: (0, gi, 0, 0))
    return pl.pallas_call(
        functools.partial(_kron1_body, n1=n1, h=h, gs=gs),
        grid=(g // gs, n_pairs),
        in_specs=[pl.BlockSpec((1, h, gs, j, c), lambda gi, p: (p, 0, gi, 0, col)),
                  pl.BlockSpec((1, h, gs, j, c), lambda gi, p: (p + imag_offset, 0, gi, 0, col)),
                  pl.BlockSpec(mat.shape, lambda gi, p: (0, 0)), tw, tw],
        out_specs=pl.BlockSpec((1, 2, n1, gs, j, c), lambda gi, p: (p, 0, 0, gi, 0, 0)),
        out_shape=jax.ShapeDtypeStruct((n_pairs, 2, n1, g, j, c), F32),
        compiler_params=_params("arbitrary", "arbitrary"),
        name="fft_kron1",
    )(z5, z5, mat, twc, tws)


def _kron3_body(b_ref, m_ref, c_ref, s_ref, zr_ref, zi_ref, gr_ref, gi_ref, d_ref, o_ref, *, n1, h, gs):
    j, c = FFT_GROUP, MIX_CH
    d = d_ref[...].reshape(1, 1, c)
    for s in range(gs):
        br, bi = b_ref[0, 0, :, s], b_ref[0, 1, :, s]
        cs, sn = _lanes(c_ref[:, s], c), _lanes(s_ref[:, s], c)
        xr = (br * cs - bi * sn).reshape(n1 * j, c)
        xi = (br * sn + bi * cs).reshape(n1 * j, c)
        y = _dot(m_ref[...], jnp.concatenate([xr, xi], axis=0))
        yr = y[:h * j].reshape(h, j, c)
        yi = y[h * j:].reshape(h, j, c)
        o_ref[0, 0, :, s] = gr_ref[0, :, s] * (yr + d * zr_ref[0, :, s])
        o_ref[1, 0, :, s] = gi_ref[0, :, s] * (yi + d * zi_ref[0, :, s])


def kron_stage3(bw, mat, twc, tws, z5, z_col, g5, g_col, d_vec, imag_offset, *, n1, h):
    p = bw.shape[0]
    g, j, c = bw.shape[3], FFT_GROUP, MIX_CH
    gs = _group_step(g)
    tw = pl.BlockSpec((n1, gs, j, LANES), lambda gi, pp: (0, gi, 0, 0))

    def src(col, off):
        return pl.BlockSpec((1, h, gs, j, c), lambda gi, pp: (pp + off, 0, gi, 0, col))

    return pl.pallas_call(
        functools.partial(_kron3_body, n1=n1, h=h, gs=gs),
        grid=(g // gs, p),
        in_specs=[pl.BlockSpec((1, 2, n1, gs, j, c), lambda gi, pp: (pp, 0, 0, gi, 0, 0)),
                  pl.BlockSpec(mat.shape, lambda gi, pp: (0, 0)), tw, tw,
                  src(z_col, 0), src(z_col, imag_offset), src(g_col, 0), src(g_col, imag_offset),
                  pl.BlockSpec((1, c), lambda gi, pp: (0, 0))],
        out_specs=pl.BlockSpec((2, 1, h, gs, j, c), lambda gi, pp: (0, pp, 0, gi, 0, 0)),
        out_shape=jax.ShapeDtypeStruct((2, p, h, g, j, c), F32),
        compiler_params=_params("arbitrary", "arbitrary"),
        name="fft_kron3",
    )(bw, mat, twc, tws, z5, z5, g5, g5, d_vec.reshape(1, c))


FILTER_TILE = 512


def _lane_dense(fn, a):
    r, w = a.shape
    f = LANES // w
    rows = r // f
    dense = jnp.concatenate([a[q * rows:(q + 1) * rows] for q in range(f)], axis=1)
    out = fn(dense)
    return jnp.concatenate([out[:, q * w:(q + 1) * w] for q in range(f)], axis=0)


def _filter_body(fw1_ref, fb1_ref, fr1_ref, fw2_ref, fb2_ref, fr2_ref, fw3_ref, fb3_ref, bands_ref, decay_ref,
                 k_ref, sum_ref, *, seq):
    i = pl.program_id(0)
    c = MIX_CH
    hp = lax.Precision.HIGHEST
    n = i * FILTER_TILE + lax.broadcasted_iota(jnp.int32, (FILTER_TILE, 1), 0)
    pos = jnp.where(n <= seq, n, 2 * seq - n).astype(F32)
    t = pos * (1.0 / (seq - 1))
    ang = (pos * (2.0 * math.pi / seq)) * bands_ref[...]
    nb = bands_ref.shape[1]
    pre = (t * fw1_ref[0:1, :]
           + jnp.dot(_lane_dense(jnp.cos, ang), fw1_ref[1:1 + nb, :], preferred_element_type=F32, precision=hp)
           - jnp.dot(_lane_dense(jnp.sin, ang), fw1_ref[1 + nb:1 + 2 * nb, :], preferred_element_type=F32,
                     precision=hp)
           + fb1_ref[...])
    h = _lane_dense(jnp.sin, fr1_ref[...] * pre)
    h = _lane_dense(jnp.sin, fr2_ref[...] * (jnp.dot(h, fw2_ref[...], preferred_element_type=F32, precision=hp)
                                             + fb2_ref[...]))
    h = jnp.dot(h, fw3_ref[...], preferred_element_type=F32, precision=hp) + fb3_ref[...]
    win = jnp.exp(-t * decay_ref[...]) + HYENA_SHIFT
    win = jnp.concatenate([win, win], axis=1)
    fwd, bwd = h[:, :2 * c], h[:, 2 * c:]
    k = jnp.where(n < seq, fwd, bwd) + jnp.where(n == 0, bwd, 0.0)
    k = jnp.where(n == seq, 0.0, k) * win
    k_ref[...] = k

    @pl.when(i == 0)
    def _():
        sum_ref[...] = jnp.zeros(sum_ref.shape, F32)

    sum_ref[...] = sum_ref[...] + jnp.sum(jnp.abs(k), axis=0, keepdims=True)


def hyena_filter_time(seq, fw1, fb1, fr1, fw2, fb2, fr2, fw3, fb3):
    c = MIX_CH
    n_bands = (fw1.shape[0] - 1) // 2
    bands = jnp.linspace(1e-4, n_bands - 1, n_bands, dtype=F32)[None, :]
    max_decay = math.log(HYENA_DECAY_TARGET) / HYENA_FAST_PCT
    min_decay = math.log(HYENA_DECAY_TARGET) / HYENA_SLOW_PCT
    decay = jnp.abs(jnp.linspace(min_decay, max_decay, c, dtype=F32))[None, :]
    assert (2 * seq) % FILTER_TILE == 0
    args = [fw1, fb1[None, :], fr1[None, :], fw2, fb2[None, :], fr2[None, :], fw3, fb3[None, :], bands, decay]
    return pl.pallas_call(
        functools.partial(_filter_body, seq=seq),
        grid=(2 * seq // FILTER_TILE,),
        in_specs=[pl.BlockSpec(a.shape, lambda i: (0, 0)) for a in args],
        out_specs=[pl.BlockSpec((FILTER_TILE, 2 * c), lambda i: (i, 0)),
                   pl.BlockSpec((SUBLANES, 2 * c), lambda i: (0, 0))],
        out_shape=[jax.ShapeDtypeStruct((2 * seq, 2 * c), F32), jax.ShapeDtypeStruct((SUBLANES, 2 * c), F32)],
        compiler_params=_params("arbitrary"),
        name="hyena_filter",
    )(*args)


def hyena_filter_spectrum(seq, filt, tabs):
    n, n1, n2 = _fft_sizes(seq)
    k, abs_sum = hyena_filter_time(seq, *filt)
    twc, tws, f_fwd, m_real = tabs[2], tabs[3], tabs[4], tabs[6]
    return [filter_mid(filter_stage1(k, o, m_real, twc, tws, n1=n1, n2=n2), f_fwd, abs_sum, o, n1=n1, n2=n2)
            for o in range(2)]


def hyena_mixer(ub, short_w, short_b, hy_d, kf, tabs):
    b, l, _ = ub.shape
    n, n1, n2 = _fft_sizes(l)
    h = n1 // 2
    p = b // 2
    c = MIX_CH
    g = n2 // FFT_GROUP
    u = hyena_short_conv(ub, short_w, short_b)
    u5 = u.reshape(b, h, g, FFT_GROUP, u.shape[-1])
    m_fwd, m_inv, twc, tws, f_fwd, f_inv = tabs[:6]
    z5 = u5
    for o in range(2):
        a = kron_stage1(z5, 0, p, p, m_fwd, twc, tws, n1=n1, h=h)
        bw = fft_mid(a.reshape(p, 2, n, c), kf[o], f_fwd, f_inv, n1=n1, n2=n2)
        z = kron_stage3(bw.reshape(a.shape), m_inv, twc, tws, z5, 0, u5, 1 + o, hy_d[o], p, n1=n1, h=h)
        z5 = z.reshape(b, h, g, FFT_GROUP, c)
    return z5.reshape(b, l, c)


def hyena_tables(seq):
    n, n1, n2 = _fft_sizes(seq)
    m_fwd, m_inv = _kron_matrices(n, n1, n1 // 2)
    m_real = _kron_matrices(n, n1, n1)[0][:, :n1 * FFT_GROUP]
    twc, tws = _twiddle_tables(n, n1, n2)
    f_fwd, f_inv = _mid_tables(n2)
    bf = lambda a: jnp.asarray(a).astype(BF16)
    return bf(m_fwd), bf(m_inv), twc, tws, bf(f_fwd), bf(f_inv), bf(m_real)


def _fnet_body(cl_ref, sl_ref, x_ref, cc_ref, sc_ref, o_ref, *, scale):
    x = x_ref[0].astype(BF16)
    pr = jnp.dot(cl_ref[...], x, preferred_element_type=F32).astype(BF16)
    qr = jnp.dot(sl_ref[...], x, preferred_element_type=F32).astype(BF16)
    o_ref[0] = (jnp.dot(pr, cc_ref[...], preferred_element_type=F32)
                - jnp.dot(qr, sc_ref[...], preferred_element_type=F32)) * scale


def _dft_tables(n):
    a = jnp.arange(n, dtype=jnp.int32)
    th = ((a[:, None] * a[None, :]) % n).astype(F32) * (2.0 * math.pi / n)
    return jnp.cos(th).astype(BF16), jnp.sin(th).astype(BF16)


FNET_DIRECT_MAX = 1024


def _fnet1_body(x_ref, ccs_ref, m_ref, c_ref, s_ref, o_ref, *, n1, gs):
    j, c = FFT_GROUP, MIX_CH
    for s in range(gs):
        x = x_ref[0, :, s].reshape(n1 * j, c).astype(BF16)
        z = jnp.dot(x, ccs_ref[...], preferred_element_type=F32)
        a = _dot(m_ref[...], jnp.concatenate([z[:, :c], z[:, c:]], axis=0))
        ar = a[:n1 * j].reshape(n1, j, c)
        ai = a[n1 * j:].reshape(n1, j, c)
        cs, sn = _lanes(c_ref[:, s], c), _lanes(s_ref[:, s], c)
        o_ref[0, 0, :, s] = ar * cs + ai * sn
        o_ref[0, 1, :, s] = ai * cs - ar * sn


def _fnet2_body(a_ref, m_ref, o_ref, *, n2, scale):
    j, c = FFT_GROUP, MIX_CH
    x = a_ref[0].reshape(2 * j * n2, c)
    y = _dot(m_ref[...], x) * scale
    o_ref[0, :, 0] = y.reshape(n2, j, c)


def fnet_tables(seq):
    c = MIX_CH
    cc, sc = _dft_tables(c)
    if seq <= FNET_DIRECT_MAX:
        return _dft_tables(seq) + (cc, sc)
    n2 = 128
    n1 = seq // n2
    m_fwd, _ = _kron_matrices(seq, n1, n1)
    twc, tws = _twiddle_tables(seq, n1, n2)
    a = np.arange(n2)
    th = 2.0 * np.pi * ((a[:, None] * a[None, :]) % n2) / n2
    eye = np.eye(FFT_GROUP)
    m2 = np.concatenate([np.einsum('kn,ij->kijn', f, eye).reshape(n2 * FFT_GROUP, FFT_GROUP * n2)
                         for f in (np.cos(th), np.sin(th))], axis=1).astype(np.float32)
    ccs = jnp.concatenate([cc, -sc], axis=1)
    return ccs, jnp.asarray(m_fwd).astype(BF16), twc, tws, jnp.asarray(m2).astype(BF16)


def fnet_two_stage(uc, tables):
    b, l, c = uc.shape
    ccs, m_fwd, twc, tws, m2 = tables
    j = FFT_GROUP
    n2 = 128
    n1 = l // n2
    g = n2 // j
    gs = _group_step(g)
    tw = pl.BlockSpec((n1, gs, j, LANES), lambda gi, bb: (0, gi, 0, 0))
    a = pl.pallas_call(
        functools.partial(_fnet1_body, n1=n1, gs=gs),
        grid=(g // gs, b),
        in_specs=[pl.BlockSpec((1, n1, gs, j, c), lambda gi, bb: (bb, 0, gi, 0, 0)),
                  pl.BlockSpec(ccs.shape, lambda gi, bb: (0, 0)),
                  pl.BlockSpec(m_fwd.shape, lambda gi, bb: (0, 0)), tw, tw],
        out_specs=pl.BlockSpec((1, 2, n1, gs, j, c), lambda gi, bb: (bb, 0, 0, gi, 0, 0)),
        out_shape=jax.ShapeDtypeStruct((b, 2, n1, g, j, c), F32),
        compiler_params=_params("arbitrary", "arbitrary"),
        name="fnet_stage1",
    )(uc.reshape(b, n1, g, j, c), ccs, m_fwd, twc, tws)
    out = pl.pallas_call(
        functools.partial(_fnet2_body, n2=n2, scale=1.0 / math.sqrt(l * c)),
        grid=(b, n1 // j),
        in_specs=[pl.BlockSpec((1, 2, j, n2, c), lambda bb, q: (bb, 0, q, 0, 0)),
                  pl.BlockSpec(m2.shape, lambda bb, q: (0, 0))],
        out_specs=pl.BlockSpec((1, n2, 1, j, c), lambda bb, q: (bb, 0, q, 0, 0)),
        out_shape=jax.ShapeDtypeStruct((b, n2, n1 // j, j, c), F32),
        compiler_params=_params("arbitrary", "arbitrary"),
        name="fnet_stage2",
    )(a.reshape(b, 2, n1, n2, c), m2)
    return out.reshape(b, l, c)


def fnet_mixer(uc, tables):
    b, l, c = uc.shape
    if l > FNET_DIRECT_MAX:
        return fnet_two_stage(uc, tables)
    cl, sl, cc, sc = tables
    tm = min(l, 512)
    row = pl.BlockSpec((tm, l), lambda i, bb: (i, 0))
    sq = pl.BlockSpec((c, c), lambda i, bb: (0, 0))
    return pl.pallas_call(
        functools.partial(_fnet_body, scale=1.0 / math.sqrt(l * c)),
        grid=(l // tm, b),
        in_specs=[row, row, pl.BlockSpec((1, l, c), lambda i, bb: (bb, 0, 0)), sq, sq],
        out_specs=pl.BlockSpec((1, tm, c), lambda i, bb: (bb, i, 0)),
        out_shape=jax.ShapeDtypeStruct((b, l, c), F32),
        compiler_params=_params("arbitrary", "arbitrary"),
        name="fnet",
    )(cl, sl, uc, cc, sc)


def _heads_rows(x, g):
    h0 = Q_PER_KV * g
    return jnp.concatenate([x[:, (h0 + r) * HEAD_DIM:(h0 + r + 1) * HEAD_DIM] for r in range(Q_PER_KV)], axis=0)


def _qk(q, k):
    return lax.dot_general(q.astype(BF16), k.astype(BF16), (((1,), (1,)), ((), ())),
                           preferred_element_type=F32)


def _sink_col(sink_ref, g, rows):
    ridx = lax.broadcasted_iota(jnp.int32, (Q_PER_KV * rows, 1), 0)
    col = jnp.full((Q_PER_KV * rows, 1), sink_ref[Q_PER_KV * g], F32)
    for r in range(1, Q_PER_KV):
        col = jnp.where(ridx >= r * rows, sink_ref[Q_PER_KV * g + r], col)
    return col * LOG2E


def _lat_attn_body(sink_ref, q_ref, qr_ref, kp_ref, kc_ref, kn_ref, ck_ref, cv_ref, o_ref, *, sub):
    i = pl.program_id(1)
    n_qblk = pl.num_programs(1) * sub
    blk = ATT_BLOCK
    span = blk + 2 * WINDOW
    kv = jnp.concatenate([kp_ref[0], kc_ref[0], kn_ref[0]], axis=0)
    ck = ck_ref[0, 0].astype(BF16)
    cv = cv_ref[0, 0].astype(BF16)
    kvw = N_KV_HEADS * HEAD_DIM
    r = lax.broadcasted_iota(jnp.int32, (Q_PER_KV * blk, span), 0) % blk
    j = lax.broadcasted_iota(jnp.int32, (Q_PER_KV * blk, span), 1)
    band = (j >= r) & (j <= r + 2 * WINDOW)
    for s in range(sub):
        qi = i * sub + s
        ok = band & ((qi > 0) | (j >= WINDOW)) & ((qi < n_qblk - 1) | (j < WINDOW + blk))
        q = q_ref[0, s * blk:(s + 1) * blk, :]
        qr = qr_ref[0, s * blk:(s + 1) * blk, :]
        outs = []
        for g in range(N_KV_HEADS):
            kl = kv[s * blk:s * blk + span, g * HEAD_DIM:(g + 1) * HEAD_DIM]
            vl = kv[s * blk:s * blk + span, kvw + g * HEAD_DIM:kvw + (g + 1) * HEAD_DIM]
            s_loc = jnp.where(ok, _qk(_heads_rows(qr, g), kl), NEG_INF)
            s_ctx = _qk(_heads_rows(q, g), ck[:, g * HEAD_DIM:(g + 1) * HEAD_DIM])
            sink = _sink_col(sink_ref, g, blk)
            m = jnp.maximum(jnp.maximum(jnp.max(s_loc, axis=-1, keepdims=True),
                                        jnp.max(s_ctx, axis=-1, keepdims=True)), sink)
            e_loc = jnp.exp2(s_loc - m)
            e_ctx = jnp.exp2(s_ctx - m)
            den = (jnp.sum(e_loc, axis=-1, keepdims=True) + jnp.sum(e_ctx, axis=-1, keepdims=True)
                   + jnp.exp2(sink - m))
            o = (jnp.dot(e_loc.astype(BF16), vl, preferred_element_type=F32)
                 + jnp.dot(e_ctx.astype(BF16), cv[:, g * HEAD_DIM:(g + 1) * HEAD_DIM],
                           preferred_element_type=F32)) * (1.0 / den)
            outs += [o[rr * blk:(rr + 1) * blk] for rr in range(Q_PER_KV)]
        o_ref[0, s * blk:(s + 1) * blk, :] = jnp.concatenate(outs, axis=1).astype(o_ref.dtype)


ATT_SUB = 4


def latent_attention(uq, uqr, ukv, cache_k, cache_v, layer, sink):
    b, l, c = uq.shape
    p = cache_k.shape[2]
    blk = ATT_BLOCK
    nblk = l // blk
    sub = math.gcd(ATT_SUB, nblk)
    rows = sub * blk
    qspec = pl.BlockSpec((1, rows, c), lambda bb, i: (bb, i, 0))
    cspec = pl.BlockSpec((1, 1, p, cache_k.shape[3]), lambda bb, i: (bb, layer, 0, 0))
    return pl.pallas_call(
        functools.partial(_lat_attn_body, sub=sub),
        grid=(b, nblk // sub),
        in_specs=[pl.BlockSpec(memory_space=pltpu.SMEM), qspec, qspec,
                  pl.BlockSpec((1, blk, c), lambda bb, i: (bb, jnp.maximum(i * sub - 1, 0), 0)),
                  qspec,
                  pl.BlockSpec((1, blk, c), lambda bb, i: (bb, jnp.minimum((i + 1) * sub, nblk - 1), 0)),
                  cspec, cspec],
        out_specs=qspec,
        out_shape=jax.ShapeDtypeStruct((b, l, c), BF16),
        compiler_params=_params("arbitrary", "arbitrary"),
        name="latent_attention",
    )(sink, uq, uqr, ukv, ukv, ukv, cache_k, cache_v)


def _ctx_attn_body(sink_ref, q_ref, kv_ref, o_ref, *, seq):
    q = q_ref[0]
    kv = kv_ref[0]
    kvw = N_KV_HEADS * HEAD_DIM
    outs = []
    for g in range(N_KV_HEADS):
        kl = kv[:, g * HEAD_DIM:(g + 1) * HEAD_DIM]
        vl = kv[:, kvw + g * HEAD_DIM:kvw + (g + 1) * HEAD_DIM]
        s = _qk(_heads_rows(q, g), kl)
        sink = _sink_col(sink_ref, g, seq)
        m = jnp.maximum(jnp.max(s, axis=-1, keepdims=True), sink)
        e = jnp.exp2(s - m)
        den = jnp.sum(e, axis=-1, keepdims=True) + jnp.exp2(sink - m)
        o = jnp.dot(e.astype(BF16), vl.astype(BF16), preferred_element_type=F32) * (1.0 / den)
        outs += [o[rr * seq:(rr + 1) * seq] for rr in range(Q_PER_KV)]
    o_ref[0] = jnp.concatenate(outs, axis=1)


def context_attention(uq, ukv, sink):
    b, s, c = uq.shape
    spec = pl.BlockSpec((1, s, c), lambda bb: (bb, 0, 0))
    return pl.pallas_call(
        functools.partial(_ctx_attn_body, seq=s),
        grid=(b,),
        in_specs=[pl.BlockSpec(memory_space=pltpu.SMEM), spec, spec],
        out_specs=spec,
        out_shape=jax.ShapeDtypeStruct((b, s, c), F32),
        compiler_params=_params("arbitrary"),
        name="context_attention",
    )(sink, uq, ukv)


def _pack_bf16_pairs(hi_rounded):
    k = hi_rounded.shape[1] // 2
    bits = lax.bitcast_convert_type(hi_rounded, jnp.uint32)
    return bits[:, :k] | (bits[:, k:] >> 16)


def _unpack_bf16_pairs(packed, dtype=BF16):
    a = lax.bitcast_convert_type(packed & jnp.uint32(0xFFFF0000), F32)
    b = lax.bitcast_convert_type(packed << 16, F32)
    return jnp.concatenate([a, b], axis=1).astype(dtype)


def _out_body(ya_ref, yb_ref, yc_ref, yd_ref, x_ref, mod_ref, g_ref, w_ref, rw_ref, rb_ref,
              x1_ref, h_ref, route_ref, cnt_ref):
    c = MIX_CH
    y = jnp.dot(ya_ref[0].astype(BF16), w_ref[0:c, :], preferred_element_type=F32)
    for j, ref in enumerate((yb_ref, yc_ref, yd_ref), start=1):
        y = y + jnp.dot(ref[0].astype(BF16), w_ref[j * c:(j + 1) * c, :], preferred_element_type=F32)
    x1 = x_ref[0] + mod_ref[0, 2:3, :] * y
    x1_ref[0] = x1
    h = _rmsnorm_mod(x1, g_ref[...], mod_ref[0, 4:5, :], mod_ref[0, 3:4, :])
    h_hi = h.astype(BF16)
    h_hi32 = h_hi.astype(F32)
    h_ref[0] = _pack_bf16_pairs(h_hi32)
    h_lo = (h - h_hi32).astype(BF16)
    tm = h.shape[0]
    prod = jnp.dot(jnp.concatenate([h_hi, h_lo], axis=0), rw_ref[...], preferred_element_type=F32)
    logits = (prod[:tm, :ROUTE_LANES] + prod[:tm, ROUTE_LANES:]
              + prod[tm:, :ROUTE_LANES] + prod[tm:, ROUTE_LANES:]) + rb_ref[...]
    lane = lax.broadcasted_iota(jnp.int32, logits.shape, 1)
    is_c = lane < N_GROUPS
    lc = jnp.where(is_c, logits, NEG_INF)
    mc = jnp.max(lc, axis=-1, keepdims=True)
    grp = jnp.min(jnp.where(lc == mc, lane, ROUTE_LANES), axis=-1, keepdims=True)
    pg = 1.0 / jnp.sum(jnp.where(is_c, jnp.exp(lc - mc), 0.0), axis=-1, keepdims=True)
    lo = N_GROUPS + grp * EXPERTS_PER_GROUP
    in_g = (lane >= lo) & (lane < lo + EXPERTS_PER_GROUP)
    lf = jnp.where(in_g, logits, NEG_INF)
    t1 = jnp.max(lf, axis=-1, keepdims=True)
    i1 = jnp.min(jnp.where(lf == t1, lane, ROUTE_LANES), axis=-1, keepdims=True)
    lf2 = jnp.where(lane == i1, NEG_INF, lf)
    t2 = jnp.max(lf2, axis=-1, keepdims=True)
    i2 = jnp.min(jnp.where(lf2 == t2, lane, ROUTE_LANES), axis=-1, keepdims=True)
    e2 = jnp.exp(t2 - t1)
    w1 = pg / (1.0 + e2)
    w2 = pg * e2 / (1.0 + e2)
    rec = jnp.where(lane == 0, (i1 - N_GROUPS).astype(F32),
                    jnp.where(lane == 1, (i2 - N_GROUPS).astype(F32),
                              jnp.where(lane == 2, w1, jnp.where(lane == 3, w2, 0.0))))
    route_ref[0] = rec

    @pl.when((pl.program_id(0) == 0) & (pl.program_id(1) == 0))
    def _():
        cnt_ref[...] = jnp.zeros(cnt_ref.shape, F32)

    e0, e1 = _choice_onehots(rec)
    cnt_ref[...] = cnt_ref[...] + jnp.sum(e0 + e1, axis=0, keepdims=True)


def out_projection(ys, x, mods, mod_row0, norm_g, w_out_bf, rw, rb, *, tm):
    b, l, d = x.shape
    c = MIX_CH
    row = (lambda bb: 0) if mod_row0 is None else (lambda bb: mod_row0 + bb)
    yspec = pl.BlockSpec((1, tm, c), lambda bb, i: (bb, i, 0))
    xspec = pl.BlockSpec((1, tm, d), lambda bb, i: (bb, i, 0))
    return pl.pallas_call(
        _out_body,
        grid=(b, l // tm),
        in_specs=[yspec] * 4 + [xspec,
                                pl.BlockSpec((1, N_MOD, d), lambda bb, i: (row(bb), 0, 0)),
                                pl.BlockSpec((1, d), lambda bb, i: (0, 0)),
                                pl.BlockSpec(w_out_bf.shape, lambda bb, i: (0, 0)),
                                pl.BlockSpec(rw.shape, lambda bb, i: (0, 0)),
                                pl.BlockSpec(rb.shape, lambda bb, i: (0, 0))],
        out_specs=[xspec, pl.BlockSpec((1, tm, d // 2), lambda bb, i: (bb, i, 0)),
                   pl.BlockSpec((1, tm, ROUTE_LANES), lambda bb, i: (bb, i, 0)),
                   pl.BlockSpec((SUBLANES, ROUTE_LANES), lambda bb, i: (0, 0))],
        out_shape=[jax.ShapeDtypeStruct((b, l, d), F32), jax.ShapeDtypeStruct((b, l, d // 2), jnp.uint32),
                   jax.ShapeDtypeStruct((b, l, ROUTE_LANES), F32),
                   jax.ShapeDtypeStruct((SUBLANES, ROUTE_LANES), F32)],
        compiler_params=_params("arbitrary", "arbitrary"),
        name="out_proj",
    )(*ys, x, mods, norm_g.reshape(1, d), w_out_bf, rw, rb)


def _expert_body(be_ref, nv_ref, xs_ref, wg_ref, wu_ref, wd_ref, o_ref, wg_s, wu_s, wd_s):
    i = pl.program_id(0)
    prev = be_ref[jnp.maximum(i - 1, 0)]

    @pl.when((i == 0) | (be_ref[i] != prev))
    def _():
        wg_s[...] = wg_ref[0, 0].astype(BF16)
        wu_s[...] = wu_ref[0, 0].astype(BF16)
        wd_s[...] = wd_ref[0, 0].astype(BF16)

    @pl.when(nv_ref[i] > 0)
    def _():
        row = lax.broadcasted_iota(jnp.int32, xs_ref.shape, 0)
        x = _unpack_bf16_pairs(jnp.where(row < nv_ref[i], xs_ref[...], jnp.uint32(0)))
        g = jnp.dot(x, wg_s[...], preferred_element_type=F32)
        u = jnp.dot(x, wu_s[...], preferred_element_type=F32)
        a = (_silu(g) * u).astype(BF16)
        y = jnp.dot(a, wd_s[...], preferred_element_type=F32)
        o_ref[...] = _pack_bf16_pairs(y.astype(BF16).astype(F32))

    @pl.when(nv_ref[i] <= 0)
    def _():
        o_ref[...] = jnp.zeros(o_ref.shape, jnp.uint32)


def expert_ffn(xs, blk_e, n_valid, layer, e_gate, e_up, e_down):
    rows, dh = xs.shape
    d = 2 * dh
    nb = rows // MOE_BLOCK
    de = e_gate.shape[-1]
    grid_spec = pltpu.PrefetchScalarGridSpec(
        num_scalar_prefetch=2,
        grid=(nb,),
        in_specs=[pl.BlockSpec((MOE_BLOCK, dh), lambda i, be, nv: (i, 0)),
                  pl.BlockSpec((1, 1, d, de), lambda i, be, nv: (layer, be[i], 0, 0)),
                  pl.BlockSpec((1, 1, d, de), lambda i, be, nv: (layer, be[i], 0, 0)),
                  pl.BlockSpec((1, 1, de, d), lambda i, be, nv: (layer, be[i], 0, 0))],
        out_specs=pl.BlockSpec((MOE_BLOCK, dh), lambda i, be, nv: (i, 0)),
        scratch_shapes=[pltpu.VMEM((d, de), BF16), pltpu.VMEM((d, de), BF16), pltpu.VMEM((de, d), BF16)],
    )
    return pl.pallas_call(
        _expert_body, grid_spec=grid_spec,
        out_shape=jax.ShapeDtypeStruct((rows, dh), jnp.uint32),
        compiler_params=_params("arbitrary"),
        name="expert_ffn",
    )(blk_e, n_valid, xs, e_gate, e_up, e_down)


RANK_TILE = 512


def _choice_onehots(rec):
    lanef = lax.broadcasted_iota(jnp.int32, rec.shape, 1).astype(F32)
    return (lanef == rec[:, 0:1]).astype(F32), (lanef == rec[:, 1:2]).astype(F32)


def _slot_body(route_ref, cnt_ref, tri_ref, slot_ref, carry_ref):
    rec = route_ref[...]
    lane = lax.broadcasted_iota(jnp.int32, rec.shape, 1)
    e0, e1 = _choice_onehots(rec)
    both = e0 + e1

    @pl.when(pl.program_id(0) == 0)
    def _():
        cnt = cnt_ref[...]
        padded = jnp.floor((cnt + (MOE_BLOCK - 1)) * (1.0 / MOE_BLOCK)) * MOE_BLOCK
        ln = lax.broadcasted_iota(jnp.int32, cnt.shape, 1)
        incl = padded
        sh = 1
        while sh < ROUTE_LANES:
            incl = incl + jnp.where(ln >= sh, pltpu.roll(incl, sh, axis=1), 0.0)
            sh *= 2
        carry_ref[...] = incl - padded

    before = jnp.dot(tri_ref[...], both.astype(BF16), preferred_element_type=F32) + carry_ref[0:1, :]
    s0 = jnp.sum(e0 * before, axis=-1, keepdims=True)
    s1 = jnp.sum(e1 * before, axis=-1, keepdims=True)
    slot_ref[...] = jnp.where(lane == 0, s0, jnp.where(lane == 1, s1, 0.0))
    carry_ref[...] = carry_ref[...] + jnp.sum(both, axis=0, keepdims=True)


def moe_slots(route, counts):
    n = route.shape[0]
    t = RANK_TILE
    tri = jnp.asarray(np.tril(np.ones((t, t), np.float32), -1)).astype(BF16)
    return pl.pallas_call(
        _slot_body,
        grid=(n // t,),
        in_specs=[pl.BlockSpec((t, ROUTE_LANES), lambda i: (i, 0)),
                  pl.BlockSpec((SUBLANES, ROUTE_LANES), lambda i: (0, 0)),
                  pl.BlockSpec((t, t), lambda i: (0, 0))],
        out_specs=pl.BlockSpec((t, ROUTE_LANES), lambda i: (i, 0)),
        out_shape=jax.ShapeDtypeStruct((n, ROUTE_LANES), F32),
        scratch_shapes=[pltpu.VMEM((SUBLANES, ROUTE_LANES), F32)],
        compiler_params=_params("arbitrary"),
        name="moe_slots",
    )(route, counts, tri)


def _sc_mesh():
    return plsc.VectorSubcoreMesh(core_axis_name="c", subcore_axis_name="s")


def _sc_worker():
    return lax.axis_index("s") * SC_CORES + lax.axis_index("c")


DISPATCH_ROWS = 64
COMBINE_ROWS = 64


def sc_dispatch(rows, dest, n_slots):
    n, w = rows.shape
    ch = DISPATCH_ROWS
    per_w = n // SC_WORKERS
    n_ch = per_w // ch

    @functools.partial(
        pl.kernel, mesh=_sc_mesh(),
        out_type=jax.ShapeDtypeStruct((n_slots, w), rows.dtype),
        scratch_types=[pltpu.VMEM((ch,), jnp.int32), pltpu.VMEM((ch, w), rows.dtype)],
    )
    def scatter_kernel(rows_hbm, dest_hbm, out_hbm, idx_v, rows_v):
        wid = _sc_worker()

        @pl.loop(0, n_ch)
        def _(j):
            chunk = wid * n_ch + j
            pltpu.sync_copy(rows_hbm.at[pl.ds(pl.multiple_of(chunk * ch, ch), ch)], rows_v)
            for k in range(2):
                pltpu.sync_copy(dest_hbm.at[k, chunk], idx_v)
                pltpu.sync_copy(rows_v, out_hbm.at[idx_v])

    return scatter_kernel(rows, dest)


def sc_gather_rows(table, idx):
    s, w = table.shape
    m = idx.shape[0]
    ch = COMBINE_ROWS
    per_w = m // SC_WORKERS
    n_ch = per_w // ch

    @functools.partial(
        pl.kernel, mesh=_sc_mesh(),
        out_type=jax.ShapeDtypeStruct((m, w), table.dtype),
        scratch_types=[pltpu.VMEM((ch,), jnp.int32), pltpu.VMEM((ch, w), table.dtype), pltpu.SemaphoreType.DMA],
    )
    def gather_kernel(table_hbm, idx_hbm, out_hbm, idx_v, rows_v, sem):
        wid = _sc_worker()

        @pl.loop(0, n_ch)
        def _(j):
            off = pl.multiple_of((wid * n_ch + j) * ch, ch)
            pltpu.sync_copy(idx_hbm.at[pl.ds(off, ch)], idx_v)
            pltpu.async_copy(table_hbm.at[idx_v], rows_v, sem).wait()
            pltpu.sync_copy(rows_v, out_hbm.at[pl.ds(off, ch)])

    return gather_kernel(table, idx)


def hier_moe(h_packed, route, cnt, layer, e_gate, e_up, e_down):
    b, l, dh = h_packed.shape
    n = b * l
    assert n % (SC_WORKERS * DISPATCH_ROWS) == 0 and (2 * n) % (SC_WORKERS * COMBINE_ROWS) == 0
    slots = moe_slots(route.reshape(n, ROUTE_LANES), cnt)
    counts = cnt[0, :N_EXPERTS].astype(jnp.int32)
    padded = (counts + MOE_BLOCK - 1) // MOE_BLOCK * MOE_BLOCK
    pend = jnp.cumsum(padded)
    nb = -(-2 * n // MOE_BLOCK) + N_EXPERTS
    blk0 = jnp.arange(nb, dtype=jnp.int32) * MOE_BLOCK
    owner = pend[None, :] <= blk0[:, None]
    blk_e = jnp.minimum(jnp.sum(owner, axis=1), N_EXPERTS - 1).astype(jnp.int32)
    run_end = jnp.sum(jnp.where(jnp.arange(N_EXPERTS)[None, :] == blk_e[:, None],
                                (pend - padded + counts)[None, :], 0), axis=1)
    n_valid = jnp.clip(run_end - blk0, 0, MOE_BLOCK).astype(jnp.int32)
    dest = slots[:, 0:2].astype(jnp.int32).T
    xs = sc_dispatch(h_packed.reshape(n, dh), dest.reshape(2, n // DISPATCH_ROWS, DISPATCH_ROWS), nb * MOE_BLOCK)
    y = expert_ffn(xs, blk_e, n_valid, layer, e_gate, e_up, e_down)
    return sc_gather_rows(y, dest.reshape(2 * n)).reshape(2, b, l, dh)


def _final_body(x_ref, y0_ref, y1_ref, route_ref, pmod_ref, g_ref, o_ref):
    x = _moe_residual(x_ref[0], y0_ref, y1_ref, route_ref, pmod_ref)
    ms = jnp.mean(x * x, axis=-1, keepdims=True)
    o_ref[0] = x * lax.rsqrt(ms + EPS) * g_ref[...]


def final_norm(x1, res, mod_row0, norm_g, *, tm):
    b, l, d = x1.shape
    row = (lambda bb: 0) if mod_row0 is None else (lambda bb: mod_row0 + bb)
    xspec = pl.BlockSpec((1, tm, d), lambda bb, i: (bb, i, 0))
    rargs, rspecs = _residual_specs(res, tm, d, row)
    return pl.pallas_call(
        _final_body,
        grid=(b, l // tm),
        in_specs=[xspec] + rspecs + [pl.BlockSpec((1, d), lambda bb, i: (0, 0))],
        out_specs=xspec,
        out_shape=jax.ShapeDtypeStruct((b, l, d), F32),
        compiler_params=_params("arbitrary", "arbitrary"),
        name="final_norm",
    )(x1, *rargs, norm_g.reshape(1, d))


def _rope_tables(seq):
    rows = seq // GRID_W
    row_pos = jnp.repeat(jnp.arange(rows, dtype=F32), GRID_W)
    col_pos = jnp.tile(jnp.arange(GRID_W, dtype=F32), rows)
    n_freq = HEAD_DIM // 4
    inv = ROPE_BASE ** (-jnp.arange(n_freq, dtype=F32) / n_freq)
    ang = jnp.concatenate([row_pos[:, None] * inv, col_pos[:, None] * inv], axis=-1)
    cs, sn = jnp.cos(ang), jnp.sin(ang)
    cos_f = jnp.tile(jnp.concatenate([cs, cs], axis=-1), (1, N_Q_HEADS))
    sin_s = jnp.tile(jnp.concatenate([-sn, sn], axis=-1), (1, N_Q_HEADS))
    return cos_f, sin_s


def kernel(x_prompt, x_sample, cache_k, cache_v, c, c_ctx, ada_w, ada_b, norm1_g, norm2_g, w_in, conv_dw_w, conv_dw_b, conv_ln_g, conv_ln_b, hy_short_w, hy_short_b, hy_fw1, hy_fb1, hy_freq1, hy_fw2, hy_fb2, hy_freq2, hy_fw3, hy_fb3, hy_d, attn_sink, w_out, router_coarse_w, router_coarse_b, router_fine_w, router_fine_b, exp_gate, exp_up, exp_down, norm_f_g):
    depth = ada_w.shape[0]
    bp, lp, d = x_prompt.shape
    bs, ls, _ = x_sample.shape
    assert bp % 2 == 0 and bs % 2 == 0 and ls % ATT_BLOCK == 0 and ls % GRID_W == 0

    n_rows = -(-(1 + bs) // SUBLANES) * SUBLANES
    cvec = jnp.concatenate([c_ctx[None, :], c, jnp.zeros((n_rows - 1 - bs, d), F32)], axis=0)
    mods = adaln_all(cvec, ada_w, ada_b)

    rope = _rope_tables(ls)
    fnet_tabs, hy_tabs = {}, {}
    for seq in {lp, ls}:
        fnet_tabs[seq] = fnet_tables(seq)
        hy_tabs[seq] = hyena_tables(seq)
    ck = cache_k.reshape(cache_k.shape[0], depth, cache_k.shape[2], -1)
    cv = cache_v.reshape(cache_v.shape[0], depth, cache_v.shape[2], -1)
    pad = ROUTE_LANES - N_GROUPS - N_EXPERTS

    tm_p = min(lp, 512)
    tm_s = min(ls, 1024)
    xp, xs = x_prompt, x_sample
    res_p = res_s = None
    ks_out, vs_out = [], []
    for l in range(depth):
        w_in_bf = w_in[l].astype(BF16)
        w_out_bf = w_out[l].astype(BF16)
        rw = jnp.concatenate([router_coarse_w[l], router_fine_w[l], jnp.zeros((d, pad), F32)], axis=1)
        rw_hi = rw.astype(BF16)
        rw = jnp.concatenate([rw_hi, (rw - rw_hi.astype(F32)).astype(BF16)], axis=1)
        rb = jnp.concatenate([router_coarse_b[l], router_fine_b[l], jnp.zeros((pad,), F32)])[None, :]
        filt = (hy_fw1[l], hy_fb1[l], hy_freq1[l], hy_fw2[l], hy_fb2[l], hy_freq2[l], hy_fw3[l], hy_fb3[l])
        sink = attn_sink[l]

        def mixers(ua, ub, uc, yd, seq):
            ya = conformer_conv(ua, conv_dw_w[l], conv_dw_b[l], conv_ln_g[l], conv_ln_b[l])
            yb = hyena_mixer(ub, hy_short_w[l], hy_short_b[l], hy_d[l], hyena_filter_spectrum(seq, filt, hy_tabs[seq]),
                             hy_tabs[seq])
            yc = fnet_mixer(uc, fnet_tabs[seq])
            return (ya, yb, yc, yd)

        outs = in_projection(xp, mods[l], None, norm1_g[l], w_in_bf, res=res_p, tm=tm_p)
        if res_p is not None:
            xp, outs = outs[0], outs[1:]
        ua, ub, uc, uq, ukv = outs
        kvw = N_KV_HEADS * HEAD_DIM
        ks_out.append(ukv[..., :kvw].reshape(bp, lp, N_KV_HEADS, HEAD_DIM))
        vs_out.append(ukv[..., kvw:].reshape(bp, lp, N_KV_HEADS, HEAD_DIM))
        ys = mixers(ua, ub, uc, context_attention(uq, ukv, sink), lp)
        x1p, hp, route, cnt = out_projection(ys, xp, mods[l], None, norm2_g[l], w_out_bf, rw, rb, tm=tm_p)
        res_p = (hier_moe(hp, route, cnt, l, exp_gate, exp_up, exp_down), route, mods[l])
        xp = x1p

        outs = in_projection(xs, mods[l], 1, norm1_g[l], w_in_bf, res=res_s, rope=rope, tm=tm_s)
        if res_s is not None:
            xs, outs = outs[0], outs[1:]
        ua, ub, uc, uq, uqr, ukv = outs
        ys = mixers(ua, ub, uc, latent_attention(uq, uqr, ukv, ck, cv, l, sink), ls)
        x1s, hs, route, cnt = out_projection(ys, xs, mods[l], 1, norm2_g[l], w_out_bf, rw, rb, tm=tm_s)
        res_s = (hier_moe(hs, route, cnt, l, exp_gate, exp_up, exp_down), route, mods[l])
        xs = x1s

    y_prompt = final_norm(xp, res_p, None, norm_f_g, tm=tm_p)
    y_sample = final_norm(xs, res_s, 1, norm_f_g, tm=tm_s)
    return (y_prompt, y_sample, jnp.stack(ks_out, axis=1), jnp.stack(vs_out, axis=1))
```

```python
import functools
import math

import numpy as np
import jax
import jax.numpy as jnp
from jax import lax
from jax.experimental import pallas as pl
from jax.experimental.pallas import tpu as pltpu
from jax.experimental.pallas import tpu_sc as plsc

F32 = jnp.float32
BF16 = jnp.bfloat16

HEAD_DIM = 64
LOG2E = math.log2(math.e)
ATT_SCALE = HEAD_DIM ** -0.5 * LOG2E
N_Q_HEADS = 4
N_KV_HEADS = 2
Q_PER_KV = N_Q_HEADS // N_KV_HEADS
WINDOW = 128
ATT_BLOCK = 128
GRID_W = 64
ROPE_BASE = 10000.0
N_GROUPS = 4
EXPERTS_PER_GROUP = 8
N_EXPERTS = N_GROUPS * EXPERTS_PER_GROUP
MOE_BLOCK = 512
N_MOD = 6
EPS = 1e-6
NEG_INF = -1e30
HYENA_DECAY_TARGET = 1e-2
HYENA_FAST_PCT = 0.3
HYENA_SLOW_PCT = 1.5
HYENA_SHIFT = 0.05

LANES = 128
SUBLANES = 8
VMEM_LIMIT = 56 * 1024 * 1024

MIX_CH = 256
ROUTE_LANES = 128
SC_CORES = 2
SC_WORKERS = SC_CORES * 16


def _params(*sem):
    return pltpu.CompilerParams(dimension_semantics=sem, vmem_limit_bytes=VMEM_LIMIT)


def _silu(x):
    return x * jax.nn.sigmoid(x)


def _ada_body(c_ref, w_ref, b_ref, o_ref):
    s = _silu(c_ref[...]).astype(BF16)
    o_ref[0] = jnp.dot(s, w_ref[0].astype(BF16), preferred_element_type=F32) + b_ref[0]


def adaln_all(cvec, ada_w, ada_b):
    depth, d, n6 = ada_w.shape
    r = cvec.shape[0]
    tn = n6 // 4
    out = pl.pallas_call(
        _ada_body,
        grid=(depth, n6 // tn),
        in_specs=[
            pl.BlockSpec((r, d), lambda l, j: (0, 0)),
            pl.BlockSpec((1, d, tn), lambda l, j: (l, 0, j)),
            pl.BlockSpec((1, 1, tn), lambda l, j: (l, 0, j)),
        ],
        out_specs=pl.BlockSpec((1, r, tn), lambda l, j: (l, 0, j)),
        out_shape=jax.ShapeDtypeStruct((depth, r, n6), F32),
        compiler_params=_params("arbitrary", "arbitrary"),
        name="adaln",
    )(cvec, ada_w, ada_b.reshape(depth, 1, n6))
    return out.reshape(depth, r, N_MOD, d)


def _swap_halves(x):
    pieces = []
    for j in range(x.shape[1] // LANES):
        xj = x[:, j * LANES:(j + 1) * LANES]
        fwd = pltpu.roll(xj, LANES - HEAD_DIM // 2, axis=1)
        bwd = pltpu.roll(xj, HEAD_DIM // 2, axis=1)
        lane = lax.broadcasted_iota(jnp.int32, xj.shape, 1)
        pieces.append(jnp.where((lane % HEAD_DIM) < HEAD_DIM // 2, fwd, bwd))
    return pieces[0] if len(pieces) == 1 else jnp.concatenate(pieces, axis=1)


def _rmsnorm_mod(x, g, scale, shift):
    ms = jnp.mean(x * x, axis=-1, keepdims=True)
    return (x * lax.rsqrt(ms + EPS)) * (g * (1.0 + scale)) + shift


def _moe_residual(x1, y0_ref, y1_ref, route_ref, pmod_ref, rows=slice(None)):
    y0 = _unpack_bf16_pairs(y0_ref[0, 0, rows, :], F32)
    y1 = _unpack_bf16_pairs(y1_ref[0, 0, rows, :], F32)
    moe = route_ref[0, rows, 2:3] * y0 + route_ref[0, rows, 3:4] * y1
    return x1 + pmod_ref[0, 5:6, :] * moe


def _residual_specs(res, tm, d, row):
    pair, route, pmods = res
    args = [pair, pair, route, pmods]
    specs = [pl.BlockSpec((1, 1, tm, d // 2), lambda bb, i: (0, bb, i, 0)),
             pl.BlockSpec((1, 1, tm, d // 2), lambda bb, i: (1, bb, i, 0)),
             pl.BlockSpec((1, tm, ROUTE_LANES), lambda bb, i: (bb, i, 0)),
             pl.BlockSpec((1, N_MOD, d), lambda bb, i: (row(bb), 0, 0))]
    return args, specs


IN_SPLIT = 2


def _in_body(*refs, fuse_res, rope):
    it = iter(refs)
    x_ref = next(it)
    if fuse_res:
        res_refs = [next(it) for _ in range(4)]
    mod_ref = next(it)
    g_ref = next(it)
    w_ref = next(it)
    if rope:
        cos_ref = next(it)
        sin_ref = next(it)
    outs = list(it)
    xo_ref = outs.pop(0) if fuse_res else None
    c = MIX_CH
    part = x_ref.shape[1] // IN_SPLIT
    hs = []
    for p in range(IN_SPLIT):
        rows = slice(p * part, (p + 1) * part)
        x = x_ref[0, rows, :]
        if fuse_res:
            x = _moe_residual(x, *res_refs, rows)
            xo_ref[0, rows, :] = x
        hs.append(_rmsnorm_mod(x, g_ref[...], mod_ref[0, 1:2, :], mod_ref[0, 0:1, :]).astype(BF16))
    ua_ref, ub_ref, uc_ref, uq_ref = outs[:4]
    for p in range(IN_SPLIT):
        rows = slice(p * part, (p + 1) * part)
        u = jnp.dot(hs[p], w_ref[...], preferred_element_type=F32)
        ua_ref[0, rows, :] = u[:, 0:2 * c]
        ub_ref[0, rows, :] = u[:, 2 * c:5 * c]
        uc_ref[0, rows, :] = u[:, 5 * c:6 * c]
        q = u[:, 6 * c:7 * c] * ATT_SCALE
        k = u[:, 7 * c:7 * c + c // 2]
        v = u[:, 7 * c + c // 2:8 * c]
        uq_ref[0, rows, :] = q.astype(uq_ref.dtype)
        if rope:
            uqr_ref, ukv_ref = outs[4:]
            cs = cos_ref[rows, :]
            sn = sin_ref[rows, :]
            uqr_ref[0, rows, :] = (q * cs + _swap_halves(q) * sn).astype(uqr_ref.dtype)
            kr = k * cs[:, :c // 2] + _swap_halves(k) * sn[:, :c // 2]
            ukv_ref[0, rows, :] = jnp.concatenate([kr, v], axis=1).astype(ukv_ref.dtype)
        else:
            outs[4][0, rows, :] = u[:, 7 * c:8 * c]


def in_projection(x, mods, mod_row0, norm_g, w_in_bf, *, res=None, rope=None, tm):
    b, l, d = x.shape
    c = MIX_CH
    grid = (b, l // tm)
    row = (lambda bb: 0) if mod_row0 is None else (lambda bb: mod_row0 + bb)
    xspec = pl.BlockSpec((1, tm, d), lambda bb, i: (bb, i, 0))
    mspec = pl.BlockSpec((1, N_MOD, d), lambda bb, i: (row(bb), 0, 0))
    args, specs = [x], [xspec]
    if res is not None:
        rargs, rspecs = _residual_specs(res, tm, d, row)
        args += rargs
        specs += rspecs
    args += [mods, norm_g.reshape(1, d), w_in_bf]
    specs += [mspec, pl.BlockSpec((1, d), lambda bb, i: (0, 0)),
              pl.BlockSpec(w_in_bf.shape, lambda bb, i: (0, 0))]
    if rope is not None:
        args += [rope[0], rope[1]]
        specs += [pl.BlockSpec((tm, c), lambda bb, i: (i, 0))] * 2

    def ospec(w):
        return pl.BlockSpec((1, tm, w), lambda bb, i: (bb, i, 0))

    out_shape, out_specs = [], []
    if res is not None:
        out_shape.append(jax.ShapeDtypeStruct((b, l, d), F32))
        out_specs.append(xspec)
    widths = [(2 * c, F32), (3 * c, F32), (c, F32), (c, BF16)] + ([(c, BF16), (c, BF16)] if rope is not None else [(c, F32)])
    for w, dt in widths:
        out_shape.append(jax.ShapeDtypeStruct((b, l, w), dt))
        out_specs.append(ospec(w))
    return pl.pallas_call(
        functools.partial(_in_body, fuse_res=res is not None, rope=rope is not None),
        grid=grid, in_specs=specs, out_specs=out_specs, out_shape=out_shape,
        compiler_params=_params("arbitrary", "arbitrary"),
        name="in_proj",
    )(*args)


def _dw_tile(win, w_ref, n_taps, first, rows):
    acc = w_ref[0:1, :] * win[first:first + rows]
    for k in range(1, n_taps):
        acc = acc + w_ref[k:k + 1, :] * win[first + k:first + k + rows]
    return acc


CONV_PAD = 16
CONV_ROWS = 256
CONV_CHUNK = 512
CONV_TAIL = CONV_PAD + CONV_ROWS + SUBLANES + SUBLANES


def _conf_body(u_ref, w_ref, b_ref, g_ref, beta_ref, o_ref, gp_ref, sh_ref, *, seq, n_taps, chunk):
    c = MIX_CH
    r = CONV_ROWS
    first = CONV_PAD - (n_taps - 1) // 2
    gp_ref[0:CONV_PAD, :] = jnp.zeros((CONV_PAD, c), F32)
    gp_ref[CONV_PAD + seq:CONV_PAD + seq + CONV_TAIL, :] = jnp.zeros((CONV_TAIL, c), F32)

    def fill(i, carry):
        r0 = pl.multiple_of(i * r, r)
        a = u_ref[0, pl.ds(r0, r), 0:c]
        g = u_ref[0, pl.ds(r0, r), c:2 * c]
        gp_ref[pl.ds(CONV_PAD + r0, r), :] = a * jax.nn.sigmoid(g)
        return carry

    lax.fori_loop(0, seq // r, fill, 0)
    n_copy_tiles = sh_ref.shape[1] // r

    def do_chunk(ci, carry):
        c0 = pl.multiple_of(ci * chunk, chunk)

        def shift_tile(ti, carry2):
            t0 = pl.multiple_of(ti * r, r)
            win = gp_ref[pl.ds(c0 + t0, r + SUBLANES), :]
            for m in range(1, SUBLANES):
                sh_ref[m - 1, pl.ds(t0, r), :] = win[m:m + r]
            return carry2

        lax.fori_loop(0, n_copy_tiles, shift_tile, 0)

        def tile(ti, carry2):
            t0 = pl.multiple_of(ti * r, r)
            acc = None
            for k in range(n_taps):
                a8, m = (first + k) // SUBLANES * SUBLANES, (first + k) % SUBLANES
                src = gp_ref[pl.ds(c0 + t0 + a8, r), :] if m == 0 else sh_ref[m - 1, pl.ds(t0 + a8, r), :]
                term = w_ref[k:k + 1, :] * src
                acc = term if acc is None else acc + term
            z = acc + b_ref[...]
            mu = jnp.mean(z, axis=-1, keepdims=True)
            zc = z - mu
            var = jnp.mean(zc * zc, axis=-1, keepdims=True)
            zn = zc * lax.rsqrt(var + EPS) * g_ref[...] + beta_ref[...]
            o_ref[0, pl.ds(c0 + t0, r), :] = _silu(zn).astype(o_ref.dtype)
            return carry2

        lax.fori_loop(0, chunk // r, tile, 0)
        return carry

    lax.fori_loop(0, seq // chunk, do_chunk, 0)


def conformer_conv(ua, dw_w, dw_b, ln_g, ln_b):
    b, l, c2 = ua.shape
    c = MIX_CH
    k = dw_w.shape[0]
    chunk = min(l, CONV_CHUNK)
    copy_rows = -(-(chunk + CONV_PAD + k) // CONV_ROWS) * CONV_ROWS
    assert l % chunk == 0 and k - 1 <= 2 * CONV_PAD and copy_rows - chunk + SUBLANES <= CONV_PAD + CONV_TAIL
    vec = pl.BlockSpec((1, c), lambda bb: (0, 0))
    return pl.pallas_call(
        functools.partial(_conf_body, seq=l, n_taps=k, chunk=chunk),
        grid=(b,),
        in_specs=[pl.BlockSpec((1, l, c2), lambda bb: (bb, 0, 0)),
                  pl.BlockSpec((k, c), lambda bb: (0, 0)), vec, vec, vec],
        out_specs=pl.BlockSpec((1, l, c), lambda bb: (bb, 0, 0)),
        out_shape=jax.ShapeDtypeStruct((b, l, c), BF16),
        scratch_shapes=[pltpu.VMEM((CONV_PAD + l + CONV_TAIL, c), F32),
                        pltpu.VMEM((SUBLANES - 1, copy_rows, c), F32)],
        compiler_params=_params("arbitrary"),
        name="conformer",
    )(ua, dw_w, dw_b.reshape(1, c), ln_g.reshape(1, c), ln_b.reshape(1, c))


SHORT_PAD = 8
SHORT_ROWS = 64


def _short_body(u_ref, w_ref, b_ref, o_ref, xp_ref, *, seq, n_taps):
    c = MIX_CH
    r = SHORT_ROWS
    half = (n_taps - 1) // 2
    zero = jnp.zeros((SHORT_PAD, c), F32)
    xp_ref[0:SHORT_PAD, :] = zero
    xp_ref[SHORT_PAD + seq:SHORT_PAD + seq + SHORT_PAD, :] = zero
    for q in range(u_ref.shape[0]):
        def fill(i, carry):
            r0 = pl.multiple_of(i * r, r)
            xp_ref[pl.ds(SHORT_PAD + r0, r), :] = u_ref[q, pl.ds(r0, r), :]
            return carry

        lax.fori_loop(0, seq // r, fill, 0)

        def tile(i, carry):
            r0 = pl.multiple_of(i * r, r)
            win = xp_ref[pl.ds(r0, r + 2 * SHORT_PAD), :]
            o_ref[q, pl.ds(r0, r), :] = _dw_tile(win, w_ref, n_taps, SHORT_PAD - half, r) + b_ref[...]
            return carry

        lax.fori_loop(0, seq // r, tile, 0)


SHORT_BLOCK_ROWS = 1024


def hyena_short_conv(ub, short_w, short_b):
    b, l, c3 = ub.shape
    c = MIX_CH
    k = short_w.shape[0]
    nbat = math.gcd(b, max(1, SHORT_BLOCK_ROWS // l))
    return pl.pallas_call(
        functools.partial(_short_body, seq=l, n_taps=k),
        grid=(b // nbat, c3 // c),
        in_specs=[pl.BlockSpec((nbat, l, c), lambda bb, j: (bb, 0, j)),
                  pl.BlockSpec((k, c), lambda bb, j: (0, j)),
                  pl.BlockSpec((1, c), lambda bb, j: (0, j))],
        out_specs=pl.BlockSpec((nbat, l, c), lambda bb, j: (bb, 0, j)),
        out_shape=jax.ShapeDtypeStruct((b, l, c3), F32),
        scratch_shapes=[pltpu.VMEM((l + 2 * SHORT_PAD, c), F32)],
        compiler_params=_params("arbitrary", "arbitrary"),
        name="hyena_short",
    )(ub, short_w, short_b.reshape(1, c3))


def _fft_sizes(seq):
    n = 2 * seq
    n2 = 128 if n >= 4096 else 32
    return n, n // n2, n2


FFT_GROUP = SUBLANES


def _kron_matrices(n, n1, h):
    k1 = np.arange(n1)[:, None]
    i1 = np.arange(h)[None, :]
    th = 2.0 * np.pi * ((k1 * i1) % n1) / n1
    eye = np.eye(FFT_GROUP)
    cs, sn = np.kron(np.cos(th), eye), np.kron(np.sin(th), eye)
    fwd = np.block([[cs, sn], [-sn, cs]])
    inv = np.block([[cs.T, -sn.T], [sn.T, cs.T]]) / n
    return fwd.astype(np.float32), inv.astype(np.float32)


def _twiddle_tables(n, n1, n2):
    k1 = jnp.arange(n1, dtype=jnp.int32)[:, None]
    i2 = jnp.arange(n2, dtype=jnp.int32)[None, :]
    th = ((k1 * i2) % n).astype(F32) * (2.0 * math.pi / n)
    shape = (n1, n2 // FFT_GROUP, FFT_GROUP, LANES)
    full = lambda a: jnp.broadcast_to(a.reshape(shape[:3] + (1,)), shape)
    return full(jnp.cos(th)), full(jnp.sin(th))


def _mid_tables(n2):
    a = np.arange(n2)
    th = 2.0 * np.pi * ((a[:, None] * a[None, :]) % n2) / n2
    cs, sn = np.cos(th), np.sin(th)
    fwd = np.concatenate([np.concatenate([cs, sn], 1), np.concatenate([-sn, cs], 1)], 0)
    inv = np.concatenate([np.concatenate([cs, -sn], 1), np.concatenate([sn, cs], 1)], 0)
    return fwd.astype(np.float32), inv.astype(np.float32)


MID_CHUNK = 16


def _dot(a, b):
    return jnp.dot(a.astype(BF16), b.astype(BF16), preferred_element_type=F32)


def _filter_stage1_body(x_ref, m_ref, c_ref, s_ref, o_ref, *, n1, gs):
    j, c = FFT_GROUP, MIX_CH
    for s in range(gs):
        a = _dot(m_ref[...], x_ref[:, s].reshape(n1 * j, c))
        ar = a[:n1 * j].reshape(n1, j, c)
        ai = a[n1 * j:].reshape(n1, j, c)
        cs, sn = _lanes(c_ref[:, s], c), _lanes(s_ref[:, s], c)
        o_ref[0, 0, :, s] = ar * cs + ai * sn
        o_ref[0, 1, :, s] = ai * cs - ar * sn


def filter_stage1(k, order, mat, twc, tws, *, n1, n2):
    g, j, c = n2 // FFT_GROUP, FFT_GROUP, MIX_CH
    gs = _group_step(g)
    tw = pl.BlockSpec((n1, gs, j, LANES), lambda gi: (0, gi, 0, 0))
    out = pl.pallas_call(
        functools.partial(_filter_stage1_body, n1=n1, gs=gs),
        grid=(g // gs,),
        in_specs=[pl.BlockSpec((n1, gs, j, c), lambda gi: (0, gi, 0, order)),
                  pl.BlockSpec(mat.shape, lambda gi: (0, 0)), tw, tw],
        out_specs=pl.BlockSpec((1, 2, n1, gs, j, c), lambda gi: (0, 0, 0, gi, 0, 0)),
        out_shape=jax.ShapeDtypeStruct((1, 2, n1, g, j, c), F32),
        compiler_params=_params("arbitrary"),
        name="filter_stage1",
    )(k.reshape(n1, g, j, k.shape[-1]), mat, twc, tws)
    return out.reshape(1, 2, n1 * n2, c)


def _mid_body(a_ref, k_ref, f_ref, g_ref, o_ref, *, n2, kc):
    for j in range(kc):
        rows = slice(j * n2, (j + 1) * n2)
        blk = jnp.concatenate([a_ref[0, 0, rows, :], a_ref[0, 1, rows, :]], axis=0)
        s = _dot(f_ref[...], blk)
        sr, si = s[:n2], s[n2:]
        kr, ki = k_ref[0, rows, :], k_ref[1, rows, :]
        y = jnp.concatenate([sr * kr - si * ki, sr * ki + si * kr], axis=0)
        bb = _dot(g_ref[...], y)
        o_ref[0, 0, rows, :] = bb[:n2].astype(o_ref.dtype)
        o_ref[0, 1, rows, :] = bb[n2:].astype(o_ref.dtype)


def fft_mid(a, kf, f_fwd, f_inv, *, n1, n2):
    p = a.shape[0]
    c = MIX_CH
    kc = min(n1, MID_CHUNK)
    rows = kc * n2
    blk = pl.BlockSpec((1, 2, rows, c), lambda j, pp: (pp, 0, j, 0))
    mat = pl.BlockSpec(f_fwd.shape, lambda j, pp: (0, 0))
    return pl.pallas_call(
        functools.partial(_mid_body, n2=n2, kc=kc),
        grid=(n1 // kc, p),
        in_specs=[blk, pl.BlockSpec((2, rows, c), lambda j, pp: (0, j, 0)), mat, mat],
        out_specs=blk,
        out_shape=jax.ShapeDtypeStruct(a.shape, a.dtype),
        compiler_params=_params("arbitrary", "arbitrary"),
        name="fft_mid",
    )(a, kf, f_fwd, f_inv)


def _filter_mid_body(a_ref, f_ref, sum_ref, o_ref, *, n2, kc):
    inv = 1.0 / (sum_ref[0:1, :] + EPS)
    for j in range(kc):
        rows = slice(j * n2, (j + 1) * n2)
        blk = jnp.concatenate([a_ref[0, 0, rows, :], a_ref[0, 1, rows, :]], axis=0)
        s = _dot(f_ref[...], blk) * inv
        o_ref[0, rows, :] = s[:n2]
        o_ref[1, rows, :] = s[n2:]


def filter_mid(a, f_fwd, abs_sum, order, *, n1, n2):
    c = MIX_CH
    kc = min(n1, MID_CHUNK)
    rows = kc * n2
    return pl.pallas_call(
        functools.partial(_filter_mid_body, n2=n2, kc=kc),
        grid=(n1 // kc,),
        in_specs=[pl.BlockSpec((1, 2, rows, c), lambda j: (0, 0, j, 0)),
                  pl.BlockSpec(f_fwd.shape, lambda j: (0, 0)),
                  pl.BlockSpec((SUBLANES, c), lambda j: (0, order))],
        out_specs=pl.BlockSpec((2, rows, c), lambda j: (0, j, 0)),
        out_shape=jax.ShapeDtypeStruct((2, n1 * n2, c), F32),
        compiler_params=_params("arbitrary"),
        name="filter_mid",
    )(a, f_fwd, abs_sum)


def _lanes(t, width):
    return t if width == LANES else jnp.concatenate([t] * (width // LANES), axis=-1)


WORK_ROWS = 2 * FFT_GROUP


def _kron1_body(zr_ref, zi_ref, m_ref, c_ref, s_ref, o_ref, *, n1, h, gs):
    j, c = FFT_GROUP, MIX_CH
    parts = []
    for s in range(gs):
        xr = zr_ref[0, :, s].reshape(h * j, c)
        xi = zi_ref[0, :, s].reshape(h * j, c)
        a = _dot(m_ref[...], jnp.concatenate([xr, xi], axis=0))
        ar = a[:n1 * j].reshape(n1, j, c)
        ai = a[n1 * j:].reshape(n1, j, c)
        cs, sn = _lanes(c_ref[:, s], c), _lanes(s_ref[:, s], c)
        parts.append((ar * cs + ai * sn, ai * cs - ar * sn))
        if s % 2 == 1:
            for plane in range(2):
                o_ref[0, plane, :, s // 2] = jnp.concatenate([parts[s - 1][plane], parts[s][plane]],
                                                             axis=1).astype(o_ref.dtype)


def _group_step(n_groups):
    return min(n_groups, 4)


def kron_stage1(z5, col, n_pairs, imag_offset, mat, twc, tws, *, n1, h):
    g, j, c = z5.shape[2], FFT_GROUP, MIX_CH
    gs = _group_step(g)
    tw = pl.BlockSpec((n1, gs, j, LANES), lambda gi, p: (0, gi, 0, 0))
    return pl.pallas_call(
        functools.partial(_kron1_body, n1=n1, h=h, gs=gs),
        grid=(g // gs, n_pairs),
        in_specs=[pl.BlockSpec((1, h, gs, j, c), lambda gi, p: (p, 0, gi, 0, col)),
                  pl.BlockSpec((1, h, gs, j, c), lambda gi, p: (p + imag_offset, 0, gi, 0, col)),
                  pl.BlockSpec(mat.shape, lambda gi, p: (0, 0)), tw, tw],
        out_specs=pl.BlockSpec((1, 2, n1, gs // 2, WORK_ROWS, c), lambda gi, p: (p, 0, 0, gi, 0, 0)),
        out_shape=jax.ShapeDtypeStruct((n_pairs, 2, n1, g // 2, WORK_ROWS, c), BF16),
        compiler_params=_params("arbitrary", "arbitrary"),
        name="fft_kron1",
    )(z5, z5, mat, twc, tws)


def _kron3_body(b_ref, m_ref, c_ref, s_ref, zr_ref, zi_ref, gr_ref, gi_ref, d_ref, o_ref, *, n1, h, gs):
    j, c = FFT_GROUP, MIX_CH
    d = d_ref[...].reshape(1, 1, c)
    for s in range(gs):
        half = slice((s % 2) * j, (s % 2 + 1) * j)
        br = b_ref[0, 0, :, s // 2].astype(F32)[:, half]
        bi = b_ref[0, 1, :, s // 2].astype(F32)[:, half]
        cs, sn = _lanes(c_ref[:, s], c), _lanes(s_ref[:, s], c)
        xr = (br * cs - bi * sn).reshape(n1 * j, c)
        xi = (br * sn + bi * cs).reshape(n1 * j, c)
        y = _dot(m_ref[...], jnp.concatenate([xr, xi], axis=0))
        yr = y[:h * j].reshape(h, j, c)
        yi = y[h * j:].reshape(h, j, c)
        o_ref[0, 0, :, s] = gr_ref[0, :, s] * (yr + d * zr_ref[0, :, s])
        o_ref[1, 0, :, s] = gi_ref[0, :, s] * (yi + d * zi_ref[0, :, s])


def kron_stage3(bw, mat, twc, tws, z5, z_col, g5, g_col, d_vec, imag_offset, *, n1, h):
    p = bw.shape[0]
    g, j, c = bw.shape[3] * 2, FFT_GROUP, MIX_CH
    gs = _group_step(g)
    tw = pl.BlockSpec((n1, gs, j, LANES), lambda gi, pp: (0, gi, 0, 0))

    def src(col, off):
        return pl.BlockSpec((1, h, gs, j, c), lambda gi, pp: (pp + off, 0, gi, 0, col))

    return pl.pallas_call(
        functools.partial(_kron3_body, n1=n1, h=h, gs=gs),
        grid=(g // gs, p),
        in_specs=[pl.BlockSpec((1, 2, n1, gs // 2, WORK_ROWS, c), lambda gi, pp: (pp, 0, 0, gi, 0, 0)),
                  pl.BlockSpec(mat.shape, lambda gi, pp: (0, 0)), tw, tw,
                  src(z_col, 0), src(z_col, imag_offset), src(g_col, 0), src(g_col, imag_offset),
                  pl.BlockSpec((1, c), lambda gi, pp: (0, 0))],
        out_specs=pl.BlockSpec((2, 1, h, gs, j, c), lambda gi, pp: (0, pp, 0, gi, 0, 0)),
        out_shape=jax.ShapeDtypeStruct((2, p, h, g, j, c), F32),
        compiler_params=_params("arbitrary", "arbitrary"),
        name="fft_kron3",
    )(bw, mat, twc, tws, z5, z5, g5, g5, d_vec.reshape(1, c))


FILTER_TILE = 512


def _lane_dense(fn, a):
    r, w = a.shape
    f = LANES // w
    rows = r // f
    dense = jnp.concatenate([a[q * rows:(q + 1) * rows] for q in range(f)], axis=1)
    out = fn(dense)
    return jnp.concatenate([out[:, q * w:(q + 1) * w] for q in range(f)], axis=0)


def _filter_body(fw1_ref, fb1_ref, fr1_ref, fw2_ref, fb2_ref, fr2_ref, fw3_ref, fb3_ref, bands_ref, decay_ref,
                 k_ref, sum_ref, *, seq):
    i = pl.program_id(0)
    c = MIX_CH
    hp = lax.Precision.HIGHEST
    n = i * FILTER_TILE + lax.broadcasted_iota(jnp.int32, (FILTER_TILE, 1), 0)
    pos = jnp.where(n <= seq, n, 2 * seq - n).astype(F32)
    t = pos * (1.0 / (seq - 1))
    ang = (pos * (2.0 * math.pi / seq)) * bands_ref[...]
    nb = bands_ref.shape[1]
    pre = (t * fw1_ref[0:1, :]
           + jnp.dot(_lane_dense(jnp.cos, ang), fw1_ref[1:1 + nb, :], preferred_element_type=F32, precision=hp)
           - jnp.dot(_lane_dense(jnp.sin, ang), fw1_ref[1 + nb:1 + 2 * nb, :], preferred_element_type=F32,
                     precision=hp)
           + fb1_ref[...])
    h = _lane_dense(jnp.sin, fr1_ref[...] * pre)
    h = _lane_dense(jnp.sin, fr2_ref[...] * (jnp.dot(h, fw2_ref[...], preferred_element_type=F32, precision=hp)
                                             + fb2_ref[...]))
    h = jnp.dot(h, fw3_ref[...], preferred_element_type=F32, precision=hp) + fb3_ref[...]
    win = jnp.exp(-t * decay_ref[...]) + HYENA_SHIFT
    win = jnp.concatenate([win, win], axis=1)
    fwd, bwd = h[:, :2 * c], h[:, 2 * c:]
    k = jnp.where(n < seq, fwd, bwd) + jnp.where(n == 0, bwd, 0.0)
    k = jnp.where(n == seq, 0.0, k) * win
    k_ref[...] = k

    @pl.when(i == 0)
    def _():
        sum_ref[...] = jnp.zeros(sum_ref.shape, F32)

    sum_ref[...] = sum_ref[...] + jnp.sum(jnp.abs(k), axis=0, keepdims=True)


def hyena_filter_time(seq, fw1, fb1, fr1, fw2, fb2, fr2, fw3, fb3):
    c = MIX_CH
    n_bands = (fw1.shape[0] - 1) // 2
    bands = jnp.linspace(1e-4, n_bands - 1, n_bands, dtype=F32)[None, :]
    max_decay = math.log(HYENA_DECAY_TARGET) / HYENA_FAST_PCT
    min_decay = math.log(HYENA_DECAY_TARGET) / HYENA_SLOW_PCT
    decay = jnp.abs(jnp.linspace(min_decay, max_decay, c, dtype=F32))[None, :]
    assert (2 * seq) % FILTER_TILE == 0
    args = [fw1, fb1[None, :], fr1[None, :], fw2, fb2[None, :], fr2[None, :], fw3, fb3[None, :], bands, decay]
    return pl.pallas_call(
        functools.partial(_filter_body, seq=seq),
        grid=(2 * seq // FILTER_TILE,),
        in_specs=[pl.BlockSpec(a.shape, lambda i: (0, 0)) for a in args],
        out_specs=[pl.BlockSpec((FILTER_TILE, 2 * c), lambda i: (i, 0)),
                   pl.BlockSpec((SUBLANES, 2 * c), lambda i: (0, 0))],
        out_shape=[jax.ShapeDtypeStruct((2 * seq, 2 * c), F32), jax.ShapeDtypeStruct((SUBLANES, 2 * c), F32)],
        compiler_params=_params("arbitrary"),
        name="hyena_filter",
    )(*args)


def hyena_filter_spectrum(seq, filt, tabs):
    n, n1, n2 = _fft_sizes(seq)
    k, abs_sum = hyena_filter_time(seq, *filt)
    twc, tws, f_fwd, m_real = tabs[2], tabs[3], tabs[4], tabs[6]
    return [filter_mid(filter_stage1(k, o, m_real, twc, tws, n1=n1, n2=n2), f_fwd, abs_sum, o, n1=n1, n2=n2)
            for o in range(2)]


def hyena_mixer(ub, short_w, short_b, hy_d, kf, tabs):
    b, l, _ = ub.shape
    n, n1, n2 = _fft_sizes(l)
    h = n1 // 2
    p = b // 2
    c = MIX_CH
    g = n2 // FFT_GROUP
    u = hyena_short_conv(ub, short_w, short_b)
    u5 = u.reshape(b, h, g, FFT_GROUP, u.shape[-1])
    m_fwd, m_inv, twc, tws, f_fwd, f_inv = tabs[:6]
    z5 = u5
    for o in range(2):
        a = kron_stage1(z5, 0, p, p, m_fwd, twc, tws, n1=n1, h=h)
        bw = fft_mid(a.reshape(p, 2, n, c), kf[o], f_fwd, f_inv, n1=n1, n2=n2)
        z = kron_stage3(bw.reshape(a.shape), m_inv, twc, tws, z5, 0, u5, 1 + o, hy_d[o], p, n1=n1, h=h)
        z5 = z.reshape(b, h, g, FFT_GROUP, c)
    return z5.reshape(b, l, c)


def hyena_tables(seq):
    n, n1, n2 = _fft_sizes(seq)
    m_fwd, m_inv = _kron_matrices(n, n1, n1 // 2)
    m_real = _kron_matrices(n, n1, n1)[0][:, :n1 * FFT_GROUP]
    twc, tws = _twiddle_tables(n, n1, n2)
    f_fwd, f_inv = _mid_tables(n2)
    bf = lambda a: jnp.asarray(a).astype(BF16)
    return bf(m_fwd), bf(m_inv), twc, tws, bf(f_fwd), bf(f_inv), bf(m_real)


def _fnet_body(cl_ref, sl_ref, x_ref, cc_ref, sc_ref, o_ref, *, scale):
    x = x_ref[0].astype(BF16)
    pr = jnp.dot(cl_ref[...], x, preferred_element_type=F32).astype(BF16)
    qr = jnp.dot(sl_ref[...], x, preferred_element_type=F32).astype(BF16)
    o_ref[0] = (jnp.dot(pr, cc_ref[...], preferred_element_type=F32)
                - jnp.dot(qr, sc_ref[...], preferred_element_type=F32)) * scale


def _dft_tables(n):
    a = jnp.arange(n, dtype=jnp.int32)
    th = ((a[:, None] * a[None, :]) % n).astype(F32) * (2.0 * math.pi / n)
    return jnp.cos(th).astype(BF16), jnp.sin(th).astype(BF16)


FNET_DIRECT_MAX = 1024


def _fnet1_body(x_ref, ccs_ref, m_ref, c_ref, s_ref, o_ref, *, n1, gs):
    j, c = FFT_GROUP, MIX_CH
    for s in range(gs):
        x = x_ref[0, :, s].reshape(n1 * j, c).astype(BF16)
        z = jnp.dot(x, ccs_ref[...], preferred_element_type=F32)
        a = _dot(m_ref[...], jnp.concatenate([z[:, :c], z[:, c:]], axis=0))
        ar = a[:n1 * j].reshape(n1, j, c)
        ai = a[n1 * j:].reshape(n1, j, c)
        cs, sn = _lanes(c_ref[:, s], c), _lanes(s_ref[:, s], c)
        o_ref[0, 0, :, s] = ar * cs + ai * sn
        o_ref[0, 1, :, s] = ai * cs - ar * sn


def _fnet2_body(a_ref, m_ref, o_ref, *, n2, scale):
    j, c = FFT_GROUP, MIX_CH
    x = a_ref[0].reshape(2 * j * n2, c)
    y = _dot(m_ref[...], x) * scale
    o_ref[0, :, 0] = y.reshape(n2, j, c)


def fnet_tables(seq):
    c = MIX_CH
    cc, sc = _dft_tables(c)
    if seq <= FNET_DIRECT_MAX:
        return _dft_tables(seq) + (cc, sc)
    n2 = 128
    n1 = seq // n2
    m_fwd, _ = _kron_matrices(seq, n1, n1)
    twc, tws = _twiddle_tables(seq, n1, n2)
    a = np.arange(n2)
    th = 2.0 * np.pi * ((a[:, None] * a[None, :]) % n2) / n2
    eye = np.eye(FFT_GROUP)
    m2 = np.concatenate([np.einsum('kn,ij->kijn', f, eye).reshape(n2 * FFT_GROUP, FFT_GROUP * n2)
                         for f in (np.cos(th), np.sin(th))], axis=1).astype(np.float32)
    ccs = jnp.concatenate([cc, -sc], axis=1)
    return ccs, jnp.asarray(m_fwd).astype(BF16), twc, tws, jnp.asarray(m2).astype(BF16)


def fnet_two_stage(uc, tables):
    b, l, c = uc.shape
    ccs, m_fwd, twc, tws, m2 = tables
    j = FFT_GROUP
    n2 = 128
    n1 = l // n2
    g = n2 // j
    gs = _group_step(g)
    tw = pl.BlockSpec((n1, gs, j, LANES), lambda gi, bb: (0, gi, 0, 0))
    a = pl.pallas_call(
        functools.partial(_fnet1_body, n1=n1, gs=gs),
        grid=(g // gs, b),
        in_specs=[pl.BlockSpec((1, n1, gs, j, c), lambda gi, bb: (bb, 0, gi, 0, 0)),
                  pl.BlockSpec(ccs.shape, lambda gi, bb: (0, 0)),
                  pl.BlockSpec(m_fwd.shape, lambda gi, bb: (0, 0)), tw, tw],
        out_specs=pl.BlockSpec((1, 2, n1, gs, j, c), lambda gi, bb: (bb, 0, 0, gi, 0, 0)),
        out_shape=jax.ShapeDtypeStruct((b, 2, n1, g, j, c), F32),
        compiler_params=_params("arbitrary", "arbitrary"),
        name="fnet_stage1",
    )(uc.reshape(b, n1, g, j, c), ccs, m_fwd, twc, tws)
    out = pl.pallas_call(
        functools.partial(_fnet2_body, n2=n2, scale=1.0 / math.sqrt(l * c)),
        grid=(b, n1 // j),
        in_specs=[pl.BlockSpec((1, 2, j, n2, c), lambda bb, q: (bb, 0, q, 0, 0)),
                  pl.BlockSpec(m2.shape, lambda bb, q: (0, 0))],
        out_specs=pl.BlockSpec((1, n2, 1, j, c), lambda bb, q: (bb, 0, q, 0, 0)),
        out_shape=jax.ShapeDtypeStruct((b, n2, n1 // j, j, c), F32),
        compiler_params=_params("arbitrary", "arbitrary"),
        name="fnet_stage2",
    )(a.reshape(b, 2, n1, n2, c), m2)
    return out.reshape(b, l, c)


def fnet_mixer(uc, tables):
    b, l, c = uc.shape
    if l > FNET_DIRECT_MAX:
        return fnet_two_stage(uc, tables)
    cl, sl, cc, sc = tables
    tm = min(l, 512)
    row = pl.BlockSpec((tm, l), lambda i, bb: (i, 0))
    sq = pl.BlockSpec((c, c), lambda i, bb: (0, 0))
    return pl.pallas_call(
        functools.partial(_fnet_body, scale=1.0 / math.sqrt(l * c)),
        grid=(l // tm, b),
        in_specs=[row, row, pl.BlockSpec((1, l, c), lambda i, bb: (bb, 0, 0)), sq, sq],
        out_specs=pl.BlockSpec((1, tm, c), lambda i, bb: (bb, i, 0)),
        out_shape=jax.ShapeDtypeStruct((b, l, c), F32),
        compiler_params=_params("arbitrary", "arbitrary"),
        name="fnet",
    )(cl, sl, uc, cc, sc)


def _heads_rows(x, g):
    h0 = Q_PER_KV * g
    return jnp.concatenate([x[:, (h0 + r) * HEAD_DIM:(h0 + r + 1) * HEAD_DIM] for r in range(Q_PER_KV)], axis=0)


def _qk(q, k):
    return lax.dot_general(q.astype(BF16), k.astype(BF16), (((1,), (1,)), ((), ())),
                           preferred_element_type=F32)


def _sink_col(sink_ref, g, rows):
    ridx = lax.broadcasted_iota(jnp.int32, (Q_PER_KV * rows, 1), 0)
    col = jnp.full((Q_PER_KV * rows, 1), sink_ref[Q_PER_KV * g], F32)
    for r in range(1, Q_PER_KV):
        col = jnp.where(ridx >= r * rows, sink_ref[Q_PER_KV * g + r], col)
    return col * LOG2E


def _lat_attn_body(sink_ref, q_ref, qr_ref, kp_ref, kc_ref, kn_ref, ck_ref, cv_ref, o_ref, *, sub):
    i = pl.program_id(1)
    n_qblk = pl.num_programs(1) * sub
    blk = ATT_BLOCK
    span = blk + 2 * WINDOW
    kv = jnp.concatenate([kp_ref[0], kc_ref[0], kn_ref[0]], axis=0)
    ck = ck_ref[0, 0].astype(BF16)
    cv = cv_ref[0, 0].astype(BF16)
    kvw = N_KV_HEADS * HEAD_DIM
    r = lax.broadcasted_iota(jnp.int32, (Q_PER_KV * blk, span), 0) % blk
    j = lax.broadcasted_iota(jnp.int32, (Q_PER_KV * blk, span), 1)
    band = (j >= r) & (j <= r + 2 * WINDOW)
    for s in range(sub):
        qi = i * sub + s
        ok = band & ((qi > 0) | (j >= WINDOW)) & ((qi < n_qblk - 1) | (j < WINDOW + blk))
        q = q_ref[0, s * blk:(s + 1) * blk, :]
        qr = qr_ref[0, s * blk:(s + 1) * blk, :]
        outs = []
        for g in range(N_KV_HEADS):
            kl = kv[s * blk:s * blk + span, g * HEAD_DIM:(g + 1) * HEAD_DIM]
            vl = kv[s * blk:s * blk + span, kvw + g * HEAD_DIM:kvw + (g + 1) * HEAD_DIM]
            s_loc = jnp.where(ok, _qk(_heads_rows(qr, g), kl), NEG_INF)
            s_ctx = _qk(_heads_rows(q, g), ck[:, g * HEAD_DIM:(g + 1) * HEAD_DIM])
            sink = _sink_col(sink_ref, g, blk)
            m = jnp.maximum(jnp.maximum(jnp.max(s_loc, axis=-1, keepdims=True),
                                        jnp.max(s_ctx, axis=-1, keepdims=True)), sink)
            e_loc = jnp.exp2(s_loc - m)
            e_ctx = jnp.exp2(s_ctx - m)
            den = (jnp.sum(e_loc, axis=-1, keepdims=True) + jnp.sum(e_ctx, axis=-1, keepdims=True)
                   + jnp.exp2(sink - m))
            o = (jnp.dot(e_loc.astype(BF16), vl, preferred_element_type=F32)
                 + jnp.dot(e_ctx.astype(BF16), cv[:, g * HEAD_DIM:(g + 1) * HEAD_DIM],
                           preferred_element_type=F32)) * (1.0 / den)
            outs += [o[rr * blk:(rr + 1) * blk] for rr in range(Q_PER_KV)]
        o_ref[0, s * blk:(s + 1) * blk, :] = jnp.concatenate(outs, axis=1).astype(o_ref.dtype)


ATT_SUB = 4


def latent_attention(uq, uqr, ukv, cache_k, cache_v, layer, sink):
    b, l, c = uq.shape
    p = cache_k.shape[2]
    blk = ATT_BLOCK
    nblk = l // blk
    sub = math.gcd(ATT_SUB, nblk)
    rows = sub * blk
    qspec = pl.BlockSpec((1, rows, c), lambda bb, i: (bb, i, 0))
    cspec = pl.BlockSpec((1, 1, p, cache_k.shape[3]), lambda bb, i: (bb, layer, 0, 0))
    return pl.pallas_call(
        functools.partial(_lat_attn_body, sub=sub),
        grid=(b, nblk // sub),
        in_specs=[pl.BlockSpec(memory_space=pltpu.SMEM), qspec, qspec,
                  pl.BlockSpec((1, blk, c), lambda bb, i: (bb, jnp.maximum(i * sub - 1, 0), 0)),
                  qspec,
                  pl.BlockSpec((1, blk, c), lambda bb, i: (bb, jnp.minimum((i + 1) * sub, nblk - 1), 0)),
                  cspec, cspec],
        out_specs=qspec,
        out_shape=jax.ShapeDtypeStruct((b, l, c), BF16),
        compiler_params=_params("arbitrary", "arbitrary"),
        name="latent_attention",
    )(sink, uq, uqr, ukv, ukv, ukv, cache_k, cache_v)


def _ctx_attn_body(sink_ref, q_ref, kv_ref, o_ref, *, seq):
    q = q_ref[0]
    kv = kv_ref[0]
    kvw = N_KV_HEADS * HEAD_DIM
    outs = []
    for g in range(N_KV_HEADS):
        kl = kv[:, g * HEAD_DIM:(g + 1) * HEAD_DIM]
        vl = kv[:, kvw + g * HEAD_DIM:kvw + (g + 1) * HEAD_DIM]
        s = _qk(_heads_rows(q, g), kl)
        sink = _sink_col(sink_ref, g, seq)
        m = jnp.maximum(jnp.max(s, axis=-1, keepdims=True), sink)
        e = jnp.exp2(s - m)
        den = jnp.sum(e, axis=-1, keepdims=True) + jnp.exp2(sink - m)
        o = jnp.dot(e.astype(BF16), vl.astype(BF16), preferred_element_type=F32) * (1.0 / den)
        outs += [o[rr * seq:(rr + 1) * seq] for rr in range(Q_PER_KV)]
    o_ref[0] = jnp.concatenate(outs, axis=1)


def context_attention(uq, ukv, sink):
    b, s, c = uq.shape
    spec = pl.BlockSpec((1, s, c), lambda bb: (bb, 0, 0))
    return pl.pallas_call(
        functools.partial(_ctx_attn_body, seq=s),
        grid=(b,),
        in_specs=[pl.BlockSpec(memory_space=pltpu.SMEM), spec, spec],
        out_specs=spec,
        out_shape=jax.ShapeDtypeStruct((b, s, c), F32),
        compiler_params=_params("arbitrary"),
        name="context_attention",
    )(sink, uq, ukv)


def _pack_bf16_pairs(hi_rounded):
    k = hi_rounded.shape[1] // 2
    bits = lax.bitcast_convert_type(hi_rounded, jnp.uint32)
    return bits[:, :k] | (bits[:, k:] >> 16)


def _unpack_bf16_pairs(packed, dtype=BF16):
    a = lax.bitcast_convert_type(packed & jnp.uint32(0xFFFF0000), F32)
    b = lax.bitcast_convert_type(packed << 16, F32)
    return jnp.concatenate([a, b], axis=1).astype(dtype)


def _out_body(ya_ref, yb_ref, yc_ref, yd_ref, x_ref, mod_ref, g_ref, w_ref, rw_ref, rb_ref,
              x1_ref, h_ref, route_ref, cnt_ref):
    c = MIX_CH
    y = jnp.dot(ya_ref[0].astype(BF16), w_ref[0:c, :], preferred_element_type=F32)
    for j, ref in enumerate((yb_ref, yc_ref, yd_ref), start=1):
        y = y + jnp.dot(ref[0].astype(BF16), w_ref[j * c:(j + 1) * c, :], preferred_element_type=F32)
    x1 = x_ref[0] + mod_ref[0, 2:3, :] * y
    x1_ref[0] = x1
    h = _rmsnorm_mod(x1, g_ref[...], mod_ref[0, 4:5, :], mod_ref[0, 3:4, :])
    h_hi = h.astype(BF16)
    h_hi32 = h_hi.astype(F32)
    h_ref[0] = _pack_bf16_pairs(h_hi32)
    h_lo = (h - h_hi32).astype(BF16)
    tm = h.shape[0]
    prod = jnp.dot(jnp.concatenate([h_hi, h_lo], axis=0), rw_ref[...], preferred_element_type=F32)
    logits = (prod[:tm, :ROUTE_LANES] + prod[:tm, ROUTE_LANES:]
              + prod[tm:, :ROUTE_LANES] + prod[tm:, ROUTE_LANES:]) + rb_ref[...]
    lane = lax.broadcasted_iota(jnp.int32, logits.shape, 1)
    is_c = lane < N_GROUPS
    lc = jnp.where(is_c, logits, NEG_INF)
    mc = jnp.max(lc, axis=-1, keepdims=True)
    grp = jnp.min(jnp.where(lc == mc, lane, ROUTE_LANES), axis=-1, keepdims=True)
    pg = 1.0 / jnp.sum(jnp.where(is_c, jnp.exp(lc - mc), 0.0), axis=-1, keepdims=True)
    lo = N_GROUPS + grp * EXPERTS_PER_GROUP
    in_g = (lane >= lo) & (lane < lo + EXPERTS_PER_GROUP)
    lf = jnp.where(in_g, logits, NEG_INF)
    t1 = jnp.max(lf, axis=-1, keepdims=True)
    i1 = jnp.min(jnp.where(lf == t1, lane, ROUTE_LANES), axis=-1, keepdims=True)
    lf2 = jnp.where(lane == i1, NEG_INF, lf)
    t2 = jnp.max(lf2, axis=-1, keepdims=True)
    i2 = jnp.min(jnp.where(lf2 == t2, lane, ROUTE_LANES), axis=-1, keepdims=True)
    e2 = jnp.exp(t2 - t1)
    w1 = pg / (1.0 + e2)
    w2 = pg * e2 / (1.0 + e2)
    rec = jnp.where(lane == 0, (i1 - N_GROUPS).astype(F32),
                    jnp.where(lane == 1, (i2 - N_GROUPS).astype(F32),
                              jnp.where(lane == 2, w1, jnp.where(lane == 3, w2, 0.0))))
    route_ref[0] = rec

    @pl.when((pl.program_id(0) == 0) & (pl.program_id(1) == 0))
    def _():
        cnt_ref[...] = jnp.zeros(cnt_ref.shape, F32)

    e0, e1 = _choice_onehots(rec)
    cnt_ref[...] = cnt_ref[...] + jnp.sum(e0 + e1, axis=0, keepdims=True)


def out_projection(ys, x, mods, mod_row0, norm_g, w_out_bf, rw, rb, *, tm):
    b, l, d = x.shape
    c = MIX_CH
    row = (lambda bb: 0) if mod_row0 is None else (lambda bb: mod_row0 + bb)
    yspec = pl.BlockSpec((1, tm, c), lambda bb, i: (bb, i, 0))
    xspec = pl.BlockSpec((1, tm, d), lambda bb, i: (bb, i, 0))
    return pl.pallas_call(
        _out_body,
        grid=(b, l // tm),
        in_specs=[yspec] * 4 + [xspec,
                                pl.BlockSpec((1, N_MOD, d), lambda bb, i: (row(bb), 0, 0)),
                                pl.BlockSpec((1, d), lambda bb, i: (0, 0)),
                                pl.BlockSpec(w_out_bf.shape, lambda bb, i: (0, 0)),
                                pl.BlockSpec(rw.shape, lambda bb, i: (0, 0)),
                                pl.BlockSpec(rb.shape, lambda bb, i: (0, 0))],
        out_specs=[xspec, pl.BlockSpec((1, tm, d // 2), lambda bb, i: (bb, i, 0)),
                   pl.BlockSpec((1, tm, ROUTE_LANES), lambda bb, i: (bb, i, 0)),
                   pl.BlockSpec((SUBLANES, ROUTE_LANES), lambda bb, i: (0, 0))],
        out_shape=[jax.ShapeDtypeStruct((b, l, d), F32), jax.ShapeDtypeStruct((b, l, d // 2), jnp.uint32),
                   jax.ShapeDtypeStruct((b, l, ROUTE_LANES), F32),
                   jax.ShapeDtypeStruct((SUBLANES, ROUTE_LANES), F32)],
        compiler_params=_params("arbitrary", "arbitrary"),
        name="out_proj",
    )(*ys, x, mods, norm_g.reshape(1, d), w_out_bf, rw, rb)


def _expert_body(be_ref, nv_ref, xs_ref, wg_ref, wu_ref, wd_ref, o_ref, wg_s, wu_s, wd_s):
    i = pl.program_id(0)
    prev = be_ref[jnp.maximum(i - 1, 0)]

    @pl.when((i == 0) | (be_ref[i] != prev))
    def _():
        wg_s[...] = wg_ref[0, 0].astype(BF16)
        wu_s[...] = wu_ref[0, 0].astype(BF16)
        wd_s[...] = wd_ref[0, 0].astype(BF16)

    @pl.when(nv_ref[i] > 0)
    def _():
        row = lax.broadcasted_iota(jnp.int32, xs_ref.shape, 0)
        x = _unpack_bf16_pairs(jnp.where(row < nv_ref[i], xs_ref[...], jnp.uint32(0)))
        g = jnp.dot(x, wg_s[...], preferred_element_type=F32)
        u = jnp.dot(x, wu_s[...], preferred_element_type=F32)
        a = (_silu(g) * u).astype(BF16)
        y = jnp.dot(a, wd_s[...], preferred_element_type=F32)
        o_ref[...] = _pack_bf16_pairs(y.astype(BF16).astype(F32))

    @pl.when(nv_ref[i] <= 0)
    def _():
        o_ref[...] = jnp.zeros(o_ref.shape, jnp.uint32)


def expert_ffn(xs, blk_e, n_valid, layer, e_gate, e_up, e_down):
    rows, dh = xs.shape
    d = 2 * dh
    nb = rows // MOE_BLOCK
    de = e_gate.shape[-1]
    grid_spec = pltpu.PrefetchScalarGridSpec(
        num_scalar_prefetch=2,
        grid=(nb,),
        in_specs=[pl.BlockSpec((MOE_BLOCK, dh), lambda i, be, nv: (i, 0)),
                  pl.BlockSpec((1, 1, d, de), lambda i, be, nv: (layer, be[i], 0, 0)),
                  pl.BlockSpec((1, 1, d, de), lambda i, be, nv: (layer, be[i], 0, 0)),
                  pl.BlockSpec((1, 1, de, d), lambda i, be, nv: (layer, be[i], 0, 0))],
        out_specs=pl.BlockSpec((MOE_BLOCK, dh), lambda i, be, nv: (i, 0)),
        scratch_shapes=[pltpu.VMEM((d, de), BF16), pltpu.VMEM((d, de), BF16), pltpu.VMEM((de, d), BF16)],
    )
    return pl.pallas_call(
        _expert_body, grid_spec=grid_spec,
        out_shape=jax.ShapeDtypeStruct((rows, dh), jnp.uint32),
        compiler_params=_params("arbitrary"),
        name="expert_ffn",
    )(blk_e, n_valid, xs, e_gate, e_up, e_down)


RANK_TILE = 512


def _choice_onehots(rec):
    lanef = lax.broadcasted_iota(jnp.int32, rec.shape, 1).astype(F32)
    return (lanef == rec[:, 0:1]).astype(F32), (lanef == rec[:, 1:2]).astype(F32)


def _slot_body(route_ref, cnt_ref, tri_ref, slot_ref, carry_ref):
    rec = route_ref[...]
    lane = lax.broadcasted_iota(jnp.int32, rec.shape, 1)
    e0, e1 = _choice_onehots(rec)
    both = e0 + e1

    @pl.when(pl.program_id(0) == 0)
    def _():
        cnt = cnt_ref[...]
        padded = jnp.floor((cnt + (MOE_BLOCK - 1)) * (1.0 / MOE_BLOCK)) * MOE_BLOCK
        ln = lax.broadcasted_iota(jnp.int32, cnt.shape, 1)
        incl = padded
        sh = 1
        while sh < ROUTE_LANES:
            incl = incl + jnp.where(ln >= sh, pltpu.roll(incl, sh, axis=1), 0.0)
            sh *= 2
        carry_ref[...] = incl - padded

    before = jnp.dot(tri_ref[...], both.astype(BF16), preferred_element_type=F32) + carry_ref[0:1, :]
    s0 = jnp.sum(e0 * before, axis=-1, keepdims=True)
    s1 = jnp.sum(e1 * before, axis=-1, keepdims=True)
    slot_ref[...] = jnp.where(lane == 0, s0, jnp.where(lane == 1, s1, 0.0))
    carry_ref[...] = carry_ref[...] + jnp.sum(both, axis=0, keepdims=True)


def moe_slots(route, counts):
    n = route.shape[0]
    t = RANK_TILE
    tri = jnp.asarray(np.tril(np.ones((t, t), np.float32), -1)).astype(BF16)
    return pl.pallas_call(
        _slot_body,
        grid=(n // t,),
        in_specs=[pl.BlockSpec((t, ROUTE_LANES), lambda i: (i, 0)),
                  pl.BlockSpec((SUBLANES, ROUTE_LANES), lambda i: (0, 0)),
                  pl.BlockSpec((t, t), lambda i: (0, 0))],
        out_specs=pl.BlockSpec((t, ROUTE_LANES), lambda i: (i, 0)),
        out_shape=jax.ShapeDtypeStruct((n, ROUTE_LANES), F32),
        scratch_shapes=[pltpu.VMEM((SUBLANES, ROUTE_LANES), F32)],
        compiler_params=_params("arbitrary"),
        name="moe_slots",
    )(route, counts, tri)


def _sc_mesh():
    return plsc.VectorSubcoreMesh(core_axis_name="c", subcore_axis_name="s")


def _sc_worker():
    return lax.axis_index("s") * SC_CORES + lax.axis_index("c")


DISPATCH_ROWS = 64
COMBINE_ROWS = 64


def sc_dispatch(rows, dest, n_slots):
    n, w = rows.shape
    ch = DISPATCH_ROWS
    per_w = n // SC_WORKERS
    n_ch = per_w // ch

    @functools.partial(
        pl.kernel, mesh=_sc_mesh(),
        out_type=jax.ShapeDtypeStruct((n_slots, w), rows.dtype),
        scratch_types=[pltpu.VMEM((ch,), jnp.int32), pltpu.VMEM((ch, w), rows.dtype)],
    )
    def scatter_kernel(rows_hbm, dest_hbm, out_hbm, idx_v, rows_v):
        wid = _sc_worker()

        @pl.loop(0, n_ch)
        def _(j):
            chunk = wid * n_ch + j
            pltpu.sync_copy(rows_hbm.at[pl.ds(pl.multiple_of(chunk * ch, ch), ch)], rows_v)
            for k in range(2):
                pltpu.sync_copy(dest_hbm.at[k, chunk], idx_v)
                pltpu.sync_copy(rows_v, out_hbm.at[idx_v])

    return scatter_kernel(rows, dest)


def sc_gather_rows(table, idx):
    s, w = table.shape
    m = idx.shape[0]
    ch = COMBINE_ROWS
    per_w = m // SC_WORKERS
    n_ch = per_w // ch

    @functools.partial(
        pl.kernel, mesh=_sc_mesh(),
        out_type=jax.ShapeDtypeStruct((m, w), table.dtype),
        scratch_types=[pltpu.VMEM((ch,), jnp.int32), pltpu.VMEM((ch, w), table.dtype), pltpu.SemaphoreType.DMA],
    )
    def gather_kernel(table_hbm, idx_hbm, out_hbm, idx_v, rows_v, sem):
        wid = _sc_worker()

        @pl.loop(0, n_ch)
        def _(j):
            off = pl.multiple_of((wid * n_ch + j) * ch, ch)
            pltpu.sync_copy(idx_hbm.at[pl.ds(off, ch)], idx_v)
            pltpu.async_copy(table_hbm.at[idx_v], rows_v, sem).wait()
            pltpu.sync_copy(rows_v, out_hbm.at[pl.ds(off, ch)])

    return gather_kernel(table, idx)


def hier_moe(h_packed, route, cnt, layer, e_gate, e_up, e_down):
    b, l, dh = h_packed.shape
    n = b * l
    assert n % (SC_WORKERS * DISPATCH_ROWS) == 0 and (2 * n) % (SC_WORKERS * COMBINE_ROWS) == 0
    slots = moe_slots(route.reshape(n, ROUTE_LANES), cnt)
    counts = cnt[0, :N_EXPERTS].astype(jnp.int32)
    padded = (counts + MOE_BLOCK - 1) // MOE_BLOCK * MOE_BLOCK
    pend = jnp.cumsum(padded)
    nb = -(-2 * n // MOE_BLOCK) + N_EXPERTS
    blk0 = jnp.arange(nb, dtype=jnp.int32) * MOE_BLOCK
    owner = pend[None, :] <= blk0[:, None]
    blk_e = jnp.minimum(jnp.sum(owner, axis=1), N_EXPERTS - 1).astype(jnp.int32)
    run_end = jnp.sum(jnp.where(jnp.arange(N_EXPERTS)[None, :] == blk_e[:, None],
                                (pend - padded + counts)[None, :], 0), axis=1)
    n_valid = jnp.clip(run_end - blk0, 0, MOE_BLOCK).astype(jnp.int32)
    dest = slots[:, 0:2].astype(jnp.int32).T
    xs = sc_dispatch(h_packed.reshape(n, dh), dest.reshape(2, n // DISPATCH_ROWS, DISPATCH_ROWS), nb * MOE_BLOCK)
    y = expert_ffn(xs, blk_e, n_valid, layer, e_gate, e_up, e_down)
    return sc_gather_rows(y, dest.reshape(2 * n)).reshape(2, b, l, dh)


def _final_body(x_ref, y0_ref, y1_ref, route_ref, pmod_ref, g_ref, o_ref):
    x = _moe_residual(x_ref[0], y0_ref, y1_ref, route_ref, pmod_ref)
    ms = jnp.mean(x * x, axis=-1, keepdims=True)
    o_ref[0] = x * lax.rsqrt(ms + EPS) * g_ref[...]


def final_norm(x1, res, mod_row0, norm_g, *, tm):
    b, l, d = x1.shape
    row = (lambda bb: 0) if mod_row0 is None else (lambda bb: mod_row0 + bb)
    xspec = pl.BlockSpec((1, tm, d), lambda bb, i: (bb, i, 0))
    rargs, rspecs = _residual_specs(res, tm, d, row)
    return pl.pallas_call(
        _final_body,
        grid=(b, l // tm),
        in_specs=[xspec] + rspecs + [pl.BlockSpec((1, d), lambda bb, i: (0, 0))],
        out_specs=xspec,
        out_shape=jax.ShapeDtypeStruct((b, l, d), F32),
        compiler_params=_params("arbitrary", "arbitrary"),
        name="final_norm",
    )(x1, *rargs, norm_g.reshape(1, d))


def _rope_tables(seq):
    rows = seq // GRID_W
    row_pos = jnp.repeat(jnp.arange(rows, dtype=F32), GRID_W)
    col_pos = jnp.tile(jnp.arange(GRID_W, dtype=F32), rows)
    n_freq = HEAD_DIM // 4
    inv = ROPE_BASE ** (-jnp.arange(n_freq, dtype=F32) / n_freq)
    ang = jnp.concatenate([row_pos[:, None] * inv, col_pos[:, None] * inv], axis=-1)
    cs, sn = jnp.cos(ang), jnp.sin(ang)
    cos_f = jnp.tile(jnp.concatenate([cs, cs], axis=-1), (1, N_Q_HEADS))
    sin_s = jnp.tile(jnp.concatenate([-sn, sn], axis=-1), (1, N_Q_HEADS))
    return cos_f, sin_s


def kernel(x_prompt, x_sample, cache_k, cache_v, c, c_ctx, ada_w, ada_b, norm1_g, norm2_g, w_in, conv_dw_w, conv_dw_b, conv_ln_g, conv_ln_b, hy_short_w, hy_short_b, hy_fw1, hy_fb1, hy_freq1, hy_fw2, hy_fb2, hy_freq2, hy_fw3, hy_fb3, hy_d, attn_sink, w_out, router_coarse_w, router_coarse_b, router_fine_w, router_fine_b, exp_gate, exp_up, exp_down, norm_f_g):
    depth = ada_w.shape[0]
    bp, lp, d = x_prompt.shape
    bs, ls, _ = x_sample.shape
    assert bp % 2 == 0 and bs % 2 == 0 and ls % ATT_BLOCK == 0 and ls % GRID_W == 0

    n_rows = -(-(1 + bs) // SUBLANES) * SUBLANES
    cvec = jnp.concatenate([c_ctx[None, :], c, jnp.zeros((n_rows - 1 - bs, d), F32)], axis=0)
    mods = adaln_all(cvec, ada_w, ada_b)

    rope = _rope_tables(ls)
    fnet_tabs, hy_tabs = {}, {}
    for seq in {lp, ls}:
        fnet_tabs[seq] = fnet_tables(seq)
        hy_tabs[seq] = hyena_tables(seq)
    ck = cache_k.reshape(cache_k.shape[0], depth, cache_k.shape[2], -1)
    cv = cache_v.reshape(cache_v.shape[0], depth, cache_v.shape[2], -1)
    pad = ROUTE_LANES - N_GROUPS - N_EXPERTS

    tm_p = min(lp, 512)
    tm_s = min(ls, 1024)
    xp, xs = x_prompt, x_sample
    res_p = res_s = None
    ks_out, vs_out = [], []
    for l in range(depth):
        w_in_bf = w_in[l].astype(BF16)
        w_out_bf = w_out[l].astype(BF16)
        rw = jnp.concatenate([router_coarse_w[l], router_fine_w[l], jnp.zeros((d, pad), F32)], axis=1)
        rw_hi = rw.astype(BF16)
        rw = jnp.concatenate([rw_hi, (rw - rw_hi.astype(F32)).astype(BF16)], axis=1)
        rb = jnp.concatenate([router_coarse_b[l], router_fine_b[l], jnp.zeros((pad,), F32)])[None, :]
        filt = (hy_fw1[l], hy_fb1[l], hy_freq1[l], hy_fw2[l], hy_fb2[l], hy_freq2[l], hy_fw3[l], hy_fb3[l])
        sink = attn_sink[l]

        def mixers(ua, ub, uc, yd, seq):
            ya = conformer_conv(ua, conv_dw_w[l], conv_dw_b[l], conv_ln_g[l], conv_ln_b[l])
            yb = hyena_mixer(ub, hy_short_w[l], hy_short_b[l], hy_d[l], hyena_filter_spectrum(seq, filt, hy_tabs[seq]),
                             hy_tabs[seq])
            yc = fnet_mixer(uc, fnet_tabs[seq])
            return (ya, yb, yc, yd)

        outs = in_projection(xp, mods[l], None, norm1_g[l], w_in_bf, res=res_p, tm=tm_p)
        if res_p is not None:
            xp, outs = outs[0], outs[1:]
        ua, ub, uc, uq, ukv = outs
        kvw = N_KV_HEADS * HEAD_DIM
        ks_out.append(ukv[..., :kvw].reshape(bp, lp, N_KV_HEADS, HEAD_DIM))
        vs_out.append(ukv[..., kvw:].reshape(bp, lp, N_KV_HEADS, HEAD_DIM))
        ys = mixers(ua, ub, uc, context_attention(uq, ukv, sink), lp)
        x1p, hp, route, cnt = out_projection(ys, xp, mods[l], None, norm2_g[l], w_out_bf, rw, rb, tm=tm_p)
        res_p = (hier_moe(hp, route, cnt, l, exp_gate, exp_up, exp_down), route, mods[l])
        xp = x1p

        outs = in_projection(xs, mods[l], 1, norm1_g[l], w_in_bf, res=res_s, rope=rope, tm=tm_s)
        if res_s is not None:
            xs, outs = outs[0], outs[1:]
        ua, ub, uc, uq, uqr, ukv = outs
        ys = mixers(ua, ub, uc, latent_attention(uq, uqr, ukv, ck, cv, l, sink), ls)
        x1s, hs, route, cnt = out_projection(ys, xs, mods[l], 1, norm2_g[l], w_out_bf, rw, rb, tm=tm_s)
        res_s = (hier_moe(hs, route, cnt, l, exp_gate, exp_up, exp_down), route, mods[l])
        xs = x1s

    y_prompt = final_norm(xp, res_p, None, norm_f_g, tm=tm_p)
    y_sample = final_norm(xs, res_s, 1, norm_f_g, tm=tm_s)
    return (y_prompt, y_sample, jnp.stack(ks_out, axis=1), jnp.stack(vs_out, axis=1))
```

```python
import functools
import math

import numpy as np
import jax
import jax.numpy as jnp
from jax import lax
from jax.experimental import pallas as pl
from jax.experimental.pallas import tpu as pltpu
from jax.experimental.pallas import tpu_sc as plsc

F32 = jnp.float32
BF16 = jnp.bfloat16

HEAD_DIM = 64
LOG2E = math.log2(math.e)
ATT_SCALE = HEAD_DIM ** -0.5 * LOG2E
N_Q_HEADS = 4
N_KV_HEADS = 2
Q_PER_KV = N_Q_HEADS // N_KV_HEADS
WINDOW = 128
ATT_BLOCK = 128
GRID_W = 64
ROPE_BASE = 10000.0
N_GROUPS = 4
EXPERTS_PER_GROUP = 8
N_EXPERTS = N_GROUPS * EXPERTS_PER_GROUP
MOE_BLOCK = 512
N_MOD = 6
EPS = 1e-6
NEG_INF = -1e30
HYENA_DECAY_TARGET = 1e-2
HYENA_FAST_PCT = 0.3
HYENA_SLOW_PCT = 1.5
HYENA_SHIFT = 0.05

LANES = 128
SUBLANES = 8
VMEM_LIMIT = 56 * 1024 * 1024

MIX_CH = 256
ROUTE_LANES = 128
SC_CORES = 2
SC_WORKERS = SC_CORES * 16


def _params(*sem):
    return pltpu.CompilerParams(dimension_semantics=sem, vmem_limit_bytes=VMEM_LIMIT)


def _silu(x):
    return x * jax.nn.sigmoid(x)


def _ada_body(c_ref, w_ref, b_ref, o_ref):
    s = _silu(c_ref[...]).astype(BF16)
    o_ref[0] = jnp.dot(s, w_ref[0].astype(BF16), preferred_element_type=F32) + b_ref[0]


def adaln_all(cvec, ada_w, ada_b):
    depth, d, n6 = ada_w.shape
    r = cvec.shape[0]
    tn = n6 // 4
    out = pl.pallas_call(
        _ada_body,
        grid=(depth, n6 // tn),
        in_specs=[
            pl.BlockSpec((r, d), lambda l, j: (0, 0)),
            pl.BlockSpec((1, d, tn), lambda l, j: (l, 0, j)),
            pl.BlockSpec((1, 1, tn), lambda l, j: (l, 0, j)),
        ],
        out_specs=pl.BlockSpec((1, r, tn), lambda l, j: (l, 0, j)),
        out_shape=jax.ShapeDtypeStruct((depth, r, n6), F32),
        compiler_params=_params("arbitrary", "arbitrary"),
        name="adaln",
    )(cvec, ada_w, ada_b.reshape(depth, 1, n6))
    return out.reshape(depth, r, N_MOD, d)


def _swap_halves(x):
    pieces = []
    for j in range(x.shape[1] // LANES):
        xj = x[:, j * LANES:(j + 1) * LANES]
        fwd = pltpu.roll(xj, LANES - HEAD_DIM // 2, axis=1)
        bwd = pltpu.roll(xj, HEAD_DIM // 2, axis=1)
        lane = lax.broadcasted_iota(jnp.int32, xj.shape, 1)
        pieces.append(jnp.where((lane % HEAD_DIM) < HEAD_DIM // 2, fwd, bwd))
    return pieces[0] if len(pieces) == 1 else jnp.concatenate(pieces, axis=1)


def _rmsnorm_mod(x, g, scale, shift):
    ms = jnp.mean(x * x, axis=-1, keepdims=True)
    return (x * lax.rsqrt(ms + EPS)) * (g * (1.0 + scale)) + shift


def _moe_residual(x1, y0_ref, y1_ref, route_ref, pmod_ref, rows=slice(None)):
    y0 = _unpack_bf16_pairs(y0_ref[0, 0, rows, :], F32)
    y1 = _unpack_bf16_pairs(y1_ref[0, 0, rows, :], F32)
    moe = route_ref[0, rows, 2:3] * y0 + route_ref[0, rows, 3:4] * y1
    return x1 + pmod_ref[0, 5:6, :] * moe


def _residual_specs(res, tm, d, row):
    pair, route, pmods = res
    args = [pair, pair, route, pmods]
    specs = [pl.BlockSpec((1, 1, tm, d // 2), lambda bb, i: (0, bb, i, 0)),
             pl.BlockSpec((1, 1, tm, d // 2), lambda bb, i: (1, bb, i, 0)),
             pl.BlockSpec((1, tm, ROUTE_LANES), lambda bb, i: (bb, i, 0)),
             pl.BlockSpec((1, N_MOD, d), lambda bb, i: (row(bb), 0, 0))]
    return args, specs


IN_SPLIT = 2


def _in_body(*refs, fuse_res, rope):
    it = iter(refs)
    x_ref = next(it)
    if fuse_res:
        res_refs = [next(it) for _ in range(4)]
    mod_ref = next(it)
    g_ref = next(it)
    w_ref = next(it)
    if rope:
        cos_ref = next(it)
        sin_ref = next(it)
    outs = list(it)
    xo_ref = outs.pop(0) if fuse_res else None
    c = MIX_CH
    part = x_ref.shape[1] // IN_SPLIT
    hs = []
    for p in range(IN_SPLIT):
        rows = slice(p * part, (p + 1) * part)
        x = x_ref[0, rows, :]
        if fuse_res:
            x = _moe_residual(x, *res_refs, rows)
            xo_ref[0, rows, :] = x
        hs.append(_rmsnorm_mod(x, g_ref[...], mod_ref[0, 1:2, :], mod_ref[0, 0:1, :]).astype(BF16))
    ua_ref, ub_ref, uc_ref, uq_ref = outs[:4]
    for p in range(IN_SPLIT):
        rows = slice(p * part, (p + 1) * part)
        u = jnp.dot(hs[p], w_ref[...], preferred_element_type=F32)
        ua_ref[0, rows, :] = u[:, 0:2 * c]
        ub_ref[0, rows, :] = u[:, 2 * c:5 * c]
        uc_ref[0, rows, :] = u[:, 5 * c:6 * c]
        q = u[:, 6 * c:7 * c] * ATT_SCALE
        k = u[:, 7 * c:7 * c + c // 2]
        v = u[:, 7 * c + c // 2:8 * c]
        uq_ref[0, rows, :] = q.astype(uq_ref.dtype)
        if rope:
            uqr_ref, ukv_ref = outs[4:]
            cs = cos_ref[rows, :]
            sn = sin_ref[rows, :]
            uqr_ref[0, rows, :] = (q * cs + _swap_halves(q) * sn).astype(uqr_ref.dtype)
            kr = k * cs[:, :c // 2] + _swap_halves(k) * sn[:, :c // 2]
            ukv_ref[0, rows, :] = jnp.concatenate([kr, v], axis=1).astype(ukv_ref.dtype)
        else:
            outs[4][0, rows, :] = u[:, 7 * c:8 * c]


def in_projection(x, mods, mod_row0, norm_g, w_in_bf, *, res=None, rope=None, tm):
    b, l, d = x.shape
    c = MIX_CH
    grid = (b, l // tm)
    row = (lambda bb: 0) if mod_row0 is None else (lambda bb: mod_row0 + bb)
    xspec = pl.BlockSpec((1, tm, d), lambda bb, i: (bb, i, 0))
    mspec = pl.BlockSpec((1, N_MOD, d), lambda bb, i: (row(bb), 0, 0))
    args, specs = [x], [xspec]
    if res is not None:
        rargs, rspecs = _residual_specs(res, tm, d, row)
        args += rargs
        specs += rspecs
    args += [mods, norm_g.reshape(1, d), w_in_bf]
    specs += [mspec, pl.BlockSpec((1, d), lambda bb, i: (0, 0)),
              pl.BlockSpec(w_in_bf.shape, lambda bb, i: (0, 0))]
    if rope is not None:
        args += [rope[0], rope[1]]
        specs += [pl.BlockSpec((tm, c), lambda bb, i: (i, 0))] * 2

    def ospec(w):
        return pl.BlockSpec((1, tm, w), lambda bb, i: (bb, i, 0))

    out_shape, out_specs = [], []
    if res is not None:
        out_shape.append(jax.ShapeDtypeStruct((b, l, d), F32))
        out_specs.append(xspec)
    widths = [(2 * c, F32), (3 * c, F32), (c, F32), (c, BF16)] + ([(c, BF16), (c, BF16)] if rope is not None else [(c, F32)])
    for w, dt in widths:
        out_shape.append(jax.ShapeDtypeStruct((b, l, w), dt))
        out_specs.append(ospec(w))
    return pl.pallas_call(
        functools.partial(_in_body, fuse_res=res is not None, rope=rope is not None),
        grid=grid, in_specs=specs, out_specs=out_specs, out_shape=out_shape,
        compiler_params=_params("arbitrary", "arbitrary"),
        name="in_proj",
    )(*args)


def _dw_tile(win, w_ref, n_taps, first, rows):
    acc = w_ref[0:1, :] * win[first:first + rows]
    for k in range(1, n_taps):
        acc = acc + w_ref[k:k + 1, :] * win[first + k:first + k + rows]
    return acc


CONV_PAD = 16
CONV_ROWS = 256
CONV_CHUNK = 512
CONV_TAIL = CONV_PAD + CONV_ROWS + SUBLANES + SUBLANES


def _conf_body(u_ref, w_ref, b_ref, g_ref, beta_ref, o_ref, gp_ref, sh_ref, *, seq, n_taps, chunk):
    c = MIX_CH
    r = CONV_ROWS
    first = CONV_PAD - (n_taps - 1) // 2
    gp_ref[0:CONV_PAD, :] = jnp.zeros((CONV_PAD, c), F32)
    gp_ref[CONV_PAD + seq:CONV_PAD + seq + CONV_TAIL, :] = jnp.zeros((CONV_TAIL, c), F32)

    def fill(i, carry):
        r0 = pl.multiple_of(i * r, r)
        a = u_ref[0, pl.ds(r0, r), 0:c]
        g = u_ref[0, pl.ds(r0, r), c:2 * c]
        gp_ref[pl.ds(CONV_PAD + r0, r), :] = a * jax.nn.sigmoid(g)
        return carry

    lax.fori_loop(0, seq // r, fill, 0)
    n_copy_tiles = sh_ref.shape[1] // r

    def do_chunk(ci, carry):
        c0 = pl.multiple_of(ci * chunk, chunk)

        def shift_tile(ti, carry2):
            t0 = pl.multiple_of(ti * r, r)
            win = gp_ref[pl.ds(c0 + t0, r + SUBLANES), :]
            for m in range(1, SUBLANES):
                sh_ref[m - 1, pl.ds(t0, r), :] = win[m:m + r]
            return carry2

        lax.fori_loop(0, n_copy_tiles, shift_tile, 0)

        def tile(ti, carry2):
            t0 = pl.multiple_of(ti * r, r)
            acc = None
            for k in range(n_taps):
                a8, m = (first + k) // SUBLANES * SUBLANES, (first + k) % SUBLANES
                src = gp_ref[pl.ds(c0 + t0 + a8, r), :] if m == 0 else sh_ref[m - 1, pl.ds(t0 + a8, r), :]
                term = w_ref[k:k + 1, :] * src
                acc = term if acc is None else acc + term
            z = acc + b_ref[...]
            mu = jnp.mean(z, axis=-1, keepdims=True)
            zc = z - mu
            var = jnp.mean(zc * zc, axis=-1, keepdims=True)
            zn = zc * lax.rsqrt(var + EPS) * g_ref[...] + beta_ref[...]
            o_ref[0, pl.ds(c0 + t0, r), :] = _silu(zn).astype(o_ref.dtype)
            return carry2

        lax.fori_loop(0, chunk // r, tile, 0)
        return carry

    lax.fori_loop(0, seq // chunk, do_chunk, 0)


def conformer_conv(ua, dw_w, dw_b, ln_g, ln_b):
    b, l, c2 = ua.shape
    c = MIX_CH
    k = dw_w.shape[0]
    chunk = min(l, CONV_CHUNK)
    copy_rows = -(-(chunk + CONV_PAD + k) // CONV_ROWS) * CONV_ROWS
    assert l % chunk == 0 and k - 1 <= 2 * CONV_PAD and copy_rows - chunk + SUBLANES <= CONV_PAD + CONV_TAIL
    vec = pl.BlockSpec((1, c), lambda bb: (0, 0))
    return pl.pallas_call(
        functools.partial(_conf_body, seq=l, n_taps=k, chunk=chunk),
        grid=(b,),
        in_specs=[pl.BlockSpec((1, l, c2), lambda bb: (bb, 0, 0)),
                  pl.BlockSpec((k, c), lambda bb: (0, 0)), vec, vec, vec],
        out_specs=pl.BlockSpec((1, l, c), lambda bb: (bb, 0, 0)),
        out_shape=jax.ShapeDtypeStruct((b, l, c), BF16),
        scratch_shapes=[pltpu.VMEM((CONV_PAD + l + CONV_TAIL, c), F32),
                        pltpu.VMEM((SUBLANES - 1, copy_rows, c), F32)],
        compiler_params=_params("arbitrary"),
        name="conformer",
    )(ua, dw_w, dw_b.reshape(1, c), ln_g.reshape(1, c), ln_b.reshape(1, c))


SHORT_PAD = 8
SHORT_ROWS = 64


def _short_body(u_ref, w_ref, b_ref, o_ref, xp_ref, *, seq, n_taps):
    c = MIX_CH
    r = SHORT_ROWS
    half = (n_taps - 1) // 2
    zero = jnp.zeros((SHORT_PAD, c), F32)
    xp_ref[0:SHORT_PAD, :] = zero
    xp_ref[SHORT_PAD + seq:SHORT_PAD + seq + SHORT_PAD, :] = zero
    for q in range(u_ref.shape[0]):
        def fill(i, carry):
            r0 = pl.multiple_of(i * r, r)
            xp_ref[pl.ds(SHORT_PAD + r0, r), :] = u_ref[q, pl.ds(r0, r), :]
            return carry

        lax.fori_loop(0, seq // r, fill, 0)

        def tile(i, carry):
            r0 = pl.multiple_of(i * r, r)
            win = xp_ref[pl.ds(r0, r + 2 * SHORT_PAD), :]
            o_ref[q, pl.ds(r0, r), :] = _dw_tile(win, w_ref, n_taps, SHORT_PAD - half, r) + b_ref[...]
            return carry

        lax.fori_loop(0, seq // r, tile, 0)


SHORT_BLOCK_ROWS = 1024


def hyena_short_conv(ub, short_w, short_b):
    b, l, c3 = ub.shape
    c = MIX_CH
    k = short_w.shape[0]
    nbat = math.gcd(b, max(1, SHORT_BLOCK_ROWS // l))
    return pl.pallas_call(
        functools.partial(_short_body, seq=l, n_taps=k),
        grid=(b // nbat, c3 // c),
        in_specs=[pl.BlockSpec((nbat, l, c), lambda bb, j: (bb, 0, j)),
                  pl.BlockSpec((k, c), lambda bb, j: (0, j)),
                  pl.BlockSpec((1, c), lambda bb, j: (0, j))],
        out_specs=pl.BlockSpec((nbat, l, c), lambda bb, j: (bb, 0, j)),
        out_shape=jax.ShapeDtypeStruct((b, l, c3), F32),
        scratch_shapes=[pltpu.VMEM((l + 2 * SHORT_PAD, c), F32)],
        compiler_params=_params("arbitrary", "arbitrary"),
        name="hyena_short",
    )(ub, short_w, short_b.reshape(1, c3))


def _fft_sizes(seq):
    n = 2 * seq
    n2 = 128 if n >= 4096 else 32
    return n, n // n2, n2


FFT_GROUP = SUBLANES


def _kron_matrices(n, n1, h):
    k1 = np.arange(n1)[:, None]
    i1 = np.arange(h)[None, :]
    th = 2.0 * np.pi * ((k1 * i1) % n1) / n1
    eye = np.eye(FFT_GROUP)
    cs, sn = np.kron(np.cos(th), eye), np.kron(np.sin(th), eye)
    fwd = np.block([[cs, sn], [-sn, cs]])
    inv = np.block([[cs.T, -sn.T], [sn.T, cs.T]]) / n
    return fwd.astype(np.float32), inv.astype(np.float32)


def _twiddle_tables(n, n1, n2):
    k1 = jnp.arange(n1, dtype=jnp.int32)[:, None]
    i2 = jnp.arange(n2, dtype=jnp.int32)[None, :]
    th = ((k1 * i2) % n).astype(F32) * (2.0 * math.pi / n)
    shape = (n1, n2 // FFT_GROUP, FFT_GROUP, LANES)
    full = lambda a: jnp.broadcast_to(a.reshape(shape[:3] + (1,)), shape)
    return full(jnp.cos(th)), full(jnp.sin(th))


def _mid_tables(n2):
    a = np.arange(n2)
    th = 2.0 * np.pi * ((a[:, None] * a[None, :]) % n2) / n2
    cs, sn = np.cos(th), np.sin(th)
    fwd = np.concatenate([np.concatenate([cs, sn], 1), np.concatenate([-sn, cs], 1)], 0)
    inv = np.concatenate([np.concatenate([cs, -sn], 1), np.concatenate([sn, cs], 1)], 0)
    return fwd.astype(np.float32), inv.astype(np.float32)


MID_CHUNK = 16


def _dot(a, b):
    return jnp.dot(a.astype(BF16), b.astype(BF16), preferred_element_type=F32)


def _filter_stage1_body(x_ref, m_ref, c_ref, s_ref, o_ref, *, n1, gs):
    j, c = FFT_GROUP, MIX_CH
    for s in range(gs):
        a = _dot(m_ref[...], x_ref[:, s].reshape(n1 * j, c))
        ar = a[:n1 * j].reshape(n1, j, c)
        ai = a[n1 * j:].reshape(n1, j, c)
        cs, sn = _lanes(c_ref[:, s], c), _lanes(s_ref[:, s], c)
        o_ref[0, 0, :, s] = ar * cs + ai * sn
        o_ref[0, 1, :, s] = ai * cs - ar * sn


def filter_stage1(k, order, mat, twc, tws, *, n1, n2):
    g, j, c = n2 // FFT_GROUP, FFT_GROUP, MIX_CH
    gs = _group_step(g)
    tw = pl.BlockSpec((n1, gs, j, LANES), lambda gi: (0, gi, 0, 0))
    out = pl.pallas_call(
        functools.partial(_filter_stage1_body, n1=n1, gs=gs),
        grid=(g // gs,),
        in_specs=[pl.BlockSpec((n1, gs, j, c), lambda gi: (0, gi, 0, order)),
                  pl.BlockSpec(mat.shape, lambda gi: (0, 0)), tw, tw],
        out_specs=pl.BlockSpec((1, 2, n1, gs, j, c), lambda gi: (0, 0, 0, gi, 0, 0)),
        out_shape=jax.ShapeDtypeStruct((1, 2, n1, g, j, c), F32),
        compiler_params=_params("arbitrary"),
        name="filter_stage1",
    )(k.reshape(n1, g, j, k.shape[-1]), mat, twc, tws)
    return out.reshape(1, 2, n1 * n2, c)


def _mid_body(a_ref, k_ref, f_ref, g_ref, o_ref, *, n2, kc):
    for j in range(kc):
        rows = slice(j * n2, (j + 1) * n2)
        blk = jnp.concatenate([a_ref[0, 0, rows, :], a_ref[0, 1, rows, :]], axis=0)
        s = _dot(f_ref[...], blk)
        sr, si = s[:n2], s[n2:]
        kr, ki = k_ref[0, rows, :], k_ref[1, rows, :]
        y = jnp.concatenate([sr * kr - si * ki, sr * ki + si * kr], axis=0)
        bb = _dot(g_ref[...], y)
        o_ref[0, 0, rows, :] = bb[:n2].astype(o_ref.dtype)
        o_ref[0, 1, rows, :] = bb[n2:].astype(o_ref.dtype)


def fft_mid(a, kf, f_fwd, f_inv, *, n1, n2):
    p = a.shape[0]
    c = MIX_CH
    kc = min(n1, MID_CHUNK)
    rows = kc * n2
    blk = pl.BlockSpec((1, 2, rows, c), lambda j, pp: (pp, 0, j, 0))
    mat = pl.BlockSpec(f_fwd.shape, lambda j, pp: (0, 0))
    return pl.pallas_call(
        functools.partial(_mid_body, n2=n2, kc=kc),
        grid=(n1 // kc, p),
        in_specs=[blk, pl.BlockSpec((2, rows, c), lambda j, pp: (0, j, 0)), mat, mat],
        out_specs=blk,
        out_shape=jax.ShapeDtypeStruct(a.shape, a.dtype),
        compiler_params=_params("arbitrary", "arbitrary"),
        name="fft_mid",
    )(a, kf, f_fwd, f_inv)


def _filter_mid_body(a_ref, f_ref, sum_ref, o_ref, *, n2, kc):
    inv = 1.0 / (sum_ref[0:1, :] + EPS)
    for j in range(kc):
        rows = slice(j * n2, (j + 1) * n2)
        blk = jnp.concatenate([a_ref[0, 0, rows, :], a_ref[0, 1, rows, :]], axis=0)
        s = _dot(f_ref[...], blk) * inv
        o_ref[0, rows, :] = s[:n2]
        o_ref[1, rows, :] = s[n2:]


def filter_mid(a, f_fwd, abs_sum, order, *, n1, n2):
    c = MIX_CH
    kc = min(n1, MID_CHUNK)
    rows = kc * n2
    return pl.pallas_call(
        functools.partial(_filter_mid_body, n2=n2, kc=kc),
        grid=(n1 // kc,),
        in_specs=[pl.BlockSpec((1, 2, rows, c), lambda j: (0, 0, j, 0)),
                  pl.BlockSpec(f_fwd.shape, lambda j: (0, 0)),
                  pl.BlockSpec((SUBLANES, c), lambda j: (0, order))],
        out_specs=pl.BlockSpec((2, rows, c), lambda j: (0, j, 0)),
        out_shape=jax.ShapeDtypeStruct((2, n1 * n2, c), F32),
        compiler_params=_params("arbitrary"),
        name="filter_mid",
    )(a, f_fwd, abs_sum)


def _lanes(t, width):
    return t if width == LANES else jnp.concatenate([t] * (width // LANES), axis=-1)


WORK_ROWS = 2 * FFT_GROUP


def _store_group_pair(o_ref, parts, s):
    if s % 2 == 1:
        for plane in range(2):
            o_ref[0, plane, :, s // 2] = jnp.concatenate([parts[s - 1][plane], parts[s][plane]],
                                                         axis=1).astype(o_ref.dtype)


def _kron1_body(zr_ref, zi_ref, m_ref, c_ref, s_ref, o_ref, *, n1, h, gs):
    j, c = FFT_GROUP, MIX_CH
    parts = []
    for s in range(gs):
        xr = zr_ref[0, :, s].reshape(h * j, c)
        xi = zi_ref[0, :, s].reshape(h * j, c)
        a = _dot(m_ref[...], jnp.concatenate([xr, xi], axis=0))
        ar = a[:n1 * j].reshape(n1, j, c)
        ai = a[n1 * j:].reshape(n1, j, c)
        cs, sn = _lanes(c_ref[:, s], c), _lanes(s_ref[:, s], c)
        parts.append((ar * cs + ai * sn, ai * cs - ar * sn))
        _store_group_pair(o_ref, parts, s)


def _group_step(n_groups):
    return min(n_groups, 8)


def kron_stage1(z5, col, n_pairs, imag_offset, mat, twc, tws, *, n1, h):
    g, j, c = z5.shape[2], FFT_GROUP, MIX_CH
    gs = _group_step(g)
    tw = pl.BlockSpec((n1, gs, j, LANES), lambda gi, p: (0, gi, 0, 0))
    return pl.pallas_call(
        functools.partial(_kron1_body, n1=n1, h=h, gs=gs),
        grid=(g // gs, n_pairs),
        in_specs=[pl.BlockSpec((1, h, gs, j, c), lambda gi, p: (p, 0, gi, 0, col)),
                  pl.BlockSpec((1, h, gs, j, c), lambda gi, p: (p + imag_offset, 0, gi, 0, col)),
                  pl.BlockSpec(mat.shape, lambda gi, p: (0, 0)), tw, tw],
        out_specs=pl.BlockSpec((1, 2, n1, gs // 2, WORK_ROWS, c), lambda gi, p: (p, 0, 0, gi, 0, 0)),
        out_shape=jax.ShapeDtypeStruct((n_pairs, 2, n1, g // 2, WORK_ROWS, c), BF16),
        compiler_params=_params("arbitrary", "arbitrary"),
        name="fft_kron1",
    )(z5, z5, mat, twc, tws)


def _kron3_body(b_ref, m_ref, c_ref, s_ref, zr_ref, zi_ref, gr_ref, gi_ref, d_ref, o_ref, *, n1, h, gs):
    j, c = FFT_GROUP, MIX_CH
    d = d_ref[...].reshape(1, 1, c)
    for s in range(gs):
        half = slice((s % 2) * j, (s % 2 + 1) * j)
        br = b_ref[0, 0, :, s // 2].astype(F32)[:, half]
        bi = b_ref[0, 1, :, s // 2].astype(F32)[:, half]
        cs, sn = _lanes(c_ref[:, s], c), _lanes(s_ref[:, s], c)
        xr = (br * cs - bi * sn).reshape(n1 * j, c)
        xi = (br * sn + bi * cs).reshape(n1 * j, c)
        y = _dot(m_ref[...], jnp.concatenate([xr, xi], axis=0))
        yr = y[:h * j].reshape(h, j, c)
        yi = y[h * j:].reshape(h, j, c)
        o_ref[0, 0, :, s] = gr_ref[0, :, s] * (yr + d * zr_ref[0, :, s])
        o_ref[1, 0, :, s] = gi_ref[0, :, s] * (yi + d * zi_ref[0, :, s])


def kron_stage3(bw, mat, twc, tws, z5, z_col, g5, g_col, d_vec, imag_offset, *, n1, h):
    p = bw.shape[0]
    g, j, c = bw.shape[3] * 2, FFT_GROUP, MIX_CH
    gs = _group_step(g)
    tw = pl.BlockSpec((n1, gs, j, LANES), lambda gi, pp: (0, gi, 0, 0))

    def src(col, off):
        return pl.BlockSpec((1, h, gs, j, c), lambda gi, pp: (pp + off, 0, gi, 0, col))

    return pl.pallas_call(
        functools.partial(_kron3_body, n1=n1, h=h, gs=gs),
        grid=(g // gs, p),
        in_specs=[pl.BlockSpec((1, 2, n1, gs // 2, WORK_ROWS, c), lambda gi, pp: (pp, 0, 0, gi, 0, 0)),
                  pl.BlockSpec(mat.shape, lambda gi, pp: (0, 0)), tw, tw,
                  src(z_col, 0), src(z_col, imag_offset), src(g_col, 0), src(g_col, imag_offset),
                  pl.BlockSpec((1, c), lambda gi, pp: (0, 0))],
        out_specs=pl.BlockSpec((2, 1, h, gs, j, c), lambda gi, pp: (0, pp, 0, gi, 0, 0)),
        out_shape=jax.ShapeDtypeStruct((2, p, h, g, j, c), F32),
        compiler_params=_params("arbitrary", "arbitrary"),
        name="fft_kron3",
    )(bw, mat, twc, tws, z5, z5, g5, g5, d_vec.reshape(1, c))


FILTER_TILE = 512


def _lane_dense(fn, a):
    r, w = a.shape
    f = LANES // w
    rows = r // f
    dense = jnp.concatenate([a[q * rows:(q + 1) * rows] for q in range(f)], axis=1)
    out = fn(dense)
    return jnp.concatenate([out[:, q * w:(q + 1) * w] for q in range(f)], axis=0)


def _filter_body(fw1_ref, fb1_ref, fr1_ref, fw2_ref, fb2_ref, fr2_ref, fw3_ref, fb3_ref, bands_ref, decay_ref,
                 k_ref, sum_ref, *, seq):
    i = pl.program_id(0)
    c = MIX_CH
    hp = lax.Precision.HIGHEST
    n = i * FILTER_TILE + lax.broadcasted_iota(jnp.int32, (FILTER_TILE, 1), 0)
    pos = jnp.where(n <= seq, n, 2 * seq - n).astype(F32)
    t = pos * (1.0 / (seq - 1))
    ang = (pos * (2.0 * math.pi / seq)) * bands_ref[...]
    nb = bands_ref.shape[1]
    pre = (t * fw1_ref[0:1, :]
           + jnp.dot(_lane_dense(jnp.cos, ang), fw1_ref[1:1 + nb, :], preferred_element_type=F32, precision=hp)
           - jnp.dot(_lane_dense(jnp.sin, ang), fw1_ref[1 + nb:1 + 2 * nb, :], preferred_element_type=F32,
                     precision=hp)
           + fb1_ref[...])
    h = _lane_dense(jnp.sin, fr1_ref[...] * pre)
    h = _lane_dense(jnp.sin, fr2_ref[...] * (jnp.dot(h, fw2_ref[...], preferred_element_type=F32, precision=hp)
                                             + fb2_ref[...]))
    h = jnp.dot(h, fw3_ref[...], preferred_element_type=F32, precision=hp) + fb3_ref[...]
    win = jnp.exp(-t * decay_ref[...]) + HYENA_SHIFT
    win = jnp.concatenate([win, win], axis=1)
    fwd, bwd = h[:, :2 * c], h[:, 2 * c:]
    k = jnp.where(n < seq, fwd, bwd) + jnp.where(n == 0, bwd, 0.0)
    k = jnp.where(n == seq, 0.0, k) * win
    k_ref[...] = k

    @pl.when(i == 0)
    def _():
        sum_ref[...] = jnp.zeros(sum_ref.shape, F32)

    sum_ref[...] = sum_ref[...] + jnp.sum(jnp.abs(k), axis=0, keepdims=True)


def hyena_filter_time(seq, fw1, fb1, fr1, fw2, fb2, fr2, fw3, fb3):
    c = MIX_CH
    n_bands = (fw1.shape[0] - 1) // 2
    bands = jnp.linspace(1e-4, n_bands - 1, n_bands, dtype=F32)[None, :]
    max_decay = math.log(HYENA_DECAY_TARGET) / HYENA_FAST_PCT
    min_decay = math.log(HYENA_DECAY_TARGET) / HYENA_SLOW_PCT
    decay = jnp.abs(jnp.linspace(min_decay, max_decay, c, dtype=F32))[None, :]
    assert (2 * seq) % FILTER_TILE == 0
    args = [fw1, fb1[None, :], fr1[None, :], fw2, fb2[None, :], fr2[None, :], fw3, fb3[None, :], bands, decay]
    return pl.pallas_call(
        functools.partial(_filter_body, seq=seq),
        grid=(2 * seq // FILTER_TILE,),
        in_specs=[pl.BlockSpec(a.shape, lambda i: (0, 0)) for a in args],
        out_specs=[pl.BlockSpec((FILTER_TILE, 2 * c), lambda i: (i, 0)),
                   pl.BlockSpec((SUBLANES, 2 * c), lambda i: (0, 0))],
        out_shape=[jax.ShapeDtypeStruct((2 * seq, 2 * c), F32), jax.ShapeDtypeStruct((SUBLANES, 2 * c), F32)],
        compiler_params=_params("arbitrary"),
        name="hyena_filter",
    )(*args)


def hyena_filter_spectrum(seq, filt, tabs):
    n, n1, n2 = _fft_sizes(seq)
    k, abs_sum = hyena_filter_time(seq, *filt)
    twc, tws, f_fwd, m_real = tabs[2], tabs[3], tabs[4], tabs[6]
    return [filter_mid(filter_stage1(k, o, m_real, twc, tws, n1=n1, n2=n2), f_fwd, abs_sum, o, n1=n1, n2=n2)
            for o in range(2)]


def hyena_mixer(ub, short_w, short_b, hy_d, kf, tabs):
    b, l, _ = ub.shape
    n, n1, n2 = _fft_sizes(l)
    h = n1 // 2
    p = b // 2
    c = MIX_CH
    g = n2 // FFT_GROUP
    u = hyena_short_conv(ub, short_w, short_b)
    u5 = u.reshape(b, h, g, FFT_GROUP, u.shape[-1])
    m_fwd, m_inv, twc, tws, f_fwd, f_inv = tabs[:6]
    z5 = u5
    for o in range(2):
        a = kron_stage1(z5, 0, p, p, m_fwd, twc, tws, n1=n1, h=h)
        bw = fft_mid(a.reshape(p, 2, n, c), kf[o], f_fwd, f_inv, n1=n1, n2=n2)
        z = kron_stage3(bw.reshape(a.shape), m_inv, twc, tws, z5, 0, u5, 1 + o, hy_d[o], p, n1=n1, h=h)
        z5 = z.reshape(b, h, g, FFT_GROUP, c)
    return z5.reshape(b, l, c)


def hyena_tables(seq):
    n, n1, n2 = _fft_sizes(seq)
    m_fwd, m_inv = _kron_matrices(n, n1, n1 // 2)
    m_real = _kron_matrices(n, n1, n1)[0][:, :n1 * FFT_GROUP]
    twc, tws = _twiddle_tables(n, n1, n2)
    f_fwd, f_inv = _mid_tables(n2)
    bf = lambda a: jnp.asarray(a).astype(BF16)
    return bf(m_fwd), bf(m_inv), twc, tws, bf(f_fwd), bf(f_inv), bf(m_real)


def _fnet_body(cl_ref, sl_ref, x_ref, cc_ref, sc_ref, o_ref, *, scale):
    x = x_ref[0].astype(BF16)
    pr = jnp.dot(cl_ref[...], x, preferred_element_type=F32).astype(BF16)
    qr = jnp.dot(sl_ref[...], x, preferred_element_type=F32).astype(BF16)
    o_ref[0] = (jnp.dot(pr, cc_ref[...], preferred_element_type=F32)
                - jnp.dot(qr, sc_ref[...], preferred_element_type=F32)) * scale


def _dft_tables(n):
    a = jnp.arange(n, dtype=jnp.int32)
    th = ((a[:, None] * a[None, :]) % n).astype(F32) * (2.0 * math.pi / n)
    return jnp.cos(th).astype(BF16), jnp.sin(th).astype(BF16)


FNET_DIRECT_MAX = 1024


def _fnet1_body(x_ref, ccs_ref, m_ref, c_ref, s_ref, o_ref, *, n1, gs):
    j, c = FFT_GROUP, MIX_CH
    parts = []
    for s in range(gs):
        x = x_ref[0, :, s].reshape(n1 * j, c).astype(BF16)
        z = jnp.dot(x, ccs_ref[...], preferred_element_type=F32)
        a = _dot(m_ref[...], jnp.concatenate([z[:, :c], z[:, c:]], axis=0))
        ar = a[:n1 * j].reshape(n1, j, c)
        ai = a[n1 * j:].reshape(n1, j, c)
        cs, sn = _lanes(c_ref[:, s], c), _lanes(s_ref[:, s], c)
        parts.append((ar * cs + ai * sn, ai * cs - ar * sn))
        _store_group_pair(o_ref, parts, s)


def _fnet2_body(a_ref, m_ref, o_ref, *, n2, scale):
    j, c = FFT_GROUP, MIX_CH
    x = a_ref[0].reshape(2 * j * n2, c)
    y = _dot(m_ref[...], x) * scale
    o_ref[0, :, 0] = y.reshape(n2, j, c)


def fnet_tables(seq):
    c = MIX_CH
    cc, sc = _dft_tables(c)
    if seq <= FNET_DIRECT_MAX:
        return _dft_tables(seq) + (cc, sc)
    n2 = 128
    n1 = seq // n2
    m_fwd, _ = _kron_matrices(seq, n1, n1)
    twc, tws = _twiddle_tables(seq, n1, n2)
    a = np.arange(n2)
    th = 2.0 * np.pi * ((a[:, None] * a[None, :]) % n2) / n2
    eye = np.eye(FFT_GROUP)
    m2 = np.concatenate([np.einsum('kn,ij->kijn', f, eye).reshape(n2 * FFT_GROUP, FFT_GROUP * n2)
                         for f in (np.cos(th), np.sin(th))], axis=1).astype(np.float32)
    ccs = jnp.concatenate([cc, -sc], axis=1)
    return ccs, jnp.asarray(m_fwd).astype(BF16), twc, tws, jnp.asarray(m2).astype(BF16)


def fnet_two_stage(uc, tables):
    b, l, c = uc.shape
    ccs, m_fwd, twc, tws, m2 = tables
    j = FFT_GROUP
    n2 = 128
    n1 = l // n2
    g = n2 // j
    gs = _group_step(g)
    tw = pl.BlockSpec((n1, gs, j, LANES), lambda gi, bb: (0, gi, 0, 0))
    a = pl.pallas_call(
        functools.partial(_fnet1_body, n1=n1, gs=gs),
        grid=(g // gs, b),
        in_specs=[pl.BlockSpec((1, n1, gs, j, c), lambda gi, bb: (bb, 0, gi, 0, 0)),
                  pl.BlockSpec(ccs.shape, lambda gi, bb: (0, 0)),
                  pl.BlockSpec(m_fwd.shape, lambda gi, bb: (0, 0)), tw, tw],
        out_specs=pl.BlockSpec((1, 2, n1, gs // 2, WORK_ROWS, c), lambda gi, bb: (bb, 0, 0, gi, 0, 0)),
        out_shape=jax.ShapeDtypeStruct((b, 2, n1, g // 2, WORK_ROWS, c), BF16),
        compiler_params=_params("arbitrary", "arbitrary"),
        name="fnet_stage1",
    )(uc.reshape(b, n1, g, j, c), ccs, m_fwd, twc, tws)
    out = pl.pallas_call(
        functools.partial(_fnet2_body, n2=n2, scale=1.0 / math.sqrt(l * c)),
        grid=(b, n1 // j),
        in_specs=[pl.BlockSpec((1, 2, j, n2, c), lambda bb, q: (bb, 0, q, 0, 0)),
                  pl.BlockSpec(m2.shape, lambda bb, q: (0, 0))],
        out_specs=pl.BlockSpec((1, n2, 1, j, c), lambda bb, q: (bb, 0, q, 0, 0)),
        out_shape=jax.ShapeDtypeStruct((b, n2, n1 // j, j, c), F32),
        compiler_params=_params("arbitrary", "arbitrary"),
        name="fnet_stage2",
    )(a.reshape(b, 2, n1, n2, c), m2)
    return out.reshape(b, l, c)


def fnet_mixer(uc, tables):
    b, l, c = uc.shape
    if l > FNET_DIRECT_MAX:
        return fnet_two_stage(uc, tables)
    cl, sl, cc, sc = tables
    tm = min(l, 512)
    row = pl.BlockSpec((tm, l), lambda i, bb: (i, 0))
    sq = pl.BlockSpec((c, c), lambda i, bb: (0, 0))
    return pl.pallas_call(
        functools.partial(_fnet_body, scale=1.0 / math.sqrt(l * c)),
        grid=(l // tm, b),
        in_specs=[row, row, pl.BlockSpec((1, l, c), lambda i, bb: (bb, 0, 0)), sq, sq],
        out_specs=pl.BlockSpec((1, tm, c), lambda i, bb: (bb, i, 0)),
        out_shape=jax.ShapeDtypeStruct((b, l, c), F32),
        compiler_params=_params("arbitrary", "arbitrary"),
        name="fnet",
    )(cl, sl, uc, cc, sc)


def _heads_rows(x, g):
    h0 = Q_PER_KV * g
    return jnp.concatenate([x[:, (h0 + r) * HEAD_DIM:(h0 + r + 1) * HEAD_DIM] for r in range(Q_PER_KV)], axis=0)


def _qk(q, k):
    return lax.dot_general(q.astype(BF16), k.astype(BF16), (((1,), (1,)), ((), ())),
                           preferred_element_type=F32)


def _sink_col(sink_ref, g, rows):
    ridx = lax.broadcasted_iota(jnp.int32, (Q_PER_KV * rows, 1), 0)
    col = jnp.full((Q_PER_KV * rows, 1), sink_ref[Q_PER_KV * g], F32)
    for r in range(1, Q_PER_KV):
        col = jnp.where(ridx >= r * rows, sink_ref[Q_PER_KV * g + r], col)
    return col * LOG2E


def _lat_attn_body(sink_ref, q_ref, qr_ref, kp_ref, kc_ref, kn_ref, ck_ref, cv_ref, o_ref, *, sub):
    i = pl.program_id(1)
    n_qblk = pl.num_programs(1) * sub
    blk = ATT_BLOCK
    span = blk + 2 * WINDOW
    kv = jnp.concatenate([kp_ref[0], kc_ref[0], kn_ref[0]], axis=0)
    ck = ck_ref[0, 0].astype(BF16)
    cv = cv_ref[0, 0].astype(BF16)
    kvw = N_KV_HEADS * HEAD_DIM
    r = lax.broadcasted_iota(jnp.int32, (Q_PER_KV * blk, span), 0) % blk
    j = lax.broadcasted_iota(jnp.int32, (Q_PER_KV * blk, span), 1)
    band = (j >= r) & (j <= r + 2 * WINDOW)
    for s in range(sub):
        qi = i * sub + s
        ok = band & ((qi > 0) | (j >= WINDOW)) & ((qi < n_qblk - 1) | (j < WINDOW + blk))
        q = q_ref[0, s * blk:(s + 1) * blk, :]
        qr = qr_ref[0, s * blk:(s + 1) * blk, :]
        outs = []
        for g in range(N_KV_HEADS):
            kl = kv[s * blk:s * blk + span, g * HEAD_DIM:(g + 1) * HEAD_DIM]
            vl = kv[s * blk:s * blk + span, kvw + g * HEAD_DIM:kvw + (g + 1) * HEAD_DIM]
            s_loc = jnp.where(ok, _qk(_heads_rows(qr, g), kl), NEG_INF)
            s_ctx = _qk(_heads_rows(q, g), ck[:, g * HEAD_DIM:(g + 1) * HEAD_DIM])
            sink = _sink_col(sink_ref, g, blk)
            m = jnp.maximum(jnp.maximum(jnp.max(s_loc, axis=-1, keepdims=True),
                                        jnp.max(s_ctx, axis=-1, keepdims=True)), sink)
            e_loc = jnp.exp2(s_loc - m)
            e_ctx = jnp.exp2(s_ctx - m)
            den = (jnp.sum(e_loc, axis=-1, keepdims=True) + jnp.sum(e_ctx, axis=-1, keepdims=True)
                   + jnp.exp2(sink - m))
            o = (jnp.dot(e_loc.astype(BF16), vl, preferred_element_type=F32)
                 + jnp.dot(e_ctx.astype(BF16), cv[:, g * HEAD_DIM:(g + 1) * HEAD_DIM],
                           preferred_element_type=F32)) * (1.0 / den)
            outs += [o[rr * blk:(rr + 1) * blk] for rr in range(Q_PER_KV)]
        o_ref[0, s * blk:(s + 1) * blk, :] = jnp.concatenate(outs, axis=1).astype(o_ref.dtype)


ATT_SUB = 4


def latent_attention(uq, uqr, ukv, cache_k, cache_v, layer, sink):
    b, l, c = uq.shape
    p = cache_k.shape[2]
    blk = ATT_BLOCK
    nblk = l // blk
    sub = math.gcd(ATT_SUB, nblk)
    rows = sub * blk
    qspec = pl.BlockSpec((1, rows, c), lambda bb, i: (bb, i, 0))
    cspec = pl.BlockSpec((1, 1, p, cache_k.shape[3]), lambda bb, i: (bb, layer, 0, 0))
    return pl.pallas_call(
        functools.partial(_lat_attn_body, sub=sub),
        grid=(b, nblk // sub),
        in_specs=[pl.BlockSpec(memory_space=pltpu.SMEM), qspec, qspec,
                  pl.BlockSpec((1, blk, c), lambda bb, i: (bb, jnp.maximum(i * sub - 1, 0), 0)),
                  qspec,
                  pl.BlockSpec((1, blk, c), lambda bb, i: (bb, jnp.minimum((i + 1) * sub, nblk - 1), 0)),
                  cspec, cspec],
        out_specs=qspec,
        out_shape=jax.ShapeDtypeStruct((b, l, c), BF16),
        compiler_params=_params("arbitrary", "arbitrary"),
        name="latent_attention",
    )(sink, uq, uqr, ukv, ukv, ukv, cache_k, cache_v)


def _ctx_attn_body(sink_ref, q_ref, kv_ref, o_ref, *, seq):
    q = q_ref[0]
    kv = kv_ref[0]
    kvw = N_KV_HEADS * HEAD_DIM
    outs = []
    for g in range(N_KV_HEADS):
        kl = kv[:, g * HEAD_DIM:(g + 1) * HEAD_DIM]
        vl = kv[:, kvw + g * HEAD_DIM:kvw + (g + 1) * HEAD_DIM]
        s = _qk(_heads_rows(q, g), kl)
        sink = _sink_col(sink_ref, g, seq)
        m = jnp.maximum(jnp.max(s, axis=-1, keepdims=True), sink)
        e = jnp.exp2(s - m)
        den = jnp.sum(e, axis=-1, keepdims=True) + jnp.exp2(sink - m)
        o = jnp.dot(e.astype(BF16), vl.astype(BF16), preferred_element_type=F32) * (1.0 / den)
        outs += [o[rr * seq:(rr + 1) * seq] for rr in range(Q_PER_KV)]
    o_ref[0] = jnp.concatenate(outs, axis=1)


def context_attention(uq, ukv, sink):
    b, s, c = uq.shape
    spec = pl.BlockSpec((1, s, c), lambda bb: (bb, 0, 0))
    return pl.pallas_call(
        functools.partial(_ctx_attn_body, seq=s),
        grid=(b,),
        in_specs=[pl.BlockSpec(memory_space=pltpu.SMEM), spec, spec],
        out_specs=spec,
        out_shape=jax.ShapeDtypeStruct((b, s, c), F32),
        compiler_params=_params("arbitrary"),
        name="context_attention",
    )(sink, uq, ukv)


def _pack_bf16_pairs(hi_rounded):
    k = hi_rounded.shape[1] // 2
    bits = lax.bitcast_convert_type(hi_rounded, jnp.uint32)
    return bits[:, :k] | (bits[:, k:] >> 16)


def _unpack_bf16_pairs(packed, dtype=BF16):
    a = lax.bitcast_convert_type(packed & jnp.uint32(0xFFFF0000), F32)
    b = lax.bitcast_convert_type(packed << 16, F32)
    return jnp.concatenate([a, b], axis=1).astype(dtype)


def _out_body(ya_ref, yb_ref, yc_ref, yd_ref, x_ref, mod_ref, g_ref, w_ref, rw_ref, rb_ref,
              x1_ref, h_ref, route_ref, cnt_ref):
    c = MIX_CH
    y = jnp.dot(ya_ref[0].astype(BF16), w_ref[0:c, :], preferred_element_type=F32)
    for j, ref in enumerate((yb_ref, yc_ref, yd_ref), start=1):
        y = y + jnp.dot(ref[0].astype(BF16), w_ref[j * c:(j + 1) * c, :], preferred_element_type=F32)
    x1 = x_ref[0] + mod_ref[0, 2:3, :] * y
    x1_ref[0] = x1
    h = _rmsnorm_mod(x1, g_ref[...], mod_ref[0, 4:5, :], mod_ref[0, 3:4, :])
    h_hi = h.astype(BF16)
    h_hi32 = h_hi.astype(F32)
    h_ref[0] = _pack_bf16_pairs(h_hi32)
    h_lo = (h - h_hi32).astype(BF16)
    tm = h.shape[0]
    prod = jnp.dot(jnp.concatenate([h_hi, h_lo], axis=0), rw_ref[...], preferred_element_type=F32)
    logits = (prod[:tm, :ROUTE_LANES] + prod[:tm, ROUTE_LANES:]
              + prod[tm:, :ROUTE_LANES] + prod[tm:, ROUTE_LANES:]) + rb_ref[...]
    lane = lax.broadcasted_iota(jnp.int32, logits.shape, 1)
    is_c = lane < N_GROUPS
    lc = jnp.where(is_c, logits, NEG_INF)
    mc = jnp.max(lc, axis=-1, keepdims=True)
    grp = jnp.min(jnp.where(lc == mc, lane, ROUTE_LANES), axis=-1, keepdims=True)
    pg = 1.0 / jnp.sum(jnp.where(is_c, jnp.exp(lc - mc), 0.0), axis=-1, keepdims=True)
    lo = N_GROUPS + grp * EXPERTS_PER_GROUP
    in_g = (lane >= lo) & (lane < lo + EXPERTS_PER_GROUP)
    lf = jnp.where(in_g, logits, NEG_INF)
    t1 = jnp.max(lf, axis=-1, keepdims=True)
    i1 = jnp.min(jnp.where(lf == t1, lane, ROUTE_LANES), axis=-1, keepdims=True)
    lf2 = jnp.where(lane == i1, NEG_INF, lf)
    t2 = jnp.max(lf2, axis=-1, keepdims=True)
    i2 = jnp.min(jnp.where(lf2 == t2, lane, ROUTE_LANES), axis=-1, keepdims=True)
    e2 = jnp.exp(t2 - t1)
    w1 = pg / (1.0 + e2)
    w2 = pg * e2 / (1.0 + e2)
    rec = jnp.where(lane == 0, (i1 - N_GROUPS).astype(F32),
                    jnp.where(lane == 1, (i2 - N_GROUPS).astype(F32),
                              jnp.where(lane == 2, w1, jnp.where(lane == 3, w2, 0.0))))
    route_ref[0] = rec

    @pl.when((pl.program_id(0) == 0) & (pl.program_id(1) == 0))
    def _():
        cnt_ref[...] = jnp.zeros(cnt_ref.shape, F32)

    e0, e1 = _choice_onehots(rec)
    cnt_ref[...] = cnt_ref[...] + jnp.sum(e0 + e1, axis=0, keepdims=True)


def out_projection(ys, x, mods, mod_row0, norm_g, w_out_bf, rw, rb, *, tm):
    b, l, d = x.shape
    c = MIX_CH
    row = (lambda bb: 0) if mod_row0 is None else (lambda bb: mod_row0 + bb)
    yspec = pl.BlockSpec((1, tm, c), lambda bb, i: (bb, i, 0))
    xspec = pl.BlockSpec((1, tm, d), lambda bb, i: (bb, i, 0))
    return pl.pallas_call(
        _out_body,
        grid=(b, l // tm),
        in_specs=[yspec] * 4 + [xspec,
                                pl.BlockSpec((1, N_MOD, d), lambda bb, i: (row(bb), 0, 0)),
                                pl.BlockSpec((1, d), lambda bb, i: (0, 0)),
                                pl.BlockSpec(w_out_bf.shape, lambda bb, i: (0, 0)),
                                pl.BlockSpec(rw.shape, lambda bb, i: (0, 0)),
                                pl.BlockSpec(rb.shape, lambda bb, i: (0, 0))],
        out_specs=[xspec, pl.BlockSpec((1, tm, d // 2), lambda bb, i: (bb, i, 0)),
                   pl.BlockSpec((1, tm, ROUTE_LANES), lambda bb, i: (bb, i, 0)),
                   pl.BlockSpec((SUBLANES, ROUTE_LANES), lambda bb, i: (0, 0))],
        out_shape=[jax.ShapeDtypeStruct((b, l, d), F32), jax.ShapeDtypeStruct((b, l, d // 2), jnp.uint32),
                   jax.ShapeDtypeStruct((b, l, ROUTE_LANES), F32),
                   jax.ShapeDtypeStruct((SUBLANES, ROUTE_LANES), F32)],
        compiler_params=_params("arbitrary", "arbitrary"),
        name="out_proj",
    )(*ys, x, mods, norm_g.reshape(1, d), w_out_bf, rw, rb)


def _expert_body(be_ref, nv_ref, xs_ref, wg_ref, wu_ref, wd_ref, o_ref, wg_s, wu_s, wd_s):
    i = pl.program_id(0)
    prev = be_ref[jnp.maximum(i - 1, 0)]

    @pl.when((i == 0) | (be_ref[i] != prev))
    def _():
        wg_s[...] = wg_ref[0, 0].astype(BF16)
        wu_s[...] = wu_ref[0, 0].astype(BF16)
        wd_s[...] = wd_ref[0, 0].astype(BF16)

    @pl.when(nv_ref[i] > 0)
    def _():
        row = lax.broadcasted_iota(jnp.int32, xs_ref.shape, 0)
        x = _unpack_bf16_pairs(jnp.where(row < nv_ref[i], xs_ref[...], jnp.uint32(0)))
        g = jnp.dot(x, wg_s[...], preferred_element_type=F32)
        u = jnp.dot(x, wu_s[...], preferred_element_type=F32)
        a = (_silu(g) * u).astype(BF16)
        y = jnp.dot(a, wd_s[...], preferred_element_type=F32)
        o_ref[...] = _pack_bf16_pairs(y.astype(BF16).astype(F32))

    @pl.when(nv_ref[i] <= 0)
    def _():
        o_ref[...] = jnp.zeros(o_ref.shape, jnp.uint32)


def expert_ffn(xs, blk_e, n_valid, layer, e_gate, e_up, e_down):
    rows, dh = xs.shape
    d = 2 * dh
    nb = rows // MOE_BLOCK
    de = e_gate.shape[-1]
    grid_spec = pltpu.PrefetchScalarGridSpec(
        num_scalar_prefetch=2,
        grid=(nb,),
        in_specs=[pl.BlockSpec((MOE_BLOCK, dh), lambda i, be, nv: (i, 0)),
                  pl.BlockSpec((1, 1, d, de), lambda i, be, nv: (layer, be[i], 0, 0)),
                  pl.BlockSpec((1, 1, d, de), lambda i, be, nv: (layer, be[i], 0, 0)),
                  pl.BlockSpec((1, 1, de, d), lambda i, be, nv: (layer, be[i], 0, 0))],
        out_specs=pl.BlockSpec((MOE_BLOCK, dh), lambda i, be, nv: (i, 0)),
        scratch_shapes=[pltpu.VMEM((d, de), BF16), pltpu.VMEM((d, de), BF16), pltpu.VMEM((de, d), BF16)],
    )
    return pl.pallas_call(
        _expert_body, grid_spec=grid_spec,
        out_shape=jax.ShapeDtypeStruct((rows, dh), jnp.uint32),
        compiler_params=_params("arbitrary"),
        name="expert_ffn",
    )(blk_e, n_valid, xs, e_gate, e_up, e_down)


RANK_TILE = 512


def _choice_onehots(rec):
    lanef = lax.broadcasted_iota(jnp.int32, rec.shape, 1).astype(F32)
    return (lanef == rec[:, 0:1]).astype(F32), (lanef == rec[:, 1:2]).astype(F32)


def _slot_body(route_ref, cnt_ref, tri_ref, slot_ref, carry_ref):
    rec = route_ref[...]
    lane = lax.broadcasted_iota(jnp.int32, rec.shape, 1)
    e0, e1 = _choice_onehots(rec)
    both = e0 + e1

    @pl.when(pl.program_id(0) == 0)
    def _():
        cnt = cnt_ref[...]
        padded = jnp.floor((cnt + (MOE_BLOCK - 1)) * (1.0 / MOE_BLOCK)) * MOE_BLOCK
        ln = lax.broadcasted_iota(jnp.int32, cnt.shape, 1)
        incl = padded
        sh = 1
        while sh < ROUTE_LANES:
            incl = incl + jnp.where(ln >= sh, pltpu.roll(incl, sh, axis=1), 0.0)
            sh *= 2
        carry_ref[...] = incl - padded

    before = jnp.dot(tri_ref[...], both.astype(BF16), preferred_element_type=F32) + carry_ref[0:1, :]
    s0 = jnp.sum(e0 * before, axis=-1, keepdims=True)
    s1 = jnp.sum(e1 * before, axis=-1, keepdims=True)
    slot_ref[...] = jnp.where(lane == 0, s0, jnp.where(lane == 1, s1, 0.0))
    carry_ref[...] = carry_ref[...] + jnp.sum(both, axis=0, keepdims=True)


def moe_slots(route, counts):
    n = route.shape[0]
    t = RANK_TILE
    tri = jnp.asarray(np.tril(np.ones((t, t), np.float32), -1)).astype(BF16)
    return pl.pallas_call(
        _slot_body,
        grid=(n // t,),
        in_specs=[pl.BlockSpec((t, ROUTE_LANES), lambda i: (i, 0)),
                  pl.BlockSpec((SUBLANES, ROUTE_LANES), lambda i: (0, 0)),
                  pl.BlockSpec((t, t), lambda i: (0, 0))],
        out_specs=pl.BlockSpec((t, ROUTE_LANES), lambda i: (i, 0)),
        out_shape=jax.ShapeDtypeStruct((n, ROUTE_LANES), F32),
        scratch_shapes=[pltpu.VMEM((SUBLANES, ROUTE_LANES), F32)],
        compiler_params=_params("arbitrary"),
        name="moe_slots",
    )(route, counts, tri)


def _sc_mesh():
    return plsc.VectorSubcoreMesh(core_axis_name="c", subcore_axis_name="s")


def _sc_worker():
    return lax.axis_index("s") * SC_CORES + lax.axis_index("c")


DISPATCH_ROWS = 64
COMBINE_ROWS = 64


def sc_dispatch(rows, dest, n_slots):
    n, w = rows.shape
    ch = DISPATCH_ROWS
    per_w = n // SC_WORKERS
    n_ch = per_w // ch

    @functools.partial(
        pl.kernel, mesh=_sc_mesh(),
        out_type=jax.ShapeDtypeStruct((n_slots, w), rows.dtype),
        scratch_types=[pltpu.VMEM((ch,), jnp.int32), pltpu.VMEM((ch, w), rows.dtype)],
    )
    def scatter_kernel(rows_hbm, dest_hbm, out_hbm, idx_v, rows_v):
        wid = _sc_worker()

        @pl.loop(0, n_ch)
        def _(j):
            chunk = wid * n_ch + j
            pltpu.sync_copy(rows_hbm.at[pl.ds(pl.multiple_of(chunk * ch, ch), ch)], rows_v)
            for k in range(2):
                pltpu.sync_copy(dest_hbm.at[k, chunk], idx_v)
                pltpu.sync_copy(rows_v, out_hbm.at[idx_v])

    return scatter_kernel(rows, dest)


def sc_gather_rows(table, idx):
    s, w = table.shape
    m = idx.shape[0]
    ch = COMBINE_ROWS
    per_w = m // SC_WORKERS
    n_ch = per_w // ch

    @functools.partial(
        pl.kernel, mesh=_sc_mesh(),
        out_type=jax.ShapeDtypeStruct((m, w), table.dtype),
        scratch_types=[pltpu.VMEM((ch,), jnp.int32), pltpu.VMEM((ch, w), table.dtype), pltpu.SemaphoreType.DMA],
    )
    def gather_kernel(table_hbm, idx_hbm, out_hbm, idx_v, rows_v, sem):
        wid = _sc_worker()

        @pl.loop(0, n_ch)
        def _(j):
            off = pl.multiple_of((wid * n_ch + j) * ch, ch)
            pltpu.sync_copy(idx_hbm.at[pl.ds(off, ch)], idx_v)
            pltpu.async_copy(table_hbm.at[idx_v], rows_v, sem).wait()
            pltpu.sync_copy(rows_v, out_hbm.at[pl.ds(off, ch)])

    return gather_kernel(table, idx)


def hier_moe(h_packed, route, cnt, layer, e_gate, e_up, e_down):
    b, l, dh = h_packed.shape
    n = b * l
    assert n % (SC_WORKERS * DISPATCH_ROWS) == 0 and (2 * n) % (SC_WORKERS * COMBINE_ROWS) == 0
    slots = moe_slots(route.reshape(n, ROUTE_LANES), cnt)
    counts = cnt[0, :N_EXPERTS].astype(jnp.int32)
    padded = (counts + MOE_BLOCK - 1) // MOE_BLOCK * MOE_BLOCK
    pend = jnp.cumsum(padded)
    nb = -(-2 * n // MOE_BLOCK) + N_EXPERTS
    blk0 = jnp.arange(nb, dtype=jnp.int32) * MOE_BLOCK
    owner = pend[None, :] <= blk0[:, None]
    blk_e = jnp.minimum(jnp.sum(owner, axis=1), N_EXPERTS - 1).astype(jnp.int32)
    run_end = jnp.sum(jnp.where(jnp.arange(N_EXPERTS)[None, :] == blk_e[:, None],
                                (pend - padded + counts)[None, :], 0), axis=1)
    n_valid = jnp.clip(run_end - blk0, 0, MOE_BLOCK).astype(jnp.int32)
    dest = slots[:, 0:2].astype(jnp.int32).T
    xs = sc_dispatch(h_packed.reshape(n, dh), dest.reshape(2, n // DISPATCH_ROWS, DISPATCH_ROWS), nb * MOE_BLOCK)
    y = expert_ffn(xs, blk_e, n_valid, layer, e_gate, e_up, e_down)
    return sc_gather_rows(y, dest.reshape(2 * n)).reshape(2, b, l, dh)


def _final_body(x_ref, y0_ref, y1_ref, route_ref, pmod_ref, g_ref, o_ref):
    x = _moe_residual(x_ref[0], y0_ref, y1_ref, route_ref, pmod_ref)
    ms = jnp.mean(x * x, axis=-1, keepdims=True)
    o_ref[0] = x * lax.rsqrt(ms + EPS) * g_ref[...]


def final_norm(x1, res, mod_row0, norm_g, *, tm):
    b, l, d = x1.shape
    row = (lambda bb: 0) if mod_row0 is None else (lambda bb: mod_row0 + bb)
    xspec = pl.BlockSpec((1, tm, d), lambda bb, i: (bb, i, 0))
    rargs, rspecs = _residual_specs(res, tm, d, row)
    return pl.pallas_call(
        _final_body,
        grid=(b, l // tm),
        in_specs=[xspec] + rspecs + [pl.BlockSpec((1, d), lambda bb, i: (0, 0))],
        out_specs=xspec,
        out_shape=jax.ShapeDtypeStruct((b, l, d), F32),
        compiler_params=_params("arbitrary", "arbitrary"),
        name="final_norm",
    )(x1, *rargs, norm_g.reshape(1, d))


def _rope_tables(seq):
    rows = seq // GRID_W
    row_pos = jnp.repeat(jnp.arange(rows, dtype=F32), GRID_W)
    col_pos = jnp.tile(jnp.arange(GRID_W, dtype=F32), rows)
    n_freq = HEAD_DIM // 4
    inv = ROPE_BASE ** (-jnp.arange(n_freq, dtype=F32) / n_freq)
    ang = jnp.concatenate([row_pos[:, None] * inv, col_pos[:, None] * inv], axis=-1)
    cs, sn = jnp.cos(ang), jnp.sin(ang)
    cos_f = jnp.tile(jnp.concatenate([cs, cs], axis=-1), (1, N_Q_HEADS))
    sin_s = jnp.tile(jnp.concatenate([-sn, sn], axis=-1), (1, N_Q_HEADS))
    return cos_f, sin_s


def kernel(x_prompt, x_sample, cache_k, cache_v, c, c_ctx, ada_w, ada_b, norm1_g, norm2_g, w_in, conv_dw_w, conv_dw_b, conv_ln_g, conv_ln_b, hy_short_w, hy_short_b, hy_fw1, hy_fb1, hy_freq1, hy_fw2, hy_fb2, hy_freq2, hy_fw3, hy_fb3, hy_d, attn_sink, w_out, router_coarse_w, router_coarse_b, router_fine_w, router_fine_b, exp_gate, exp_up, exp_down, norm_f_g):
    depth = ada_w.shape[0]
    bp, lp, d = x_prompt.shape
    bs, ls, _ = x_sample.shape
    assert bp % 2 == 0 and bs % 2 == 0 and ls % ATT_BLOCK == 0 and ls % GRID_W == 0

    n_rows = -(-(1 + bs) // SUBLANES) * SUBLANES
    cvec = jnp.concatenate([c_ctx[None, :], c, jnp.zeros((n_rows - 1 - bs, d), F32)], axis=0)
    mods = adaln_all(cvec, ada_w, ada_b)

    rope = _rope_tables(ls)
    fnet_tabs, hy_tabs = {}, {}
    for seq in {lp, ls}:
        fnet_tabs[seq] = fnet_tables(seq)
        hy_tabs[seq] = hyena_tables(seq)
    ck = cache_k.reshape(cache_k.shape[0], depth, cache_k.shape[2], -1)
    cv = cache_v.reshape(cache_v.shape[0], depth, cache_v.shape[2], -1)
    pad = ROUTE_LANES - N_GROUPS - N_EXPERTS

    tm_p = min(lp, 512)
    tm_s = min(ls, 1024)
    xp, xs = x_prompt, x_sample
    res_p = res_s = None
    ks_out, vs_out = [], []
    for l in range(depth):
        w_in_bf = w_in[l].astype(BF16)
        w_out_bf = w_out[l].astype(BF16)
        rw = jnp.concatenate([router_coarse_w[l], router_fine_w[l], jnp.zeros((d, pad), F32)], axis=1)
        rw_hi = rw.astype(BF16)
        rw = jnp.concatenate([rw_hi, (rw - rw_hi.astype(F32)).astype(BF16)], axis=1)
        rb = jnp.concatenate([router_coarse_b[l], router_fine_b[l], jnp.zeros((pad,), F32)])[None, :]
        filt = (hy_fw1[l], hy_fb1[l], hy_freq1[l], hy_fw2[l], hy_fb2[l], hy_freq2[l], hy_fw3[l], hy_fb3[l])
        sink = attn_sink[l]

        def mixers(ua, ub, uc, yd, seq):
            ya = conformer_conv(ua, conv_dw_w[l], conv_dw_b[l], conv_ln_g[l], conv_ln_b[l])
            yb = hyena_mixer(ub, hy_short_w[l], hy_short_b[l], hy_d[l], hyena_filter_spectrum(seq, filt, hy_tabs[seq]),
                             hy_tabs[seq])
            yc = fnet_mixer(uc, fnet_tabs[seq])
            return (ya, yb, yc, yd)

        outs = in_projection(xp, mods[l], None, norm1_g[l], w_in_bf, res=res_p, tm=tm_p)
        if res_p is not None:
            xp, outs = outs[0], outs[1:]
        ua, ub, uc, uq, ukv = outs
        kvw = N_KV_HEADS * HEAD_DIM
        ks_out.append(ukv[..., :kvw].reshape(bp, lp, N_KV_HEADS, HEAD_DIM))
        vs_out.append(ukv[..., kvw:].reshape(bp, lp, N_KV_HEADS, HEAD_DIM))
        ys = mixers(ua, ub, uc, context_attention(uq, ukv, sink), lp)
        x1p, hp, route, cnt = out_projection(ys, xp, mods[l], None, norm2_g[l], w_out_bf, rw, rb, tm=tm_p)
        res_p = (hier_moe(hp, route, cnt, l, exp_gate, exp_up, exp_down), route, mods[l])
        xp = x1p

        outs = in_projection(xs, mods[l], 1, norm1_g[l], w_in_bf, res=res_s, rope=rope, tm=tm_s)
        if res_s is not None:
            xs, outs = outs[0], outs[1:]
        ua, ub, uc, uq, uqr, ukv = outs
        ys = mixers(ua, ub, uc, latent_attention(uq, uqr, ukv, ck, cv, l, sink), ls)
        x1s, hs, route, cnt = out_projection(ys, xs, mods[l], 1, norm2_g[l], w_out_bf, rw, rb, tm=tm_s)
        res_s = (hier_moe(hs, route, cnt, l, exp_gate, exp_up, exp_down), route, mods[l])
        xs = x1s

    y_prompt = final_norm(xp, res_p, None, norm_f_g, tm=tm_p)
    y_sample = final_norm(xs, res_s, 1, norm_f_g, tm=tm_s)
    return (y_prompt, y_sample, jnp.stack(ks_out, axis=1), jnp.stack(vs_out, axis=1))
```

```python
import functools
import math

import numpy as np
import jax
import jax.numpy as jnp
from jax import lax
from jax.experimental import pallas as pl
from jax.experimental.pallas import tpu as pltpu
from jax.experimental.pallas import tpu_sc as plsc

F32 = jnp.float32
BF16 = jnp.bfloat16

HEAD_DIM = 64
LOG2E = math.log2(math.e)
ATT_SCALE = HEAD_DIM ** -0.5 * LOG2E
N_Q_HEADS = 4
N_KV_HEADS = 2
Q_PER_KV = N_Q_HEADS // N_KV_HEADS
WINDOW = 128
ATT_BLOCK = 128
GRID_W = 64
ROPE_BASE = 10000.0
N_GROUPS = 4
EXPERTS_PER_GROUP = 8
N_EXPERTS = N_GROUPS * EXPERTS_PER_GROUP
MOE_BLOCK = 512
N_MOD = 6
EPS = 1e-6
NEG_INF = -1e30
HYENA_DECAY_TARGET = 1e-2
HYENA_FAST_PCT = 0.3
HYENA_SLOW_PCT = 1.5
HYENA_SHIFT = 0.05

LANES = 128
SUBLANES = 8
VMEM_LIMIT = 56 * 1024 * 1024

MIX_CH = 256
ROUTE_LANES = 128
SC_CORES = 2
SC_WORKERS = SC_CORES * 16


def _params(*sem):
    return pltpu.CompilerParams(dimension_semantics=sem, vmem_limit_bytes=VMEM_LIMIT)


def _silu(x):
    return x * jax.nn.sigmoid(x)


def _ada_body(c_ref, w_ref, b_ref, o_ref):
    s = _silu(c_ref[...]).astype(BF16)
    o_ref[0] = jnp.dot(s, w_ref[0].astype(BF16), preferred_element_type=F32) + b_ref[0]


def adaln_all(cvec, ada_w, ada_b):
    depth, d, n6 = ada_w.shape
    r = cvec.shape[0]
    tn = n6 // 4
    out = pl.pallas_call(
        _ada_body,
        grid=(depth, n6 // tn),
        in_specs=[
            pl.BlockSpec((r, d), lambda l, j: (0, 0)),
            pl.BlockSpec((1, d, tn), lambda l, j: (l, 0, j)),
            pl.BlockSpec((1, 1, tn), lambda l, j: (l, 0, j)),
        ],
        out_specs=pl.BlockSpec((1, r, tn), lambda l, j: (l, 0, j)),
        out_shape=jax.ShapeDtypeStruct((depth, r, n6), F32),
        compiler_params=_params("arbitrary", "arbitrary"),
        name="adaln",
    )(cvec, ada_w, ada_b.reshape(depth, 1, n6))
    return out.reshape(depth, r, N_MOD, d)


def _swap_halves(x):
    pieces = []
    for j in range(x.shape[1] // LANES):
        xj = x[:, j * LANES:(j + 1) * LANES]
        fwd = pltpu.roll(xj, LANES - HEAD_DIM // 2, axis=1)
        bwd = pltpu.roll(xj, HEAD_DIM // 2, axis=1)
        lane = lax.broadcasted_iota(jnp.int32, xj.shape, 1)
        pieces.append(jnp.where((lane % HEAD_DIM) < HEAD_DIM // 2, fwd, bwd))
    return pieces[0] if len(pieces) == 1 else jnp.concatenate(pieces, axis=1)


def _rmsnorm_mod(x, g, scale, shift):
    ms = jnp.mean(x * x, axis=-1, keepdims=True)
    return (x * lax.rsqrt(ms + EPS)) * (g * (1.0 + scale)) + shift


def _moe_residual(x1, y0_ref, y1_ref, route_ref, pmod_ref, rows=slice(None)):
    y0 = _unpack_bf16_pairs(y0_ref[0, 0, rows, :], F32)
    y1 = _unpack_bf16_pairs(y1_ref[0, 0, rows, :], F32)
    moe = route_ref[0, rows, 2:3] * y0 + route_ref[0, rows, 3:4] * y1
    return x1 + pmod_ref[0, 5:6, :] * moe


def _residual_specs(res, tm, d, row):
    pair, route, pmods = res
    args = [pair, pair, route, pmods]
    specs = [pl.BlockSpec((1, 1, tm, d // 2), lambda bb, i: (0, bb, i, 0)),
             pl.BlockSpec((1, 1, tm, d // 2), lambda bb, i: (1, bb, i, 0)),
             pl.BlockSpec((1, tm, ROUTE_LANES), lambda bb, i: (bb, i, 0)),
             pl.BlockSpec((1, N_MOD, d), lambda bb, i: (row(bb), 0, 0))]
    return args, specs


IN_SPLIT = 2


def _in_body(*refs, fuse_res, rope):
    it = iter(refs)
    x_ref = next(it)
    if fuse_res:
        res_refs = [next(it) for _ in range(4)]
    mod_ref = next(it)
    g_ref = next(it)
    w_ref = next(it)
    if rope:
        cos_ref = next(it)
        sin_ref = next(it)
    outs = list(it)
    xo_ref = outs.pop(0) if fuse_res else None
    c = MIX_CH
    part = x_ref.shape[1] // IN_SPLIT
    hs = []
    for p in range(IN_SPLIT):
        rows = slice(p * part, (p + 1) * part)
        x = x_ref[0, rows, :]
        if fuse_res:
            x = _moe_residual(x, *res_refs, rows)
            xo_ref[0, rows, :] = x
        hs.append(_rmsnorm_mod(x, g_ref[...], mod_ref[0, 1:2, :], mod_ref[0, 0:1, :]).astype(BF16))
    ua_ref, ub_ref, uc_ref, uq_ref = outs[:4]
    for p in range(IN_SPLIT):
        rows = slice(p * part, (p + 1) * part)
        u = jnp.dot(hs[p], w_ref[...], preferred_element_type=F32)
        ua_ref[0, rows, :] = u[:, 0:2 * c]
        ub_ref[0, rows, :] = u[:, 2 * c:5 * c]
        uc_ref[0, rows, :] = u[:, 5 * c:6 * c]
        q = u[:, 6 * c:7 * c] * ATT_SCALE
        k = u[:, 7 * c:7 * c + c // 2]
        v = u[:, 7 * c + c // 2:8 * c]
        uq_ref[0, rows, :] = q.astype(uq_ref.dtype)
        if rope:
            uqr_ref, ukv_ref = outs[4:]
            cs = cos_ref[rows, :]
            sn = sin_ref[rows, :]
            uqr_ref[0, rows, :] = (q * cs + _swap_halves(q) * sn).astype(uqr_ref.dtype)
            kr = k * cs[:, :c // 2] + _swap_halves(k) * sn[:, :c // 2]
            ukv_ref[0, rows, :] = jnp.concatenate([kr, v], axis=1).astype(ukv_ref.dtype)
        else:
            outs[4][0, rows, :] = u[:, 7 * c:8 * c]


def in_projection(x, mods, mod_row0, norm_g, w_in_bf, *, res=None, rope=None, tm):
    b, l, d = x.shape
    c = MIX_CH
    grid = (b, l // tm)
    row = (lambda bb: 0) if mod_row0 is None else (lambda bb: mod_row0 + bb)
    xspec = pl.BlockSpec((1, tm, d), lambda bb, i: (bb, i, 0))
    mspec = pl.BlockSpec((1, N_MOD, d), lambda bb, i: (row(bb), 0, 0))
    args, specs = [x], [xspec]
    if res is not None:
        rargs, rspecs = _residual_specs(res, tm, d, row)
        args += rargs
        specs += rspecs
    args += [mods, norm_g.reshape(1, d), w_in_bf]
    specs += [mspec, pl.BlockSpec((1, d), lambda bb, i: (0, 0)),
              pl.BlockSpec(w_in_bf.shape, lambda bb, i: (0, 0))]
    if rope is not None:
        args += [rope[0], rope[1]]
        specs += [pl.BlockSpec((tm, c), lambda bb, i: (i, 0))] * 2

    def ospec(w):
        return pl.BlockSpec((1, tm, w), lambda bb, i: (bb, i, 0))

    out_shape, out_specs = [], []
    if res is not None:
        out_shape.append(jax.ShapeDtypeStruct((b, l, d), F32))
        out_specs.append(xspec)
    widths = [(2 * c, F32), (3 * c, F32), (c, F32), (c, BF16)] + ([(c, BF16), (c, BF16)] if rope is not None else [(c, F32)])
    for w, dt in widths:
        out_shape.append(jax.ShapeDtypeStruct((b, l, w), dt))
        out_specs.append(ospec(w))
    return pl.pallas_call(
        functools.partial(_in_body, fuse_res=res is not None, rope=rope is not None),
        grid=grid, in_specs=specs, out_specs=out_specs, out_shape=out_shape,
        compiler_params=_params("arbitrary", "arbitrary"),
        name="in_proj",
    )(*args)


def _dw_tile(win, w_ref, n_taps, first, rows):
    acc = w_ref[0:1, :] * win[first:first + rows]
    for k in range(1, n_taps):
        acc = acc + w_ref[k:k + 1, :] * win[first + k:first + k + rows]
    return acc


CONV_PAD = 16
CONV_ROWS = 256
CONV_CHUNK = 1024
CONV_TAIL = CONV_PAD + CONV_ROWS + SUBLANES + SUBLANES


def _conf_body(u_ref, w_ref, b_ref, g_ref, beta_ref, o_ref, gp_ref, sh_ref, *, seq, n_taps, chunk):
    c = MIX_CH
    r = CONV_ROWS
    first = CONV_PAD - (n_taps - 1) // 2
    gp_ref[0:CONV_PAD, :] = jnp.zeros((CONV_PAD, c), F32)
    gp_ref[CONV_PAD + seq:CONV_PAD + seq + CONV_TAIL, :] = jnp.zeros((CONV_TAIL, c), F32)

    def fill(i, carry):
        r0 = pl.multiple_of(i * r, r)
        a = u_ref[0, pl.ds(r0, r), 0:c]
        g = u_ref[0, pl.ds(r0, r), c:2 * c]
        gp_ref[pl.ds(CONV_PAD + r0, r), :] = a * jax.nn.sigmoid(g)
        return carry

    lax.fori_loop(0, seq // r, fill, 0)
    n_copy_tiles = sh_ref.shape[1] // r

    def do_chunk(ci, carry):
        c0 = pl.multiple_of(ci * chunk, chunk)

        def shift_tile(ti, carry2):
            t0 = pl.multiple_of(ti * r, r)
            win = gp_ref[pl.ds(c0 + t0, r + SUBLANES), :]
            for m in range(1, SUBLANES):
                sh_ref[m - 1, pl.ds(t0, r), :] = win[m:m + r]
            return carry2

        lax.fori_loop(0, n_copy_tiles, shift_tile, 0)

        def tile(ti, carry2):
            t0 = pl.multiple_of(ti * r, r)
            acc = None
            for k in range(n_taps):
                a8, m = (first + k) // SUBLANES * SUBLANES, (first + k) % SUBLANES
                src = gp_ref[pl.ds(c0 + t0 + a8, r), :] if m == 0 else sh_ref[m - 1, pl.ds(t0 + a8, r), :]
                term = w_ref[k:k + 1, :] * src
                acc = term if acc is None else acc + term
            z = acc + b_ref[...]
            mu = jnp.mean(z, axis=-1, keepdims=True)
            zc = z - mu
            var = jnp.mean(zc * zc, axis=-1, keepdims=True)
            zn = zc * lax.rsqrt(var + EPS) * g_ref[...] + beta_ref[...]
            o_ref[0, pl.ds(c0 + t0, r), :] = _silu(zn).astype(o_ref.dtype)
            return carry2

        lax.fori_loop(0, chunk // r, tile, 0)
        return carry

    lax.fori_loop(0, seq // chunk, do_chunk, 0)


def conformer_conv(ua, dw_w, dw_b, ln_g, ln_b):
    b, l, c2 = ua.shape
    c = MIX_CH
    k = dw_w.shape[0]
    chunk = min(l, CONV_CHUNK)
    copy_rows = -(-(chunk + CONV_PAD + k) // CONV_ROWS) * CONV_ROWS
    assert l % chunk == 0 and k - 1 <= 2 * CONV_PAD and copy_rows - chunk + SUBLANES <= CONV_PAD + CONV_TAIL
    vec = pl.BlockSpec((1, c), lambda bb: (0, 0))
    return pl.pallas_call(
        functools.partial(_conf_body, seq=l, n_taps=k, chunk=chunk),
        grid=(b,),
        in_specs=[pl.BlockSpec((1, l, c2), lambda bb: (bb, 0, 0)),
                  pl.BlockSpec((k, c), lambda bb: (0, 0)), vec, vec, vec],
        out_specs=pl.BlockSpec((1, l, c), lambda bb: (bb, 0, 0)),
        out_shape=jax.ShapeDtypeStruct((b, l, c), BF16),
        scratch_shapes=[pltpu.VMEM((CONV_PAD + l + CONV_TAIL, c), F32),
                        pltpu.VMEM((SUBLANES - 1, copy_rows, c), F32)],
        compiler_params=_params("arbitrary"),
        name="conformer",
    )(ua, dw_w, dw_b.reshape(1, c), ln_g.reshape(1, c), ln_b.reshape(1, c))


SHORT_PAD = 8
SHORT_ROWS = 64


def _short_body(u_ref, w_ref, b_ref, o_ref, xp_ref, *, seq, n_taps):
    c = MIX_CH
    r = SHORT_ROWS
    half = (n_taps - 1) // 2
    zero = jnp.zeros((SHORT_PAD, c), F32)
    xp_ref[0:SHORT_PAD, :] = zero
    xp_ref[SHORT_PAD + seq:SHORT_PAD + seq + SHORT_PAD, :] = zero
    for q in range(u_ref.shape[0]):
        def fill(i, carry):
            r0 = pl.multiple_of(i * r, r)
            xp_ref[pl.ds(SHORT_PAD + r0, r), :] = u_ref[q, pl.ds(r0, r), :]
            return carry

        lax.fori_loop(0, seq // r, fill, 0)

        def tile(i, carry):
            r0 = pl.multiple_of(i * r, r)
            win = xp_ref[pl.ds(r0, r + 2 * SHORT_PAD), :]
            o_ref[q, pl.ds(r0, r), :] = _dw_tile(win, w_ref, n_taps, SHORT_PAD - half, r) + b_ref[...]
            return carry

        lax.fori_loop(0, seq // r, tile, 0)


SHORT_BLOCK_ROWS = 1024


def hyena_short_conv(ub, short_w, short_b):
    b, l, c3 = ub.shape
    c = MIX_CH
    k = short_w.shape[0]
    nbat = math.gcd(b, max(1, SHORT_BLOCK_ROWS // l))
    return pl.pallas_call(
        functools.partial(_short_body, seq=l, n_taps=k),
        grid=(b // nbat, c3 // c),
        in_specs=[pl.BlockSpec((nbat, l, c), lambda bb, j: (bb, 0, j)),
                  pl.BlockSpec((k, c), lambda bb, j: (0, j)),
                  pl.BlockSpec((1, c), lambda bb, j: (0, j))],
        out_specs=pl.BlockSpec((nbat, l, c), lambda bb, j: (bb, 0, j)),
        out_shape=jax.ShapeDtypeStruct((b, l, c3), F32),
        scratch_shapes=[pltpu.VMEM((l + 2 * SHORT_PAD, c), F32)],
        compiler_params=_params("arbitrary", "arbitrary"),
        name="hyena_short",
    )(ub, short_w, short_b.reshape(1, c3))


def _fft_sizes(seq):
    n = 2 * seq
    n2 = 128 if n >= 4096 else 32
    return n, n // n2, n2


FFT_GROUP = SUBLANES


def _kron_matrices(n, n1, h):
    k1 = np.arange(n1)[:, None]
    i1 = np.arange(h)[None, :]
    th = 2.0 * np.pi * ((k1 * i1) % n1) / n1
    eye = np.eye(FFT_GROUP)
    cs, sn = np.kron(np.cos(th), eye), np.kron(np.sin(th), eye)
    fwd = np.block([[cs, sn], [-sn, cs]])
    inv = np.block([[cs.T, -sn.T], [sn.T, cs.T]]) / n
    return fwd.astype(np.float32), inv.astype(np.float32)


def _twiddle_tables(n, n1, n2):
    k1 = jnp.arange(n1, dtype=jnp.int32)[:, None]
    i2 = jnp.arange(n2, dtype=jnp.int32)[None, :]
    th = ((k1 * i2) % n).astype(F32) * (2.0 * math.pi / n)
    shape = (n1, n2 // FFT_GROUP, FFT_GROUP, LANES)
    full = lambda a: jnp.broadcast_to(a.reshape(shape[:3] + (1,)), shape)
    return full(jnp.cos(th)), full(jnp.sin(th))


def _mid_tables(n2):
    a = np.arange(n2)
    th = 2.0 * np.pi * ((a[:, None] * a[None, :]) % n2) / n2
    cs, sn = np.cos(th), np.sin(th)
    fwd = np.concatenate([np.concatenate([cs, sn], 1), np.concatenate([-sn, cs], 1)], 0)
    inv = np.concatenate([np.concatenate([cs, -sn], 1), np.concatenate([sn, cs], 1)], 0)
    return fwd.astype(np.float32), inv.astype(np.float32)


MID_CHUNK = 32


def _dot(a, b):
    return jnp.dot(a.astype(BF16), b.astype(BF16), preferred_element_type=F32)


def _filter_stage1_body(x_ref, m_ref, c_ref, s_ref, o_ref, *, n1, gs):
    j, c = FFT_GROUP, MIX_CH
    for s in range(gs):
        a = _dot(m_ref[...], x_ref[:, s].reshape(n1 * j, c))
        ar = a[:n1 * j].reshape(n1, j, c)
        ai = a[n1 * j:].reshape(n1, j, c)
        cs, sn = _lanes(c_ref[:, s], c), _lanes(s_ref[:, s], c)
        o_ref[0, 0, :, s] = ar * cs + ai * sn
        o_ref[0, 1, :, s] = ai * cs - ar * sn


def filter_stage1(k, order, mat, twc, tws, *, n1, n2):
    g, j, c = n2 // FFT_GROUP, FFT_GROUP, MIX_CH
    gs = _group_step(g)
    tw = pl.BlockSpec((n1, gs, j, LANES), lambda gi: (0, gi, 0, 0))
    out = pl.pallas_call(
        functools.partial(_filter_stage1_body, n1=n1, gs=gs),
        grid=(g // gs,),
        in_specs=[pl.BlockSpec((n1, gs, j, c), lambda gi: (0, gi, 0, order)),
                  pl.BlockSpec(mat.shape, lambda gi: (0, 0)), tw, tw],
        out_specs=pl.BlockSpec((1, 2, n1, gs, j, c), lambda gi: (0, 0, 0, gi, 0, 0)),
        out_shape=jax.ShapeDtypeStruct((1, 2, n1, g, j, c), F32),
        compiler_params=_params("arbitrary"),
        name="filter_stage1",
    )(k.reshape(n1, g, j, k.shape[-1]), mat, twc, tws)
    return out.reshape(1, 2, n1 * n2, c)


def _mid_body(a_ref, k_ref, f_ref, g_ref, o_ref, *, n2, kc):
    for j in range(kc):
        rows = slice(j * n2, (j + 1) * n2)
        blk = jnp.concatenate([a_ref[0, 0, rows, :], a_ref[0, 1, rows, :]], axis=0)
        s = _dot(f_ref[...], blk)
        sr, si = s[:n2], s[n2:]
        kr, ki = k_ref[0, rows, :], k_ref[1, rows, :]
        y = jnp.concatenate([sr * kr - si * ki, sr * ki + si * kr], axis=0)
        bb = _dot(g_ref[...], y)
        o_ref[0, 0, rows, :] = bb[:n2].astype(o_ref.dtype)
        o_ref[0, 1, rows, :] = bb[n2:].astype(o_ref.dtype)


def fft_mid(a, kf, f_fwd, f_inv, *, n1, n2):
    p = a.shape[0]
    c = MIX_CH
    kc = min(n1, MID_CHUNK)
    rows = kc * n2
    blk = pl.BlockSpec((1, 2, rows, c), lambda j, pp: (pp, 0, j, 0))
    mat = pl.BlockSpec(f_fwd.shape, lambda j, pp: (0, 0))
    return pl.pallas_call(
        functools.partial(_mid_body, n2=n2, kc=kc),
        grid=(n1 // kc, p),
        in_specs=[blk, pl.BlockSpec((2, rows, c), lambda j, pp: (0, j, 0)), mat, mat],
        out_specs=blk,
        out_shape=jax.ShapeDtypeStruct(a.shape, a.dtype),
        compiler_params=_params("arbitrary", "arbitrary"),
        name="fft_mid",
    )(a, kf, f_fwd, f_inv)


def _filter_mid_body(a_ref, f_ref, sum_ref, o_ref, *, n2, kc):
    inv = 1.0 / (sum_ref[0:1, :] + EPS)
    for j in range(kc):
        rows = slice(j * n2, (j + 1) * n2)
        blk = jnp.concatenate([a_ref[0, 0, rows, :], a_ref[0, 1, rows, :]], axis=0)
        s = _dot(f_ref[...], blk) * inv
        o_ref[0, rows, :] = s[:n2]
        o_ref[1, rows, :] = s[n2:]


def filter_mid(a, f_fwd, abs_sum, order, *, n1, n2):
    c = MIX_CH
    kc = min(n1, MID_CHUNK)
    rows = kc * n2
    return pl.pallas_call(
        functools.partial(_filter_mid_body, n2=n2, kc=kc),
        grid=(n1 // kc,),
        in_specs=[pl.BlockSpec((1, 2, rows, c), lambda j: (0, 0, j, 0)),
                  pl.BlockSpec(f_fwd.shape, lambda j: (0, 0)),
                  pl.BlockSpec((SUBLANES, c), lambda j: (0, order))],
        out_specs=pl.BlockSpec((2, rows, c), lambda j: (0, j, 0)),
        out_shape=jax.ShapeDtypeStruct((2, n1 * n2, c), F32),
        compiler_params=_params("arbitrary"),
        name="filter_mid",
    )(a, f_fwd, abs_sum)


def _lanes(t, width):
    return t if width == LANES else jnp.concatenate([t] * (width // LANES), axis=-1)


WORK_ROWS = 2 * FFT_GROUP


def _store_group_pair(o_ref, parts, s):
    if s % 2 == 1:
        for plane in range(2):
            o_ref[0, plane, :, s // 2] = jnp.concatenate([parts[s - 1][plane], parts[s][plane]],
                                                         axis=1).astype(o_ref.dtype)


def _kron1_body(zr_ref, zi_ref, m_ref, c_ref, s_ref, o_ref, *, n1, h, gs):
    j, c = FFT_GROUP, MIX_CH
    parts = []
    for s in range(gs):
        xr = zr_ref[0, :, s].reshape(h * j, c)
        xi = zi_ref[0, :, s].reshape(h * j, c)
        a = _dot(m_ref[...], jnp.concatenate([xr, xi], axis=0))
        ar = a[:n1 * j].reshape(n1, j, c)
        ai = a[n1 * j:].reshape(n1, j, c)
        cs, sn = _lanes(c_ref[:, s], c), _lanes(s_ref[:, s], c)
        parts.append((ar * cs + ai * sn, ai * cs - ar * sn))
        _store_group_pair(o_ref, parts, s)


def _group_step(n_groups):
    return min(n_groups, 8)


def kron_stage1(z5, col, n_pairs, imag_offset, mat, twc, tws, *, n1, h):
    g, j, c = z5.shape[2], FFT_GROUP, MIX_CH
    gs = _group_step(g)
    tw = pl.BlockSpec((n1, gs, j, LANES), lambda gi, p: (0, gi, 0, 0))
    return pl.pallas_call(
        functools.partial(_kron1_body, n1=n1, h=h, gs=gs),
        grid=(g // gs, n_pairs),
        in_specs=[pl.BlockSpec((1, h, gs, j, c), lambda gi, p: (p, 0, gi, 0, col)),
                  pl.BlockSpec((1, h, gs, j, c), lambda gi, p: (p + imag_offset, 0, gi, 0, col)),
                  pl.BlockSpec(mat.shape, lambda gi, p: (0, 0)), tw, tw],
        out_specs=pl.BlockSpec((1, 2, n1, gs // 2, WORK_ROWS, c), lambda gi, p: (p, 0, 0, gi, 0, 0)),
        out_shape=jax.ShapeDtypeStruct((n_pairs, 2, n1, g // 2, WORK_ROWS, c), BF16),
        compiler_params=_params("arbitrary", "arbitrary"),
        name="fft_kron1",
    )(z5, z5, mat, twc, tws)


def _kron3_body(b_ref, m_ref, c_ref, s_ref, zr_ref, zi_ref, gr_ref, gi_ref, d_ref, o_ref, *, n1, h, gs):
    j, c = FFT_GROUP, MIX_CH
    d = d_ref[...].reshape(1, 1, c)
    for s in range(gs):
        half = slice((s % 2) * j, (s % 2 + 1) * j)
        br = b_ref[0, 0, :, s // 2].astype(F32)[:, half]
        bi = b_ref[0, 1, :, s // 2].astype(F32)[:, half]
        cs, sn = _lanes(c_ref[:, s], c), _lanes(s_ref[:, s], c)
        xr = (br * cs - bi * sn).reshape(n1 * j, c)
        xi = (br * sn + bi * cs).reshape(n1 * j, c)
        y = _dot(m_ref[...], jnp.concatenate([xr, xi], axis=0))
        yr = y[:h * j].reshape(h, j, c)
        yi = y[h * j:].reshape(h, j, c)
        o_ref[0, 0, :, s] = gr_ref[0, :, s] * (yr + d * zr_ref[0, :, s])
        o_ref[1, 0, :, s] = gi_ref[0, :, s] * (yi + d * zi_ref[0, :, s])


def kron_stage3(bw, mat, twc, tws, z5, z_col, g5, g_col, d_vec, imag_offset, *, n1, h):
    p = bw.shape[0]
    g, j, c = bw.shape[3] * 2, FFT_GROUP, MIX_CH
    gs = _group_step(g)
    tw = pl.BlockSpec((n1, gs, j, LANES), lambda gi, pp: (0, gi, 0, 0))

    def src(col, off):
        return pl.BlockSpec((1, h, gs, j, c), lambda gi, pp: (pp + off, 0, gi, 0, col))

    return pl.pallas_call(
        functools.partial(_kron3_body, n1=n1, h=h, gs=gs),
        grid=(g // gs, p),
        in_specs=[pl.BlockSpec((1, 2, n1, gs // 2, WORK_ROWS, c), lambda gi, pp: (pp, 0, 0, gi, 0, 0)),
                  pl.BlockSpec(mat.shape, lambda gi, pp: (0, 0)), tw, tw,
                  src(z_col, 0), src(z_col, imag_offset), src(g_col, 0), src(g_col, imag_offset),
                  pl.BlockSpec((1, c), lambda gi, pp: (0, 0))],
        out_specs=pl.BlockSpec((2, 1, h, gs, j, c), lambda gi, pp: (0, pp, 0, gi, 0, 0)),
        out_shape=jax.ShapeDtypeStruct((2, p, h, g, j, c), F32),
        compiler_params=_params("arbitrary", "arbitrary"),
        name="fft_kron3",
    )(bw, mat, twc, tws, z5, z5, g5, g5, d_vec.reshape(1, c))


FILTER_TILE = 512


def _lane_dense(fn, a):
    r, w = a.shape
    f = LANES // w
    rows = r // f
    dense = jnp.concatenate([a[q * rows:(q + 1) * rows] for q in range(f)], axis=1)
    out = fn(dense)
    return jnp.concatenate([out[:, q * w:(q + 1) * w] for q in range(f)], axis=0)


def _filter_body(fw1_ref, fb1_ref, fr1_ref, fw2_ref, fb2_ref, fr2_ref, fw3_ref, fb3_ref, bands_ref, decay_ref,
                 k_ref, sum_ref, *, seq):
    i = pl.program_id(0)
    c = MIX_CH
    hp = lax.Precision.HIGHEST
    n = i * FILTER_TILE + lax.broadcasted_iota(jnp.int32, (FILTER_TILE, 1), 0)
    pos = jnp.where(n <= seq, n, 2 * seq - n).astype(F32)
    t = pos * (1.0 / (seq - 1))
    ang = (pos * (2.0 * math.pi / seq)) * bands_ref[...]
    nb = bands_ref.shape[1]
    pre = (t * fw1_ref[0:1, :]
           + jnp.dot(_lane_dense(jnp.cos, ang), fw1_ref[1:1 + nb, :], preferred_element_type=F32, precision=hp)
           - jnp.dot(_lane_dense(jnp.sin, ang), fw1_ref[1 + nb:1 + 2 * nb, :], preferred_element_type=F32,
                     precision=hp)
           + fb1_ref[...])
    h = _lane_dense(jnp.sin, fr1_ref[...] * pre)
    h = _lane_dense(jnp.sin, fr2_ref[...] * (jnp.dot(h, fw2_ref[...], preferred_element_type=F32, precision=hp)
                                             + fb2_ref[...]))
    h = jnp.dot(h, fw3_ref[...], preferred_element_type=F32, precision=hp) + fb3_ref[...]
    win = jnp.exp(-t * decay_ref[...]) + HYENA_SHIFT
    win = jnp.concatenate([win, win], axis=1)
    fwd, bwd = h[:, :2 * c], h[:, 2 * c:]
    k = jnp.where(n < seq, fwd, bwd) + jnp.where(n == 0, bwd, 0.0)
    k = jnp.where(n == seq, 0.0, k) * win
    k_ref[...] = k

    @pl.when(i == 0)
    def _():
        sum_ref[...] = jnp.zeros(sum_ref.shape, F32)

    sum_ref[...] = sum_ref[...] + jnp.sum(jnp.abs(k), axis=0, keepdims=True)


def hyena_filter_time(seq, fw1, fb1, fr1, fw2, fb2, fr2, fw3, fb3):
    c = MIX_CH
    n_bands = (fw1.shape[0] - 1) // 2
    bands = jnp.linspace(1e-4, n_bands - 1, n_bands, dtype=F32)[None, :]
    max_decay = math.log(HYENA_DECAY_TARGET) / HYENA_FAST_PCT
    min_decay = math.log(HYENA_DECAY_TARGET) / HYENA_SLOW_PCT
    decay = jnp.abs(jnp.linspace(min_decay, max_decay, c, dtype=F32))[None, :]
    assert (2 * seq) % FILTER_TILE == 0
    args = [fw1, fb1[None, :], fr1[None, :], fw2, fb2[None, :], fr2[None, :], fw3, fb3[None, :], bands, decay]
    return pl.pallas_call(
        functools.partial(_filter_body, seq=seq),
        grid=(2 * seq // FILTER_TILE,),
        in_specs=[pl.BlockSpec(a.shape, lambda i: (0, 0)) for a in args],
        out_specs=[pl.BlockSpec((FILTER_TILE, 2 * c), lambda i: (i, 0)),
                   pl.BlockSpec((SUBLANES, 2 * c), lambda i: (0, 0))],
        out_shape=[jax.ShapeDtypeStruct((2 * seq, 2 * c), F32), jax.ShapeDtypeStruct((SUBLANES, 2 * c), F32)],
        compiler_params=_params("arbitrary"),
        name="hyena_filter",
    )(*args)


def hyena_filter_spectrum(seq, filt, tabs):
    n, n1, n2 = _fft_sizes(seq)
    k, abs_sum = hyena_filter_time(seq, *filt)
    twc, tws, f_fwd, m_real = tabs[2], tabs[3], tabs[4], tabs[6]
    return [filter_mid(filter_stage1(k, o, m_real, twc, tws, n1=n1, n2=n2), f_fwd, abs_sum, o, n1=n1, n2=n2)
            for o in range(2)]


def hyena_mixer(ub, short_w, short_b, hy_d, kf, tabs):
    b, l, _ = ub.shape
    n, n1, n2 = _fft_sizes(l)
    h = n1 // 2
    p = b // 2
    c = MIX_CH
    g = n2 // FFT_GROUP
    u = hyena_short_conv(ub, short_w, short_b)
    u5 = u.reshape(b, h, g, FFT_GROUP, u.shape[-1])
    m_fwd, m_inv, twc, tws, f_fwd, f_inv = tabs[:6]
    z5 = u5
    for o in range(2):
        a = kron_stage1(z5, 0, p, p, m_fwd, twc, tws, n1=n1, h=h)
        bw = fft_mid(a.reshape(p, 2, n, c), kf[o], f_fwd, f_inv, n1=n1, n2=n2)
        z = kron_stage3(bw.reshape(a.shape), m_inv, twc, tws, z5, 0, u5, 1 + o, hy_d[o], p, n1=n1, h=h)
        z5 = z.reshape(b, h, g, FFT_GROUP, c)
    return z5.reshape(b, l, c)


def hyena_tables(seq):
    n, n1, n2 = _fft_sizes(seq)
    m_fwd, m_inv = _kron_matrices(n, n1, n1 // 2)
    m_real = _kron_matrices(n, n1, n1)[0][:, :n1 * FFT_GROUP]
    twc, tws = _twiddle_tables(n, n1, n2)
    f_fwd, f_inv = _mid_tables(n2)
    bf = lambda a: jnp.asarray(a).astype(BF16)
    return bf(m_fwd), bf(m_inv), twc, tws, bf(f_fwd), bf(f_inv), bf(m_real)


def _fnet_body(cl_ref, sl_ref, x_ref, cc_ref, sc_ref, o_ref, *, scale):
    x = x_ref[0].astype(BF16)
    pr = jnp.dot(cl_ref[...], x, preferred_element_type=F32).astype(BF16)
    qr = jnp.dot(sl_ref[...], x, preferred_element_type=F32).astype(BF16)
    o_ref[0] = (jnp.dot(pr, cc_ref[...], preferred_element_type=F32)
                - jnp.dot(qr, sc_ref[...], preferred_element_type=F32)) * scale


def _dft_tables(n):
    a = jnp.arange(n, dtype=jnp.int32)
    th = ((a[:, None] * a[None, :]) % n).astype(F32) * (2.0 * math.pi / n)
    return jnp.cos(th).astype(BF16), jnp.sin(th).astype(BF16)


FNET_DIRECT_MAX = 1024


def _fnet1_body(x_ref, ccs_ref, m_ref, c_ref, s_ref, o_ref, *, n1, gs):
    j, c = FFT_GROUP, MIX_CH
    parts = []
    for s in range(gs):
        x = x_ref[0, :, s].reshape(n1 * j, c).astype(BF16)
        z = jnp.dot(x, ccs_ref[...], preferred_element_type=F32)
        a = _dot(m_ref[...], jnp.concatenate([z[:, :c], z[:, c:]], axis=0))
        ar = a[:n1 * j].reshape(n1, j, c)
        ai = a[n1 * j:].reshape(n1, j, c)
        cs, sn = _lanes(c_ref[:, s], c), _lanes(s_ref[:, s], c)
        parts.append((ar * cs + ai * sn, ai * cs - ar * sn))
        _store_group_pair(o_ref, parts, s)


def _fnet2_body(a_ref, m_ref, o_ref, *, n2, scale):
    j, c = FFT_GROUP, MIX_CH
    x = a_ref[0].reshape(2 * j * n2, c)
    y = _dot(m_ref[...], x) * scale
    o_ref[0, :, 0] = y.reshape(n2, j, c)


def fnet_tables(seq):
    c = MIX_CH
    cc, sc = _dft_tables(c)
    if seq <= FNET_DIRECT_MAX:
        return _dft_tables(seq) + (cc, sc)
    n2 = 128
    n1 = seq // n2
    m_fwd, _ = _kron_matrices(seq, n1, n1)
    twc, tws = _twiddle_tables(seq, n1, n2)
    a = np.arange(n2)
    th = 2.0 * np.pi * ((a[:, None] * a[None, :]) % n2) / n2
    eye = np.eye(FFT_GROUP)
    m2 = np.concatenate([np.einsum('kn,ij->kijn', f, eye).reshape(n2 * FFT_GROUP, FFT_GROUP * n2)
                         for f in (np.cos(th), np.sin(th))], axis=1).astype(np.float32)
    ccs = jnp.concatenate([cc, -sc], axis=1)
    return ccs, jnp.asarray(m_fwd).astype(BF16), twc, tws, jnp.asarray(m2).astype(BF16)


def fnet_two_stage(uc, tables):
    b, l, c = uc.shape
    ccs, m_fwd, twc, tws, m2 = tables
    j = FFT_GROUP
    n2 = 128
    n1 = l // n2
    g = n2 // j
    gs = _group_step(g)
    tw = pl.BlockSpec((n1, gs, j, LANES), lambda gi, bb: (0, gi, 0, 0))
    a = pl.pallas_call(
        functools.partial(_fnet1_body, n1=n1, gs=gs),
        grid=(g // gs, b),
        in_specs=[pl.BlockSpec((1, n1, gs, j, c), lambda gi, bb: (bb, 0, gi, 0, 0)),
                  pl.BlockSpec(ccs.shape, lambda gi, bb: (0, 0)),
                  pl.BlockSpec(m_fwd.shape, lambda gi, bb: (0, 0)), tw, tw],
        out_specs=pl.BlockSpec((1, 2, n1, gs // 2, WORK_ROWS, c), lambda gi, bb: (bb, 0, 0, gi, 0, 0)),
        out_shape=jax.ShapeDtypeStruct((b, 2, n1, g // 2, WORK_ROWS, c), BF16),
        compiler_params=_params("arbitrary", "arbitrary"),
        name="fnet_stage1",
    )(uc.reshape(b, n1, g, j, c), ccs, m_fwd, twc, tws)
    out = pl.pallas_call(
        functools.partial(_fnet2_body, n2=n2, scale=1.0 / math.sqrt(l * c)),
        grid=(b, n1 // j),
        in_specs=[pl.BlockSpec((1, 2, j, n2, c), lambda bb, q: (bb, 0, q, 0, 0)),
                  pl.BlockSpec(m2.shape, lambda bb, q: (0, 0))],
        out_specs=pl.BlockSpec((1, n2, 1, j, c), lambda bb, q: (bb, 0, q, 0, 0)),
        out_shape=jax.ShapeDtypeStruct((b, n2, n1 // j, j, c), F32),
        compiler_params=_params("arbitrary", "arbitrary"),
        name="fnet_stage2",
    )(a.reshape(b, 2, n1, n2, c), m2)
    return out.reshape(b, l, c)


def fnet_mixer(uc, tables):
    b, l, c = uc.shape
    if l > FNET_DIRECT_MAX:
        return fnet_two_stage(uc, tables)
    cl, sl, cc, sc = tables
    tm = min(l, 512)
    row = pl.BlockSpec((tm, l), lambda i, bb: (i, 0))
    sq = pl.BlockSpec((c, c), lambda i, bb: (0, 0))
    return pl.pallas_call(
        functools.partial(_fnet_body, scale=1.0 / math.sqrt(l * c)),
        grid=(l // tm, b),
        in_specs=[row, row, pl.BlockSpec((1, l, c), lambda i, bb: (bb, 0, 0)), sq, sq],
        out_specs=pl.BlockSpec((1, tm, c), lambda i, bb: (bb, i, 0)),
        out_shape=jax.ShapeDtypeStruct((b, l, c), F32),
        compiler_params=_params("arbitrary", "arbitrary"),
        name="fnet",
    )(cl, sl, uc, cc, sc)


def _heads_rows(x, g):
    h0 = Q_PER_KV * g
    return jnp.concatenate([x[:, (h0 + r) * HEAD_DIM:(h0 + r + 1) * HEAD_DIM] for r in range(Q_PER_KV)], axis=0)


def _qk(q, k):
    return lax.dot_general(q.astype(BF16), k.astype(BF16), (((1,), (1,)), ((), ())),
                           preferred_element_type=F32)


def _sink_col(sink_ref, g, rows):
    ridx = lax.broadcasted_iota(jnp.int32, (Q_PER_KV * rows, 1), 0)
    col = jnp.full((Q_PER_KV * rows, 1), sink_ref[Q_PER_KV * g], F32)
    for r in range(1, Q_PER_KV):
        col = jnp.where(ridx >= r * rows, sink_ref[Q_PER_KV * g + r], col)
    return col * LOG2E


def _lat_attn_body(sink_ref, q_ref, qr_ref, kp_ref, kc_ref, kn_ref, ck_ref, cv_ref, o_ref, *, sub):
    i = pl.program_id(1)
    n_qblk = pl.num_programs(1) * sub
    blk = ATT_BLOCK
    span = blk + 2 * WINDOW
    kv = jnp.concatenate([kp_ref[0], kc_ref[0], kn_ref[0]], axis=0)
    ck = ck_ref[0, 0].astype(BF16)
    cv = cv_ref[0, 0].astype(BF16)
    kvw = N_KV_HEADS * HEAD_DIM
    r = lax.broadcasted_iota(jnp.int32, (Q_PER_KV * blk, span), 0) % blk
    j = lax.broadcasted_iota(jnp.int32, (Q_PER_KV * blk, span), 1)
    band = (j >= r) & (j <= r + 2 * WINDOW)
    for s in range(sub):
        qi = i * sub + s
        ok = band & ((qi > 0) | (j >= WINDOW)) & ((qi < n_qblk - 1) | (j < WINDOW + blk))
        q = q_ref[0, s * blk:(s + 1) * blk, :]
        qr = qr_ref[0, s * blk:(s + 1) * blk, :]
        outs = []
        for g in range(N_KV_HEADS):
            kl = kv[s * blk:s * blk + span, g * HEAD_DIM:(g + 1) * HEAD_DIM]
            vl = kv[s * blk:s * blk + span, kvw + g * HEAD_DIM:kvw + (g + 1) * HEAD_DIM]
            s_loc = jnp.where(ok, _qk(_heads_rows(qr, g), kl), NEG_INF)
            s_ctx = _qk(_heads_rows(q, g), ck[:, g * HEAD_DIM:(g + 1) * HEAD_DIM])
            sink = _sink_col(sink_ref, g, blk)
            m = jnp.maximum(jnp.maximum(jnp.max(s_loc, axis=-1, keepdims=True),
                                        jnp.max(s_ctx, axis=-1, keepdims=True)), sink)
            e_loc = jnp.exp2(s_loc - m)
            e_ctx = jnp.exp2(s_ctx - m)
            den = (jnp.sum(e_loc, axis=-1, keepdims=True) + jnp.sum(e_ctx, axis=-1, keepdims=True)
                   + jnp.exp2(sink - m))
            o = (jnp.dot(e_loc.astype(BF16), vl, preferred_element_type=F32)
                 + jnp.dot(e_ctx.astype(BF16), cv[:, g * HEAD_DIM:(g + 1) * HEAD_DIM],
                           preferred_element_type=F32)) * (1.0 / den)
            outs += [o[rr * blk:(rr + 1) * blk] for rr in range(Q_PER_KV)]
        o_ref[0, s * blk:(s + 1) * blk, :] = jnp.concatenate(outs, axis=1).astype(o_ref.dtype)


ATT_SUB = 4


def latent_attention(uq, uqr, ukv, cache_k, cache_v, layer, sink):
    b, l, c = uq.shape
    p = cache_k.shape[2]
    blk = ATT_BLOCK
    nblk = l // blk
    sub = math.gcd(ATT_SUB, nblk)
    rows = sub * blk
    qspec = pl.BlockSpec((1, rows, c), lambda bb, i: (bb, i, 0))
    cspec = pl.BlockSpec((1, 1, p, cache_k.shape[3]), lambda bb, i: (bb, layer, 0, 0))
    return pl.pallas_call(
        functools.partial(_lat_attn_body, sub=sub),
        grid=(b, nblk // sub),
        in_specs=[pl.BlockSpec(memory_space=pltpu.SMEM), qspec, qspec,
                  pl.BlockSpec((1, blk, c), lambda bb, i: (bb, jnp.maximum(i * sub - 1, 0), 0)),
                  qspec,
                  pl.BlockSpec((1, blk, c), lambda bb, i: (bb, jnp.minimum((i + 1) * sub, nblk - 1), 0)),
                  cspec, cspec],
        out_specs=qspec,
        out_shape=jax.ShapeDtypeStruct((b, l, c), BF16),
        compiler_params=_params("arbitrary", "arbitrary"),
        name="latent_attention",
    )(sink, uq, uqr, ukv, ukv, ukv, cache_k, cache_v)


def _ctx_attn_body(sink_ref, q_ref, kv_ref, o_ref, *, seq):
    q = q_ref[0]
    kv = kv_ref[0]
    kvw = N_KV_HEADS * HEAD_DIM
    outs = []
    for g in range(N_KV_HEADS):
        kl = kv[:, g * HEAD_DIM:(g + 1) * HEAD_DIM]
        vl = kv[:, kvw + g * HEAD_DIM:kvw + (g + 1) * HEAD_DIM]
        s = _qk(_heads_rows(q, g), kl)
        sink = _sink_col(sink_ref, g, seq)
        m = jnp.maximum(jnp.max(s, axis=-1, keepdims=True), sink)
        e = jnp.exp2(s - m)
        den = jnp.sum(e, axis=-1, keepdims=True) + jnp.exp2(sink - m)
        o = jnp.dot(e.astype(BF16), vl.astype(BF16), preferred_element_type=F32) * (1.0 / den)
        outs += [o[rr * seq:(rr + 1) * seq] for rr in range(Q_PER_KV)]
    o_ref[0] = jnp.concatenate(outs, axis=1)


def context_attention(uq, ukv, sink):
    b, s, c = uq.shape
    spec = pl.BlockSpec((1, s, c), lambda bb: (bb, 0, 0))
    return pl.pallas_call(
        functools.partial(_ctx_attn_body, seq=s),
        grid=(b,),
        in_specs=[pl.BlockSpec(memory_space=pltpu.SMEM), spec, spec],
        out_specs=spec,
        out_shape=jax.ShapeDtypeStruct((b, s, c), F32),
        compiler_params=_params("arbitrary"),
        name="context_attention",
    )(sink, uq, ukv)


def _pack_bf16_pairs(hi_rounded):
    k = hi_rounded.shape[1] // 2
    bits = lax.bitcast_convert_type(hi_rounded, jnp.uint32)
    return bits[:, :k] | (bits[:, k:] >> 16)


def _unpack_bf16_pairs(packed, dtype=BF16):
    a = lax.bitcast_convert_type(packed & jnp.uint32(0xFFFF0000), F32)
    b = lax.bitcast_convert_type(packed << 16, F32)
    return jnp.concatenate([a, b], axis=1).astype(dtype)


def _out_body(ya_ref, yb_ref, yc_ref, yd_ref, x_ref, mod_ref, g_ref, w_ref, rw_ref, rb_ref,
              x1_ref, h_ref, route_ref, cnt_ref):
    c = MIX_CH
    y = jnp.dot(ya_ref[0].astype(BF16), w_ref[0:c, :], preferred_element_type=F32)
    for j, ref in enumerate((yb_ref, yc_ref, yd_ref), start=1):
        y = y + jnp.dot(ref[0].astype(BF16), w_ref[j * c:(j + 1) * c, :], preferred_element_type=F32)
    x1 = x_ref[0] + mod_ref[0, 2:3, :] * y
    x1_ref[0] = x1
    h = _rmsnorm_mod(x1, g_ref[...], mod_ref[0, 4:5, :], mod_ref[0, 3:4, :])
    h_hi = h.astype(BF16)
    h_hi32 = h_hi.astype(F32)
    h_ref[0] = _pack_bf16_pairs(h_hi32)
    h_lo = (h - h_hi32).astype(BF16)
    tm = h.shape[0]
    prod = jnp.dot(jnp.concatenate([h_hi, h_lo], axis=0), rw_ref[...], preferred_element_type=F32)
    logits = (prod[:tm, :ROUTE_LANES] + prod[:tm, ROUTE_LANES:]
              + prod[tm:, :ROUTE_LANES] + prod[tm:, ROUTE_LANES:]) + rb_ref[...]
    lane = lax.broadcasted_iota(jnp.int32, logits.shape, 1)
    is_c = lane < N_GROUPS
    lc = jnp.where(is_c, logits, NEG_INF)
    mc = jnp.max(lc, axis=-1, keepdims=True)
    grp = jnp.min(jnp.where(lc == mc, lane, ROUTE_LANES), axis=-1, keepdims=True)
    pg = 1.0 / jnp.sum(jnp.where(is_c, jnp.exp(lc - mc), 0.0), axis=-1, keepdims=True)
    lo = N_GROUPS + grp * EXPERTS_PER_GROUP
    in_g = (lane >= lo) & (lane < lo + EXPERTS_PER_GROUP)
    lf = jnp.where(in_g, logits, NEG_INF)
    t1 = jnp.max(lf, axis=-1, keepdims=True)
    i1 = jnp.min(jnp.where(lf == t1, lane, ROUTE_LANES), axis=-1, keepdims=True)
    lf2 = jnp.where(lane == i1, NEG_INF, lf)
    t2 = jnp.max(lf2, axis=-1, keepdims=True)
    i2 = jnp.min(jnp.where(lf2 == t2, lane, ROUTE_LANES), axis=-1, keepdims=True)
    e2 = jnp.exp(t2 - t1)
    w1 = pg / (1.0 + e2)
    w2 = pg * e2 / (1.0 + e2)
    rec = jnp.where(lane == 0, (i1 - N_GROUPS).astype(F32),
                    jnp.where(lane == 1, (i2 - N_GROUPS).astype(F32),
                              jnp.where(lane == 2, w1, jnp.where(lane == 3, w2, 0.0))))
    route_ref[0] = rec

    @pl.when((pl.program_id(0) == 0) & (pl.program_id(1) == 0))
    def _():
        cnt_ref[...] = jnp.zeros(cnt_ref.shape, F32)

    e0, e1 = _choice_onehots(rec)
    cnt_ref[...] = cnt_ref[...] + jnp.sum(e0 + e1, axis=0, keepdims=True)


def out_projection(ys, x, mods, mod_row0, norm_g, w_out_bf, rw, rb, *, tm):
    b, l, d = x.shape
    c = MIX_CH
    row = (lambda bb: 0) if mod_row0 is None else (lambda bb: mod_row0 + bb)
    yspec = pl.BlockSpec((1, tm, c), lambda bb, i: (bb, i, 0))
    xspec = pl.BlockSpec((1, tm, d), lambda bb, i: (bb, i, 0))
    return pl.pallas_call(
        _out_body,
        grid=(b, l // tm),
        in_specs=[yspec] * 4 + [xspec,
                                pl.BlockSpec((1, N_MOD, d), lambda bb, i: (row(bb), 0, 0)),
                                pl.BlockSpec((1, d), lambda bb, i: (0, 0)),
                                pl.BlockSpec(w_out_bf.shape, lambda bb, i: (0, 0)),
                                pl.BlockSpec(rw.shape, lambda bb, i: (0, 0)),
                                pl.BlockSpec(rb.shape, lambda bb, i: (0, 0))],
        out_specs=[xspec, pl.BlockSpec((1, tm, d // 2), lambda bb, i: (bb, i, 0)),
                   pl.BlockSpec((1, tm, ROUTE_LANES), lambda bb, i: (bb, i, 0)),
                   pl.BlockSpec((SUBLANES, ROUTE_LANES), lambda bb, i: (0, 0))],
        out_shape=[jax.ShapeDtypeStruct((b, l, d), F32), jax.ShapeDtypeStruct((b, l, d // 2), jnp.uint32),
                   jax.ShapeDtypeStruct((b, l, ROUTE_LANES), F32),
                   jax.ShapeDtypeStruct((SUBLANES, ROUTE_LANES), F32)],
        compiler_params=_params("arbitrary", "arbitrary"),
        name="out_proj",
    )(*ys, x, mods, norm_g.reshape(1, d), w_out_bf, rw, rb)


def _expert_body(be_ref, nv_ref, xs_ref, wg_ref, wu_ref, wd_ref, o_ref, wg_s, wu_s, wd_s):
    i = pl.program_id(0)
    prev = be_ref[jnp.maximum(i - 1, 0)]

    @pl.when((i == 0) | (be_ref[i] != prev))
    def _():
        wg_s[...] = wg_ref[0, 0].astype(BF16)
        wu_s[...] = wu_ref[0, 0].astype(BF16)
        wd_s[...] = wd_ref[0, 0].astype(BF16)

    @pl.when(nv_ref[i] > 0)
    def _():
        row = lax.broadcasted_iota(jnp.int32, xs_ref.shape, 0)
        x = _unpack_bf16_pairs(jnp.where(row < nv_ref[i], xs_ref[...], jnp.uint32(0)))
        g = jnp.dot(x, wg_s[...], preferred_element_type=F32)
        u = jnp.dot(x, wu_s[...], preferred_element_type=F32)
        a = (_silu(g) * u).astype(BF16)
        y = jnp.dot(a, wd_s[...], preferred_element_type=F32)
        o_ref[...] = _pack_bf16_pairs(y.astype(BF16).astype(F32))

    @pl.when(nv_ref[i] <= 0)
    def _():
        o_ref[...] = jnp.zeros(o_ref.shape, jnp.uint32)


def expert_ffn(xs, blk_e, n_valid, layer, e_gate, e_up, e_down):
    rows, dh = xs.shape
    d = 2 * dh
    nb = rows // MOE_BLOCK
    de = e_gate.shape[-1]
    grid_spec = pltpu.PrefetchScalarGridSpec(
        num_scalar_prefetch=2,
        grid=(nb,),
        in_specs=[pl.BlockSpec((MOE_BLOCK, dh), lambda i, be, nv: (i, 0)),
                  pl.BlockSpec((1, 1, d, de), lambda i, be, nv: (layer, be[i], 0, 0)),
                  pl.BlockSpec((1, 1, d, de), lambda i, be, nv: (layer, be[i], 0, 0)),
                  pl.BlockSpec((1, 1, de, d), lambda i, be, nv: (layer, be[i], 0, 0))],
        out_specs=pl.BlockSpec((MOE_BLOCK, dh), lambda i, be, nv: (i, 0)),
        scratch_shapes=[pltpu.VMEM((d, de), BF16), pltpu.VMEM((d, de), BF16), pltpu.VMEM((de, d), BF16)],
    )
    return pl.pallas_call(
        _expert_body, grid_spec=grid_spec,
        out_shape=jax.ShapeDtypeStruct((rows, dh), jnp.uint32),
        compiler_params=_params("arbitrary"),
        name="expert_ffn",
    )(blk_e, n_valid, xs, e_gate, e_up, e_down)


RANK_TILE = 1024


def _choice_onehots(rec):
    lanef = lax.broadcasted_iota(jnp.int32, rec.shape, 1).astype(F32)
    return (lanef == rec[:, 0:1]).astype(F32), (lanef == rec[:, 1:2]).astype(F32)


def _slot_body(route_ref, cnt_ref, tri_ref, slot_ref, carry_ref):
    rec = route_ref[...]
    lane = lax.broadcasted_iota(jnp.int32, rec.shape, 1)
    e0, e1 = _choice_onehots(rec)
    both = e0 + e1

    @pl.when(pl.program_id(0) == 0)
    def _():
        cnt = cnt_ref[...]
        padded = jnp.floor((cnt + (MOE_BLOCK - 1)) * (1.0 / MOE_BLOCK)) * MOE_BLOCK
        ln = lax.broadcasted_iota(jnp.int32, cnt.shape, 1)
        incl = padded
        sh = 1
        while sh < ROUTE_LANES:
            incl = incl + jnp.where(ln >= sh, pltpu.roll(incl, sh, axis=1), 0.0)
            sh *= 2
        carry_ref[...] = incl - padded

    before = jnp.dot(tri_ref[...], both.astype(BF16), preferred_element_type=F32) + carry_ref[0:1, :]
    s0 = jnp.sum(e0 * before, axis=-1, keepdims=True)
    s1 = jnp.sum(e1 * before, axis=-1, keepdims=True)
    slot_ref[...] = jnp.where(lane == 0, s0, jnp.where(lane == 1, s1, 0.0))
    carry_ref[...] = carry_ref[...] + jnp.sum(both, axis=0, keepdims=True)


def moe_slots(route, counts):
    n = route.shape[0]
    t = RANK_TILE
    tri = jnp.asarray(np.tril(np.ones((t, t), np.float32), -1)).astype(BF16)
    return pl.pallas_call(
        _slot_body,
        grid=(n // t,),
        in_specs=[pl.BlockSpec((t, ROUTE_LANES), lambda i: (i, 0)),
                  pl.BlockSpec((SUBLANES, ROUTE_LANES), lambda i: (0, 0)),
                  pl.BlockSpec((t, t), lambda i: (0, 0))],
        out_specs=pl.BlockSpec((t, ROUTE_LANES), lambda i: (i, 0)),
        out_shape=jax.ShapeDtypeStruct((n, ROUTE_LANES), F32),
        scratch_shapes=[pltpu.VMEM((SUBLANES, ROUTE_LANES), F32)],
        compiler_params=_params("arbitrary"),
        name="moe_slots",
    )(route, counts, tri)


def _sc_mesh():
    return plsc.VectorSubcoreMesh(core_axis_name="c", subcore_axis_name="s")


def _sc_worker():
    return lax.axis_index("s") * SC_CORES + lax.axis_index("c")


DISPATCH_ROWS = 64
COMBINE_ROWS = 64


def sc_dispatch(rows, dest, n_slots):
    n, w = rows.shape
    ch = DISPATCH_ROWS
    per_w = n // SC_WORKERS
    n_ch = per_w // ch

    @functools.partial(
        pl.kernel, mesh=_sc_mesh(),
        out_type=jax.ShapeDtypeStruct((n_slots, w), rows.dtype),
        scratch_types=[pltpu.VMEM((ch,), jnp.int32), pltpu.VMEM((ch, w), rows.dtype)],
    )
    def scatter_kernel(rows_hbm, dest_hbm, out_hbm, idx_v, rows_v):
        wid = _sc_worker()

        @pl.loop(0, n_ch)
        def _(j):
            chunk = wid * n_ch + j
            pltpu.sync_copy(rows_hbm.at[pl.ds(pl.multiple_of(chunk * ch, ch), ch)], rows_v)
            for k in range(2):
                pltpu.sync_copy(dest_hbm.at[k, chunk], idx_v)
                pltpu.sync_copy(rows_v, out_hbm.at[idx_v])

    return scatter_kernel(rows, dest)


def sc_gather_rows(table, idx):
    s, w = table.shape
    m = idx.shape[0]
    ch = COMBINE_ROWS
    per_w = m // SC_WORKERS
    n_ch = per_w // ch

    @functools.partial(
        pl.kernel, mesh=_sc_mesh(),
        out_type=jax.ShapeDtypeStruct((m, w), table.dtype),
        scratch_types=[pltpu.VMEM((ch,), jnp.int32), pltpu.VMEM((ch, w), table.dtype), pltpu.SemaphoreType.DMA],
    )
    def gather_kernel(table_hbm, idx_hbm, out_hbm, idx_v, rows_v, sem):
        wid = _sc_worker()

        @pl.loop(0, n_ch)
        def _(j):
            off = pl.multiple_of((wid * n_ch + j) * ch, ch)
            pltpu.sync_copy(idx_hbm.at[pl.ds(off, ch)], idx_v)
            pltpu.async_copy(table_hbm.at[idx_v], rows_v, sem).wait()
            pltpu.sync_copy(rows_v, out_hbm.at[pl.ds(off, ch)])

    return gather_kernel(table, idx)


def hier_moe(h_packed, route, cnt, layer, e_gate, e_up, e_down):
    b, l, dh = h_packed.shape
    n = b * l
    assert n % (SC_WORKERS * DISPATCH_ROWS) == 0 and (2 * n) % (SC_WORKERS * COMBINE_ROWS) == 0
    slots = moe_slots(route.reshape(n, ROUTE_LANES), cnt)
    counts = cnt[0, :N_EXPERTS].astype(jnp.int32)
    padded = (counts + MOE_BLOCK - 1) // MOE_BLOCK * MOE_BLOCK
    pend = jnp.cumsum(padded)
    nb = -(-2 * n // MOE_BLOCK) + N_EXPERTS
    blk0 = jnp.arange(nb, dtype=jnp.int32) * MOE_BLOCK
    owner = pend[None, :] <= blk0[:, None]
    blk_e = jnp.minimum(jnp.sum(owner, axis=1), N_EXPERTS - 1).astype(jnp.int32)
    run_end = jnp.sum(jnp.where(jnp.arange(N_EXPERTS)[None, :] == blk_e[:, None],
                                (pend - padded + counts)[None, :], 0), axis=1)
    n_valid = jnp.clip(run_end - blk0, 0, MOE_BLOCK).astype(jnp.int32)
    dest = slots[:, 0:2].astype(jnp.int32).T
    xs = sc_dispatch(h_packed.reshape(n, dh), dest.reshape(2, n // DISPATCH_ROWS, DISPATCH_ROWS), nb * MOE_BLOCK)
    y = expert_ffn(xs, blk_e, n_valid, layer, e_gate, e_up, e_down)
    return sc_gather_rows(y, dest.reshape(2 * n)).reshape(2, b, l, dh)


def _final_body(x_ref, y0_ref, y1_ref, route_ref, pmod_ref, g_ref, o_ref):
    x = _moe_residual(x_ref[0], y0_ref, y1_ref, route_ref, pmod_ref)
    ms = jnp.mean(x * x, axis=-1, keepdims=True)
    o_ref[0] = x * lax.rsqrt(ms + EPS) * g_ref[...]


def final_norm(x1, res, mod_row0, norm_g, *, tm):
    b, l, d = x1.shape
    row = (lambda bb: 0) if mod_row0 is None else (lambda bb: mod_row0 + bb)
    xspec = pl.BlockSpec((1, tm, d), lambda bb, i: (bb, i, 0))
    rargs, rspecs = _residual_specs(res, tm, d, row)
    return pl.pallas_call(
        _final_body,
        grid=(b, l // tm),
        in_specs=[xspec] + rspecs + [pl.BlockSpec((1, d), lambda bb, i: (0, 0))],
        out_specs=xspec,
        out_shape=jax.ShapeDtypeStruct((b, l, d), F32),
        compiler_params=_params("arbitrary", "arbitrary"),
        name="final_norm",
    )(x1, *rargs, norm_g.reshape(1, d))


def _rope_tables(seq):
    rows = seq // GRID_W
    row_pos = jnp.repeat(jnp.arange(rows, dtype=F32), GRID_W)
    col_pos = jnp.tile(jnp.arange(GRID_W, dtype=F32), rows)
    n_freq = HEAD_DIM // 4
    inv = ROPE_BASE ** (-jnp.arange(n_freq, dtype=F32) / n_freq)
    ang = jnp.concatenate([row_pos[:, None] * inv, col_pos[:, None] * inv], axis=-1)
    cs, sn = jnp.cos(ang), jnp.sin(ang)
    cos_f = jnp.tile(jnp.concatenate([cs, cs], axis=-1), (1, N_Q_HEADS))
    sin_s = jnp.tile(jnp.concatenate([-sn, sn], axis=-1), (1, N_Q_HEADS))
    return cos_f, sin_s


def kernel(x_prompt, x_sample, cache_k, cache_v, c, c_ctx, ada_w, ada_b, norm1_g, norm2_g, w_in, conv_dw_w, conv_dw_b, conv_ln_g, conv_ln_b, hy_short_w, hy_short_b, hy_fw1, hy_fb1, hy_freq1, hy_fw2, hy_fb2, hy_freq2, hy_fw3, hy_fb3, hy_d, attn_sink, w_out, router_coarse_w, router_coarse_b, router_fine_w, router_fine_b, exp_gate, exp_up, exp_down, norm_f_g):
    depth = ada_w.shape[0]
    bp, lp, d = x_prompt.shape
    bs, ls, _ = x_sample.shape
    assert bp % 2 == 0 and bs % 2 == 0 and ls % ATT_BLOCK == 0 and ls % GRID_W == 0

    n_rows = -(-(1 + bs) // SUBLANES) * SUBLANES
    cvec = jnp.concatenate([c_ctx[None, :], c, jnp.zeros((n_rows - 1 - bs, d), F32)], axis=0)
    mods = adaln_all(cvec, ada_w, ada_b)

    rope = _rope_tables(ls)
    fnet_tabs, hy_tabs = {}, {}
    for seq in {lp, ls}:
        fnet_tabs[seq] = fnet_tables(seq)
        hy_tabs[seq] = hyena_tables(seq)
    ck = cache_k.reshape(cache_k.shape[0], depth, cache_k.shape[2], -1)
    cv = cache_v.reshape(cache_v.shape[0], depth, cache_v.shape[2], -1)
    pad = ROUTE_LANES - N_GROUPS - N_EXPERTS

    tm_p = min(lp, 512)
    tm_s = min(ls, 1024)
    xp, xs = x_prompt, x_sample
    res_p = res_s = None
    ks_out, vs_out = [], []
    for l in range(depth):
        w_in_bf = w_in[l].astype(BF16)
        w_out_bf = w_out[l].astype(BF16)
        rw = jnp.concatenate([router_coarse_w[l], router_fine_w[l], jnp.zeros((d, pad), F32)], axis=1)
        rw_hi = rw.astype(BF16)
        rw = jnp.concatenate([rw_hi, (rw - rw_hi.astype(F32)).astype(BF16)], axis=1)
        rb = jnp.concatenate([router_coarse_b[l], router_fine_b[l], jnp.zeros((pad,), F32)])[None, :]
        filt = (hy_fw1[l], hy_fb1[l], hy_freq1[l], hy_fw2[l], hy_fb2[l], hy_freq2[l], hy_fw3[l], hy_fb3[l])
        sink = attn_sink[l]

        def mixers(ua, ub, uc, yd, seq):
            ya = conformer_conv(ua, conv_dw_w[l], conv_dw_b[l], conv_ln_g[l], conv_ln_b[l])
            yb = hyena_mixer(ub, hy_short_w[l], hy_short_b[l], hy_d[l], hyena_filter_spectrum(seq, filt, hy_tabs[seq]),
                             hy_tabs[seq])
            yc = fnet_mixer(uc, fnet_tabs[seq])
            return (ya, yb, yc, yd)

        outs = in_projection(xp, mods[l], None, norm1_g[l], w_in_bf, res=res_p, tm=tm_p)
        if res_p is not None:
            xp, outs = outs[0], outs[1:]
        ua, ub, uc, uq, ukv = outs
        kvw = N_KV_HEADS * HEAD_DIM
        ks_out.append(ukv[..., :kvw].reshape(bp, lp, N_KV_HEADS, HEAD_DIM))
        vs_out.append(ukv[..., kvw:].reshape(bp, lp, N_KV_HEADS, HEAD_DIM))
        ys = mixers(ua, ub, uc, context_attention(uq, ukv, sink), lp)
        x1p, hp, route, cnt = out_projection(ys, xp, mods[l], None, norm2_g[l], w_out_bf, rw, rb, tm=tm_p)
        res_p = (hier_moe(hp, route, cnt, l, exp_gate, exp_up, exp_down), route, mods[l])
        xp = x1p

        outs = in_projection(xs, mods[l], 1, norm1_g[l], w_in_bf, res=res_s, rope=rope, tm=tm_s)
        if res_s is not None:
            xs, outs = outs[0], outs[1:]
        ua, ub, uc, uq, uqr, ukv = outs
        ys = mixers(ua, ub, uc, latent_attention(uq, uqr, ukv, ck, cv, l, sink), ls)
        x1s, hs, route, cnt = out_projection(ys, xs, mods[l], 1, norm2_g[l], w_out_bf, rw, rb, tm=tm_s)
        res_s = (hier_moe(hs, route, cnt, l, exp_gate, exp_up, exp_down), route, mods[l])
        xs = x1s

    y_prompt = final_norm(xp, res_p, None, norm_f_g, tm=tm_p)
    y_sample = final_norm(xs, res_s, 1, norm_f_g, tm=tm_s)
    return (y_prompt, y_sample, jnp.stack(ks_out, axis=1), jnp.stack(vs_out, axis=1))
```

```python
import functools
import math

import numpy as np
import jax
import jax.numpy as jnp
from jax import lax
from jax.experimental import pallas as pl
from jax.experimental.pallas import tpu as pltpu
from jax.experimental.pallas import tpu_sc as plsc

F32 = jnp.float32
BF16 = jnp.bfloat16

HEAD_DIM = 64
LOG2E = math.log2(math.e)
ATT_SCALE = HEAD_DIM ** -0.5 * LOG2E
N_Q_HEADS = 4
N_KV_HEADS = 2
Q_PER_KV = N_Q_HEADS // N_KV_HEADS
WINDOW = 128
ATT_BLOCK = 128
GRID_W = 64
ROPE_BASE = 10000.0
N_GROUPS = 4
EXPERTS_PER_GROUP = 8
N_EXPERTS = N_GROUPS * EXPERTS_PER_GROUP
MOE_BLOCK = 512
N_MOD = 6
EPS = 1e-6
NEG_INF = -1e30
HYENA_DECAY_TARGET = 1e-2
HYENA_FAST_PCT = 0.3
HYENA_SLOW_PCT = 1.5
HYENA_SHIFT = 0.05

LANES = 128
SUBLANES = 8
VMEM_LIMIT = 56 * 1024 * 1024

MIX_CH = 256
ROUTE_LANES = 128
SC_CORES = 2
SC_WORKERS = SC_CORES * 16


def _params(*sem):
    return pltpu.CompilerParams(dimension_semantics=sem, vmem_limit_bytes=VMEM_LIMIT)


def _silu(x):
    return x * jax.nn.sigmoid(x)


def _ada_body(c_ref, w_ref, b_ref, o_ref):
    s = _silu(c_ref[...]).astype(BF16)
    o_ref[0] = jnp.dot(s, w_ref[0].astype(BF16), preferred_element_type=F32) + b_ref[0]


def adaln_all(cvec, ada_w, ada_b):
    depth, d, n6 = ada_w.shape
    r = cvec.shape[0]
    tn = n6 // 4
    out = pl.pallas_call(
        _ada_body,
        grid=(depth, n6 // tn),
        in_specs=[
            pl.BlockSpec((r, d), lambda l, j: (0, 0)),
            pl.BlockSpec((1, d, tn), lambda l, j: (l, 0, j)),
            pl.BlockSpec((1, 1, tn), lambda l, j: (l, 0, j)),
        ],
        out_specs=pl.BlockSpec((1, r, tn), lambda l, j: (l, 0, j)),
        out_shape=jax.ShapeDtypeStruct((depth, r, n6), F32),
        compiler_params=_params("arbitrary", "arbitrary"),
        name="adaln",
    )(cvec, ada_w, ada_b.reshape(depth, 1, n6))
    return out.reshape(depth, r, N_MOD, d)


def _swap_halves(x):
    pieces = []
    for j in range(x.shape[1] // LANES):
        xj = x[:, j * LANES:(j + 1) * LANES]
        fwd = pltpu.roll(xj, LANES - HEAD_DIM // 2, axis=1)
        bwd = pltpu.roll(xj, HEAD_DIM // 2, axis=1)
        lane = lax.broadcasted_iota(jnp.int32, xj.shape, 1)
        pieces.append(jnp.where((lane % HEAD_DIM) < HEAD_DIM // 2, fwd, bwd))
    return pieces[0] if len(pieces) == 1 else jnp.concatenate(pieces, axis=1)


def _rmsnorm_mod(x, g, scale, shift):
    ms = jnp.mean(x * x, axis=-1, keepdims=True)
    return (x * lax.rsqrt(ms + EPS)) * (g * (1.0 + scale)) + shift


def _moe_residual(x1, y0_ref, y1_ref, route_ref, pmod_ref, rows=slice(None)):
    y0 = _unpack_bf16_pairs(y0_ref[0, 0, rows, :], F32)
    y1 = _unpack_bf16_pairs(y1_ref[0, 0, rows, :], F32)
    moe = route_ref[0, rows, 2:3] * y0 + route_ref[0, rows, 3:4] * y1
    return x1 + pmod_ref[0, 5:6, :] * moe


def _residual_specs(res, tm, d, row):
    pair, route, pmods = res
    args = [pair, pair, route, pmods]
    specs = [pl.BlockSpec((1, 1, tm, d // 2), lambda bb, i: (0, bb, i, 0)),
             pl.BlockSpec((1, 1, tm, d // 2), lambda bb, i: (1, bb, i, 0)),
             pl.BlockSpec((1, tm, ROUTE_LANES), lambda bb, i: (bb, i, 0)),
             pl.BlockSpec((1, N_MOD, d), lambda bb, i: (row(bb), 0, 0))]
    return args, specs


IN_PART_ROWS = 256


def _in_body(*refs, fuse_res, rope):
    it = iter(refs)
    x_ref = next(it)
    if fuse_res:
        res_refs = [next(it) for _ in range(4)]
    mod_ref = next(it)
    g_ref = next(it)
    w_ref = next(it)
    if rope:
        cos_ref = next(it)
        sin_ref = next(it)
    outs = list(it)
    xo_ref = outs.pop(0) if fuse_res else None
    c = MIX_CH
    n_parts = max(2, x_ref.shape[1] // IN_PART_ROWS)
    part = x_ref.shape[1] // n_parts
    hs = []
    for p in range(n_parts):
        rows = slice(p * part, (p + 1) * part)
        x = x_ref[0, rows, :]
        if fuse_res:
            x = _moe_residual(x, *res_refs, rows)
            xo_ref[0, rows, :] = x
        hs.append(_rmsnorm_mod(x, g_ref[...], mod_ref[0, 1:2, :], mod_ref[0, 0:1, :]).astype(BF16))
    ua_ref, ub_ref, uc_ref, uq_ref = outs[:4]
    for p in range(n_parts):
        rows = slice(p * part, (p + 1) * part)
        u = jnp.dot(hs[p], w_ref[...], preferred_element_type=F32)
        ua_ref[0, rows, :] = u[:, 0:2 * c]
        ub_ref[0, rows, :] = u[:, 2 * c:5 * c]
        uc_ref[0, rows, :] = u[:, 5 * c:6 * c]
        q = u[:, 6 * c:7 * c] * ATT_SCALE
        k = u[:, 7 * c:7 * c + c // 2]
        v = u[:, 7 * c + c // 2:8 * c]
        uq_ref[0, rows, :] = q.astype(uq_ref.dtype)
        if rope:
            uqr_ref, ukv_ref = outs[4:]
            cs = cos_ref[rows, :]
            sn = sin_ref[rows, :]
            uqr_ref[0, rows, :] = (q * cs + _swap_halves(q) * sn).astype(uqr_ref.dtype)
            kr = k * cs[:, :c // 2] + _swap_halves(k) * sn[:, :c // 2]
            ukv_ref[0, rows, :] = jnp.concatenate([kr, v], axis=1).astype(ukv_ref.dtype)
        else:
            outs[4][0, rows, :] = u[:, 7 * c:8 * c]


def in_projection(x, mods, mod_row0, norm_g, w_in_bf, *, res=None, rope=None, tm):
    b, l, d = x.shape
    c = MIX_CH
    grid = (b, l // tm)
    row = (lambda bb: 0) if mod_row0 is None else (lambda bb: mod_row0 + bb)
    xspec = pl.BlockSpec((1, tm, d), lambda bb, i: (bb, i, 0))
    mspec = pl.BlockSpec((1, N_MOD, d), lambda bb, i: (row(bb), 0, 0))
    args, specs = [x], [xspec]
    if res is not None:
        rargs, rspecs = _residual_specs(res, tm, d, row)
        args += rargs
        specs += rspecs
    args += [mods, norm_g.reshape(1, d), w_in_bf]
    specs += [mspec, pl.BlockSpec((1, d), lambda bb, i: (0, 0)),
              pl.BlockSpec(w_in_bf.shape, lambda bb, i: (0, 0))]
    if rope is not None:
        args += [rope[0], rope[1]]
        specs += [pl.BlockSpec((tm, c), lambda bb, i: (i, 0))] * 2

    def ospec(w):
        return pl.BlockSpec((1, tm, w), lambda bb, i: (bb, i, 0))

    out_shape, out_specs = [], []
    if res is not None:
        out_shape.append(jax.ShapeDtypeStruct((b, l, d), F32))
        out_specs.append(xspec)
    widths = [(2 * c, F32), (3 * c, F32), (c, F32), (c, BF16)] + ([(c, BF16), (c, BF16)] if rope is not None else [(c, F32)])
    for w, dt in widths:
        out_shape.append(jax.ShapeDtypeStruct((b, l, w), dt))
        out_specs.append(ospec(w))
    return pl.pallas_call(
        functools.partial(_in_body, fuse_res=res is not None, rope=rope is not None),
        grid=grid, in_specs=specs, out_specs=out_specs, out_shape=out_shape,
        compiler_params=_params("arbitrary", "arbitrary"),
        name="in_proj",
    )(*args)


def _dw_tile(win, w_ref, n_taps, first, rows):
    acc = w_ref[0:1, :] * win[first:first + rows]
    for k in range(1, n_taps):
        acc = acc + w_ref[k:k + 1, :] * win[first + k:first + k + rows]
    return acc


CONV_PAD = 16
CONV_ROWS = 256
CONV_CHUNK = 1024
CONV_TAIL = CONV_PAD + CONV_ROWS + SUBLANES + SUBLANES


def _conf_body(u_ref, w_ref, b_ref, g_ref, beta_ref, o_ref, gp_ref, sh_ref, *, seq, n_taps, chunk):
    c = MIX_CH
    r = CONV_ROWS
    first = CONV_PAD - (n_taps - 1) // 2
    gp_ref[0:CONV_PAD, :] = jnp.zeros((CONV_PAD, c), F32)
    gp_ref[CONV_PAD + seq:CONV_PAD + seq + CONV_TAIL, :] = jnp.zeros((CONV_TAIL, c), F32)

    def fill(i, carry):
        r0 = pl.multiple_of(i * r, r)
        a = u_ref[0, pl.ds(r0, r), 0:c]
        g = u_ref[0, pl.ds(r0, r), c:2 * c]
        gp_ref[pl.ds(CONV_PAD + r0, r), :] = a * jax.nn.sigmoid(g)
        return carry

    lax.fori_loop(0, seq // r, fill, 0)
    n_copy_tiles = sh_ref.shape[1] // r

    def do_chunk(ci, carry):
        c0 = pl.multiple_of(ci * chunk, chunk)

        def shift_tile(ti, carry2):
            t0 = pl.multiple_of(ti * r, r)
            win = gp_ref[pl.ds(c0 + t0, r + SUBLANES), :]
            for m in range(1, SUBLANES):
                sh_ref[m - 1, pl.ds(t0, r), :] = win[m:m + r]
            return carry2

        lax.fori_loop(0, n_copy_tiles, shift_tile, 0)

        def tile(ti, carry2):
            t0 = pl.multiple_of(ti * r, r)
            acc = None
            for k in range(n_taps):
                a8, m = (first + k) // SUBLANES * SUBLANES, (first + k) % SUBLANES
                src = gp_ref[pl.ds(c0 + t0 + a8, r), :] if m == 0 else sh_ref[m - 1, pl.ds(t0 + a8, r), :]
                term = w_ref[k:k + 1, :] * src
                acc = term if acc is None else acc + term
            z = acc + b_ref[...]
            mu = jnp.mean(z, axis=-1, keepdims=True)
            zc = z - mu
            var = jnp.mean(zc * zc, axis=-1, keepdims=True)
            zn = zc * lax.rsqrt(var + EPS) * g_ref[...] + beta_ref[...]
            o_ref[0, pl.ds(c0 + t0, r), :] = _silu(zn).astype(o_ref.dtype)
            return carry2

        lax.fori_loop(0, chunk // r, tile, 0)
        return carry

    lax.fori_loop(0, seq // chunk, do_chunk, 0)


def conformer_conv(ua, dw_w, dw_b, ln_g, ln_b):
    b, l, c2 = ua.shape
    c = MIX_CH
    k = dw_w.shape[0]
    chunk = min(l, CONV_CHUNK)
    copy_rows = -(-(chunk + CONV_PAD + k) // CONV_ROWS) * CONV_ROWS
    assert l % chunk == 0 and k - 1 <= 2 * CONV_PAD and copy_rows - chunk + SUBLANES <= CONV_PAD + CONV_TAIL
    vec = pl.BlockSpec((1, c), lambda bb: (0, 0))
    return pl.pallas_call(
        functools.partial(_conf_body, seq=l, n_taps=k, chunk=chunk),
        grid=(b,),
        in_specs=[pl.BlockSpec((1, l, c2), lambda bb: (bb, 0, 0)),
                  pl.BlockSpec((k, c), lambda bb: (0, 0)), vec, vec, vec],
        out_specs=pl.BlockSpec((1, l, c), lambda bb: (bb, 0, 0)),
        out_shape=jax.ShapeDtypeStruct((b, l, c), BF16),
        scratch_shapes=[pltpu.VMEM((CONV_PAD + l + CONV_TAIL, c), F32),
                        pltpu.VMEM((SUBLANES - 1, copy_rows, c), F32)],
        compiler_params=_params("arbitrary"),
        name="conformer",
    )(ua, dw_w, dw_b.reshape(1, c), ln_g.reshape(1, c), ln_b.reshape(1, c))


SHORT_PAD = 8
SHORT_ROWS = 64


def _short_body(u_ref, w_ref, b_ref, o_ref, xp_ref, *, seq, n_taps):
    c = MIX_CH
    r = SHORT_ROWS
    half = (n_taps - 1) // 2
    zero = jnp.zeros((SHORT_PAD, c), F32)
    xp_ref[0:SHORT_PAD, :] = zero
    xp_ref[SHORT_PAD + seq:SHORT_PAD + seq + SHORT_PAD, :] = zero
    for q in range(u_ref.shape[0]):
        def fill(i, carry):
            r0 = pl.multiple_of(i * r, r)
            xp_ref[pl.ds(SHORT_PAD + r0, r), :] = u_ref[q, pl.ds(r0, r), :]
            return carry

        lax.fori_loop(0, seq // r, fill, 0)

        def tile(i, carry):
            r0 = pl.multiple_of(i * r, r)
            win = xp_ref[pl.ds(r0, r + 2 * SHORT_PAD), :]
            o_ref[q, pl.ds(r0, r), :] = _dw_tile(win, w_ref, n_taps, SHORT_PAD - half, r) + b_ref[...]
            return carry

        lax.fori_loop(0, seq // r, tile, 0)


SHORT_BLOCK_ROWS = 1024


def hyena_short_conv(ub, short_w, short_b):
    b, l, c3 = ub.shape
    c = MIX_CH
    k = short_w.shape[0]
    nbat = math.gcd(b, max(1, SHORT_BLOCK_ROWS // l))
    return pl.pallas_call(
        functools.partial(_short_body, seq=l, n_taps=k),
        grid=(b // nbat, c3 // c),
        in_specs=[pl.BlockSpec((nbat, l, c), lambda bb, j: (bb, 0, j)),
                  pl.BlockSpec((k, c), lambda bb, j: (0, j)),
                  pl.BlockSpec((1, c), lambda bb, j: (0, j))],
        out_specs=pl.BlockSpec((nbat, l, c), lambda bb, j: (bb, 0, j)),
        out_shape=jax.ShapeDtypeStruct((b, l, c3), F32),
        scratch_shapes=[pltpu.VMEM((l + 2 * SHORT_PAD, c), F32)],
        compiler_params=_params("arbitrary", "arbitrary"),
        name="hyena_short",
    )(ub, short_w, short_b.reshape(1, c3))


def _fft_sizes(seq):
    n = 2 * seq
    n2 = 128 if n >= 4096 else 32
    return n, n // n2, n2


FFT_GROUP = SUBLANES


def _kron_matrices(n, n1, h):
    k1 = np.arange(n1)[:, None]
    i1 = np.arange(h)[None, :]
    th = 2.0 * np.pi * ((k1 * i1) % n1) / n1
    eye = np.eye(FFT_GROUP)
    cs, sn = np.kron(np.cos(th), eye), np.kron(np.sin(th), eye)
    fwd = np.block([[cs, sn], [-sn, cs]])
    inv = np.block([[cs.T, -sn.T], [sn.T, cs.T]]) / n
    return fwd.astype(np.float32), inv.astype(np.float32)


def _twiddle_tables(n, n1, n2):
    k1 = jnp.arange(n1, dtype=jnp.int32)[:, None]
    i2 = jnp.arange(n2, dtype=jnp.int32)[None, :]
    th = ((k1 * i2) % n).astype(F32) * (2.0 * math.pi / n)
    shape = (n1, n2 // FFT_GROUP, FFT_GROUP, LANES)
    full = lambda a: jnp.broadcast_to(a.reshape(shape[:3] + (1,)), shape)
    return full(jnp.cos(th)), full(jnp.sin(th))


def _mid_tables(n2):
    a = np.arange(n2)
    th = 2.0 * np.pi * ((a[:, None] * a[None, :]) % n2) / n2
    cs, sn = np.cos(th), np.sin(th)
    fwd = np.concatenate([np.concatenate([cs, sn], 1), np.concatenate([-sn, cs], 1)], 0)
    inv = np.concatenate([np.concatenate([cs, -sn], 1), np.concatenate([sn, cs], 1)], 0)
    return fwd.astype(np.float32), inv.astype(np.float32)


MID_CHUNK = 32


def _dot(a, b):
    return jnp.dot(a.astype(BF16), b.astype(BF16), preferred_element_type=F32)


def _filter_stage1_body(x_ref, m_ref, c_ref, s_ref, o_ref, *, n1, gs):
    j, c = FFT_GROUP, MIX_CH
    for s in range(gs):
        a = _dot(m_ref[...], x_ref[:, s].reshape(n1 * j, c))
        ar = a[:n1 * j].reshape(n1, j, c)
        ai = a[n1 * j:].reshape(n1, j, c)
        cs, sn = _lanes(c_ref[:, s], c), _lanes(s_ref[:, s], c)
        o_ref[0, 0, :, s] = ar * cs + ai * sn
        o_ref[0, 1, :, s] = ai * cs - ar * sn


def filter_stage1(k, order, mat, twc, tws, *, n1, n2):
    g, j, c = n2 // FFT_GROUP, FFT_GROUP, MIX_CH
    gs = _group_step(g)
    tw = pl.BlockSpec((n1, gs, j, LANES), lambda gi: (0, gi, 0, 0))
    out = pl.pallas_call(
        functools.partial(_filter_stage1_body, n1=n1, gs=gs),
        grid=(g // gs,),
        in_specs=[pl.BlockSpec((n1, gs, j, c), lambda gi: (0, gi, 0, order)),
                  pl.BlockSpec(mat.shape, lambda gi: (0, 0)), tw, tw],
        out_specs=pl.BlockSpec((1, 2, n1, gs, j, c), lambda gi: (0, 0, 0, gi, 0, 0)),
        out_shape=jax.ShapeDtypeStruct((1, 2, n1, g, j, c), F32),
        compiler_params=_params("arbitrary"),
        name="filter_stage1",
    )(k.reshape(n1, g, j, k.shape[-1]), mat, twc, tws)
    return out.reshape(1, 2, n1 * n2, c)


def _mid_body(a_ref, k_ref, f_ref, g_ref, o_ref, *, n2, kc):
    for j in range(kc):
        rows = slice(j * n2, (j + 1) * n2)
        blk = jnp.concatenate([a_ref[0, 0, rows, :], a_ref[0, 1, rows, :]], axis=0)
        s = _dot(f_ref[...], blk)
        sr, si = s[:n2], s[n2:]
        kr, ki = k_ref[0, rows, :], k_ref[1, rows, :]
        y = jnp.concatenate([sr * kr - si * ki, sr * ki + si * kr], axis=0)
        bb = _dot(g_ref[...], y)
        o_ref[0, 0, rows, :] = bb[:n2].astype(o_ref.dtype)
        o_ref[0, 1, rows, :] = bb[n2:].astype(o_ref.dtype)


def fft_mid(a, kf, f_fwd, f_inv, *, n1, n2):
    p = a.shape[0]
    c = MIX_CH
    kc = min(n1, MID_CHUNK)
    rows = kc * n2
    blk = pl.BlockSpec((1, 2, rows, c), lambda j, pp: (pp, 0, j, 0))
    mat = pl.BlockSpec(f_fwd.shape, lambda j, pp: (0, 0))
    return pl.pallas_call(
        functools.partial(_mid_body, n2=n2, kc=kc),
        grid=(n1 // kc, p),
        in_specs=[blk, pl.BlockSpec((2, rows, c), lambda j, pp: (0, j, 0)), mat, mat],
        out_specs=blk,
        out_shape=jax.ShapeDtypeStruct(a.shape, a.dtype),
        compiler_params=_params("arbitrary", "arbitrary"),
        name="fft_mid",
    )(a, kf, f_fwd, f_inv)


def _filter_mid_body(a_ref, f_ref, sum_ref, o_ref, *, n2, kc):
    inv = 1.0 / (sum_ref[0:1, :] + EPS)
    for j in range(kc):
        rows = slice(j * n2, (j + 1) * n2)
        blk = jnp.concatenate([a_ref[0, 0, rows, :], a_ref[0, 1, rows, :]], axis=0)
        s = _dot(f_ref[...], blk) * inv
        o_ref[0, rows, :] = s[:n2]
        o_ref[1, rows, :] = s[n2:]


def filter_mid(a, f_fwd, abs_sum, order, *, n1, n2):
    c = MIX_CH
    kc = min(n1, MID_CHUNK)
    rows = kc * n2
    return pl.pallas_call(
        functools.partial(_filter_mid_body, n2=n2, kc=kc),
        grid=(n1 // kc,),
        in_specs=[pl.BlockSpec((1, 2, rows, c), lambda j: (0, 0, j, 0)),
                  pl.BlockSpec(f_fwd.shape, lambda j: (0, 0)),
                  pl.BlockSpec((SUBLANES, c), lambda j: (0, order))],
        out_specs=pl.BlockSpec((2, rows, c), lambda j: (0, j, 0)),
        out_shape=jax.ShapeDtypeStruct((2, n1 * n2, c), F32),
        compiler_params=_params("arbitrary"),
        name="filter_mid",
    )(a, f_fwd, abs_sum)


def _lanes(t, width):
    return t if width == LANES else jnp.concatenate([t] * (width // LANES), axis=-1)


WORK_ROWS = 2 * FFT_GROUP


def _store_group_pair(o_ref, parts, s):
    if s % 2 == 1:
        for plane in range(2):
            o_ref[0, plane, :, s // 2] = jnp.concatenate([parts[s - 1][plane], parts[s][plane]],
                                                         axis=1).astype(o_ref.dtype)


def _kron1_body(zr_ref, zi_ref, m_ref, c_ref, s_ref, o_ref, *, n1, h, gs):
    j, c = FFT_GROUP, MIX_CH
    parts = []
    for s in range(gs):
        xr = zr_ref[0, :, s].reshape(h * j, c)
        xi = zi_ref[0, :, s].reshape(h * j, c)
        a = _dot(m_ref[...], jnp.concatenate([xr, xi], axis=0))
        ar = a[:n1 * j].reshape(n1, j, c)
        ai = a[n1 * j:].reshape(n1, j, c)
        cs, sn = _lanes(c_ref[:, s], c), _lanes(s_ref[:, s], c)
        parts.append((ar * cs + ai * sn, ai * cs - ar * sn))
        _store_group_pair(o_ref, parts, s)


def _group_step(n_groups):
    return min(n_groups, 8)


def kron_stage1(z5, col, n_pairs, imag_offset, mat, twc, tws, *, n1, h):
    g, j, c = z5.shape[2], FFT_GROUP, MIX_CH
    gs = _group_step(g)
    tw = pl.BlockSpec((n1, gs, j, LANES), lambda gi, p: (0, gi, 0, 0))
    return pl.pallas_call(
        functools.partial(_kron1_body, n1=n1, h=h, gs=gs),
        grid=(g // gs, n_pairs),
        in_specs=[pl.BlockSpec((1, h, gs, j, c), lambda gi, p: (p, 0, gi, 0, col)),
                  pl.BlockSpec((1, h, gs, j, c), lambda gi, p: (p + imag_offset, 0, gi, 0, col)),
                  pl.BlockSpec(mat.shape, lambda gi, p: (0, 0)), tw, tw],
        out_specs=pl.BlockSpec((1, 2, n1, gs // 2, WORK_ROWS, c), lambda gi, p: (p, 0, 0, gi, 0, 0)),
        out_shape=jax.ShapeDtypeStruct((n_pairs, 2, n1, g // 2, WORK_ROWS, c), BF16),
        compiler_params=_params("arbitrary", "arbitrary"),
        name="fft_kron1",
    )(z5, z5, mat, twc, tws)


def _kron3_body(b_ref, m_ref, c_ref, s_ref, zr_ref, zi_ref, gr_ref, gi_ref, d_ref, o_ref, *, n1, h, gs):
    j, c = FFT_GROUP, MIX_CH
    d = d_ref[...].reshape(1, 1, c)
    for s in range(gs):
        half = slice((s % 2) * j, (s % 2 + 1) * j)
        br = b_ref[0, 0, :, s // 2].astype(F32)[:, half]
        bi = b_ref[0, 1, :, s // 2].astype(F32)[:, half]
        cs, sn = _lanes(c_ref[:, s], c), _lanes(s_ref[:, s], c)
        xr = (br * cs - bi * sn).reshape(n1 * j, c)
        xi = (br * sn + bi * cs).reshape(n1 * j, c)
        y = _dot(m_ref[...], jnp.concatenate([xr, xi], axis=0))
        yr = y[:h * j].reshape(h, j, c)
        yi = y[h * j:].reshape(h, j, c)
        o_ref[0, 0, :, s] = gr_ref[0, :, s] * (yr + d * zr_ref[0, :, s])
        o_ref[1, 0, :, s] = gi_ref[0, :, s] * (yi + d * zi_ref[0, :, s])


def kron_stage3(bw, mat, twc, tws, z5, z_col, g5, g_col, d_vec, imag_offset, *, n1, h):
    p = bw.shape[0]
    g, j, c = bw.shape[3] * 2, FFT_GROUP, MIX_CH
    gs = _group_step(g)
    tw = pl.BlockSpec((n1, gs, j, LANES), lambda gi, pp: (0, gi, 0, 0))

    def src(col, off):
        return pl.BlockSpec((1, h, gs, j, c), lambda gi, pp: (pp + off, 0, gi, 0, col))

    return pl.pallas_call(
        functools.partial(_kron3_body, n1=n1, h=h, gs=gs),
        grid=(g // gs, p),
        in_specs=[pl.BlockSpec((1, 2, n1, gs // 2, WORK_ROWS, c), lambda gi, pp: (pp, 0, 0, gi, 0, 0)),
                  pl.BlockSpec(mat.shape, lambda gi, pp: (0, 0)), tw, tw,
                  src(z_col, 0), src(z_col, imag_offset), src(g_col, 0), src(g_col, imag_offset),
                  pl.BlockSpec((1, c), lambda gi, pp: (0, 0))],
        out_specs=pl.BlockSpec((2, 1, h, gs, j, c), lambda gi, pp: (0, pp, 0, gi, 0, 0)),
        out_shape=jax.ShapeDtypeStruct((2, p, h, g, j, c), F32),
        compiler_params=_params("arbitrary", "arbitrary"),
        name="fft_kron3",
    )(bw, mat, twc, tws, z5, z5, g5, g5, d_vec.reshape(1, c))


FILTER_TILE = 512


def _lane_dense(fn, a):
    r, w = a.shape
    f = LANES // w
    rows = r // f
    dense = jnp.concatenate([a[q * rows:(q + 1) * rows] for q in range(f)], axis=1)
    out = fn(dense)
    return jnp.concatenate([out[:, q * w:(q + 1) * w] for q in range(f)], axis=0)


def _filter_body(fw1_ref, fb1_ref, fr1_ref, fw2_ref, fb2_ref, fr2_ref, fw3_ref, fb3_ref, bands_ref, decay_ref,
                 k_ref, sum_ref, *, seq):
    i = pl.program_id(0)
    c = MIX_CH
    hp = lax.Precision.HIGHEST
    n = i * FILTER_TILE + lax.broadcasted_iota(jnp.int32, (FILTER_TILE, 1), 0)
    pos = jnp.where(n <= seq, n, 2 * seq - n).astype(F32)
    t = pos * (1.0 / (seq - 1))
    ang = (pos * (2.0 * math.pi / seq)) * bands_ref[...]
    nb = bands_ref.shape[1]
    pre = (t * fw1_ref[0:1, :]
           + jnp.dot(_lane_dense(jnp.cos, ang), fw1_ref[1:1 + nb, :], preferred_element_type=F32, precision=hp)
           - jnp.dot(_lane_dense(jnp.sin, ang), fw1_ref[1 + nb:1 + 2 * nb, :], preferred_element_type=F32,
                     precision=hp)
           + fb1_ref[...])
    h = _lane_dense(jnp.sin, fr1_ref[...] * pre)
    h = _lane_dense(jnp.sin, fr2_ref[...] * (jnp.dot(h, fw2_ref[...], preferred_element_type=F32, precision=hp)
                                             + fb2_ref[...]))
    h = jnp.dot(h, fw3_ref[...], preferred_element_type=F32, precision=hp) + fb3_ref[...]
    win = jnp.exp(-t * decay_ref[...]) + HYENA_SHIFT
    win = jnp.concatenate([win, win], axis=1)
    fwd, bwd = h[:, :2 * c], h[:, 2 * c:]
    k = jnp.where(n < seq, fwd, bwd) + jnp.where(n == 0, bwd, 0.0)
    k = jnp.where(n == seq, 0.0, k) * win
    k_ref[...] = k

    @pl.when(i == 0)
    def _():
        sum_ref[...] = jnp.zeros(sum_ref.shape, F32)

    sum_ref[...] = sum_ref[...] + jnp.sum(jnp.abs(k), axis=0, keepdims=True)


def hyena_filter_time(seq, fw1, fb1, fr1, fw2, fb2, fr2, fw3, fb3):
    c = MIX_CH
    n_bands = (fw1.shape[0] - 1) // 2
    bands = jnp.linspace(1e-4, n_bands - 1, n_bands, dtype=F32)[None, :]
    max_decay = math.log(HYENA_DECAY_TARGET) / HYENA_FAST_PCT
    min_decay = math.log(HYENA_DECAY_TARGET) / HYENA_SLOW_PCT
    decay = jnp.abs(jnp.linspace(min_decay, max_decay, c, dtype=F32))[None, :]
    assert (2 * seq) % FILTER_TILE == 0
    args = [fw1, fb1[None, :], fr1[None, :], fw2, fb2[None, :], fr2[None, :], fw3, fb3[None, :], bands, decay]
    return pl.pallas_call(
        functools.partial(_filter_body, seq=seq),
        grid=(2 * seq // FILTER_TILE,),
        in_specs=[pl.BlockSpec(a.shape, lambda i: (0, 0)) for a in args],
        out_specs=[pl.BlockSpec((FILTER_TILE, 2 * c), lambda i: (i, 0)),
                   pl.BlockSpec((SUBLANES, 2 * c), lambda i: (0, 0))],
        out_shape=[jax.ShapeDtypeStruct((2 * seq, 2 * c), F32), jax.ShapeDtypeStruct((SUBLANES, 2 * c), F32)],
        compiler_params=_params("arbitrary"),
        name="hyena_filter",
    )(*args)


def hyena_filter_spectrum(seq, filt, tabs):
    n, n1, n2 = _fft_sizes(seq)
    k, abs_sum = hyena_filter_time(seq, *filt)
    twc, tws, f_fwd, m_real = tabs[2], tabs[3], tabs[4], tabs[6]
    return [filter_mid(filter_stage1(k, o, m_real, twc, tws, n1=n1, n2=n2), f_fwd, abs_sum, o, n1=n1, n2=n2)
            for o in range(2)]


def hyena_mixer(ub, short_w, short_b, hy_d, kf, tabs):
    b, l, _ = ub.shape
    n, n1, n2 = _fft_sizes(l)
    h = n1 // 2
    p = b // 2
    c = MIX_CH
    g = n2 // FFT_GROUP
    u = hyena_short_conv(ub, short_w, short_b)
    u5 = u.reshape(b, h, g, FFT_GROUP, u.shape[-1])
    m_fwd, m_inv, twc, tws, f_fwd, f_inv = tabs[:6]
    z5 = u5
    for o in range(2):
        a = kron_stage1(z5, 0, p, p, m_fwd, twc, tws, n1=n1, h=h)
        bw = fft_mid(a.reshape(p, 2, n, c), kf[o], f_fwd, f_inv, n1=n1, n2=n2)
        z = kron_stage3(bw.reshape(a.shape), m_inv, twc, tws, z5, 0, u5, 1 + o, hy_d[o], p, n1=n1, h=h)
        z5 = z.reshape(b, h, g, FFT_GROUP, c)
    return z5.reshape(b, l, c)


def hyena_tables(seq):
    n, n1, n2 = _fft_sizes(seq)
    m_fwd, m_inv = _kron_matrices(n, n1, n1 // 2)
    m_real = _kron_matrices(n, n1, n1)[0][:, :n1 * FFT_GROUP]
    twc, tws = _twiddle_tables(n, n1, n2)
    f_fwd, f_inv = _mid_tables(n2)
    bf = lambda a: jnp.asarray(a).astype(BF16)
    return bf(m_fwd), bf(m_inv), twc, tws, bf(f_fwd), bf(f_inv), bf(m_real)


def _fnet_body(cl_ref, sl_ref, x_ref, cc_ref, sc_ref, o_ref, *, scale):
    x = x_ref[0].astype(BF16)
    pr = jnp.dot(cl_ref[...], x, preferred_element_type=F32).astype(BF16)
    qr = jnp.dot(sl_ref[...], x, preferred_element_type=F32).astype(BF16)
    o_ref[0] = (jnp.dot(pr, cc_ref[...], preferred_element_type=F32)
                - jnp.dot(qr, sc_ref[...], preferred_element_type=F32)) * scale


def _dft_tables(n):
    a = jnp.arange(n, dtype=jnp.int32)
    th = ((a[:, None] * a[None, :]) % n).astype(F32) * (2.0 * math.pi / n)
    return jnp.cos(th).astype(BF16), jnp.sin(th).astype(BF16)


FNET_DIRECT_MAX = 1024


def _fnet1_body(x_ref, ccs_ref, m_ref, c_ref, s_ref, o_ref, *, n1, gs):
    j, c = FFT_GROUP, MIX_CH
    parts = []
    for s in range(gs):
        x = x_ref[0, :, s].reshape(n1 * j, c).astype(BF16)
        z = jnp.dot(x, ccs_ref[...], preferred_element_type=F32)
        a = _dot(m_ref[...], jnp.concatenate([z[:, :c], z[:, c:]], axis=0))
        ar = a[:n1 * j].reshape(n1, j, c)
        ai = a[n1 * j:].reshape(n1, j, c)
        cs, sn = _lanes(c_ref[:, s], c), _lanes(s_ref[:, s], c)
        parts.append((ar * cs + ai * sn, ai * cs - ar * sn))
        _store_group_pair(o_ref, parts, s)


def _fnet2_body(a_ref, m_ref, o_ref, *, n2, scale):
    j, c = FFT_GROUP, MIX_CH
    x = a_ref[0].reshape(2 * j * n2, c)
    y = _dot(m_ref[...], x) * scale
    o_ref[0, :, 0] = y.reshape(n2, j, c)


def fnet_tables(seq):
    c = MIX_CH
    cc, sc = _dft_tables(c)
    if seq <= FNET_DIRECT_MAX:
        return _dft_tables(seq) + (cc, sc)
    n2 = 128
    n1 = seq // n2
    m_fwd, _ = _kron_matrices(seq, n1, n1)
    twc, tws = _twiddle_tables(seq, n1, n2)
    a = np.arange(n2)
    th = 2.0 * np.pi * ((a[:, None] * a[None, :]) % n2) / n2
    eye = np.eye(FFT_GROUP)
    m2 = np.concatenate([np.einsum('kn,ij->kijn', f, eye).reshape(n2 * FFT_GROUP, FFT_GROUP * n2)
                         for f in (np.cos(th), np.sin(th))], axis=1).astype(np.float32)
    ccs = jnp.concatenate([cc, -sc], axis=1)
    return ccs, jnp.asarray(m_fwd).astype(BF16), twc, tws, jnp.asarray(m2).astype(BF16)


def fnet_two_stage(uc, tables):
    b, l, c = uc.shape
    ccs, m_fwd, twc, tws, m2 = tables
    j = FFT_GROUP
    n2 = 128
    n1 = l // n2
    g = n2 // j
    gs = _group_step(g)
    tw = pl.BlockSpec((n1, gs, j, LANES), lambda gi, bb: (0, gi, 0, 0))
    a = pl.pallas_call(
        functools.partial(_fnet1_body, n1=n1, gs=gs),
        grid=(g // gs, b),
        in_specs=[pl.BlockSpec((1, n1, gs, j, c), lambda gi, bb: (bb, 0, gi, 0, 0)),
                  pl.BlockSpec(ccs.shape, lambda gi, bb: (0, 0)),
                  pl.BlockSpec(m_fwd.shape, lambda gi, bb: (0, 0)), tw, tw],
        out_specs=pl.BlockSpec((1, 2, n1, gs // 2, WORK_ROWS, c), lambda gi, bb: (bb, 0, 0, gi, 0, 0)),
        out_shape=jax.ShapeDtypeStruct((b, 2, n1, g // 2, WORK_ROWS, c), BF16),
        compiler_params=_params("arbitrary", "arbitrary"),
        name="fnet_stage1",
    )(uc.reshape(b, n1, g, j, c), ccs, m_fwd, twc, tws)
    out = pl.pallas_call(
        functools.partial(_fnet2_body, n2=n2, scale=1.0 / math.sqrt(l * c)),
        grid=(b, n1 // j),
        in_specs=[pl.BlockSpec((1, 2, j, n2, c), lambda bb, q: (bb, 0, q, 0, 0)),
                  pl.BlockSpec(m2.shape, lambda bb, q: (0, 0))],
        out_specs=pl.BlockSpec((1, n2, 1, j, c), lambda bb, q: (bb, 0, q, 0, 0)),
        out_shape=jax.ShapeDtypeStruct((b, n2, n1 // j, j, c), F32),
        compiler_params=_params("arbitrary", "arbitrary"),
        name="fnet_stage2",
    )(a.reshape(b, 2, n1, n2, c), m2)
    return out.reshape(b, l, c)


def fnet_mixer(uc, tables):
    b, l, c = uc.shape
    if l > FNET_DIRECT_MAX:
        return fnet_two_stage(uc, tables)
    cl, sl, cc, sc = tables
    tm = min(l, 512)
    row = pl.BlockSpec((tm, l), lambda i, bb: (i, 0))
    sq = pl.BlockSpec((c, c), lambda i, bb: (0, 0))
    return pl.pallas_call(
        functools.partial(_fnet_body, scale=1.0 / math.sqrt(l * c)),
        grid=(l // tm, b),
        in_specs=[row, row, pl.BlockSpec((1, l, c), lambda i, bb: (bb, 0, 0)), sq, sq],
        out_specs=pl.BlockSpec((1, tm, c), lambda i, bb: (bb, i, 0)),
        out_shape=jax.ShapeDtypeStruct((b, l, c), F32),
        compiler_params=_params("arbitrary", "arbitrary"),
        name="fnet",
    )(cl, sl, uc, cc, sc)


def _heads_rows(x, g):
    h0 = Q_PER_KV * g
    return jnp.concatenate([x[:, (h0 + r) * HEAD_DIM:(h0 + r + 1) * HEAD_DIM] for r in range(Q_PER_KV)], axis=0)


def _qk(q, k):
    return lax.dot_general(q.astype(BF16), k.astype(BF16), (((1,), (1,)), ((), ())),
                           preferred_element_type=F32)


def _sink_col(sink_ref, g, rows):
    ridx = lax.broadcasted_iota(jnp.int32, (Q_PER_KV * rows, 1), 0)
    col = jnp.full((Q_PER_KV * rows, 1), sink_ref[Q_PER_KV * g], F32)
    for r in range(1, Q_PER_KV):
        col = jnp.where(ridx >= r * rows, sink_ref[Q_PER_KV * g + r], col)
    return col * LOG2E


def _lat_attn_body(sink_ref, q_ref, qr_ref, kp_ref, kc_ref, kn_ref, ck_ref, cv_ref, o_ref, *, sub):
    i = pl.program_id(1)
    n_qblk = pl.num_programs(1) * sub
    blk = ATT_BLOCK
    span = blk + 2 * WINDOW
    kv = jnp.concatenate([kp_ref[0], kc_ref[0], kn_ref[0]], axis=0)
    ck = ck_ref[0, 0].astype(BF16)
    cv = cv_ref[0, 0].astype(BF16)
    kvw = N_KV_HEADS * HEAD_DIM
    r = lax.broadcasted_iota(jnp.int32, (Q_PER_KV * blk, span), 0) % blk
    j = lax.broadcasted_iota(jnp.int32, (Q_PER_KV * blk, span), 1)
    band = (j >= r) & (j <= r + 2 * WINDOW)
    for s in range(sub):
        qi = i * sub + s
        ok = band & ((qi > 0) | (j >= WINDOW)) & ((qi < n_qblk - 1) | (j < WINDOW + blk))
        q = q_ref[0, s * blk:(s + 1) * blk, :]
        qr = qr_ref[0, s * blk:(s + 1) * blk, :]
        outs = []
        for g in range(N_KV_HEADS):
            kl = kv[s * blk:s * blk + span, g * HEAD_DIM:(g + 1) * HEAD_DIM]
            vl = kv[s * blk:s * blk + span, kvw + g * HEAD_DIM:kvw + (g + 1) * HEAD_DIM]
            s_loc = jnp.where(ok, _qk(_heads_rows(qr, g), kl), NEG_INF)
            s_ctx = _qk(_heads_rows(q, g), ck[:, g * HEAD_DIM:(g + 1) * HEAD_DIM])
            sink = _sink_col(sink_ref, g, blk)
            m = jnp.maximum(jnp.maximum(jnp.max(s_loc, axis=-1, keepdims=True),
                                        jnp.max(s_ctx, axis=-1, keepdims=True)), sink)
            e_loc = jnp.exp2(s_loc - m)
            e_ctx = jnp.exp2(s_ctx - m)
            den = (jnp.sum(e_loc, axis=-1, keepdims=True) + jnp.sum(e_ctx, axis=-1, keepdims=True)
                   + jnp.exp2(sink - m))
            o = (jnp.dot(e_loc.astype(BF16), vl, preferred_element_type=F32)
                 + jnp.dot(e_ctx.astype(BF16), cv[:, g * HEAD_DIM:(g + 1) * HEAD_DIM],
                           preferred_element_type=F32)) * (1.0 / den)
            outs += [o[rr * blk:(rr + 1) * blk] for rr in range(Q_PER_KV)]
        o_ref[0, s * blk:(s + 1) * blk, :] = jnp.concatenate(outs, axis=1).astype(o_ref.dtype)


ATT_SUB = 8


def latent_attention(uq, uqr, ukv, cache_k, cache_v, layer, sink):
    b, l, c = uq.shape
    p = cache_k.shape[2]
    blk = ATT_BLOCK
    nblk = l // blk
    sub = math.gcd(ATT_SUB, nblk)
    rows = sub * blk
    qspec = pl.BlockSpec((1, rows, c), lambda bb, i: (bb, i, 0))
    cspec = pl.BlockSpec((1, 1, p, cache_k.shape[3]), lambda bb, i: (bb, layer, 0, 0))
    return pl.pallas_call(
        functools.partial(_lat_attn_body, sub=sub),
        grid=(b, nblk // sub),
        in_specs=[pl.BlockSpec(memory_space=pltpu.SMEM), qspec, qspec,
                  pl.BlockSpec((1, blk, c), lambda bb, i: (bb, jnp.maximum(i * sub - 1, 0), 0)),
                  qspec,
                  pl.BlockSpec((1, blk, c), lambda bb, i: (bb, jnp.minimum((i + 1) * sub, nblk - 1), 0)),
                  cspec, cspec],
        out_specs=qspec,
        out_shape=jax.ShapeDtypeStruct((b, l, c), BF16),
        compiler_params=_params("arbitrary", "arbitrary"),
        name="latent_attention",
    )(sink, uq, uqr, ukv, ukv, ukv, cache_k, cache_v)


def _ctx_attn_body(sink_ref, q_ref, kv_ref, o_ref, *, seq):
    q = q_ref[0]
    kv = kv_ref[0]
    kvw = N_KV_HEADS * HEAD_DIM
    outs = []
    for g in range(N_KV_HEADS):
        kl = kv[:, g * HEAD_DIM:(g + 1) * HEAD_DIM]
        vl = kv[:, kvw + g * HEAD_DIM:kvw + (g + 1) * HEAD_DIM]
        s = _qk(_heads_rows(q, g), kl)
        sink = _sink_col(sink_ref, g, seq)
        m = jnp.maximum(jnp.max(s, axis=-1, keepdims=True), sink)
        e = jnp.exp2(s - m)
        den = jnp.sum(e, axis=-1, keepdims=True) + jnp.exp2(sink - m)
        o = jnp.dot(e.astype(BF16), vl.astype(BF16), preferred_element_type=F32) * (1.0 / den)
        outs += [o[rr * seq:(rr + 1) * seq] for rr in range(Q_PER_KV)]
    o_ref[0] = jnp.concatenate(outs, axis=1)


def context_attention(uq, ukv, sink):
    b, s, c = uq.shape
    spec = pl.BlockSpec((1, s, c), lambda bb: (bb, 0, 0))
    return pl.pallas_call(
        functools.partial(_ctx_attn_body, seq=s),
        grid=(b,),
        in_specs=[pl.BlockSpec(memory_space=pltpu.SMEM), spec, spec],
        out_specs=spec,
        out_shape=jax.ShapeDtypeStruct((b, s, c), F32),
        compiler_params=_params("arbitrary"),
        name="context_attention",
    )(sink, uq, ukv)


def _pack_bf16_pairs(hi_rounded):
    k = hi_rounded.shape[1] // 2
    bits = lax.bitcast_convert_type(hi_rounded, jnp.uint32)
    return bits[:, :k] | (bits[:, k:] >> 16)


def _unpack_bf16_pairs(packed, dtype=BF16):
    a = lax.bitcast_convert_type(packed & jnp.uint32(0xFFFF0000), F32)
    b = lax.bitcast_convert_type(packed << 16, F32)
    return jnp.concatenate([a, b], axis=1).astype(dtype)


def _out_body(ya_ref, yb_ref, yc_ref, yd_ref, x_ref, mod_ref, g_ref, w_ref, rw_ref, rb_ref,
              x1_ref, h_ref, route_ref, cnt_ref):
    c = MIX_CH
    y = jnp.dot(ya_ref[0].astype(BF16), w_ref[0:c, :], preferred_element_type=F32)
    for j, ref in enumerate((yb_ref, yc_ref, yd_ref), start=1):
        y = y + jnp.dot(ref[0].astype(BF16), w_ref[j * c:(j + 1) * c, :], preferred_element_type=F32)
    x1 = x_ref[0] + mod_ref[0, 2:3, :] * y
    x1_ref[0] = x1
    h = _rmsnorm_mod(x1, g_ref[...], mod_ref[0, 4:5, :], mod_ref[0, 3:4, :])
    h_hi = h.astype(BF16)
    h_hi32 = h_hi.astype(F32)
    h_ref[0] = _pack_bf16_pairs(h_hi32)
    h_lo = (h - h_hi32).astype(BF16)
    tm = h.shape[0]
    prod = jnp.dot(jnp.concatenate([h_hi, h_lo], axis=0), rw_ref[...], preferred_element_type=F32)
    logits = (prod[:tm, :ROUTE_LANES] + prod[:tm, ROUTE_LANES:]
              + prod[tm:, :ROUTE_LANES] + prod[tm:, ROUTE_LANES:]) + rb_ref[...]
    lane = lax.broadcasted_iota(jnp.int32, logits.shape, 1)
    is_c = lane < N_GROUPS
    lc = jnp.where(is_c, logits, NEG_INF)
    mc = jnp.max(lc, axis=-1, keepdims=True)
    grp = jnp.min(jnp.where(lc == mc, lane, ROUTE_LANES), axis=-1, keepdims=True)
    pg = 1.0 / jnp.sum(jnp.where(is_c, jnp.exp(lc - mc), 0.0), axis=-1, keepdims=True)
    lo = N_GROUPS + grp * EXPERTS_PER_GROUP
    in_g = (lane >= lo) & (lane < lo + EXPERTS_PER_GROUP)
    lf = jnp.where(in_g, logits, NEG_INF)
    t1 = jnp.max(lf, axis=-1, keepdims=True)
    i1 = jnp.min(jnp.where(lf == t1, lane, ROUTE_LANES), axis=-1, keepdims=True)
    lf2 = jnp.where(lane == i1, NEG_INF, lf)
    t2 = jnp.max(lf2, axis=-1, keepdims=True)
    i2 = jnp.min(jnp.where(lf2 == t2, lane, ROUTE_LANES), axis=-1, keepdims=True)
    e2 = jnp.exp(t2 - t1)
    w1 = pg / (1.0 + e2)
    w2 = pg * e2 / (1.0 + e2)
    rec = jnp.where(lane == 0, (i1 - N_GROUPS).astype(F32),
                    jnp.where(lane == 1, (i2 - N_GROUPS).astype(F32),
                              jnp.where(lane == 2, w1, jnp.where(lane == 3, w2, 0.0))))
    route_ref[0] = rec

    @pl.when((pl.program_id(0) == 0) & (pl.program_id(1) == 0))
    def _():
        cnt_ref[...] = jnp.zeros(cnt_ref.shape, F32)

    e0, e1 = _choice_onehots(rec)
    cnt_ref[...] = cnt_ref[...] + jnp.sum(e0 + e1, axis=0, keepdims=True)


def out_projection(ys, x, mods, mod_row0, norm_g, w_out_bf, rw, rb, *, tm):
    b, l, d = x.shape
    c = MIX_CH
    row = (lambda bb: 0) if mod_row0 is None else (lambda bb: mod_row0 + bb)
    yspec = pl.BlockSpec((1, tm, c), lambda bb, i: (bb, i, 0))
    xspec = pl.BlockSpec((1, tm, d), lambda bb, i: (bb, i, 0))
    return pl.pallas_call(
        _out_body,
        grid=(b, l // tm),
        in_specs=[yspec] * 4 + [xspec,
                                pl.BlockSpec((1, N_MOD, d), lambda bb, i: (row(bb), 0, 0)),
                                pl.BlockSpec((1, d), lambda bb, i: (0, 0)),
                                pl.BlockSpec(w_out_bf.shape, lambda bb, i: (0, 0)),
                                pl.BlockSpec(rw.shape, lambda bb, i: (0, 0)),
                                pl.BlockSpec(rb.shape, lambda bb, i: (0, 0))],
        out_specs=[xspec, pl.BlockSpec((1, tm, d // 2), lambda bb, i: (bb, i, 0)),
                   pl.BlockSpec((1, tm, ROUTE_LANES), lambda bb, i: (bb, i, 0)),
                   pl.BlockSpec((SUBLANES, ROUTE_LANES), lambda bb, i: (0, 0))],
        out_shape=[jax.ShapeDtypeStruct((b, l, d), F32), jax.ShapeDtypeStruct((b, l, d // 2), jnp.uint32),
                   jax.ShapeDtypeStruct((b, l, ROUTE_LANES), F32),
                   jax.ShapeDtypeStruct((SUBLANES, ROUTE_LANES), F32)],
        compiler_params=_params("arbitrary", "arbitrary"),
        name="out_proj",
    )(*ys, x, mods, norm_g.reshape(1, d), w_out_bf, rw, rb)


def _expert_body(be_ref, nv_ref, xs_ref, wg_ref, wu_ref, wd_ref, o_ref, wg_s, wu_s, wd_s):
    i = pl.program_id(0)
    prev = be_ref[jnp.maximum(i - 1, 0)]

    @pl.when((i == 0) | (be_ref[i] != prev))
    def _():
        wg_s[...] = wg_ref[0, 0].astype(BF16)
        wu_s[...] = wu_ref[0, 0].astype(BF16)
        wd_s[...] = wd_ref[0, 0].astype(BF16)

    @pl.when(nv_ref[i] > 0)
    def _():
        row = lax.broadcasted_iota(jnp.int32, xs_ref.shape, 0)
        x = _unpack_bf16_pairs(jnp.where(row < nv_ref[i], xs_ref[...], jnp.uint32(0)))
        g = jnp.dot(x, wg_s[...], preferred_element_type=F32)
        u = jnp.dot(x, wu_s[...], preferred_element_type=F32)
        a = (_silu(g) * u).astype(BF16)
        y = jnp.dot(a, wd_s[...], preferred_element_type=F32)
        o_ref[...] = _pack_bf16_pairs(y.astype(BF16).astype(F32))

    @pl.when(nv_ref[i] <= 0)
    def _():
        o_ref[...] = jnp.zeros(o_ref.shape, jnp.uint32)


def expert_ffn(xs, blk_e, n_valid, layer, e_gate, e_up, e_down):
    rows, dh = xs.shape
    d = 2 * dh
    nb = rows // MOE_BLOCK
    de = e_gate.shape[-1]
    grid_spec = pltpu.PrefetchScalarGridSpec(
        num_scalar_prefetch=2,
        grid=(nb,),
        in_specs=[pl.BlockSpec((MOE_BLOCK, dh), lambda i, be, nv: (i, 0)),
                  pl.BlockSpec((1, 1, d, de), lambda i, be, nv: (layer, be[i], 0, 0)),
                  pl.BlockSpec((1, 1, d, de), lambda i, be, nv: (layer, be[i], 0, 0)),
                  pl.BlockSpec((1, 1, de, d), lambda i, be, nv: (layer, be[i], 0, 0))],
        out_specs=pl.BlockSpec((MOE_BLOCK, dh), lambda i, be, nv: (i, 0)),
        scratch_shapes=[pltpu.VMEM((d, de), BF16), pltpu.VMEM((d, de), BF16), pltpu.VMEM((de, d), BF16)],
    )
    return pl.pallas_call(
        _expert_body, grid_spec=grid_spec,
        out_shape=jax.ShapeDtypeStruct((rows, dh), jnp.uint32),
        compiler_params=_params("arbitrary"),
        name="expert_ffn",
    )(blk_e, n_valid, xs, e_gate, e_up, e_down)


RANK_TILE = 1024


def _choice_onehots(rec):
    lanef = lax.broadcasted_iota(jnp.int32, rec.shape, 1).astype(F32)
    return (lanef == rec[:, 0:1]).astype(F32), (lanef == rec[:, 1:2]).astype(F32)


def _slot_body(route_ref, cnt_ref, tri_ref, slot_ref, carry_ref):
    rec = route_ref[...]
    lane = lax.broadcasted_iota(jnp.int32, rec.shape, 1)
    e0, e1 = _choice_onehots(rec)
    both = e0 + e1

    @pl.when(pl.program_id(0) == 0)
    def _():
        cnt = cnt_ref[...]
        padded = jnp.floor((cnt + (MOE_BLOCK - 1)) * (1.0 / MOE_BLOCK)) * MOE_BLOCK
        ln = lax.broadcasted_iota(jnp.int32, cnt.shape, 1)
        incl = padded
        sh = 1
        while sh < ROUTE_LANES:
            incl = incl + jnp.where(ln >= sh, pltpu.roll(incl, sh, axis=1), 0.0)
            sh *= 2
        carry_ref[...] = incl - padded

    before = jnp.dot(tri_ref[...], both.astype(BF16), preferred_element_type=F32) + carry_ref[0:1, :]
    s0 = jnp.sum(e0 * before, axis=-1, keepdims=True)
    s1 = jnp.sum(e1 * before, axis=-1, keepdims=True)
    slot_ref[...] = jnp.where(lane == 0, s0, jnp.where(lane == 1, s1, 0.0))
    carry_ref[...] = carry_ref[...] + jnp.sum(both, axis=0, keepdims=True)


def moe_slots(route, counts):
    n = route.shape[0]
    t = RANK_TILE
    tri = jnp.asarray(np.tril(np.ones((t, t), np.float32), -1)).astype(BF16)
    return pl.pallas_call(
        _slot_body,
        grid=(n // t,),
        in_specs=[pl.BlockSpec((t, ROUTE_LANES), lambda i: (i, 0)),
                  pl.BlockSpec((SUBLANES, ROUTE_LANES), lambda i: (0, 0)),
                  pl.BlockSpec((t, t), lambda i: (0, 0))],
        out_specs=pl.BlockSpec((t, ROUTE_LANES), lambda i: (i, 0)),
        out_shape=jax.ShapeDtypeStruct((n, ROUTE_LANES), F32),
        scratch_shapes=[pltpu.VMEM((SUBLANES, ROUTE_LANES), F32)],
        compiler_params=_params("arbitrary"),
        name="moe_slots",
    )(route, counts, tri)


def _sc_mesh():
    return plsc.VectorSubcoreMesh(core_axis_name="c", subcore_axis_name="s")


def _sc_worker():
    return lax.axis_index("s") * SC_CORES + lax.axis_index("c")


DISPATCH_ROWS = 64
COMBINE_ROWS = 64


def sc_dispatch(rows, dest, n_slots):
    n, w = rows.shape
    ch = DISPATCH_ROWS
    per_w = n // SC_WORKERS
    n_ch = per_w // ch

    @functools.partial(
        pl.kernel, mesh=_sc_mesh(),
        out_type=jax.ShapeDtypeStruct((n_slots, w), rows.dtype),
        scratch_types=[pltpu.VMEM((ch,), jnp.int32), pltpu.VMEM((ch, w), rows.dtype)],
    )
    def scatter_kernel(rows_hbm, dest_hbm, out_hbm, idx_v, rows_v):
        wid = _sc_worker()

        @pl.loop(0, n_ch)
        def _(j):
            chunk = wid * n_ch + j
            pltpu.sync_copy(rows_hbm.at[pl.ds(pl.multiple_of(chunk * ch, ch), ch)], rows_v)
            for k in range(2):
                pltpu.sync_copy(dest_hbm.at[k, chunk], idx_v)
                pltpu.sync_copy(rows_v, out_hbm.at[idx_v])

    return scatter_kernel(rows, dest)


def sc_gather_rows(table, idx):
    s, w = table.shape
    m = idx.shape[0]
    ch = COMBINE_ROWS
    per_w = m // SC_WORKERS
    n_ch = per_w // ch

    @functools.partial(
        pl.kernel, mesh=_sc_mesh(),
        out_type=jax.ShapeDtypeStruct((m, w), table.dtype),
        scratch_types=[pltpu.VMEM((ch,), jnp.int32), pltpu.VMEM((ch, w), table.dtype), pltpu.SemaphoreType.DMA],
    )
    def gather_kernel(table_hbm, idx_hbm, out_hbm, idx_v, rows_v, sem):
        wid = _sc_worker()

        @pl.loop(0, n_ch)
        def _(j):
            off = pl.multiple_of((wid * n_ch + j) * ch, ch)
            pltpu.sync_copy(idx_hbm.at[pl.ds(off, ch)], idx_v)
            pltpu.async_copy(table_hbm.at[idx_v], rows_v, sem).wait()
            pltpu.sync_copy(rows_v, out_hbm.at[pl.ds(off, ch)])

    return gather_kernel(table, idx)


def hier_moe(h_packed, route, cnt, layer, e_gate, e_up, e_down):
    b, l, dh = h_packed.shape
    n = b * l
    assert n % (SC_WORKERS * DISPATCH_ROWS) == 0 and (2 * n) % (SC_WORKERS * COMBINE_ROWS) == 0
    slots = moe_slots(route.reshape(n, ROUTE_LANES), cnt)
    counts = cnt[0, :N_EXPERTS].astype(jnp.int32)
    padded = (counts + MOE_BLOCK - 1) // MOE_BLOCK * MOE_BLOCK
    pend = jnp.cumsum(padded)
    nb = -(-2 * n // MOE_BLOCK) + N_EXPERTS
    blk0 = jnp.arange(nb, dtype=jnp.int32) * MOE_BLOCK
    owner = pend[None, :] <= blk0[:, None]
    blk_e = jnp.minimum(jnp.sum(owner, axis=1), N_EXPERTS - 1).astype(jnp.int32)
    run_end = jnp.sum(jnp.where(jnp.arange(N_EXPERTS)[None, :] == blk_e[:, None],
                                (pend - padded + counts)[None, :], 0), axis=1)
    n_valid = jnp.clip(run_end - blk0, 0, MOE_BLOCK).astype(jnp.int32)
    dest = slots[:, 0:2].astype(jnp.int32).T
    xs = sc_dispatch(h_packed.reshape(n, dh), dest.reshape(2, n // DISPATCH_ROWS, DISPATCH_ROWS), nb * MOE_BLOCK)
    y = expert_ffn(xs, blk_e, n_valid, layer, e_gate, e_up, e_down)
    return sc_gather_rows(y, dest.reshape(2 * n)).reshape(2, b, l, dh)


def _final_body(x_ref, y0_ref, y1_ref, route_ref, pmod_ref, g_ref, o_ref):
    x = _moe_residual(x_ref[0], y0_ref, y1_ref, route_ref, pmod_ref)
    ms = jnp.mean(x * x, axis=-1, keepdims=True)
    o_ref[0] = x * lax.rsqrt(ms + EPS) * g_ref[...]


def final_norm(x1, res, mod_row0, norm_g, *, tm):
    b, l, d = x1.shape
    row = (lambda bb: 0) if mod_row0 is None else (lambda bb: mod_row0 + bb)
    xspec = pl.BlockSpec((1, tm, d), lambda bb, i: (bb, i, 0))
    rargs, rspecs = _residual_specs(res, tm, d, row)
    return pl.pallas_call(
        _final_body,
        grid=(b, l // tm),
        in_specs=[xspec] + rspecs + [pl.BlockSpec((1, d), lambda bb, i: (0, 0))],
        out_specs=xspec,
        out_shape=jax.ShapeDtypeStruct((b, l, d), F32),
        compiler_params=_params("arbitrary", "arbitrary"),
        name="final_norm",
    )(x1, *rargs, norm_g.reshape(1, d))


def _rope_tables(seq):
    rows = seq // GRID_W
    row_pos = jnp.repeat(jnp.arange(rows, dtype=F32), GRID_W)
    col_pos = jnp.tile(jnp.arange(GRID_W, dtype=F32), rows)
    n_freq = HEAD_DIM // 4
    inv = ROPE_BASE ** (-jnp.arange(n_freq, dtype=F32) / n_freq)
    ang = jnp.concatenate([row_pos[:, None] * inv, col_pos[:, None] * inv], axis=-1)
    cs, sn = jnp.cos(ang), jnp.sin(ang)
    cos_f = jnp.tile(jnp.concatenate([cs, cs], axis=-1), (1, N_Q_HEADS))
    sin_s = jnp.tile(jnp.concatenate([-sn, sn], axis=-1), (1, N_Q_HEADS))
    return cos_f, sin_s


def kernel(x_prompt, x_sample, cache_k, cache_v, c, c_ctx, ada_w, ada_b, norm1_g, norm2_g, w_in, conv_dw_w, conv_dw_b, conv_ln_g, conv_ln_b, hy_short_w, hy_short_b, hy_fw1, hy_fb1, hy_freq1, hy_fw2, hy_fb2, hy_freq2, hy_fw3, hy_fb3, hy_d, attn_sink, w_out, router_coarse_w, router_coarse_b, router_fine_w, router_fine_b, exp_gate, exp_up, exp_down, norm_f_g):
    depth = ada_w.shape[0]
    bp, lp, d = x_prompt.shape
    bs, ls, _ = x_sample.shape
    assert bp % 2 == 0 and bs % 2 == 0 and ls % ATT_BLOCK == 0 and ls % GRID_W == 0

    n_rows = -(-(1 + bs) // SUBLANES) * SUBLANES
    cvec = jnp.concatenate([c_ctx[None, :], c, jnp.zeros((n_rows - 1 - bs, d), F32)], axis=0)
    mods = adaln_all(cvec, ada_w, ada_b)

    rope = _rope_tables(ls)
    fnet_tabs, hy_tabs = {}, {}
    for seq in {lp, ls}:
        fnet_tabs[seq] = fnet_tables(seq)
        hy_tabs[seq] = hyena_tables(seq)
    ck = cache_k.reshape(cache_k.shape[0], depth, cache_k.shape[2], -1)
    cv = cache_v.reshape(cache_v.shape[0], depth, cache_v.shape[2], -1)
    pad = ROUTE_LANES - N_GROUPS - N_EXPERTS

    tm_p = min(lp, 512)
    tm_s = min(ls, 1024)
    xp, xs = x_prompt, x_sample
    res_p = res_s = None
    ks_out, vs_out = [], []
    for l in range(depth):
        w_in_bf = w_in[l].astype(BF16)
        w_out_bf = w_out[l].astype(BF16)
        rw = jnp.concatenate([router_coarse_w[l], router_fine_w[l], jnp.zeros((d, pad), F32)], axis=1)
        rw_hi = rw.astype(BF16)
        rw = jnp.concatenate([rw_hi, (rw - rw_hi.astype(F32)).astype(BF16)], axis=1)
        rb = jnp.concatenate([router_coarse_b[l], router_fine_b[l], jnp.zeros((pad,), F32)])[None, :]
        filt = (hy_fw1[l], hy_fb1[l], hy_freq1[l], hy_fw2[l], hy_fb2[l], hy_freq2[l], hy_fw3[l], hy_fb3[l])
        sink = attn_sink[l]

        def mixers(ua, ub, uc, yd, seq):
            ya = conformer_conv(ua, conv_dw_w[l], conv_dw_b[l], conv_ln_g[l], conv_ln_b[l])
            yb = hyena_mixer(ub, hy_short_w[l], hy_short_b[l], hy_d[l], hyena_filter_spectrum(seq, filt, hy_tabs[seq]),
                             hy_tabs[seq])
            yc = fnet_mixer(uc, fnet_tabs[seq])
            return (ya, yb, yc, yd)

        outs = in_projection(xp, mods[l], None, norm1_g[l], w_in_bf, res=res_p, tm=tm_p)
        if res_p is not None:
            xp, outs = outs[0], outs[1:]
        ua, ub, uc, uq, ukv = outs
        kvw = N_KV_HEADS * HEAD_DIM
        ks_out.append(ukv[..., :kvw].reshape(bp, lp, N_KV_HEADS, HEAD_DIM))
        vs_out.append(ukv[..., kvw:].reshape(bp, lp, N_KV_HEADS, HEAD_DIM))
        ys = mixers(ua, ub, uc, context_attention(uq, ukv, sink), lp)
        x1p, hp, route, cnt = out_projection(ys, xp, mods[l], None, norm2_g[l], w_out_bf, rw, rb, tm=tm_p)
        res_p = (hier_moe(hp, route, cnt, l, exp_gate, exp_up, exp_down), route, mods[l])
        xp = x1p

        outs = in_projection(xs, mods[l], 1, norm1_g[l], w_in_bf, res=res_s, rope=rope, tm=tm_s)
        if res_s is not None:
            xs, outs = outs[0], outs[1:]
        ua, ub, uc, uq, uqr, ukv = outs
        ys = mixers(ua, ub, uc, latent_attention(uq, uqr, ukv, ck, cv, l, sink), ls)
        x1s, hs, route, cnt = out_projection(ys, xs, mods[l], 1, norm2_g[l], w_out_bf, rw, rb, tm=tm_s)
        res_s = (hier_moe(hs, route, cnt, l, exp_gate, exp_up, exp_down), route, mods[l])
        xs = x1s

    y_prompt = final_norm(xp, res_p, None, norm_f_g, tm=tm_p)
    y_sample = final_norm(xs, res_s, 1, norm_f_g, tm=tm_s)
    return (y_prompt, y_sample, jnp.stack(ks_out, axis=1), jnp.stack(vs_out, axis=1))
```

```python
import functools
import math

import numpy as np
import jax
import jax.numpy as jnp
from jax import lax
from jax.experimental import pallas as pl
from jax.experimental.pallas import tpu as pltpu
from jax.experimental.pallas import tpu_sc as plsc

F32 = jnp.float32
BF16 = jnp.bfloat16

HEAD_DIM = 64
LOG2E = math.log2(math.e)
ATT_SCALE = HEAD_DIM ** -0.5 * LOG2E
N_Q_HEADS = 4
N_KV_HEADS = 2
Q_PER_KV = N_Q_HEADS // N_KV_HEADS
WINDOW = 128
ATT_BLOCK = 128
GRID_W = 64
ROPE_BASE = 10000.0
N_GROUPS = 4
EXPERTS_PER_GROUP = 8
N_EXPERTS = N_GROUPS * EXPERTS_PER_GROUP
MOE_BLOCK = 512
N_MOD = 6
EPS = 1e-6
NEG_INF = -1e30
HYENA_DECAY_TARGET = 1e-2
HYENA_FAST_PCT = 0.3
HYENA_SLOW_PCT = 1.5
HYENA_SHIFT = 0.05

LANES = 128
SUBLANES = 8
VMEM_LIMIT = 56 * 1024 * 1024

MIX_CH = 256
ROUTE_LANES = 128
SC_CORES = 2
SC_WORKERS = SC_CORES * 16


def _params(*sem):
    return pltpu.CompilerParams(dimension_semantics=sem, vmem_limit_bytes=VMEM_LIMIT)


def _silu(x):
    return x * jax.nn.sigmoid(x)


def _ada_body(c_ref, w_ref, b_ref, o_ref):
    s = _silu(c_ref[...]).astype(BF16)
    o_ref[0] = jnp.dot(s, w_ref[0].astype(BF16), preferred_element_type=F32) + b_ref[0]


def adaln_all(cvec, ada_w, ada_b):
    depth, d, n6 = ada_w.shape
    r = cvec.shape[0]
    tn = n6 // 4
    out = pl.pallas_call(
        _ada_body,
        grid=(depth, n6 // tn),
        in_specs=[
            pl.BlockSpec((r, d), lambda l, j: (0, 0)),
            pl.BlockSpec((1, d, tn), lambda l, j: (l, 0, j)),
            pl.BlockSpec((1, 1, tn), lambda l, j: (l, 0, j)),
        ],
        out_specs=pl.BlockSpec((1, r, tn), lambda l, j: (l, 0, j)),
        out_shape=jax.ShapeDtypeStruct((depth, r, n6), F32),
        compiler_params=_params("arbitrary", "arbitrary"),
        name="adaln",
    )(cvec, ada_w, ada_b.reshape(depth, 1, n6))
    return out.reshape(depth, r, N_MOD, d)


def _swap_halves(x):
    pieces = []
    for j in range(x.shape[1] // LANES):
        xj = x[:, j * LANES:(j + 1) * LANES]
        fwd = pltpu.roll(xj, LANES - HEAD_DIM // 2, axis=1)
        bwd = pltpu.roll(xj, HEAD_DIM // 2, axis=1)
        lane = lax.broadcasted_iota(jnp.int32, xj.shape, 1)
        pieces.append(jnp.where((lane % HEAD_DIM) < HEAD_DIM // 2, fwd, bwd))
    return pieces[0] if len(pieces) == 1 else jnp.concatenate(pieces, axis=1)


def _rmsnorm_mod(x, g, scale, shift):
    ms = jnp.mean(x * x, axis=-1, keepdims=True)
    return (x * lax.rsqrt(ms + EPS)) * (g * (1.0 + scale)) + shift


def _moe_residual(x1, y0_ref, y1_ref, route_ref, pmod_ref, rows=slice(None)):
    y0 = _unpack_bf16_pairs(y0_ref[0, 0, rows, :], F32)
    y1 = _unpack_bf16_pairs(y1_ref[0, 0, rows, :], F32)
    moe = route_ref[0, rows, 2:3] * y0 + route_ref[0, rows, 3:4] * y1
    return x1 + pmod_ref[0, 5:6, :] * moe


def _residual_specs(res, tm, d, row):
    pair, route, pmods = res
    args = [pair, pair, route, pmods]
    specs = [pl.BlockSpec((1, 1, tm, d // 2), lambda bb, i: (0, bb, i, 0)),
             pl.BlockSpec((1, 1, tm, d // 2), lambda bb, i: (1, bb, i, 0)),
             pl.BlockSpec((1, tm, ROUTE_LANES), lambda bb, i: (bb, i, 0)),
             pl.BlockSpec((1, N_MOD, d), lambda bb, i: (row(bb), 0, 0))]
    return args, specs


IN_SPLIT = 2


def _in_body(*refs, fuse_res, rope):
    it = iter(refs)
    x_ref = next(it)
    if fuse_res:
        res_refs = [next(it) for _ in range(4)]
    mod_ref = next(it)
    g_ref = next(it)
    w_ref = next(it)
    if rope:
        cos_ref = next(it)
        sin_ref = next(it)
    outs = list(it)
    xo_ref = outs.pop(0) if fuse_res else None
    c = MIX_CH
    part = x_ref.shape[1] // IN_SPLIT
    hs = []
    for p in range(IN_SPLIT):
        rows = slice(p * part, (p + 1) * part)
        x = x_ref[0, rows, :]
        if fuse_res:
            x = _moe_residual(x, *res_refs, rows)
            xo_ref[0, rows, :] = x
        hs.append(_rmsnorm_mod(x, g_ref[...], mod_ref[0, 1:2, :], mod_ref[0, 0:1, :]).astype(BF16))
    ua_ref, ub_ref, uc_ref, uq_ref = outs[:4]
    for p in range(IN_SPLIT):
        rows = slice(p * part, (p + 1) * part)
        u = jnp.dot(hs[p], w_ref[...], preferred_element_type=F32)
        ua_ref[0, rows, :] = u[:, 0:2 * c]
        ub_ref[0, rows, :] = u[:, 2 * c:5 * c]
        uc_ref[0, rows, :] = u[:, 5 * c:6 * c]
        q = u[:, 6 * c:7 * c] * ATT_SCALE
        k = u[:, 7 * c:7 * c + c // 2]
        v = u[:, 7 * c + c // 2:8 * c]
        uq_ref[0, rows, :] = q.astype(uq_ref.dtype)
        if rope:
            uqr_ref, ukv_ref = outs[4:]
            cs = cos_ref[rows, :]
            sn = sin_ref[rows, :]
            uqr_ref[0, rows, :] = (q * cs + _swap_halves(q) * sn).astype(uqr_ref.dtype)
            kr = k * cs[:, :c // 2] + _swap_halves(k) * sn[:, :c // 2]
            ukv_ref[0, rows, :] = jnp.concatenate([kr, v], axis=1).astype(ukv_ref.dtype)
        else:
            outs[4][0, rows, :] = u[:, 7 * c:8 * c]


def in_projection(x, mods, mod_row0, norm_g, w_in_bf, *, res=None, rope=None, tm):
    b, l, d = x.shape
    c = MIX_CH
    grid = (b, l // tm)
    row = (lambda bb: 0) if mod_row0 is None else (lambda bb: mod_row0 + bb)
    xspec = pl.BlockSpec((1, tm, d), lambda bb, i: (bb, i, 0))
    mspec = pl.BlockSpec((1, N_MOD, d), lambda bb, i: (row(bb), 0, 0))
    args, specs = [x], [xspec]
    if res is not None:
        rargs, rspecs = _residual_specs(res, tm, d, row)
        args += rargs
        specs += rspecs
    args += [mods, norm_g.reshape(1, d), w_in_bf]
    specs += [mspec, pl.BlockSpec((1, d), lambda bb, i: (0, 0)),
              pl.BlockSpec(w_in_bf.shape, lambda bb, i: (0, 0))]
    if rope is not None:
        args += [rope[0], rope[1]]
        specs += [pl.BlockSpec((tm, c), lambda bb, i: (i, 0))] * 2

    def ospec(w):
        return pl.BlockSpec((1, tm, w), lambda bb, i: (bb, i, 0))

    out_shape, out_specs = [], []
    if res is not None:
        out_shape.append(jax.ShapeDtypeStruct((b, l, d), F32))
        out_specs.append(xspec)
    widths = [(2 * c, F32), (3 * c, F32), (c, F32), (c, BF16)] + ([(c, BF16), (c, BF16)] if rope is not None else [(c, F32)])
    for w, dt in widths:
        out_shape.append(jax.ShapeDtypeStruct((b, l, w), dt))
        out_specs.append(ospec(w))
    return pl.pallas_call(
        functools.partial(_in_body, fuse_res=res is not None, rope=rope is not None),
        grid=grid, in_specs=specs, out_specs=out_specs, out_shape=out_shape,
        compiler_params=_params("arbitrary", "arbitrary"),
        name="in_proj",
    )(*args)


def _dw_tile(win, w_ref, n_taps, first, rows):
    acc = w_ref[0:1, :] * win[first:first + rows]
    for k in range(1, n_taps):
        acc = acc + w_ref[k:k + 1, :] * win[first + k:first + k + rows]
    return acc


CONV_PAD = 16
CONV_ROWS = 256
CONV_CHUNK = 1024
CONV_TAIL = CONV_PAD + CONV_ROWS + SUBLANES + SUBLANES


def _conf_body(u_ref, w_ref, b_ref, g_ref, beta_ref, o_ref, gp_ref, sh_ref, *, seq, n_taps, chunk):
    c = MIX_CH
    r = CONV_ROWS
    first = CONV_PAD - (n_taps - 1) // 2
    gp_ref[0:CONV_PAD, :] = jnp.zeros((CONV_PAD, c), F32)
    gp_ref[CONV_PAD + seq:CONV_PAD + seq + CONV_TAIL, :] = jnp.zeros((CONV_TAIL, c), F32)

    def fill(i, carry):
        r0 = pl.multiple_of(i * r, r)
        a = u_ref[0, pl.ds(r0, r), 0:c]
        g = u_ref[0, pl.ds(r0, r), c:2 * c]
        gp_ref[pl.ds(CONV_PAD + r0, r), :] = a * jax.nn.sigmoid(g)
        return carry

    lax.fori_loop(0, seq // r, fill, 0)
    n_copy_tiles = sh_ref.shape[1] // r

    def do_chunk(ci, carry):
        c0 = pl.multiple_of(ci * chunk, chunk)

        def shift_tile(ti, carry2):
            t0 = pl.multiple_of(ti * r, r)
            win = gp_ref[pl.ds(c0 + t0, r + SUBLANES), :]
            for m in range(1, SUBLANES):
                sh_ref[m - 1, pl.ds(t0, r), :] = win[m:m + r]
            return carry2

        lax.fori_loop(0, n_copy_tiles, shift_tile, 0)

        def tile(ti, carry2):
            t0 = pl.multiple_of(ti * r, r)
            acc = None
            for k in range(n_taps):
                a8, m = (first + k) // SUBLANES * SUBLANES, (first + k) % SUBLANES
                src = gp_ref[pl.ds(c0 + t0 + a8, r), :] if m == 0 else sh_ref[m - 1, pl.ds(t0 + a8, r), :]
                term = w_ref[k:k + 1, :] * src
                acc = term if acc is None else acc + term
            z = acc + b_ref[...]
            mu = jnp.mean(z, axis=-1, keepdims=True)
            zc = z - mu
            var = jnp.mean(zc * zc, axis=-1, keepdims=True)
            zn = zc * lax.rsqrt(var + EPS) * g_ref[...] + beta_ref[...]
            o_ref[0, pl.ds(c0 + t0, r), :] = _silu(zn).astype(o_ref.dtype)
            return carry2

        lax.fori_loop(0, chunk // r, tile, 0)
        return carry

    lax.fori_loop(0, seq // chunk, do_chunk, 0)


def conformer_conv(ua, dw_w, dw_b, ln_g, ln_b):
    b, l, c2 = ua.shape
    c = MIX_CH
    k = dw_w.shape[0]
    chunk = min(l, CONV_CHUNK)
    copy_rows = -(-(chunk + CONV_PAD + k) // CONV_ROWS) * CONV_ROWS
    assert l % chunk == 0 and k - 1 <= 2 * CONV_PAD and copy_rows - chunk + SUBLANES <= CONV_PAD + CONV_TAIL
    vec = pl.BlockSpec((1, c), lambda bb: (0, 0))
    return pl.pallas_call(
        functools.partial(_conf_body, seq=l, n_taps=k, chunk=chunk),
        grid=(b,),
        in_specs=[pl.BlockSpec((1, l, c2), lambda bb: (bb, 0, 0)),
                  pl.BlockSpec((k, c), lambda bb: (0, 0)), vec, vec, vec],
        out_specs=pl.BlockSpec((1, l, c), lambda bb: (bb, 0, 0)),
        out_shape=jax.ShapeDtypeStruct((b, l, c), BF16),
        scratch_shapes=[pltpu.VMEM((CONV_PAD + l + CONV_TAIL, c), F32),
                        pltpu.VMEM((SUBLANES - 1, copy_rows, c), F32)],
        compiler_params=_params("arbitrary"),
        name="conformer",
    )(ua, dw_w, dw_b.reshape(1, c), ln_g.reshape(1, c), ln_b.reshape(1, c))


SHORT_PAD = 8
SHORT_ROWS = 64


def _short_body(u_ref, w_ref, b_ref, o_ref, xp_ref, *, seq, n_taps):
    c = MIX_CH
    r = SHORT_ROWS
    half = (n_taps - 1) // 2
    zero = jnp.zeros((SHORT_PAD, c), F32)
    xp_ref[0:SHORT_PAD, :] = zero
    xp_ref[SHORT_PAD + seq:SHORT_PAD + seq + SHORT_PAD, :] = zero
    for q in range(u_ref.shape[0]):
        def fill(i, carry):
            r0 = pl.multiple_of(i * r, r)
            xp_ref[pl.ds(SHORT_PAD + r0, r), :] = u_ref[q, pl.ds(r0, r), :]
            return carry

        lax.fori_loop(0, seq // r, fill, 0)

        def tile(i, carry):
            r0 = pl.multiple_of(i * r, r)
            win = xp_ref[pl.ds(r0, r + 2 * SHORT_PAD), :]
            o_ref[q, pl.ds(r0, r), :] = _dw_tile(win, w_ref, n_taps, SHORT_PAD - half, r) + b_ref[...]
            return carry

        lax.fori_loop(0, seq // r, tile, 0)


SHORT_BLOCK_ROWS = 1024


def hyena_short_conv(ub, short_w, short_b):
    b, l, c3 = ub.shape
    c = MIX_CH
    k = short_w.shape[0]
    nbat = math.gcd(b, max(1, SHORT_BLOCK_ROWS // l))
    return pl.pallas_call(
        functools.partial(_short_body, seq=l, n_taps=k),
        grid=(b // nbat, c3 // c),
        in_specs=[pl.BlockSpec((nbat, l, c), lambda bb, j: (bb, 0, j)),
                  pl.BlockSpec((k, c), lambda bb, j: (0, j)),
                  pl.BlockSpec((1, c), lambda bb, j: (0, j))],
        out_specs=pl.BlockSpec((nbat, l, c), lambda bb, j: (bb, 0, j)),
        out_shape=jax.ShapeDtypeStruct((b, l, c3), F32),
        scratch_shapes=[pltpu.VMEM((l + 2 * SHORT_PAD, c), F32)],
        compiler_params=_params("arbitrary", "arbitrary"),
        name="hyena_short",
    )(ub, short_w, short_b.reshape(1, c3))


def _fft_sizes(seq):
    n = 2 * seq
    n2 = 128 if n >= 4096 else 32
    return n, n // n2, n2


FFT_GROUP = SUBLANES


def _kron_matrices(n, n1, h):
    k1 = np.arange(n1)[:, None]
    i1 = np.arange(h)[None, :]
    th = 2.0 * np.pi * ((k1 * i1) % n1) / n1
    eye = np.eye(FFT_GROUP)
    cs, sn = np.kron(np.cos(th), eye), np.kron(np.sin(th), eye)
    fwd = np.block([[cs, sn], [-sn, cs]])
    inv = np.block([[cs.T, -sn.T], [sn.T, cs.T]]) / n
    return fwd.astype(np.float32), inv.astype(np.float32)


def _twiddle_tables(n, n1, n2):
    k1 = jnp.arange(n1, dtype=jnp.int32)[:, None]
    i2 = jnp.arange(n2, dtype=jnp.int32)[None, :]
    th = ((k1 * i2) % n).astype(F32) * (2.0 * math.pi / n)
    shape = (n1, n2 // FFT_GROUP, FFT_GROUP, LANES)
    full = lambda a: jnp.broadcast_to(a.reshape(shape[:3] + (1,)), shape)
    return full(jnp.cos(th)), full(jnp.sin(th))


def _mid_tables(n2):
    a = np.arange(n2)
    th = 2.0 * np.pi * ((a[:, None] * a[None, :]) % n2) / n2
    cs, sn = np.cos(th), np.sin(th)
    fwd = np.concatenate([np.concatenate([cs, sn], 1), np.concatenate([-sn, cs], 1)], 0)
    inv = np.concatenate([np.concatenate([cs, -sn], 1), np.concatenate([sn, cs], 1)], 0)
    return fwd.astype(np.float32), inv.astype(np.float32)


MID_CHUNK = 32


def _dot(a, b):
    return jnp.dot(a.astype(BF16), b.astype(BF16), preferred_element_type=F32)


def _filter_stage1_body(x_ref, m_ref, c_ref, s_ref, o_ref, *, n1, gs):
    j, c = FFT_GROUP, MIX_CH
    for s in range(gs):
        a = _dot(m_ref[...], x_ref[:, s].reshape(n1 * j, c))
        ar = a[:n1 * j].reshape(n1, j, c)
        ai = a[n1 * j:].reshape(n1, j, c)
        cs, sn = _lanes(c_ref[:, s], c), _lanes(s_ref[:, s], c)
        o_ref[0, 0, :, s] = ar * cs + ai * sn
        o_ref[0, 1, :, s] = ai * cs - ar * sn


def filter_stage1(k, order, mat, twc, tws, *, n1, n2):
    g, j, c = n2 // FFT_GROUP, FFT_GROUP, MIX_CH
    gs = _group_step(g)
    tw = pl.BlockSpec((n1, gs, j, LANES), lambda gi: (0, gi, 0, 0))
    out = pl.pallas_call(
        functools.partial(_filter_stage1_body, n1=n1, gs=gs),
        grid=(g // gs,),
        in_specs=[pl.BlockSpec((n1, gs, j, c), lambda gi: (0, gi, 0, order)),
                  pl.BlockSpec(mat.shape, lambda gi: (0, 0)), tw, tw],
        out_specs=pl.BlockSpec((1, 2, n1, gs, j, c), lambda gi: (0, 0, 0, gi, 0, 0)),
        out_shape=jax.ShapeDtypeStruct((1, 2, n1, g, j, c), F32),
        compiler_params=_params("arbitrary"),
        name="filter_stage1",
    )(k.reshape(n1, g, j, k.shape[-1]), mat, twc, tws)
    return out.reshape(1, 2, n1 * n2, c)


def _mid_body(a_ref, k_ref, f_ref, g_ref, o_ref, *, n2, kc):
    for j in range(kc):
        rows = slice(j * n2, (j + 1) * n2)
        blk = jnp.concatenate([a_ref[0, 0, rows, :], a_ref[0, 1, rows, :]], axis=0)
        s = _dot(f_ref[...], blk)
        sr, si = s[:n2], s[n2:]
        kr, ki = k_ref[0, rows, :], k_ref[1, rows, :]
        y = jnp.concatenate([sr * kr - si * ki, sr * ki + si * kr], axis=0)
        bb = _dot(g_ref[...], y)
        o_ref[0, 0, rows, :] = bb[:n2].astype(o_ref.dtype)
        o_ref[0, 1, rows, :] = bb[n2:].astype(o_ref.dtype)


def fft_mid(a, kf, f_fwd, f_inv, *, n1, n2):
    p = a.shape[0]
    c = MIX_CH
    kc = min(n1, MID_CHUNK)
    rows = kc * n2
    blk = pl.BlockSpec((1, 2, rows, c), lambda j, pp: (pp, 0, j, 0))
    mat = pl.BlockSpec(f_fwd.shape, lambda j, pp: (0, 0))
    return pl.pallas_call(
        functools.partial(_mid_body, n2=n2, kc=kc),
        grid=(n1 // kc, p),
        in_specs=[blk, pl.BlockSpec((2, rows, c), lambda j, pp: (0, j, 0)), mat, mat],
        out_specs=blk,
        out_shape=jax.ShapeDtypeStruct(a.shape, a.dtype),
        compiler_params=_params("arbitrary", "arbitrary"),
        name="fft_mid",
    )(a, kf, f_fwd, f_inv)


def _filter_mid_body(a_ref, f_ref, sum_ref, o_ref, *, n2, kc):
    inv = 1.0 / (sum_ref[0:1, :] + EPS)
    for j in range(kc):
        rows = slice(j * n2, (j + 1) * n2)
        blk = jnp.concatenate([a_ref[0, 0, rows, :], a_ref[0, 1, rows, :]], axis=0)
        s = _dot(f_ref[...], blk) * inv
        o_ref[0, rows, :] = s[:n2]
        o_ref[1, rows, :] = s[n2:]


def filter_mid(a, f_fwd, abs_sum, order, *, n1, n2):
    c = MIX_CH
    kc = min(n1, MID_CHUNK)
    rows = kc * n2
    return pl.pallas_call(
        functools.partial(_filter_mid_body, n2=n2, kc=kc),
        grid=(n1 // kc,),
        in_specs=[pl.BlockSpec((1, 2, rows, c), lambda j: (0, 0, j, 0)),
                  pl.BlockSpec(f_fwd.shape, lambda j: (0, 0)),
                  pl.BlockSpec((SUBLANES, c), lambda j: (0, order))],
        out_specs=pl.BlockSpec((2, rows, c), lambda j: (0, j, 0)),
        out_shape=jax.ShapeDtypeStruct((2, n1 * n2, c), F32),
        compiler_params=_params("arbitrary"),
        name="filter_mid",
    )(a, f_fwd, abs_sum)


def _lanes(t, width):
    return t if width == LANES else jnp.concatenate([t] * (width // LANES), axis=-1)


WORK_ROWS = 2 * FFT_GROUP


def _store_group_pair(o_ref, parts, s):
    if s % 2 == 1:
        for plane in range(2):
            o_ref[0, plane, :, s // 2] = jnp.concatenate([parts[s - 1][plane], parts[s][plane]],
                                                         axis=1).astype(o_ref.dtype)


def _kron1_body(zr_ref, zi_ref, m_ref, c_ref, s_ref, o_ref, *, n1, h, gs):
    j, c = FFT_GROUP, MIX_CH
    parts = []
    for s in range(gs):
        xr = zr_ref[0, :, s].reshape(h * j, c)
        xi = zi_ref[0, :, s].reshape(h * j, c)
        a = _dot(m_ref[...], jnp.concatenate([xr, xi], axis=0))
        ar = a[:n1 * j].reshape(n1, j, c)
        ai = a[n1 * j:].reshape(n1, j, c)
        cs, sn = _lanes(c_ref[:, s], c), _lanes(s_ref[:, s], c)
        parts.append((ar * cs + ai * sn, ai * cs - ar * sn))
        _store_group_pair(o_ref, parts, s)


def _group_step(n_groups):
    return min(n_groups, 8)


def kron_stage1(z5, col, n_pairs, imag_offset, mat, twc, tws, *, n1, h):
    g, j, c = z5.shape[2], FFT_GROUP, MIX_CH
    gs = _group_step(g)
    tw = pl.BlockSpec((n1, gs, j, LANES), lambda gi, p: (0, gi, 0, 0))
    return pl.pallas_call(
        functools.partial(_kron1_body, n1=n1, h=h, gs=gs),
        grid=(g // gs, n_pairs),
        in_specs=[pl.BlockSpec((1, h, gs, j, c), lambda gi, p: (p, 0, gi, 0, col)),
                  pl.BlockSpec((1, h, gs, j, c), lambda gi, p: (p + imag_offset, 0, gi, 0, col)),
                  pl.BlockSpec(mat.shape, lambda gi, p: (0, 0)), tw, tw],
        out_specs=pl.BlockSpec((1, 2, n1, gs // 2, WORK_ROWS, c), lambda gi, p: (p, 0, 0, gi, 0, 0)),
        out_shape=jax.ShapeDtypeStruct((n_pairs, 2, n1, g // 2, WORK_ROWS, c), BF16),
        compiler_params=_params("arbitrary", "arbitrary"),
        name="fft_kron1",
    )(z5, z5, mat, twc, tws)


def _kron3_body(b_ref, m_ref, c_ref, s_ref, zr_ref, zi_ref, gr_ref, gi_ref, d_ref, o_ref, *, n1, h, gs):
    j, c = FFT_GROUP, MIX_CH
    d = d_ref[...].reshape(1, 1, c)
    for s in range(gs):
        half = slice((s % 2) * j, (s % 2 + 1) * j)
        br = b_ref[0, 0, :, s // 2].astype(F32)[:, half]
        bi = b_ref[0, 1, :, s // 2].astype(F32)[:, half]
        cs, sn = _lanes(c_ref[:, s], c), _lanes(s_ref[:, s], c)
        xr = (br * cs - bi * sn).reshape(n1 * j, c)
        xi = (br * sn + bi * cs).reshape(n1 * j, c)
        y = _dot(m_ref[...], jnp.concatenate([xr, xi], axis=0))
        yr = y[:h * j].reshape(h, j, c)
        yi = y[h * j:].reshape(h, j, c)
        o_ref[0, 0, :, s] = gr_ref[0, :, s] * (yr + d * zr_ref[0, :, s])
        o_ref[1, 0, :, s] = gi_ref[0, :, s] * (yi + d * zi_ref[0, :, s])


def kron_stage3(bw, mat, twc, tws, z5, z_col, g5, g_col, d_vec, imag_offset, *, n1, h):
    p = bw.shape[0]
    g, j, c = bw.shape[3] * 2, FFT_GROUP, MIX_CH
    gs = _group_step(g)
    tw = pl.BlockSpec((n1, gs, j, LANES), lambda gi, pp: (0, gi, 0, 0))

    def src(col, off):
        return pl.BlockSpec((1, h, gs, j, c), lambda gi, pp: (pp + off, 0, gi, 0, col))

    return pl.pallas_call(
        functools.partial(_kron3_body, n1=n1, h=h, gs=gs),
        grid=(g // gs, p),
        in_specs=[pl.BlockSpec((1, 2, n1, gs // 2, WORK_ROWS, c), lambda gi, pp: (pp, 0, 0, gi, 0, 0)),
                  pl.BlockSpec(mat.shape, lambda gi, pp: (0, 0)), tw, tw,
                  src(z_col, 0), src(z_col, imag_offset), src(g_col, 0), src(g_col, imag_offset),
                  pl.BlockSpec((1, c), lambda gi, pp: (0, 0))],
        out_specs=pl.BlockSpec((2, 1, h, gs, j, c), lambda gi, pp: (0, pp, 0, gi, 0, 0)),
        out_shape=jax.ShapeDtypeStruct((2, p, h, g, j, c), F32),
        compiler_params=_params("arbitrary", "arbitrary"),
        name="fft_kron3",
    )(bw, mat, twc, tws, z5, z5, g5, g5, d_vec.reshape(1, c))


FILTER_TILE = 512


def _lane_dense(fn, a):
    r, w = a.shape
    f = LANES // w
    rows = r // f
    dense = jnp.concatenate([a[q * rows:(q + 1) * rows] for q in range(f)], axis=1)
    out = fn(dense)
    return jnp.concatenate([out[:, q * w:(q + 1) * w] for q in range(f)], axis=0)


def _filter_body(fw1_ref, fb1_ref, fr1_ref, fw2_ref, fb2_ref, fr2_ref, fw3_ref, fb3_ref, bands_ref, decay_ref,
                 k_ref, sum_ref, *, seq):
    i = pl.program_id(0)
    c = MIX_CH
    hp = lax.Precision.HIGHEST
    n = i * FILTER_TILE + lax.broadcasted_iota(jnp.int32, (FILTER_TILE, 1), 0)
    pos = jnp.where(n <= seq, n, 2 * seq - n).astype(F32)
    t = pos * (1.0 / (seq - 1))
    ang = (pos * (2.0 * math.pi / seq)) * bands_ref[...]
    nb = bands_ref.shape[1]
    pre = (t * fw1_ref[0:1, :]
           + jnp.dot(_lane_dense(jnp.cos, ang), fw1_ref[1:1 + nb, :], preferred_element_type=F32, precision=hp)
           - jnp.dot(_lane_dense(jnp.sin, ang), fw1_ref[1 + nb:1 + 2 * nb, :], preferred_element_type=F32,
                     precision=hp)
           + fb1_ref[...])
    h = _lane_dense(jnp.sin, fr1_ref[...] * pre)
    h = _lane_dense(jnp.sin, fr2_ref[...] * (jnp.dot(h, fw2_ref[...], preferred_element_type=F32, precision=hp)
                                             + fb2_ref[...]))
    h = jnp.dot(h, fw3_ref[...], preferred_element_type=F32, precision=hp) + fb3_ref[...]
    win = jnp.exp(-t * decay_ref[...]) + HYENA_SHIFT
    win = jnp.concatenate([win, win], axis=1)
    fwd, bwd = h[:, :2 * c], h[:, 2 * c:]
    k = jnp.where(n < seq, fwd, bwd) + jnp.where(n == 0, bwd, 0.0)
    k = jnp.where(n == seq, 0.0, k) * win
    k_ref[...] = k

    @pl.when(i == 0)
    def _():
        sum_ref[...] = jnp.zeros(sum_ref.shape, F32)

    sum_ref[...] = sum_ref[...] + jnp.sum(jnp.abs(k), axis=0, keepdims=True)


def hyena_filter_time(seq, fw1, fb1, fr1, fw2, fb2, fr2, fw3, fb3):
    c = MIX_CH
    n_bands = (fw1.shape[0] - 1) // 2
    bands = jnp.linspace(1e-4, n_bands - 1, n_bands, dtype=F32)[None, :]
    max_decay = math.log(HYENA_DECAY_TARGET) / HYENA_FAST_PCT
    min_decay = math.log(HYENA_DECAY_TARGET) / HYENA_SLOW_PCT
    decay = jnp.abs(jnp.linspace(min_decay, max_decay, c, dtype=F32))[None, :]
    assert (2 * seq) % FILTER_TILE == 0
    args = [fw1, fb1[None, :], fr1[None, :], fw2, fb2[None, :], fr2[None, :], fw3, fb3[None, :], bands, decay]
    return pl.pallas_call(
        functools.partial(_filter_body, seq=seq),
        grid=(2 * seq // FILTER_TILE,),
        in_specs=[pl.BlockSpec(a.shape, lambda i: (0, 0)) for a in args],
        out_specs=[pl.BlockSpec((FILTER_TILE, 2 * c), lambda i: (i, 0)),
                   pl.BlockSpec((SUBLANES, 2 * c), lambda i: (0, 0))],
        out_shape=[jax.ShapeDtypeStruct((2 * seq, 2 * c), F32), jax.ShapeDtypeStruct((SUBLANES, 2 * c), F32)],
        compiler_params=_params("arbitrary"),
        name="hyena_filter",
    )(*args)


def hyena_filter_spectrum(seq, filt, tabs):
    n, n1, n2 = _fft_sizes(seq)
    k, abs_sum = hyena_filter_time(seq, *filt)
    twc, tws, f_fwd, m_real = tabs[2], tabs[3], tabs[4], tabs[6]
    return [filter_mid(filter_stage1(k, o, m_real, twc, tws, n1=n1, n2=n2), f_fwd, abs_sum, o, n1=n1, n2=n2)
            for o in range(2)]


def hyena_mixer(ub, short_w, short_b, hy_d, kf, tabs):
    b, l, _ = ub.shape
    n, n1, n2 = _fft_sizes(l)
    h = n1 // 2
    p = b // 2
    c = MIX_CH
    g = n2 // FFT_GROUP
    u = hyena_short_conv(ub, short_w, short_b)
    u5 = u.reshape(b, h, g, FFT_GROUP, u.shape[-1])
    m_fwd, m_inv, twc, tws, f_fwd, f_inv = tabs[:6]
    z5 = u5
    for o in range(2):
        a = kron_stage1(z5, 0, p, p, m_fwd, twc, tws, n1=n1, h=h)
        bw = fft_mid(a.reshape(p, 2, n, c), kf[o], f_fwd, f_inv, n1=n1, n2=n2)
        z = kron_stage3(bw.reshape(a.shape), m_inv, twc, tws, z5, 0, u5, 1 + o, hy_d[o], p, n1=n1, h=h)
        z5 = z.reshape(b, h, g, FFT_GROUP, c)
    return z5.reshape(b, l, c)


def hyena_tables(seq):
    n, n1, n2 = _fft_sizes(seq)
    m_fwd, m_inv = _kron_matrices(n, n1, n1 // 2)
    m_real = _kron_matrices(n, n1, n1)[0][:, :n1 * FFT_GROUP]
    twc, tws = _twiddle_tables(n, n1, n2)
    f_fwd, f_inv = _mid_tables(n2)
    bf = lambda a: jnp.asarray(a).astype(BF16)
    return bf(m_fwd), bf(m_inv), twc, tws, bf(f_fwd), bf(f_inv), bf(m_real)


def _fnet_body(cl_ref, sl_ref, x_ref, cc_ref, sc_ref, o_ref, *, scale):
    x = x_ref[0].astype(BF16)
    pr = jnp.dot(cl_ref[...], x, preferred_element_type=F32).astype(BF16)
    qr = jnp.dot(sl_ref[...], x, preferred_element_type=F32).astype(BF16)
    o_ref[0] = (jnp.dot(pr, cc_ref[...], preferred_element_type=F32)
                - jnp.dot(qr, sc_ref[...], preferred_element_type=F32)) * scale


def _dft_tables(n):
    a = jnp.arange(n, dtype=jnp.int32)
    th = ((a[:, None] * a[None, :]) % n).astype(F32) * (2.0 * math.pi / n)
    return jnp.cos(th).astype(BF16), jnp.sin(th).astype(BF16)


FNET_DIRECT_MAX = 1024


def _fnet1_body(x_ref, ccs_ref, m_ref, c_ref, s_ref, o_ref, *, n1, gs):
    j, c = FFT_GROUP, MIX_CH
    parts = []
    for s in range(gs):
        x = x_ref[0, :, s].reshape(n1 * j, c).astype(BF16)
        z = jnp.dot(x, ccs_ref[...], preferred_element_type=F32)
        a = _dot(m_ref[...], jnp.concatenate([z[:, :c], z[:, c:]], axis=0))
        ar = a[:n1 * j].reshape(n1, j, c)
        ai = a[n1 * j:].reshape(n1, j, c)
        cs, sn = _lanes(c_ref[:, s], c), _lanes(s_ref[:, s], c)
        parts.append((ar * cs + ai * sn, ai * cs - ar * sn))
        _store_group_pair(o_ref, parts, s)


def _fnet2_body(a_ref, m_ref, o_ref, *, n2, scale):
    j, c = FFT_GROUP, MIX_CH
    x = a_ref[0].reshape(2 * j * n2, c)
    y = _dot(m_ref[...], x) * scale
    o_ref[0, :, 0] = y.reshape(n2, j, c)


def fnet_tables(seq):
    c = MIX_CH
    cc, sc = _dft_tables(c)
    if seq <= FNET_DIRECT_MAX:
        return _dft_tables(seq) + (cc, sc)
    n2 = 128
    n1 = seq // n2
    m_fwd, _ = _kron_matrices(seq, n1, n1)
    twc, tws = _twiddle_tables(seq, n1, n2)
    a = np.arange(n2)
    th = 2.0 * np.pi * ((a[:, None] * a[None, :]) % n2) / n2
    eye = np.eye(FFT_GROUP)
    m2 = np.concatenate([np.einsum('kn,ij->kijn', f, eye).reshape(n2 * FFT_GROUP, FFT_GROUP * n2)
                         for f in (np.cos(th), np.sin(th))], axis=1).astype(np.float32)
    ccs = jnp.concatenate([cc, -sc], axis=1)
    return ccs, jnp.asarray(m_fwd).astype(BF16), twc, tws, jnp.asarray(m2).astype(BF16)


def fnet_two_stage(uc, tables):
    b, l, c = uc.shape
    ccs, m_fwd, twc, tws, m2 = tables
    j = FFT_GROUP
    n2 = 128
    n1 = l // n2
    g = n2 // j
    gs = _group_step(g)
    tw = pl.BlockSpec((n1, gs, j, LANES), lambda gi, bb: (0, gi, 0, 0))
    a = pl.pallas_call(
        functools.partial(_fnet1_body, n1=n1, gs=gs),
        grid=(g // gs, b),
        in_specs=[pl.BlockSpec((1, n1, gs, j, c), lambda gi, bb: (bb, 0, gi, 0, 0)),
                  pl.BlockSpec(ccs.shape, lambda gi, bb: (0, 0)),
                  pl.BlockSpec(m_fwd.shape, lambda gi, bb: (0, 0)), tw, tw],
        out_specs=pl.BlockSpec((1, 2, n1, gs // 2, WORK_ROWS, c), lambda gi, bb: (bb, 0, 0, gi, 0, 0)),
        out_shape=jax.ShapeDtypeStruct((b, 2, n1, g // 2, WORK_ROWS, c), BF16),
        compiler_params=_params("arbitrary", "arbitrary"),
        name="fnet_stage1",
    )(uc.reshape(b, n1, g, j, c), ccs, m_fwd, twc, tws)
    out = pl.pallas_call(
        functools.partial(_fnet2_body, n2=n2, scale=1.0 / math.sqrt(l * c)),
        grid=(b, n1 // j),
        in_specs=[pl.BlockSpec((1, 2, j, n2, c), lambda bb, q: (bb, 0, q, 0, 0)),
                  pl.BlockSpec(m2.shape, lambda bb, q: (0, 0))],
        out_specs=pl.BlockSpec((1, n2, 1, j, c), lambda bb, q: (bb, 0, q, 0, 0)),
        out_shape=jax.ShapeDtypeStruct((b, n2, n1 // j, j, c), F32),
        compiler_params=_params("arbitrary", "arbitrary"),
        name="fnet_stage2",
    )(a.reshape(b, 2, n1, n2, c), m2)
    return out.reshape(b, l, c)


def fnet_mixer(uc, tables):
    b, l, c = uc.shape
    if l > FNET_DIRECT_MAX:
        return fnet_two_stage(uc, tables)
    cl, sl, cc, sc = tables
    tm = min(l, 512)
    row = pl.BlockSpec((tm, l), lambda i, bb: (i, 0))
    sq = pl.BlockSpec((c, c), lambda i, bb: (0, 0))
    return pl.pallas_call(
        functools.partial(_fnet_body, scale=1.0 / math.sqrt(l * c)),
        grid=(l // tm, b),
        in_specs=[row, row, pl.BlockSpec((1, l, c), lambda i, bb: (bb, 0, 0)), sq, sq],
        out_specs=pl.BlockSpec((1, tm, c), lambda i, bb: (bb, i, 0)),
        out_shape=jax.ShapeDtypeStruct((b, l, c), F32),
        compiler_params=_params("arbitrary", "arbitrary"),
        name="fnet",
    )(cl, sl, uc, cc, sc)


def _heads_rows(x, g):
    h0 = Q_PER_KV * g
    return jnp.concatenate([x[:, (h0 + r) * HEAD_DIM:(h0 + r + 1) * HEAD_DIM] for r in range(Q_PER_KV)], axis=0)


def _qk(q, k):
    return lax.dot_general(q.astype(BF16), k.astype(BF16), (((1,), (1,)), ((), ())),
                           preferred_element_type=F32)


def _sink_col(sink_ref, g, rows):
    ridx = lax.broadcasted_iota(jnp.int32, (Q_PER_KV * rows, 1), 0)
    col = jnp.full((Q_PER_KV * rows, 1), sink_ref[Q_PER_KV * g], F32)
    for r in range(1, Q_PER_KV):
        col = jnp.where(ridx >= r * rows, sink_ref[Q_PER_KV * g + r], col)
    return col * LOG2E


def _lat_attn_body(sink_ref, q_ref, qr_ref, kp_ref, kc_ref, kn_ref, ck_ref, cv_ref, o_ref, *, sub):
    i = pl.program_id(1)
    n_qblk = pl.num_programs(1) * sub
    blk = ATT_BLOCK
    span = blk + 2 * WINDOW
    kv = jnp.concatenate([kp_ref[0], kc_ref[0], kn_ref[0]], axis=0)
    ck = ck_ref[0, 0].astype(BF16)
    cv = cv_ref[0, 0].astype(BF16)
    kvw = N_KV_HEADS * HEAD_DIM
    r = lax.broadcasted_iota(jnp.int32, (Q_PER_KV * blk, span), 0) % blk
    j = lax.broadcasted_iota(jnp.int32, (Q_PER_KV * blk, span), 1)
    band = (j >= r) & (j <= r + 2 * WINDOW)
    for s in range(sub):
        qi = i * sub + s
        ok = band & ((qi > 0) | (j >= WINDOW)) & ((qi < n_qblk - 1) | (j < WINDOW + blk))
        q = q_ref[0, s * blk:(s + 1) * blk, :]
        qr = qr_ref[0, s * blk:(s + 1) * blk, :]
        outs = []
        for g in range(N_KV_HEADS):
            kl = kv[s * blk:s * blk + span, g * HEAD_DIM:(g + 1) * HEAD_DIM]
            vl = kv[s * blk:s * blk + span, kvw + g * HEAD_DIM:kvw + (g + 1) * HEAD_DIM]
            s_loc = jnp.where(ok, _qk(_heads_rows(qr, g), kl), NEG_INF)
            s_ctx = _qk(_heads_rows(q, g), ck[:, g * HEAD_DIM:(g + 1) * HEAD_DIM])
            sink = _sink_col(sink_ref, g, blk)
            m = jnp.maximum(jnp.maximum(jnp.max(s_loc, axis=-1, keepdims=True),
                                        jnp.max(s_ctx, axis=-1, keepdims=True)), sink)
            e_loc = jnp.exp2(s_loc - m)
            e_ctx = jnp.exp2(s_ctx - m)
            den = (jnp.sum(e_loc, axis=-1, keepdims=True) + jnp.sum(e_ctx, axis=-1, keepdims=True)
                   + jnp.exp2(sink - m))
            o = (jnp.dot(e_loc.astype(BF16), vl, preferred_element_type=F32)
                 + jnp.dot(e_ctx.astype(BF16), cv[:, g * HEAD_DIM:(g + 1) * HEAD_DIM],
                           preferred_element_type=F32)) * (1.0 / den)
            outs += [o[rr * blk:(rr + 1) * blk] for rr in range(Q_PER_KV)]
        o_ref[0, s * blk:(s + 1) * blk, :] = jnp.concatenate(outs, axis=1).astype(o_ref.dtype)


ATT_SUB = 4


def latent_attention(uq, uqr, ukv, cache_k, cache_v, layer, sink):
    b, l, c = uq.shape
    p = cache_k.shape[2]
    blk = ATT_BLOCK
    nblk = l // blk
    sub = math.gcd(ATT_SUB, nblk)
    rows = sub * blk
    qspec = pl.BlockSpec((1, rows, c), lambda bb, i: (bb, i, 0))
    cspec = pl.BlockSpec((1, 1, p, cache_k.shape[3]), lambda bb, i: (bb, layer, 0, 0))
    return pl.pallas_call(
        functools.partial(_lat_attn_body, sub=sub),
        grid=(b, nblk // sub),
        in_specs=[pl.BlockSpec(memory_space=pltpu.SMEM), qspec, qspec,
                  pl.BlockSpec((1, blk, c), lambda bb, i: (bb, jnp.maximum(i * sub - 1, 0), 0)),
                  qspec,
                  pl.BlockSpec((1, blk, c), lambda bb, i: (bb, jnp.minimum((i + 1) * sub, nblk - 1), 0)),
                  cspec, cspec],
        out_specs=qspec,
        out_shape=jax.ShapeDtypeStruct((b, l, c), BF16),
        compiler_params=_params("arbitrary", "arbitrary"),
        name="latent_attention",
    )(sink, uq, uqr, ukv, ukv, ukv, cache_k, cache_v)


def _ctx_attn_body(sink_ref, q_ref, kv_ref, o_ref, *, seq):
    q = q_ref[0]
    kv = kv_ref[0]
    kvw = N_KV_HEADS * HEAD_DIM
    outs = []
    for g in range(N_KV_HEADS):
        kl = kv[:, g * HEAD_DIM:(g + 1) * HEAD_DIM]
        vl = kv[:, kvw + g * HEAD_DIM:kvw + (g + 1) * HEAD_DIM]
        s = _qk(_heads_rows(q, g), kl)
        sink = _sink_col(sink_ref, g, seq)
        m = jnp.maximum(jnp.max(s, axis=-1, keepdims=True), sink)
        e = jnp.exp2(s - m)
        den = jnp.sum(e, axis=-1, keepdims=True) + jnp.exp2(sink - m)
        o = jnp.dot(e.astype(BF16), vl.astype(BF16), preferred_element_type=F32) * (1.0 / den)
        outs += [o[rr * seq:(rr + 1) * seq] for rr in range(Q_PER_KV)]
    o_ref[0] = jnp.concatenate(outs, axis=1)


def context_attention(uq, ukv, sink):
    b, s, c = uq.shape
    spec = pl.BlockSpec((1, s, c), lambda bb: (bb, 0, 0))
    return pl.pallas_call(
        functools.partial(_ctx_attn_body, seq=s),
        grid=(b,),
        in_specs=[pl.BlockSpec(memory_space=pltpu.SMEM), spec, spec],
        out_specs=spec,
        out_shape=jax.ShapeDtypeStruct((b, s, c), F32),
        compiler_params=_params("arbitrary"),
        name="context_attention",
    )(sink, uq, ukv)


def _pack_bf16_pairs(hi_rounded):
    k = hi_rounded.shape[1] // 2
    bits = lax.bitcast_convert_type(hi_rounded, jnp.uint32)
    return bits[:, :k] | (bits[:, k:] >> 16)


def _unpack_bf16_pairs(packed, dtype=BF16):
    a = lax.bitcast_convert_type(packed & jnp.uint32(0xFFFF0000), F32)
    b = lax.bitcast_convert_type(packed << 16, F32)
    return jnp.concatenate([a, b], axis=1).astype(dtype)


def _out_body(ya_ref, yb_ref, yc_ref, yd_ref, x_ref, mod_ref, g_ref, w_ref, rw_ref, rb_ref,
              x1_ref, h_ref, route_ref, cnt_ref):
    c = MIX_CH
    y = jnp.dot(ya_ref[0].astype(BF16), w_ref[0:c, :], preferred_element_type=F32)
    for j, ref in enumerate((yb_ref, yc_ref, yd_ref), start=1):
        y = y + jnp.dot(ref[0].astype(BF16), w_ref[j * c:(j + 1) * c, :], preferred_element_type=F32)
    x1 = x_ref[0] + mod_ref[0, 2:3, :] * y
    x1_ref[0] = x1
    h = _rmsnorm_mod(x1, g_ref[...], mod_ref[0, 4:5, :], mod_ref[0, 3:4, :])
    h_hi = h.astype(BF16)
    h_hi32 = h_hi.astype(F32)
    h_ref[0] = _pack_bf16_pairs(h_hi32)
    h_lo = (h - h_hi32).astype(BF16)
    tm = h.shape[0]
    prod = jnp.dot(jnp.concatenate([h_hi, h_lo], axis=0), rw_ref[...], preferred_element_type=F32)
    logits = (prod[:tm, :ROUTE_LANES] + prod[:tm, ROUTE_LANES:]
              + prod[tm:, :ROUTE_LANES] + prod[tm:, ROUTE_LANES:]) + rb_ref[...]
    lane = lax.broadcasted_iota(jnp.int32, logits.shape, 1)
    is_c = lane < N_GROUPS
    lc = jnp.where(is_c, logits, NEG_INF)
    mc = jnp.max(lc, axis=-1, keepdims=True)
    grp = jnp.min(jnp.where(lc == mc, lane, ROUTE_LANES), axis=-1, keepdims=True)
    pg = 1.0 / jnp.sum(jnp.where(is_c, jnp.exp(lc - mc), 0.0), axis=-1, keepdims=True)
    lo = N_GROUPS + grp * EXPERTS_PER_GROUP
    in_g = (lane >= lo) & (lane < lo + EXPERTS_PER_GROUP)
    lf = jnp.where(in_g, logits, NEG_INF)
    t1 = jnp.max(lf, axis=-1, keepdims=True)
    i1 = jnp.min(jnp.where(lf == t1, lane, ROUTE_LANES), axis=-1, keepdims=True)
    lf2 = jnp.where(lane == i1, NEG_INF, lf)
    t2 = jnp.max(lf2, axis=-1, keepdims=True)
    i2 = jnp.min(jnp.where(lf2 == t2, lane, ROUTE_LANES), axis=-1, keepdims=True)
    e2 = jnp.exp(t2 - t1)
    w1 = pg / (1.0 + e2)
    w2 = pg * e2 / (1.0 + e2)
    rec = jnp.where(lane == 0, (i1 - N_GROUPS).astype(F32),
                    jnp.where(lane == 1, (i2 - N_GROUPS).astype(F32),
                              jnp.where(lane == 2, w1, jnp.where(lane == 3, w2, 0.0))))
    route_ref[0] = rec

    @pl.when((pl.program_id(0) == 0) & (pl.program_id(1) == 0))
    def _():
        cnt_ref[...] = jnp.zeros(cnt_ref.shape, F32)

    e0, e1 = _choice_onehots(rec)
    cnt_ref[...] = cnt_ref[...] + jnp.sum(e0 + e1, axis=0, keepdims=True)


def out_projection(ys, x, mods, mod_row0, norm_g, w_out_bf, rw, rb, *, tm):
    b, l, d = x.shape
    c = MIX_CH
    row = (lambda bb: 0) if mod_row0 is None else (lambda bb: mod_row0 + bb)
    yspec = pl.BlockSpec((1, tm, c), lambda bb, i: (bb, i, 0))
    xspec = pl.BlockSpec((1, tm, d), lambda bb, i: (bb, i, 0))
    return pl.pallas_call(
        _out_body,
        grid=(b, l // tm),
        in_specs=[yspec] * 4 + [xspec,
                                pl.BlockSpec((1, N_MOD, d), lambda bb, i: (row(bb), 0, 0)),
                                pl.BlockSpec((1, d), lambda bb, i: (0, 0)),
                                pl.BlockSpec(w_out_bf.shape, lambda bb, i: (0, 0)),
                                pl.BlockSpec(rw.shape, lambda bb, i: (0, 0)),
                                pl.BlockSpec(rb.shape, lambda bb, i: (0, 0))],
        out_specs=[xspec, pl.BlockSpec((1, tm, d // 2), lambda bb, i: (bb, i, 0)),
                   pl.BlockSpec((1, tm, ROUTE_LANES), lambda bb, i: (bb, i, 0)),
                   pl.BlockSpec((SUBLANES, ROUTE_LANES), lambda bb, i: (0, 0))],
        out_shape=[jax.ShapeDtypeStruct((b, l, d), F32), jax.ShapeDtypeStruct((b, l, d // 2), jnp.uint32),
                   jax.ShapeDtypeStruct((b, l, ROUTE_LANES), F32),
                   jax.ShapeDtypeStruct((SUBLANES, ROUTE_LANES), F32)],
        compiler_params=_params("arbitrary", "arbitrary"),
        name="out_proj",
    )(*ys, x, mods, norm_g.reshape(1, d), w_out_bf, rw, rb)


def _expert_body(be_ref, nv_ref, xs_ref, wg_ref, wu_ref, wd_ref, o_ref, wg_s, wu_s, wd_s):
    i = pl.program_id(0)
    prev = be_ref[jnp.maximum(i - 1, 0)]

    @pl.when((i == 0) | (be_ref[i] != prev))
    def _():
        wg_s[...] = wg_ref[0, 0].astype(BF16)
        wu_s[...] = wu_ref[0, 0].astype(BF16)
        wd_s[...] = wd_ref[0, 0].astype(BF16)

    @pl.when(nv_ref[i] > 0)
    def _():
        row = lax.broadcasted_iota(jnp.int32, xs_ref.shape, 0)
        x = _unpack_bf16_pairs(jnp.where(row < nv_ref[i], xs_ref[...], jnp.uint32(0)))
        g = jnp.dot(x, wg_s[...], preferred_element_type=F32)
        u = jnp.dot(x, wu_s[...], preferred_element_type=F32)
        a = (_silu(g) * u).astype(BF16)
        y = jnp.dot(a, wd_s[...], preferred_element_type=F32)
        o_ref[...] = _pack_bf16_pairs(y.astype(BF16).astype(F32))

    @pl.when(nv_ref[i] <= 0)
    def _():
        o_ref[...] = jnp.zeros(o_ref.shape, jnp.uint32)


def expert_ffn(xs, blk_e, n_valid, layer, e_gate, e_up, e_down):
    rows, dh = xs.shape
    d = 2 * dh
    nb = rows // MOE_BLOCK
    de = e_gate.shape[-1]
    grid_spec = pltpu.PrefetchScalarGridSpec(
        num_scalar_prefetch=2,
        grid=(nb,),
        in_specs=[pl.BlockSpec((MOE_BLOCK, dh), lambda i, be, nv: (i, 0)),
                  pl.BlockSpec((1, 1, d, de), lambda i, be, nv: (layer, be[i], 0, 0)),
                  pl.BlockSpec((1, 1, d, de), lambda i, be, nv: (layer, be[i], 0, 0)),
                  pl.BlockSpec((1, 1, de, d), lambda i, be, nv: (layer, be[i], 0, 0))],
        out_specs=pl.BlockSpec((MOE_BLOCK, dh), lambda i, be, nv: (i, 0)),
        scratch_shapes=[pltpu.VMEM((d, de), BF16), pltpu.VMEM((d, de), BF16), pltpu.VMEM((de, d), BF16)],
    )
    return pl.pallas_call(
        _expert_body, grid_spec=grid_spec,
        out_shape=jax.ShapeDtypeStruct((rows, dh), jnp.uint32),
        compiler_params=_params("arbitrary"),
        name="expert_ffn",
    )(blk_e, n_valid, xs, e_gate, e_up, e_down)


RANK_TILE = 1024


def _choice_onehots(rec):
    lanef = lax.broadcasted_iota(jnp.int32, rec.shape, 1).astype(F32)
    return (lanef == rec[:, 0:1]).astype(F32), (lanef == rec[:, 1:2]).astype(F32)


def _slot_body(route_ref, cnt_ref, tri_ref, slot_ref, carry_ref):
    rec = route_ref[...]
    lane = lax.broadcasted_iota(jnp.int32, rec.shape, 1)
    e0, e1 = _choice_onehots(rec)
    both = e0 + e1

    @pl.when(pl.program_id(0) == 0)
    def _():
        cnt = cnt_ref[...]
        padded = jnp.floor((cnt + (MOE_BLOCK - 1)) * (1.0 / MOE_BLOCK)) * MOE_BLOCK
        ln = lax.broadcasted_iota(jnp.int32, cnt.shape, 1)
        incl = padded
        sh = 1
        while sh < ROUTE_LANES:
            incl = incl + jnp.where(ln >= sh, pltpu.roll(incl, sh, axis=1), 0.0)
            sh *= 2
        carry_ref[...] = incl - padded

    before = jnp.dot(tri_ref[...], both.astype(BF16), preferred_element_type=F32) + carry_ref[0:1, :]
    s0 = jnp.sum(e0 * before, axis=-1, keepdims=True)
    s1 = jnp.sum(e1 * before, axis=-1, keepdims=True)
    rec_t = jnp.transpose(jnp.where(lane == 0, s0, jnp.where(lane == 1, s1, 0.0)))
    slot_ref[...] = rec_t[0:SUBLANES, :].astype(jnp.int32)
    carry_ref[...] = carry_ref[...] + jnp.sum(both, axis=0, keepdims=True)


def moe_slots(route, counts):
    n = route.shape[0]
    t = RANK_TILE
    tri = jnp.asarray(np.tril(np.ones((t, t), np.float32), -1)).astype(BF16)
    return pl.pallas_call(
        _slot_body,
        grid=(n // t,),
        in_specs=[pl.BlockSpec((t, ROUTE_LANES), lambda i: (i, 0)),
                  pl.BlockSpec((SUBLANES, ROUTE_LANES), lambda i: (0, 0)),
                  pl.BlockSpec((t, t), lambda i: (0, 0))],
        out_specs=pl.BlockSpec((SUBLANES, t), lambda i: (0, i)),
        out_shape=jax.ShapeDtypeStruct((SUBLANES, n), jnp.int32),
        scratch_shapes=[pltpu.VMEM((SUBLANES, ROUTE_LANES), F32)],
        compiler_params=_params("arbitrary"),
        name="moe_slots",
    )(route, counts, tri)


def _sc_mesh():
    return plsc.VectorSubcoreMesh(core_axis_name="c", subcore_axis_name="s")


def _sc_worker():
    return lax.axis_index("s") * SC_CORES + lax.axis_index("c")


DISPATCH_ROWS = 64
COMBINE_ROWS = 64


def sc_dispatch(rows, dest, n_slots):
    n, w = rows.shape
    ch = DISPATCH_ROWS
    per_w = n // SC_WORKERS
    n_ch = per_w // ch

    @functools.partial(
        pl.kernel, mesh=_sc_mesh(),
        out_type=jax.ShapeDtypeStruct((n_slots, w), rows.dtype),
        scratch_types=[pltpu.VMEM((ch,), jnp.int32), pltpu.VMEM((ch, w), rows.dtype)],
    )
    def scatter_kernel(rows_hbm, dest_hbm, out_hbm, idx_v, rows_v):
        wid = _sc_worker()

        @pl.loop(0, n_ch)
        def _(j):
            chunk = wid * n_ch + j
            pltpu.sync_copy(rows_hbm.at[pl.ds(pl.multiple_of(chunk * ch, ch), ch)], rows_v)
            for k in range(2):
                pltpu.sync_copy(dest_hbm.at[k, chunk], idx_v)
                pltpu.sync_copy(rows_v, out_hbm.at[idx_v])

    return scatter_kernel(rows, dest)


def sc_gather_rows(table, idx):
    s, w = table.shape
    m = idx.shape[0]
    ch = COMBINE_ROWS
    per_w = m // SC_WORKERS
    n_ch = per_w // ch

    @functools.partial(
        pl.kernel, mesh=_sc_mesh(),
        out_type=jax.ShapeDtypeStruct((m, w), table.dtype),
        scratch_types=[pltpu.VMEM((ch,), jnp.int32), pltpu.VMEM((ch, w), table.dtype), pltpu.SemaphoreType.DMA],
    )
    def gather_kernel(table_hbm, idx_hbm, out_hbm, idx_v, rows_v, sem):
        wid = _sc_worker()

        @pl.loop(0, n_ch)
        def _(j):
            off = pl.multiple_of((wid * n_ch + j) * ch, ch)
            pltpu.sync_copy(idx_hbm.at[pl.ds(off, ch)], idx_v)
            pltpu.async_copy(table_hbm.at[idx_v], rows_v, sem).wait()
            pltpu.sync_copy(rows_v, out_hbm.at[pl.ds(off, ch)])

    return gather_kernel(table, idx)


def hier_moe(h_packed, route, cnt, layer, e_gate, e_up, e_down):
    b, l, dh = h_packed.shape
    n = b * l
    assert n % (SC_WORKERS * DISPATCH_ROWS) == 0 and (2 * n) % (SC_WORKERS * COMBINE_ROWS) == 0
    slots = moe_slots(route.reshape(n, ROUTE_LANES), cnt)
    counts = cnt[0, :N_EXPERTS].astype(jnp.int32)
    padded = (counts + MOE_BLOCK - 1) // MOE_BLOCK * MOE_BLOCK
    pend = jnp.cumsum(padded)
    nb = -(-2 * n // MOE_BLOCK) + N_EXPERTS
    blk0 = jnp.arange(nb, dtype=jnp.int32) * MOE_BLOCK
    owner = pend[None, :] <= blk0[:, None]
    blk_e = jnp.minimum(jnp.sum(owner, axis=1), N_EXPERTS - 1).astype(jnp.int32)
    run_end = jnp.sum(jnp.where(jnp.arange(N_EXPERTS)[None, :] == blk_e[:, None],
                                (pend - padded + counts)[None, :], 0), axis=1)
    n_valid = jnp.clip(run_end - blk0, 0, MOE_BLOCK).astype(jnp.int32)
    dest = slots[0:2]
    xs = sc_dispatch(h_packed.reshape(n, dh), dest.reshape(2, n // DISPATCH_ROWS, DISPATCH_ROWS), nb * MOE_BLOCK)
    y = expert_ffn(xs, blk_e, n_valid, layer, e_gate, e_up, e_down)
    return sc_gather_rows(y, dest.reshape(2 * n)).reshape(2, b, l, dh)


def _final_body(x_ref, y0_ref, y1_ref, route_ref, pmod_ref, g_ref, o_ref):
    x = _moe_residual(x_ref[0], y0_ref, y1_ref, route_ref, pmod_ref)
    ms = jnp.mean(x * x, axis=-1, keepdims=True)
    o_ref[0] = x * lax.rsqrt(ms + EPS) * g_ref[...]


def final_norm(x1, res, mod_row0, norm_g, *, tm):
    b, l, d = x1.shape
    row = (lambda bb: 0) if mod_row0 is None else (lambda bb: mod_row0 + bb)
    xspec = pl.BlockSpec((1, tm, d), lambda bb, i: (bb, i, 0))
    rargs, rspecs = _residual_specs(res, tm, d, row)
    return pl.pallas_call(
        _final_body,
        grid=(b, l // tm),
        in_specs=[xspec] + rspecs + [pl.BlockSpec((1, d), lambda bb, i: (0, 0))],
        out_specs=xspec,
        out_shape=jax.ShapeDtypeStruct((b, l, d), F32),
        compiler_params=_params("arbitrary", "arbitrary"),
        name="final_norm",
    )(x1, *rargs, norm_g.reshape(1, d))


def _rope_tables(seq):
    rows = seq // GRID_W
    row_pos = jnp.repeat(jnp.arange(rows, dtype=F32), GRID_W)
    col_pos = jnp.tile(jnp.arange(GRID_W, dtype=F32), rows)
    n_freq = HEAD_DIM // 4
    inv = ROPE_BASE ** (-jnp.arange(n_freq, dtype=F32) / n_freq)
    ang = jnp.concatenate([row_pos[:, None] * inv, col_pos[:, None] * inv], axis=-1)
    cs, sn = jnp.cos(ang), jnp.sin(ang)
    cos_f = jnp.tile(jnp.concatenate([cs, cs], axis=-1), (1, N_Q_HEADS))
    sin_s = jnp.tile(jnp.concatenate([-sn, sn], axis=-1), (1, N_Q_HEADS))
    return cos_f, sin_s


def kernel(x_prompt, x_sample, cache_k, cache_v, c, c_ctx, ada_w, ada_b, norm1_g, norm2_g, w_in, conv_dw_w, conv_dw_b, conv_ln_g, conv_ln_b, hy_short_w, hy_short_b, hy_fw1, hy_fb1, hy_freq1, hy_fw2, hy_fb2, hy_freq2, hy_fw3, hy_fb3, hy_d, attn_sink, w_out, router_coarse_w, router_coarse_b, router_fine_w, router_fine_b, exp_gate, exp_up, exp_down, norm_f_g):
    depth = ada_w.shape[0]
    bp, lp, d = x_prompt.shape
    bs, ls, _ = x_sample.shape
    assert bp % 2 == 0 and bs % 2 == 0 and ls % ATT_BLOCK == 0 and ls % GRID_W == 0

    n_rows = -(-(1 + bs) // SUBLANES) * SUBLANES
    cvec = jnp.concatenate([c_ctx[None, :], c, jnp.zeros((n_rows - 1 - bs, d), F32)], axis=0)
    mods = adaln_all(cvec, ada_w, ada_b)

    rope = _rope_tables(ls)
    fnet_tabs, hy_tabs = {}, {}
    for seq in {lp, ls}:
        fnet_tabs[seq] = fnet_tables(seq)
        hy_tabs[seq] = hyena_tables(seq)
    ck = cache_k.reshape(cache_k.shape[0], depth, cache_k.shape[2], -1)
    cv = cache_v.reshape(cache_v.shape[0], depth, cache_v.shape[2], -1)
    pad = ROUTE_LANES - N_GROUPS - N_EXPERTS

    tm_p = min(lp, 512)
    tm_s = min(ls, 1024)
    xp, xs = x_prompt, x_sample
    res_p = res_s = None
    ks_out, vs_out = [], []
    for l in range(depth):
        w_in_bf = w_in[l].astype(BF16)
        w_out_bf = w_out[l].astype(BF16)
        rw = jnp.concatenate([router_coarse_w[l], router_fine_w[l], jnp.zeros((d, pad), F32)], axis=1)
        rw_hi = rw.astype(BF16)
        rw = jnp.concatenate([rw_hi, (rw - rw_hi.astype(F32)).astype(BF16)], axis=1)
        rb = jnp.concatenate([router_coarse_b[l], router_fine_b[l], jnp.zeros((pad,), F32)])[None, :]
        filt = (hy_fw1[l], hy_fb1[l], hy_freq1[l], hy_fw2[l], hy_fb2[l], hy_freq2[l], hy_fw3[l], hy_fb3[l])
        sink = attn_sink[l]

        def mixers(ua, ub, uc, yd, seq):
            ya = conformer_conv(ua, conv_dw_w[l], conv_dw_b[l], conv_ln_g[l], conv_ln_b[l])
            yb = hyena_mixer(ub, hy_short_w[l], hy_short_b[l], hy_d[l], hyena_filter_spectrum(seq, filt, hy_tabs[seq]),
                             hy_tabs[seq])
            yc = fnet_mixer(uc, fnet_tabs[seq])
            return (ya, yb, yc, yd)

        outs = in_projection(xp, mods[l], None, norm1_g[l], w_in_bf, res=res_p, tm=tm_p)
        if res_p is not None:
            xp, outs = outs[0], outs[1:]
        ua, ub, uc, uq, ukv = outs
        kvw = N_KV_HEADS * HEAD_DIM
        ks_out.append(ukv[..., :kvw].reshape(bp, lp, N_KV_HEADS, HEAD_DIM))
        vs_out.append(ukv[..., kvw:].reshape(bp, lp, N_KV_HEADS, HEAD_DIM))
        ys = mixers(ua, ub, uc, context_attention(uq, ukv, sink), lp)
        x1p, hp, route, cnt = out_projection(ys, xp, mods[l], None, norm2_g[l], w_out_bf, rw, rb, tm=tm_p)
        res_p = (hier_moe(hp, route, cnt, l, exp_gate, exp_up, exp_down), route, mods[l])
        xp = x1p

        outs = in_projection(xs, mods[l], 1, norm1_g[l], w_in_bf, res=res_s, rope=rope, tm=tm_s)
        if res_s is not None:
            xs, outs = outs[0], outs[1:]
        ua, ub, uc, uq, uqr, ukv = outs
        ys = mixers(ua, ub, uc, latent_attention(uq, uqr, ukv, ck, cv, l, sink), ls)
        x1s, hs, route, cnt = out_projection(ys, xs, mods[l], 1, norm2_g[l], w_out_bf, rw, rb, tm=tm_s)
        res_s = (hier_moe(hs, route, cnt, l, exp_gate, exp_up, exp_down), route, mods[l])
        xs = x1s

    y_prompt = final_norm(xp, res_p, None, norm_f_g, tm=tm_p)
    y_sample = final_norm(xs, res_s, 1, norm_f_g, tm=tm_s)
    return (y_prompt, y_sample, jnp.stack(ks_out, axis=1), jnp.stack(vs_out, axis=1))
```
